```python
import jax, jax.numpy as jnp
from jax import lax
import numpy as np

D_MODEL = 2048
BATCH = 4
SEQ = 4096
DEPTH = 2

GRID_W = 64
CTX_LEN = 256
MLA_HEADS = 8
QK_NOPE = 64
QK_ROPE = 32
V_HEAD = 64
Q_LORA = 512
KV_LORA = 256
ROPE_BASE = 10000.0
Q_BLOCK = 128
CONV_CH = 512
CONV_WIDTH = 31
FFT_CH = 512
FFT_GROUPS = 4
POOL_CH = 512
POOL_WINDOWS = (2, 4, 8, 16)
POOL_GROUP = POOL_CH // len(POOL_WINDOWS)
N_BRANCHES = 4
N_EXPERTS = 64
N_EXPERT_GROUPS = 8
TOPK_GROUPS = 4
TOP_K = 8
EXPERT_FF = 512
SHARED_FF = 512
ROUTED_SCALE = 2.5
EXPERT_BLOCK = 256
EPS = 1e-6

MLA_IN = Q_LORA + KV_LORA + QK_ROPE
IN_COLS = MLA_IN + 2 * CONV_CH + FFT_CH + POOL_CH + N_BRANCHES * D_MODEL

kernel_name = 'hybrid_mla_conv_fnet_pool_moe_dit'

F32 = jnp.float32


def rmsnorm(x, g):
    xf = x.astype(F32)
    y = xf * lax.rsqrt(jnp.mean(xf * xf, axis=-1, keepdims=True) + EPS)
    return y.astype(x.dtype) * g


def layernorm(x, g, b):
    xf = x.astype(F32)
    mu = jnp.mean(xf, axis=-1, keepdims=True)
    var = jnp.mean(jnp.square(xf - mu), axis=-1, keepdims=True)
    return ((xf - mu) * lax.rsqrt(var + EPS)).astype(x.dtype) * g + b


def modulation(cvec, w_mod, b_mod):
    m = jax.nn.silu(cvec) @ w_mod + b_mod
    return m.reshape(cvec.shape[0], 6, cvec.shape[-1])


def modulate(h, shift, scale):
    return h * (1 + scale[:, None, :]) + shift[:, None, :]


def axial_rope(rows):
    row = jnp.repeat(jnp.arange(rows, dtype=F32), GRID_W)
    col = jnp.tile(jnp.arange(GRID_W, dtype=F32), rows)
    half = QK_ROPE // 2
    inv = ROPE_BASE ** (-jnp.arange(0, half, 2, dtype=F32) / half)
    ang_r = row[:, None] * inv
    ang_c = col[:, None] * inv
    ang = jnp.concatenate([ang_r, ang_r, ang_c, ang_c], axis=-1)
    return jnp.cos(ang), jnp.sin(ang)


def apply_rope(x, cos, sin):
    xr = x.reshape(x.shape[:-1] + (2, 2, QK_ROPE // 4))
    rot = jnp.stack([-xr[..., 1, :], xr[..., 0, :]], axis=-2).reshape(x.shape)
    return (x.astype(F32) * cos + rot.astype(F32) * sin).astype(x.dtype)


def mla_project(p, g_cq, w_uq, g_ckv, w_ukv, cos, sin):
    B, n = p.shape[0], p.shape[1]
    cq = p[..., :Q_LORA]
    ckv = p[..., Q_LORA:Q_LORA + KV_LORA]
    kr = p[..., Q_LORA + KV_LORA:MLA_IN]
    q = (rmsnorm(cq, g_cq) @ w_uq).reshape(B, n, MLA_HEADS, QK_NOPE + QK_ROPE)
    kv = (rmsnorm(ckv, g_ckv) @ w_ukv).reshape(B, n, MLA_HEADS, QK_NOPE + V_HEAD)
    q_nope, q_rope = q[..., :QK_NOPE], q[..., QK_NOPE:]
    k_nope, v = kv[..., :QK_NOPE], kv[..., QK_NOPE:]
    if cos is not None:
        q_rope = apply_rope(q_rope, cos[:, None, :], sin[:, None, :])
        kr = apply_rope(kr, cos, sin)
    k_rope = jnp.broadcast_to(kr[:, :, None, :], (B, n, MLA_HEADS, QK_ROPE))
    q = jnp.concatenate([q_nope, q_rope], axis=-1)
    k = jnp.concatenate([k_nope, k_rope], axis=-1)
    return q, k, v


def attend(q, k, v):
    scale = (QK_NOPE + QK_ROPE) ** -0.5
    s = jnp.einsum('bqhd,bkhd->bhqk', q, k).astype(F32) * scale
    pr = jax.nn.softmax(s, axis=-1).astype(v.dtype)
    return jnp.einsum('bhqk,bkhd->bqhd', pr, v)


def latent_attention(q, k_all, v_all):
    B, L, H, Dk = q.shape
    nb = L // Q_BLOCK
    qb = q.reshape(B, nb, Q_BLOCK, H, Dk).transpose(1, 0, 2, 3, 4)
    o = lax.map(lambda qi: attend(qi, k_all, v_all), qb)
    return o.transpose(1, 0, 2, 3, 4).reshape(B, L, H * V_HEAD)


def conv_branch(pc, conv_w, conv_b, ln_g, ln_b, w_o):
    a, b = pc[..., :CONV_CH], pc[..., CONV_CH:]
    u = a * jax.nn.sigmoid(b)
    pad = CONV_WIDTH // 2
    u = lax.conv_general_dilated(u, conv_w[:, None, :], window_strides=(1,), padding=[(pad, pad)],
                                 dimension_numbers=('NWC', 'WIO', 'NWC'), feature_group_count=CONV_CH) + conv_b
    u = jax.nn.silu(layernorm(u, ln_g, ln_b))
    return u @ w_o


def fourier_branch(pf, w_o):
    B, n = pf.shape[0], pf.shape[1]
    z = pf.reshape(B, n, FFT_GROUPS, FFT_CH // FFT_GROUPS).astype(F32)
    f = jnp.fft.fftn(z, axes=(1, 3), norm='ortho').real.astype(pf.dtype)
    return f.reshape(B, n, FFT_CH) @ w_o


def pool_branch(pp, pool_w, pool_scale, w_o):
    B, n = pp.shape[0], pp.shape[1]
    z = pp.reshape(B, n, len(POOL_WINDOWS), POOL_GROUP)
    P = jnp.concatenate([jnp.zeros((B, 1) + z.shape[2:], F32), jnp.cumsum(z.astype(F32), axis=1)], axis=1)
    t = jnp.arange(n)
    pooled = []
    for gi, w in enumerate(POOL_WINDOWS):
        lo = jnp.clip(t - w // 2, 0, n)
        hi = jnp.clip(t - w // 2 + w, 0, n)
        s = P[:, hi, gi] - P[:, lo, gi]
        pooled.append(s / (hi - lo).astype(F32)[:, None])
    d = jnp.stack(pooled, axis=2).astype(pp.dtype) - z
    y = jnp.einsum('bngc,gce->bnge', d, pool_w).reshape(B, n, POOL_CH) * pool_scale
    return y @ w_o


def merged_mixer(p, attn, w_o_mla, conv_w, conv_b, conv_ln_g, conv_ln_b, w_o_conv, w_o_fft,
                 pool_w, pool_scale, w_o_pool, w_out):
    o = MLA_IN
    p_conv = p[..., o:o + 2 * CONV_CH]
    o += 2 * CONV_CH
    p_fft = p[..., o:o + FFT_CH]
    o += FFT_CH
    p_pool = p[..., o:o + POOL_CH]
    o += POOL_CH
    g = jax.nn.sigmoid(p[..., o:].reshape(p.shape[:-1] + (N_BRANCHES, D_MODEL)))
    mix = (g[..., 0, :] * (attn @ w_o_mla)
           + g[..., 1, :] * conv_branch(p_conv, conv_w, conv_b, conv_ln_g, conv_ln_b, w_o_conv)
           + g[..., 2, :] * fourier_branch(p_fft, w_o_fft)
           + g[..., 3, :] * pool_branch(p_pool, pool_w, pool_scale, w_o_pool))
    return mix @ w_out


def moe_ffn(xt, w_router, router_bias, w_exp_gate, w_exp_up, w_exp_down, w_sh_gate, w_sh_up, w_sh_down):
    T = xt.shape[0]
    scores = jax.nn.sigmoid((xt @ w_router).astype(F32))
    sel = scores + router_bias.astype(F32)
    per_group = N_EXPERTS // N_EXPERT_GROUPS
    grp_score = lax.top_k(sel.reshape(T, N_EXPERT_GROUPS, per_group), 2)[0].sum(-1)
    _, gidx = lax.top_k(grp_score, TOPK_GROUPS)
    gmask = jnp.any(gidx[..., None] == jnp.arange(N_EXPERT_GROUPS), axis=1)
    emask = jnp.repeat(gmask, per_group, axis=1)
    _, eidx = lax.top_k(jnp.where(emask, sel, -jnp.inf), TOP_K)
    wts = jnp.take_along_axis(scores, eidx, axis=1)
    wts = wts / jnp.sum(wts, axis=-1, keepdims=True) * ROUTED_SCALE
    A = T * TOP_K
    e_flat = eidx.reshape(A).astype(jnp.int32)
    tok_flat = jnp.repeat(jnp.arange(T, dtype=jnp.int32), TOP_K)
    w_flat = wts.reshape(A)
    order = jnp.argsort(e_flat, stable=True)
    e_s, tok_s, w_s = e_flat[order], tok_flat[order], w_flat[order]
    counts = jnp.zeros((N_EXPERTS,), jnp.int32).at[e_flat].add(1)
    starts = jnp.cumsum(counts) - counts
    pcounts = (counts + EXPERT_BLOCK - 1) // EXPERT_BLOCK * EXPERT_BLOCK
    pends = jnp.cumsum(pcounts)
    pstarts = pends - pcounts
    dest = pstarts[e_s] + jnp.arange(A, dtype=jnp.int32) - starts[e_s]
    nb = -(-A // EXPERT_BLOCK) + N_EXPERTS
    buf_tok = jnp.zeros((nb * EXPERT_BLOCK,), jnp.int32).at[dest].set(tok_s).reshape(nb, EXPERT_BLOCK)
    buf_w = jnp.zeros((nb * EXPERT_BLOCK,), F32).at[dest].set(w_s).reshape(nb, EXPERT_BLOCK)
    blk_e = jnp.minimum(jnp.searchsorted(pends, jnp.arange(nb, dtype=jnp.int32) * EXPERT_BLOCK, side='right'),
                        N_EXPERTS - 1).astype(jnp.int32)

    def block(acc, inp):
        e, tok, wb = inp
        xb = xt[tok]
        hb = jax.nn.silu(xb @ w_exp_gate[e]) * (xb @ w_exp_up[e])
        yb = (hb @ w_exp_down[e]) * wb[:, None].astype(xt.dtype)
        return acc.at[tok].add(yb.astype(acc.dtype)), None

    routed, _ = lax.scan(block, jnp.zeros_like(xt), (blk_e, buf_tok, buf_w))
    shared = (jax.nn.silu(xt @ w_sh_gate) * (xt @ w_sh_up)) @ w_sh_down
    return routed + shared


def setup_inputs(seed: int = 0) -> dict:
    key = jax.random.key(seed)
    ks = iter(jax.random.split(key, 48))
    D = D_MODEL

    def nrm(shape, scale):
        return jax.random.normal(next(ks), shape, F32) * scale

    def gain(shape):
        return 1.0 + nrm(shape, 0.02)

    return {
        'x': nrm((BATCH, SEQ, D), 1.0),
        'c': nrm((BATCH, D), 1.0),
        'ctx': nrm((BATCH, CTX_LEN, D), 1.0),
        'c_ctx': nrm((D,), 1.0),
        'w_mod': nrm((DEPTH, D, 6 * D), 0.5 * D ** -0.5),
        'b_mod': nrm((DEPTH, 6 * D), 0.02),
        'g_norm1': gain((DEPTH, D)),
        'g_norm2': gain((DEPTH, D)),
        'w_in': nrm((DEPTH, D, IN_COLS), D ** -0.5),
        'g_cq': gain((DEPTH, Q_LORA)),
        'w_uq': nrm((DEPTH, Q_LORA, MLA_HEADS * (QK_NOPE + QK_ROPE)), Q_LORA ** -0.5),
        'g_ckv': gain((DEPTH, KV_LORA)),
        'w_ukv': nrm((DEPTH, KV_LORA, MLA_HEADS * (QK_NOPE + V_HEAD)), KV_LORA ** -0.5),
        'w_o_mla': nrm((DEPTH, MLA_HEADS * V_HEAD, D), (MLA_HEADS * V_HEAD) ** -0.5),
        'conv_w': nrm((DEPTH, CONV_WIDTH, CONV_CH), CONV_WIDTH ** -0.5),
        'conv_b': nrm((DEPTH, CONV_CH), 0.02),
        'conv_ln_g': gain((DEPTH, CONV_CH)),
        'conv_ln_b': nrm((DEPTH, CONV_CH), 0.02),
        'w_o_conv': nrm((DEPTH, CONV_CH, D), CONV_CH ** -0.5),
        'w_o_fft': nrm((DEPTH, FFT_CH, D), FFT_CH ** -0.5),
        'pool_w': nrm((DEPTH, len(POOL_WINDOWS), POOL_GROUP, POOL_GROUP), POOL_GROUP ** -0.5),
        'pool_scale': gain((DEPTH, POOL_CH)),
        'w_o_pool': nrm((DEPTH, POOL_CH, D), POOL_CH ** -0.5),
        'w_out': nrm((DEPTH, D, D), D ** -0.5),
        'w_router': nrm((DEPTH, D, N_EXPERTS), D ** -0.5),
        'router_bias': nrm((DEPTH, N_EXPERTS), 0.01),
        'w_exp_gate': nrm((DEPTH, N_EXPERTS, D, EXPERT_FF), D ** -0.5),
        'w_exp_up': nrm((DEPTH, N_EXPERTS, D, EXPERT_FF), D ** -0.5),
        'w_exp_down': nrm((DEPTH, N_EXPERTS, EXPERT_FF, D), EXPERT_FF ** -0.5),
        'w_sh_gate': nrm((DEPTH, D, SHARED_FF), D ** -0.5),
        'w_sh_up': nrm((DEPTH, D, SHARED_FF), D ** -0.5),
        'w_sh_down': nrm((DEPTH, SHARED_FF, D), SHARED_FF ** -0.5),
        'g_final': gain((D,)),
    }


def reference(x, c, ctx, c_ctx, w_mod, b_mod, g_norm1, g_norm2, w_in, g_cq, w_uq, g_ckv, w_ukv, w_o_mla,
              conv_w, conv_b, conv_ln_g, conv_ln_b, w_o_conv, w_o_fft, pool_w, pool_scale, w_o_pool, w_out,
              w_router, router_bias, w_exp_gate, w_exp_up, w_exp_down, w_sh_gate, w_sh_up, w_sh_down, g_final):
    B, L, D = x.shape
    Lc = ctx.shape[1]
    ROWS = L // GRID_W
    cos, sin = axial_rope(ROWS)
    xc = ctx
    for l in range(DEPTH):
        last = l == DEPTH - 1
        mod = modulation(c, w_mod[l], b_mod[l])
        mod_c = modulation(c_ctx[None, :], w_mod[l], b_mod[l])
        h = modulate(rmsnorm(x, g_norm1[l]), mod[:, 0], mod[:, 1])
        hc = modulate(rmsnorm(xc, g_norm1[l]), mod_c[:, 0], mod_c[:, 1])
        p = h @ w_in[l]
        pc = hc @ w_in[l][:, :MLA_IN] if last else hc @ w_in[l]
        q, k, v = mla_project(p, g_cq[l], w_uq[l], g_ckv[l], w_ukv[l], cos, sin)
        qc, kc, vc = mla_project(pc, g_cq[l], w_uq[l], g_ckv[l], w_ukv[l], None, None)
        attn = latent_attention(q, jnp.concatenate([k, kc], axis=1), jnp.concatenate([v, vc], axis=1))
        y = merged_mixer(p, attn, w_o_mla[l], conv_w[l], conv_b[l], conv_ln_g[l], conv_ln_b[l], w_o_conv[l],
                         w_o_fft[l], pool_w[l], pool_scale[l], w_o_pool[l], w_out[l])
        x = x + mod[:, 2][:, None, :] * y
        if not last:
            attn_c = attend(qc, kc, vc).reshape(B, Lc, MLA_HEADS * V_HEAD)
            yc = merged_mixer(pc, attn_c, w_o_mla[l], conv_w[l], conv_b[l], conv_ln_g[l], conv_ln_b[l],
                              w_o_conv[l], w_o_fft[l], pool_w[l], pool_scale[l], w_o_pool[l], w_out[l])
            xc = xc + mod_c[:, 2][:, None, :] * yc
        h2 = modulate(rmsnorm(x, g_norm2[l]), mod[:, 3], mod[:, 4]).reshape(B * L, D)
        if last:
            tokens = h2
        else:
            h2c = modulate(rmsnorm(xc, g_norm2[l]), mod_c[:, 3], mod_c[:, 4]).reshape(B * Lc, D)
            tokens = jnp.concatenate([h2, h2c], axis=0)
        f = moe_ffn(tokens, w_router[l], router_bias[l], w_exp_gate[l], w_exp_up[l], w_exp_down[l],
                    w_sh_gate[l], w_sh_up[l], w_sh_down[l])
        x = x + mod[:, 5][:, None, :] * f[:B * L].reshape(B, L, D)
        if not last:
            xc = xc + mod_c[:, 5][:, None, :] * f[B * L:].reshape(B, Lc, D)
    return rmsnorm(x, g_final)
```

```python
import functools

import jax
import jax.numpy as jnp
import numpy as np
from jax import lax
from jax.experimental import pallas as pl
from jax.experimental.pallas import tpu as pltpu

F32 = jnp.float32
BF16 = jnp.bfloat16
I32 = jnp.int32

D_MODEL = 2048
BATCH = 4
SEQ = 4096
DEPTH = 2
GRID_W = 64
CTX_LEN = 256
MLA_HEADS = 8
QK_NOPE = 64
QK_ROPE = 32
V_HEAD = 64
Q_LORA = 512
KV_LORA = 256
ROPE_BASE = 10000.0
CONV_CH = 512
CONV_WIDTH = 31
FFT_CH = 512
FFT_GROUPS = 4
POOL_CH = 512
POOL_WINDOWS = (2, 4, 8, 16)
POOL_GROUP = POOL_CH // len(POOL_WINDOWS)
N_BRANCHES = 4
N_EXPERTS = 64
N_EXPERT_GROUPS = 8
TOPK_GROUPS = 4
TOP_K = 8
EXPERT_FF = 512
SHARED_FF = 512
ROUTED_SCALE = 2.5
EPS = 1e-6

MLA_IN = Q_LORA + KV_LORA + QK_ROPE
T_LAT = BATCH * SEQ
T_CTX = BATCH * CTX_LEN
T_ALL = T_LAT + T_CTX

LANES = 128
SUBLANES = 8
VMEM_LIMIT_BYTES = 56 * 1024 * 1024

P_GATE = 0
P_MLA = N_BRANCHES * D_MODEL
P_MLA_W = 1024
P_CONV = P_MLA + P_MLA_W
P_FFT = P_CONV + 2 * CONV_CH
P_POOL = P_FFT + FFT_CH
P_COLS = P_POOL + POOL_CH
HEAD_PAD = 128
ATT_SCALE = (QK_NOPE + QK_ROPE) ** -0.5

ROW_TILE = 1024
EXPERT_ROWS = 256
N_BLOCKS = -(-(T_ALL * TOP_K) // EXPERT_ROWS) + N_EXPERTS


def _cparams(sem, vmem=VMEM_LIMIT_BYTES):
    return pltpu.CompilerParams(dimension_semantics=sem, vmem_limit_bytes=vmem)


def _sigmoid(x):
    return 1.0 / (1.0 + jnp.exp(-x))


def _silu(x):
    return x * _sigmoid(x)


def _mod_row(tile, tiles_per_seq):
    return jnp.minimum(tile // tiles_per_seq, BATCH)


def _modulation_kernel(c_ref, w_ref, b_ref, o_ref):
    c = c_ref[...]
    s = _silu(c).astype(BF16)
    o_ref[0] = jnp.dot(s, w_ref[0].astype(BF16), preferred_element_type=F32) + b_ref[0]


def modulation_all(cvec, w_mod, b_mod):
    tn = 1024
    n = 6 * D_MODEL
    return pl.pallas_call(
        _modulation_kernel,
        out_shape=jax.ShapeDtypeStruct((DEPTH, SUBLANES, n), F32),
        grid=(DEPTH, n // tn),
        in_specs=[
            pl.BlockSpec((SUBLANES, D_MODEL), lambda l, j: (0, 0)),
            pl.BlockSpec((1, D_MODEL, tn), lambda l, j: (l, 0, j)),
            pl.BlockSpec((1, 1, tn), lambda l, j: (l, 0, j)),
        ],
        out_specs=pl.BlockSpec((1, SUBLANES, tn), lambda l, j: (l, 0, j)),
        compiler_params=_cparams(("arbitrary", "arbitrary")),
        name="modulation",
    )(cvec, w_mod, b_mod.reshape(DEPTH, 1, n))


def _adaln(x, g, shift, scale):
    y = x * lax.rsqrt(jnp.mean(x * x, axis=-1, keepdims=True) + EPS)
    return (y * g) * (1.0 + scale) + shift


def _normproj_kernel(x_ref, g_ref, mod_ref, w_ref, o_ref, h_ref, *, shift_row, chunk):
    @pl.when(pl.program_id(1) == 0)
    def _():
        shift = mod_ref[0, shift_row:shift_row + 1, :]
        scale = mod_ref[0, shift_row + 1:shift_row + 2, :]
        for r in range(0, x_ref.shape[0], chunk):
            h = _adaln(x_ref[r:r + chunk, :], g_ref[...], shift, scale)
            h_ref[r:r + chunk, :] = h.astype(BF16)

    o_ref[...] = jnp.dot(h_ref[...], w_ref[...], preferred_element_type=F32).astype(o_ref.dtype)


def normproj(x, g, mod, w, *, tile0, ntiles, col0, ncols, shift_row=0):
    tm, tn = ROW_TILE, 1024
    tps = SEQ // tm
    cb0 = col0 // tn
    return pl.pallas_call(
        functools.partial(_normproj_kernel, shift_row=shift_row, chunk=256),
        out_shape=jax.ShapeDtypeStruct((ntiles * tm, ncols), BF16),
        grid=(ntiles, ncols // tn),
        in_specs=[
            pl.BlockSpec((tm, D_MODEL), lambda i, j: (i + tile0, 0)),
            pl.BlockSpec((1, D_MODEL), lambda i, j: (0, 0)),
            pl.BlockSpec((1, 6, D_MODEL), lambda i, j: (_mod_row(i + tile0, tps), 0, 0)),
            pl.BlockSpec((D_MODEL, tn), lambda i, j: (0, j + cb0)),
        ],
        out_specs=pl.BlockSpec((tm, tn), lambda i, j: (i, j)),
        scratch_shapes=[pltpu.VMEM((tm, D_MODEL), BF16)],
        compiler_params=_cparams(("arbitrary", "arbitrary")),
        name="normproj",
    )(x, g.reshape(1, D_MODEL), mod, w)


def _rms_gain(x, g):
    y = x * lax.rsqrt(jnp.mean(x * x, axis=-1, keepdims=True) + EPS)
    return (y * g).astype(BF16)


def _mla_prep_kernel(p_ref, gq_ref, gkv_ref, wq_ref, wk_ref, wv_ref, pk_ref, cos_ref, sin_ref,
                     q_ref, k_ref, v_ref):
    hw = MLA_HEADS * HEAD_PAD
    cqn = _rms_gain(p_ref[:, :Q_LORA].astype(F32), gq_ref[...])
    ckvn = _rms_gain(p_ref[:, Q_LORA:Q_LORA + KV_LORA].astype(F32), gkv_ref[...])
    kr = p_ref[:, Q_LORA + KV_LORA:Q_LORA + KV_LORA + LANES]
    q2 = jnp.dot(cqn, wq_ref[...], preferred_element_type=F32)
    kk = jnp.dot(ckvn, wk_ref[...], preferred_element_type=F32)
    kr2 = jnp.dot(kr, pk_ref[...], preferred_element_type=F32)
    v_ref[...] = jnp.dot(ckvn, wv_ref[...], preferred_element_type=F32).astype(BF16)
    cos = cos_ref[...]
    sin = sin_ref[...]
    for h in range(MLA_HEADS):
        a = slice(h * HEAD_PAD, (h + 1) * HEAD_PAD)
        b = slice(hw + h * HEAD_PAD, hw + (h + 1) * HEAD_PAD)
        q_ref[0, h] = ((q2[:, a] * cos + q2[:, b] * sin) * ATT_SCALE).astype(BF16)
        k_ref[0, h] = (kk[:, a] + kr2[:, a] * cos + kr2[:, b] * sin).astype(BF16)


def mla_prep(p, wts, cos_t, sin_t, *, row0, col0, nseq, n):
    tm = 256
    tps = n // tm
    hw = MLA_HEADS * HEAD_PAD
    blk0 = row0 // tm
    cblk = col0 // P_MLA_W
    qk_shape = jax.ShapeDtypeStruct((nseq, MLA_HEADS, n, HEAD_PAD), BF16)
    qk_spec = pl.BlockSpec((1, MLA_HEADS, tm, HEAD_PAD), lambda i: (i // tps, 0, i % tps, 0))
    const = lambda i: (0, 0)
    return pl.pallas_call(
        _mla_prep_kernel,
        out_shape=(qk_shape, qk_shape, jax.ShapeDtypeStruct((nseq * n, MLA_HEADS * V_HEAD), BF16)),
        grid=(nseq * tps,),
        in_specs=[
            pl.BlockSpec((tm, P_MLA_W), lambda i: (i + blk0, cblk)),
            pl.BlockSpec((1, Q_LORA), const),
            pl.BlockSpec((1, KV_LORA), const),
            pl.BlockSpec((Q_LORA, 2 * hw), const),
            pl.BlockSpec((KV_LORA, hw), const),
            pl.BlockSpec((KV_LORA, MLA_HEADS * V_HEAD), const),
            pl.BlockSpec((LANES, 2 * hw), const),
            pl.BlockSpec((tm, HEAD_PAD), lambda i: (i % tps, 0)),
            pl.BlockSpec((tm, HEAD_PAD), lambda i: (i % tps, 0)),
        ],
        out_specs=(qk_spec, qk_spec,
                   pl.BlockSpec((tm, MLA_HEADS * V_HEAD), lambda i: (i, 0))),
        compiler_params=_cparams(("arbitrary",)),
        name="mla_prep",
    )(p, wts["g_cq"], wts["g_ckv"], wts["wq2"], wts["wk"], wts["wv"], wts["pk"], cos_t, sin_t)


def _attn_kernel(q_ref, k_ref, v_ref, o_ref, *, nk, tk):
    tq = q_ref.shape[2]
    outs = []
    for h in range(2):
        q = q_ref[0, h]

        def body(c, carry, h=h, q=q):
            m, l, acc = carry
            r = pl.multiple_of(c * tk, tk)
            k = k_ref[0, h, pl.ds(r, tk), :]
            v = v_ref[0, pl.ds(r, tk), :]
            s = lax.dot_general(q, k, (((1,), (1,)), ((), ())), preferred_element_type=F32)
            m_new = jnp.maximum(m, jnp.max(s, axis=-1, keepdims=True))
            alpha = jnp.exp(m - m_new)
            p = jnp.exp(s - m_new)
            l = alpha * l + jnp.sum(p, axis=-1, keepdims=True)
            acc = alpha * acc + jnp.dot(p.astype(BF16), v, preferred_element_type=F32)
            return m_new, l, acc

        init = (jnp.full((tq, 1), -jnp.inf, F32), jnp.zeros((tq, 1), F32),
                jnp.zeros((tq, 2 * V_HEAD), F32))
        _, l, acc = lax.fori_loop(0, nk // tk, body, init)
        outs.append(acc / l)
    lane = lax.broadcasted_iota(I32, (tq, 2 * V_HEAD), 1)
    o_ref[...] = jnp.where(lane < V_HEAD, outs[0], outs[1]).astype(o_ref.dtype)


def attention(q4, k4, v3, *, nq, nk):
    tq = 256
    tk = 256
    nqt = nq // tq
    return pl.pallas_call(
        functools.partial(_attn_kernel, nk=nk, tk=tk),
        out_shape=jax.ShapeDtypeStruct((BATCH * nq, MLA_HEADS * V_HEAD), BF16),
        grid=(BATCH, MLA_HEADS // 2, nqt),
        in_specs=[
            pl.BlockSpec((1, 2, tq, HEAD_PAD), lambda b, hp, i: (b, hp, i, 0)),
            pl.BlockSpec((1, 2, nk, HEAD_PAD), lambda b, hp, i: (b, hp, 0, 0)),
            pl.BlockSpec((1, nk, 2 * V_HEAD), lambda b, hp, i: (b, 0, hp)),
        ],
        out_specs=pl.BlockSpec((tq, 2 * V_HEAD), lambda b, hp, i: (b * nqt + i, hp)),
        compiler_params=_cparams(("arbitrary", "arbitrary", "arbitrary")),
        name="attention",
    )(q4, k4, v3)


CONV_PAD = 16


def _conv_kernel(p_ref, w_ref, cb_ref, lg_ref, lb_ref, o_ref, u_ref, *, n, tr):
    zeros = jnp.zeros((CONV_PAD, CONV_CH), F32)
    u_ref[0:CONV_PAD, :] = zeros
    u_ref[CONV_PAD + n:2 * CONV_PAD + n, :] = zeros

    def glu(i, carry):
        r = pl.multiple_of(i * tr, tr)
        a = p_ref[pl.ds(r, tr), 0:CONV_CH].astype(F32)
        b = p_ref[pl.ds(r, tr), CONV_CH:2 * CONV_CH].astype(F32)
        u_ref[pl.ds(CONV_PAD + r, tr), :] = a * _sigmoid(b)
        return carry

    lax.fori_loop(0, n // tr, glu, 0)
    off = CONV_PAD - CONV_WIDTH // 2
    win_rows = tr + 2 * CONV_PAD

    def conv(i, carry):
        r = pl.multiple_of(i * tr, tr)
        strips = []
        for c0 in range(0, CONV_CH, LANES):
            win = u_ref[pl.ds(r, win_rows), c0:c0 + LANES]
            acc = jnp.zeros((tr, LANES), F32)
            for b in range(SUBLANES):
                wb = pltpu.roll(win, win_rows - (off + b), axis=0)
                for k in range(b, CONV_WIDTH, SUBLANES):
                    acc = acc + wb[k - b:k - b + tr, :] * w_ref[k:k + 1, c0:c0 + LANES]
            strips.append(acc)
        acc = jnp.concatenate(strips, axis=-1) + cb_ref[...]
        mu = jnp.mean(acc, axis=-1, keepdims=True)
        d = acc - mu
        var = jnp.mean(d * d, axis=-1, keepdims=True)
        y = (d * lax.rsqrt(var + EPS)) * lg_ref[...] + lb_ref[...]
        o_ref[pl.ds(r, tr), :] = _silu(y).astype(o_ref.dtype)
        return carry

    lax.fori_loop(0, n // tr, conv, 0)


def conv_branch(p, conv_w, conv_b, ln_g, ln_b, *, row0, nseq, n):
    tr = 128
    blk0 = row0 // n
    const = lambda s: (0, 0)
    return pl.pallas_call(
        functools.partial(_conv_kernel, n=n, tr=tr),
        out_shape=jax.ShapeDtypeStruct((nseq * n, CONV_CH), BF16),
        grid=(nseq,),
        in_specs=[
            pl.BlockSpec((n, 2 * CONV_CH), lambda s: (s + blk0, P_CONV // (2 * CONV_CH))),
            pl.BlockSpec((CONV_WIDTH, CONV_CH), const),
            pl.BlockSpec((1, CONV_CH), const),
            pl.BlockSpec((1, CONV_CH), const),
            pl.BlockSpec((1, CONV_CH), const),
        ],
        out_specs=pl.BlockSpec((n, CONV_CH), lambda s: (s, 0)),
        scratch_shapes=[pltpu.VMEM((n + 2 * CONV_PAD, CONV_CH), F32)],
        compiler_params=_cparams(("arbitrary",)),
        name="conv_branch",
    )(p, conv_w, conv_b.reshape(1, -1), ln_g.reshape(1, -1), ln_b.reshape(1, -1))


def _fft_kernel(z_ref, cs_ref, m_ref, o_ref, ab_ref, *, n, tc):
    gw = FFT_CH // FFT_GROUPS

    @pl.when(pl.program_id(1) == 0)
    def _():
        def chan(i, carry):
            r = pl.multiple_of(i * tc, tc)
            for g in range(FFT_GROUPS):
                zg = z_ref[pl.ds(r, tc), g * gw:(g + 1) * gw]
                ab = jnp.dot(zg, cs_ref[...], preferred_element_type=F32)
                ab_ref[pl.ds(r, tc), g * gw:(g + 1) * gw] = ab[:, :gw].astype(BF16)
                ab_ref[pl.ds(n + r, tc), g * gw:(g + 1) * gw] = ab[:, gw:].astype(BF16)
            return carry

        lax.fori_loop(0, n // tc, chan, 0)

    norm = 1.0 / float(np.sqrt(n * gw))
    o_ref[...] = (jnp.dot(m_ref[...], ab_ref[...], preferred_element_type=F32) * norm).astype(o_ref.dtype)


def fft_branch(p, cs, mseq, *, row0, nseq, n):
    tm = min(n, 512)
    tc = min(n, 512)
    blk0 = row0 // n
    nt = n // tm
    return pl.pallas_call(
        functools.partial(_fft_kernel, n=n, tc=tc),
        out_shape=jax.ShapeDtypeStruct((nseq * n, FFT_CH), BF16),
        grid=(nseq, nt),
        in_specs=[
            pl.BlockSpec((n, FFT_CH), lambda s, i: (s + blk0, P_FFT // FFT_CH)),
            pl.BlockSpec((FFT_CH // FFT_GROUPS, 2 * FFT_CH // FFT_GROUPS), lambda s, i: (0, 0)),
            pl.BlockSpec((tm, 2 * n), lambda s, i: (i, 0)),
        ],
        out_specs=pl.BlockSpec((tm, FFT_CH), lambda s, i: (s * nt + i, 0)),
        scratch_shapes=[pltpu.VMEM((2 * n, FFT_CH), BF16)],
        compiler_params=_cparams(("arbitrary", "arbitrary")),
        name="fft_branch",
    )(p, cs, mseq)


POOL_PAD = 16


def _pool_kernel(z_ref, pw_ref, ps_ref, o_ref, zp_ref, d_ref, *, n, tr, tc):
    zeros = jnp.zeros((POOL_PAD, POOL_CH), F32)
    zp_ref[0:POOL_PAD, :] = zeros
    zp_ref[POOL_PAD + n:2 * POOL_PAD + n, :] = zeros

    def fill(i, carry):
        r = pl.multiple_of(i * tc, tc)
        zp_ref[pl.ds(POOL_PAD + r, tc), :] = z_ref[pl.ds(r, tc), :].astype(F32)
        return carry

    lax.fori_loop(0, n // tc, fill, 0)

    win_rows = tr + 2 * POOL_PAD

    def pool(i, carry):
        r = pl.multiple_of(i * tr, tr)
        t = (r + lax.broadcasted_iota(I32, (tr, POOL_GROUP), 0)).astype(F32)
        for gi, w in enumerate(POOL_WINDOWS):
            cols = slice(gi * POOL_GROUP, (gi + 1) * POOL_GROUP)
            win = zp_ref[pl.ds(r, win_rows), cols]
            s = jnp.zeros((tr, POOL_GROUP), F32)
            for j in range(w):
                start = POOL_PAD - w // 2 + j
                if start % SUBLANES == 0:
                    s = s + win[start:start + tr, :]
                else:
                    s = s + pltpu.roll(win, win_rows - start, axis=0)[0:tr, :]
            lo = jnp.maximum(t - (w // 2), 0.0)
            hi = jnp.minimum(t - (w // 2) + w, float(n))
            z = win[POOL_PAD:POOL_PAD + tr, :]
            d_ref[pl.ds(r, tr), cols] = (s / (hi - lo) - z).astype(BF16)
        return carry

    lax.fori_loop(0, n // tr, pool, 0)

    def proj(i, carry):
        r = pl.multiple_of(i * tc, tc)
        for gi in range(len(POOL_WINDOWS)):
            cols = slice(gi * POOL_GROUP, (gi + 1) * POOL_GROUP)
            y = jnp.dot(d_ref[pl.ds(r, tc), cols], pw_ref[gi], preferred_element_type=F32)
            o_ref[pl.ds(r, tc), cols] = (y * ps_ref[:, cols]).astype(o_ref.dtype)
        return carry

    lax.fori_loop(0, n // tc, proj, 0)


def pool_branch(p, pool_w, pool_scale, *, row0, nseq, n):
    tr = 128
    tc = min(n, 512)
    blk0 = row0 // n
    return pl.pallas_call(
        functools.partial(_pool_kernel, n=n, tr=tr, tc=tc),
        out_shape=jax.ShapeDtypeStruct((nseq * n, POOL_CH), BF16),
        grid=(nseq,),
        in_specs=[
            pl.BlockSpec((n, POOL_CH), lambda s: (s + blk0, P_POOL // POOL_CH)),
            pl.BlockSpec((len(POOL_WINDOWS), POOL_GROUP, POOL_GROUP), lambda s: (0, 0, 0)),
            pl.BlockSpec((1, POOL_CH), lambda s: (0, 0)),
        ],
        out_specs=pl.BlockSpec((n, POOL_CH), lambda s: (s, 0)),
        scratch_shapes=[pltpu.VMEM((n + 2 * POOL_PAD, POOL_CH), F32),
                        pltpu.VMEM((n, POOL_CH), BF16)],
        compiler_params=_cparams(("arbitrary",)),
        name="pool_branch",
    )(p, pool_w, pool_scale.reshape(1, -1))


def _merge_kernel(x_ref, mod_ref, at_ref, cv_ref, ff_ref, po_ref, g_ref, wo_ref, wout_ref, o_ref):
    mix = None
    for b, br in enumerate((at_ref, cv_ref, ff_ref, po_ref)):
        y = jnp.dot(br[...], wo_ref[b], preferred_element_type=F32)
        gate = _sigmoid(g_ref[:, b * D_MODEL:(b + 1) * D_MODEL].astype(F32))
        mix = gate * y if mix is None else mix + gate * y
    out = jnp.dot(mix.astype(BF16), wout_ref[...], preferred_element_type=F32)
    o_ref[...] = x_ref[...] + mod_ref[0, 2:3, :] * out


def merge(x, mod, attn, conv, fft, pool, p, wo4, w_out, *, ntiles):
    tm = 256
    tps = SEQ // tm
    br_spec = pl.BlockSpec((tm, 512), lambda i: (i, 0))
    return pl.pallas_call(
        _merge_kernel,
        out_shape=jax.ShapeDtypeStruct((ntiles * tm, D_MODEL), F32),
        grid=(ntiles,),
        in_specs=[
            pl.BlockSpec((tm, D_MODEL), lambda i: (i, 0)),
            pl.BlockSpec((1, 6, D_MODEL), lambda i: (_mod_row(i, tps), 0, 0)),
            br_spec, br_spec, br_spec, br_spec,
            pl.BlockSpec((tm, N_BRANCHES * D_MODEL), lambda i: (i, P_GATE // (N_BRANCHES * D_MODEL))),
            pl.BlockSpec((N_BRANCHES, 512, D_MODEL), lambda i: (0, 0, 0), pipeline_mode=pl.Buffered(1)),
            pl.BlockSpec((D_MODEL, D_MODEL), lambda i: (0, 0), pipeline_mode=pl.Buffered(1)),
        ],
        out_specs=pl.BlockSpec((tm, D_MODEL), lambda i: (i, 0)),
        compiler_params=_cparams(("arbitrary",)),
        name="merge",
    )(x, mod, attn, conv, fft, pool, p, wo4, w_out)


def _first_index(hit_value, cand, ids, big):
    return jnp.min(jnp.where(cand == hit_value, ids, big), axis=0, keepdims=True)


def _router_kernel(x_ref, g_ref, mod_ref, wrh_ref, wrl_ref, rb_ref, tok_ref, eidx_ref, wk_ref):
    tm = x_ref.shape[0]
    h = _adaln(x_ref[...], g_ref[...], mod_ref[0, 3:4, :], mod_ref[0, 4:5, :])
    tok_ref[...] = h
    hh = h.astype(BF16)
    hl = (h - hh.astype(F32)).astype(BF16)
    nt = (((1,), (1,)), ((), ()))
    logits = (lax.dot_general(wrh_ref[...], hh, nt, preferred_element_type=F32)
              + lax.dot_general(wrh_ref[...], hl, nt, preferred_element_type=F32)
              + lax.dot_general(wrl_ref[...], hh, nt, preferred_element_type=F32))
    scores = _sigmoid(logits)
    sel = scores + rb_ref[...]
    per = N_EXPERTS // N_EXPERT_GROUPS
    assert per == SUBLANES and N_EXPERT_GROUPS == SUBLANES and TOP_K == SUBLANES
    neg = -jnp.inf
    sub = lax.broadcasted_iota(I32, (SUBLANES, tm), 0).astype(F32)
    sg = [sel[g * per:(g + 1) * per, :] for g in range(N_EXPERT_GROUPS)]
    sc = [scores[g * per:(g + 1) * per, :] for g in range(N_EXPERT_GROUPS)]
    gsc = jnp.zeros((SUBLANES, tm), F32)
    for g in range(N_EXPERT_GROUPS):
        m1 = jnp.max(sg[g], axis=0, keepdims=True)
        i1 = _first_index(m1, sg[g], sub, float(per))
        m2 = jnp.max(jnp.where(sub == i1, neg, sg[g]), axis=0, keepdims=True)
        gsc = jnp.where(sub == float(g), m1 + m2, gsc)
    gsel = jnp.zeros((SUBLANES, tm), F32)
    for _ in range(TOPK_GROUPS):
        m = jnp.max(gsc, axis=0, keepdims=True)
        hit = sub == _first_index(m, gsc, sub, float(N_EXPERT_GROUPS))
        gsel = jnp.where(hit, 1.0, gsel)
        gsc = jnp.where(hit, neg, gsc)
    cand = []
    for g in range(N_EXPERT_GROUPS):
        allowed = jnp.max(jnp.where(sub == float(g), gsel, 0.0), axis=0, keepdims=True)
        cand.append(jnp.where(allowed > 0.0, sg[g], neg))
    eid = [sub + float(g * per) for g in range(N_EXPERT_GROUPS)]
    idxs = jnp.zeros((SUBLANES, tm), F32)
    vals = jnp.zeros((SUBLANES, tm), F32)
    for k in range(TOP_K):
        m = functools.reduce(jnp.maximum, [jnp.max(c, axis=0, keepdims=True) for c in cand])
        idx = functools.reduce(
            jnp.minimum, [_first_index(m, cand[g], eid[g], float(N_EXPERTS)) for g in range(N_EXPERT_GROUPS)])
        val = jnp.zeros((1, tm), F32)
        for g in range(N_EXPERT_GROUPS):
            hit = eid[g] == idx
            val = val + jnp.sum(jnp.where(hit, sc[g], 0.0), axis=0, keepdims=True)
            cand[g] = jnp.where(hit, neg, cand[g])
        idxs = jnp.where(sub == float(k), idx, idxs)
        vals = jnp.where(sub == float(k), val, vals)
    eidx_ref[...] = idxs.astype(I32)
    wk_ref[...] = vals / jnp.sum(vals, axis=0, keepdims=True) * ROUTED_SCALE


def norm_router(x, g, mod, wr_hi, wr_lo, rbias, *, ntiles):
    tm = 256
    tps = SEQ // tm
    t = ntiles * tm
    return pl.pallas_call(
        _router_kernel,
        out_shape=(jax.ShapeDtypeStruct((t, D_MODEL), F32),
                   jax.ShapeDtypeStruct((TOP_K, t), I32),
                   jax.ShapeDtypeStruct((TOP_K, t), F32)),
        grid=(ntiles,),
        in_specs=[
            pl.BlockSpec((tm, D_MODEL), lambda i: (i, 0)),
            pl.BlockSpec((1, D_MODEL), lambda i: (0, 0)),
            pl.BlockSpec((1, 6, D_MODEL), lambda i: (_mod_row(i, tps), 0, 0)),
            pl.BlockSpec((N_EXPERTS, D_MODEL), lambda i: (0, 0)),
            pl.BlockSpec((N_EXPERTS, D_MODEL), lambda i: (0, 0)),
            pl.BlockSpec((N_EXPERTS, 1), lambda i: (0, 0)),
        ],
        out_specs=(pl.BlockSpec((tm, D_MODEL), lambda i: (i, 0)),
                   pl.BlockSpec((TOP_K, tm), lambda i: (0, i)),
                   pl.BlockSpec((TOP_K, tm), lambda i: (0, i))),
        compiler_params=_cparams(("arbitrary",)),
        name="norm_router",
    )(x, g.reshape(1, D_MODEL), mod, wr_hi, wr_lo, rbias.reshape(N_EXPERTS, 1))


def _row_copy(src, src_row, dst, dst_row, sem):
    return pltpu.make_async_copy(src.at[src_row], dst.at[dst_row], sem)


def _dispatch_kernel(pends_ref, tok_ref, dest_hbm, xs_hbm, dest_smem, zero_ref, sem_idx, sem_z, sem_s):
    i = pl.program_id(0)
    tm = tok_ref.shape[0]
    idx_copy = pltpu.make_async_copy(dest_hbm.at[i], dest_smem, sem_idx)
    idx_copy.start()

    def pad_copy(e):
        row = pl.multiple_of(pends_ref[e] - EXPERT_ROWS, EXPERT_ROWS)
        return pltpu.make_async_copy(zero_ref, xs_hbm.at[pl.ds(row, EXPERT_ROWS)], sem_z)

    @pl.when(i == 0)
    def _():
        zero_ref[...] = jnp.zeros_like(zero_ref)

        def start(e, prev):
            @pl.when(pends_ref[e] > prev)
            def _():
                pad_copy(e).start()
            return pends_ref[e]

        lax.fori_loop(0, N_EXPERTS, start, 0)

        def wait(e, prev):
            @pl.when(pends_ref[e] > prev)
            def _():
                pad_copy(e).wait()
            return pends_ref[e]

        lax.fori_loop(0, N_EXPERTS, wait, 0)

    idx_copy.wait()

    def scatter(r, carry):
        for k in range(TOP_K):
            _row_copy(tok_ref, r, xs_hbm, dest_smem[k * tm + r], sem_s).start()
        return carry

    lax.fori_loop(0, tm, scatter, 0)

    def drain(r, carry):
        for k in range(TOP_K):
            _row_copy(tok_ref, r, xs_hbm, dest_smem[k * tm + r], sem_s).wait()
        return carry

    lax.fori_loop(0, tm, drain, 0)


def dispatch(tokens, dest_tiles, pends, *, ntiles):
    tm = tokens.shape[0] // ntiles
    return pl.pallas_call(
        _dispatch_kernel,
        out_shape=jax.ShapeDtypeStruct((N_BLOCKS * EXPERT_ROWS, D_MODEL), F32),
        grid_spec=pltpu.PrefetchScalarGridSpec(
            num_scalar_prefetch=1,
            grid=(ntiles,),
            in_specs=[
                pl.BlockSpec((tm, D_MODEL), lambda i, pends: (i, 0)),
                pl.BlockSpec(memory_space=pl.ANY),
            ],
            out_specs=pl.BlockSpec(memory_space=pl.ANY),
            scratch_shapes=[
                pltpu.SMEM((TOP_K * tm,), I32),
                pltpu.VMEM((EXPERT_ROWS, D_MODEL), F32),
                pltpu.SemaphoreType.DMA,
                pltpu.SemaphoreType.DMA,
                pltpu.SemaphoreType.DMA,
            ],
        ),
        compiler_params=_cparams(("arbitrary",)),
        name="moe_dispatch",
    )(pends, tokens, dest_tiles)


def _experts_kernel(blk_e_ref, nused_ref, xs_ref, wg_ref, wu_ref, wd_ref, y_ref, wgb, wub, wdb):
    i = pl.program_id(0)
    e = blk_e_ref[i]
    prev = blk_e_ref[jnp.maximum(i - 1, 0)]

    @pl.when(i < nused_ref[0])
    def _():
        @pl.when((i == 0) | (e != prev))
        def _():
            wgb[...] = wg_ref[0].astype(BF16)
            wub[...] = wu_ref[0].astype(BF16)
            wdb[...] = wd_ref[0].astype(BF16)

        x = xs_ref[...].astype(BF16)
        g = jnp.dot(x, wgb[...], preferred_element_type=F32)
        u = jnp.dot(x, wub[...], preferred_element_type=F32)
        hb = (_silu(g) * u).astype(BF16)
        y_ref[...] = jnp.dot(hb, wdb[...], preferred_element_type=F32)


def experts(xs, blk_e, nused, w_gate, w_up, w_down):
    def row_map(i, blk_e, nused):
        return (jnp.minimum(i, nused[0] - 1), 0)

    def w_map(i, blk_e, nused):
        return (blk_e[jnp.minimum(i, nused[0] - 1)], 0, 0)

    return pl.pallas_call(
        _experts_kernel,
        out_shape=jax.ShapeDtypeStruct((N_BLOCKS * EXPERT_ROWS, D_MODEL), F32),
        grid_spec=pltpu.PrefetchScalarGridSpec(
            num_scalar_prefetch=2,
            grid=(N_BLOCKS,),
            in_specs=[
                pl.BlockSpec((EXPERT_ROWS, D_MODEL), row_map),
                pl.BlockSpec((1, D_MODEL, EXPERT_FF), w_map),
                pl.BlockSpec((1, D_MODEL, EXPERT_FF), w_map),
                pl.BlockSpec((1, EXPERT_FF, D_MODEL), w_map),
            ],
            out_specs=pl.BlockSpec((EXPERT_ROWS, D_MODEL), row_map),
            scratch_shapes=[
                pltpu.VMEM((D_MODEL, EXPERT_FF), BF16),
                pltpu.VMEM((D_MODEL, EXPERT_FF), BF16),
                pltpu.VMEM((EXPERT_FF, D_MODEL), BF16),
            ],
        ),
        compiler_params=_cparams(("arbitrary",)),
        name="moe_experts",
    )(blk_e, nused, xs, w_gate, w_up, w_down)


def _shared_kernel(t_ref, wg_ref, wu_ref, wd_ref, o_ref):
    x = t_ref[...].astype(BF16)
    g = jnp.dot(x, wg_ref[...], preferred_element_type=F32)
    u = jnp.dot(x, wu_ref[...], preferred_element_type=F32)
    o_ref[...] = jnp.dot((_silu(g) * u).astype(BF16), wd_ref[...], preferred_element_type=F32)


def shared_expert(tokens, wg, wu, wd):
    tm = 512
    t = tokens.shape[0]
    return pl.pallas_call(
        _shared_kernel,
        out_shape=jax.ShapeDtypeStruct((t, D_MODEL), F32),
        grid=(t // tm,),
        in_specs=[
            pl.BlockSpec((tm, D_MODEL), lambda i: (i, 0)),
            pl.BlockSpec((D_MODEL, SHARED_FF), lambda i: (0, 0)),
            pl.BlockSpec((D_MODEL, SHARED_FF), lambda i: (0, 0)),
            pl.BlockSpec((SHARED_FF, D_MODEL), lambda i: (0, 0)),
        ],
        out_specs=pl.BlockSpec((tm, D_MODEL), lambda i: (i, 0)),
        compiler_params=_cparams(("arbitrary",)),
        name="shared_expert",
    )(tokens, wg, wu, wd)


def _combine_kernel(x_ref, sh_ref, mod_ref, wk_ref, gf_ref, dest_hbm, ys_hbm, o_ref,
                    dest_smem, buf_ref, sem_idx, sem_g, *, final):
    i = pl.program_id(0)
    tm = x_ref.shape[0]
    idx_copy = pltpu.make_async_copy(dest_hbm.at[i], dest_smem, sem_idx)
    idx_copy.start()
    idx_copy.wait()

    def gather(r, carry):
        for k in range(TOP_K):
            _row_copy(ys_hbm, dest_smem[k * tm + r], buf_ref.at[k], r, sem_g).start()
        return carry

    lax.fori_loop(0, tm, gather, 0)

    def drain(r, carry):
        for k in range(TOP_K):
            _row_copy(ys_hbm, dest_smem[k * tm + r], buf_ref.at[k], r, sem_g).wait()
        return carry

    lax.fori_loop(0, tm, drain, 0)
    f = sh_ref[...]
    for k in range(TOP_K):
        f = f + buf_ref[k] * wk_ref[:, k:k + 1]
    out = x_ref[...] + mod_ref[0, 5:6, :] * f
    if final:
        y = out * lax.rsqrt(jnp.mean(out * out, axis=-1, keepdims=True) + EPS)
        out = y * gf_ref[...]
    o_ref[...] = out


def combine(x, shared, mod, wk_t, g_final, dest_tiles, ys, *, ntiles, final):
    tm = 128
    tps = SEQ // tm
    return pl.pallas_call(
        functools.partial(_combine_kernel, final=final),
        out_shape=jax.ShapeDtypeStruct((ntiles * tm, D_MODEL), F32),
        grid=(ntiles,),
        in_specs=[
            pl.BlockSpec((tm, D_MODEL), lambda i: (i, 0)),
            pl.BlockSpec((tm, D_MODEL), lambda i: (i, 0)),
            pl.BlockSpec((1, 6, D_MODEL), lambda i: (_mod_row(i, tps), 0, 0)),
            pl.BlockSpec((tm, TOP_K), lambda i: (i, 0)),
            pl.BlockSpec((1, D_MODEL), lambda i: (0, 0)),
            pl.BlockSpec(memory_space=pl.ANY),
            pl.BlockSpec(memory_space=pl.ANY),
        ],
        out_specs=pl.BlockSpec((tm, D_MODEL), lambda i: (i, 0)),
        scratch_shapes=[
            pltpu.SMEM((TOP_K * tm,), I32),
            pltpu.VMEM((TOP_K, tm, D_MODEL), F32),
            pltpu.SemaphoreType.DMA,
            pltpu.SemaphoreType.DMA,
        ],
        compiler_params=_cparams(("arbitrary",)),
        name="moe_combine",
    )(x, shared, mod, wk_t, g_final.reshape(1, D_MODEL), dest_tiles, ys)


def _routing_plan(eidx):
    t = eidx.shape[1]
    onehot = (eidx[None, :, :] == jnp.arange(N_EXPERTS, dtype=I32)[:, None, None])
    mask = jnp.any(onehot, axis=1).astype(I32)
    incl = jnp.cumsum(mask, axis=1)
    counts = incl[:, -1]
    rank = incl - mask
    pcounts = (counts + EXPERT_ROWS - 1) // EXPERT_ROWS * EXPERT_ROWS
    pends = jnp.cumsum(pcounts)
    pstarts = pends - pcounts
    slot = pstarts[:, None] + rank
    dest = jnp.sum(jnp.where(onehot, slot[:, None, :], 0), axis=0)
    nused = (pends[-1] // EXPERT_ROWS).astype(I32).reshape(1)
    first_row = jnp.arange(N_BLOCKS, dtype=I32) * EXPERT_ROWS
    blk_e = jnp.minimum(jnp.sum((pends[None, :] <= first_row[:, None]).astype(I32), axis=1),
                        N_EXPERTS - 1).astype(I32)
    return dest.astype(I32), pends.astype(I32), blk_e, nused


def _tile_major(dest, tm):
    k, t = dest.shape
    return dest.reshape(k, t // tm, tm).transpose(1, 0, 2).reshape(t // tm, k * tm)


def _rope_tables(n_rows):
    row = jnp.repeat(jnp.arange(n_rows, dtype=F32), GRID_W)
    col = jnp.tile(jnp.arange(GRID_W, dtype=F32), n_rows)
    half = QK_ROPE // 2
    inv = ROPE_BASE ** (-jnp.arange(0, half, 2, dtype=F32) / half)
    ang_r = row[:, None] * inv
    ang_c = col[:, None] * inv
    ang = jnp.concatenate([ang_r, ang_r, ang_c, ang_c], axis=-1)
    n = ang.shape[0]
    ones = jnp.ones((n, QK_NOPE), F32)
    zeros = jnp.zeros((n, QK_NOPE), F32)
    tail = jnp.zeros((n, HEAD_PAD - QK_NOPE - QK_ROPE), F32)
    cos = jnp.concatenate([ones, jnp.cos(ang), tail], axis=-1)
    sin = jnp.concatenate([zeros, jnp.sin(ang), tail], axis=-1)
    return cos, sin


def _identity_tables(n):
    cos = jnp.concatenate([jnp.ones((n, QK_NOPE + QK_ROPE), F32),
                           jnp.zeros((n, HEAD_PAD - QK_NOPE - QK_ROPE), F32)], axis=-1)
    return cos, jnp.zeros((n, HEAD_PAD), F32)


def _rotate_cols(w):
    i = np.arange(QK_ROPE)
    first = (i % (QK_ROPE // 2)) < (QK_ROPE // 4)
    perm = np.where(first, i + QK_ROPE // 4, i - QK_ROPE // 4)
    sign = np.where(first, -1.0, 1.0).astype(np.float32)
    return w[:, perm] * sign


def _mla_weights(g_cq, w_uq, g_ckv, w_ukv):
    dk = QK_NOPE + QK_ROPE
    wq = w_uq.reshape(Q_LORA, MLA_HEADS, dk)
    pad = jnp.zeros((Q_LORA, MLA_HEADS, HEAD_PAD - dk), F32)
    zero_nope = jnp.zeros((Q_LORA, MLA_HEADS, QK_NOPE), F32)
    wq_rot = _rotate_cols(wq[..., QK_NOPE:].reshape(Q_LORA * MLA_HEADS, QK_ROPE)).reshape(
        Q_LORA, MLA_HEADS, QK_ROPE)
    wq_a = jnp.concatenate([wq, pad], axis=-1).reshape(Q_LORA, -1)
    wq_b = jnp.concatenate([zero_nope, wq_rot, pad], axis=-1).reshape(Q_LORA, -1)
    wkv = w_ukv.reshape(KV_LORA, MLA_HEADS, QK_NOPE + V_HEAD)
    wk = jnp.concatenate([wkv[..., :QK_NOPE],
                          jnp.zeros((KV_LORA, MLA_HEADS, HEAD_PAD - QK_NOPE), F32)], axis=-1)
    wv = wkv[..., QK_NOPE:]
    eye = jnp.eye(QK_ROPE, dtype=F32)
    place = jnp.zeros((LANES, MLA_HEADS, HEAD_PAD), F32)
    place_a = place.at[:QK_ROPE, :, QK_NOPE:dk].set(jnp.broadcast_to(eye[:, None, :], (QK_ROPE, MLA_HEADS, QK_ROPE)))
    rot = _rotate_cols(eye)
    place_b = place.at[:QK_ROPE, :, QK_NOPE:dk].set(jnp.broadcast_to(rot[:, None, :], (QK_ROPE, MLA_HEADS, QK_ROPE)))
    return {
        "g_cq": g_cq.reshape(1, -1), "g_ckv": g_ckv.reshape(1, -1),
        "wq2": jnp.concatenate([wq_a, wq_b], axis=-1).astype(BF16),
        "wk": wk.reshape(KV_LORA, -1).astype(BF16),
        "wv": wv.reshape(KV_LORA, -1).astype(BF16),
        "pk": jnp.concatenate([place_a.reshape(LANES, -1), place_b.reshape(LANES, -1)], axis=-1).astype(BF16),
    }


def _pack_w_in(w):
    gates0 = MLA_IN + 2 * CONV_CH + FFT_CH + POOL_CH
    pad = jnp.zeros((D_MODEL, P_MLA_W - MLA_IN), w.dtype)
    return jnp.concatenate([w[:, gates0:], w[:, :MLA_IN], pad, w[:, MLA_IN:gates0]], axis=-1).astype(BF16)


def _dft_tables(n):
    gw = FFT_CH // FFT_GROUPS

    def angles(m):
        j = jnp.arange(m, dtype=I32)
        return (j[:, None] * j[None, :] % m).astype(F32) * (2.0 * np.pi / m)

    ac = angles(gw)
    an = angles(n)
    cs = jnp.concatenate([jnp.cos(ac), jnp.sin(ac)], axis=-1).astype(BF16)
    mseq = jnp.concatenate([jnp.cos(an), -jnp.sin(an)], axis=-1).astype(BF16)
    return cs, mseq


def _mixers(p, row0, nseq, n, wts, tables, conv_args, pool_args, dft):
    cos_t, sin_t = tables
    q4, k4, v = mla_prep(p, wts, cos_t, sin_t, row0=row0, col0=P_MLA, nseq=nseq, n=n)
    cv = conv_branch(p, *conv_args, row0=row0, nseq=nseq, n=n)
    ff = fft_branch(p, dft[0], dft[1], row0=row0, nseq=nseq, n=n)
    po = pool_branch(p, *pool_args, row0=row0, nseq=nseq, n=n)
    return q4, k4, v, cv, ff, po


def kernel(x, c, ctx, c_ctx, w_mod, b_mod, g_norm1, g_norm2, w_in, g_cq, w_uq, g_ckv, w_ukv, w_o_mla, conv_w, conv_b, conv_ln_g, conv_ln_b, w_o_conv, w_o_fft, pool_w, pool_scale, w_o_pool, w_out, w_router, router_bias, w_exp_gate, w_exp_up, w_exp_down, w_sh_gate, w_sh_up, w_sh_down, g_final):
    xa = jnp.concatenate([x.reshape(T_LAT, D_MODEL), ctx.reshape(T_CTX, D_MODEL)], axis=0)
    cvec = jnp.concatenate([c, c_ctx[None, :], jnp.zeros((SUBLANES - BATCH - 1, D_MODEL), F32)], axis=0)
    mod_all = modulation_all(cvec, w_mod, b_mod).reshape(DEPTH, SUBLANES, 6, D_MODEL)
    rope_lat = _rope_tables(SEQ // GRID_W)
    rope_ctx = _identity_tables(CTX_LEN)
    dft_lat = _dft_tables(SEQ)
    dft_ctx = _dft_tables(CTX_LEN)
    lat_tiles = T_LAT // ROW_TILE
    all_tiles = T_ALL // ROW_TILE

    for l in range(DEPTH):
        last = l == DEPTH - 1
        mod = mod_all[l]
        w_in_p = _pack_w_in(w_in[l])
        wts = _mla_weights(g_cq[l], w_uq[l], g_ckv[l], w_ukv[l])
        conv_args = (conv_w[l], conv_b[l], conv_ln_g[l], conv_ln_b[l])
        pool_args = (pool_w[l].astype(BF16), pool_scale[l])
        wo4 = jnp.stack([w_o_mla[l], w_o_conv[l], w_o_fft[l], w_o_pool[l]], axis=0).astype(BF16)

        if last:
            p = normproj(xa, g_norm1[l], mod, w_in_p, tile0=0, ntiles=lat_tiles, col0=0, ncols=P_COLS)
            p_ctx = normproj(xa, g_norm1[l], mod, w_in_p, tile0=lat_tiles, ntiles=all_tiles - lat_tiles,
                             col0=P_MLA, ncols=P_MLA_W)
        else:
            p = normproj(xa, g_norm1[l], mod, w_in_p, tile0=0, ntiles=all_tiles, col0=0, ncols=P_COLS)
        q4, k4, v, cv, ff, po = _mixers(p, 0, BATCH, SEQ, wts, rope_lat, conv_args, pool_args, dft_lat)
        if last:
            qc, kc, vc = mla_prep(p_ctx, wts, *rope_ctx, row0=0, col0=0, nseq=BATCH, n=CTX_LEN)
        else:
            qc, kc, vc, cvc, ffc, poc = _mixers(p, T_LAT, BATCH, CTX_LEN, wts, rope_ctx,
                                                conv_args, pool_args, dft_ctx)
        k_all = jnp.concatenate([k4, kc], axis=2)
        v_all = jnp.concatenate([v.reshape(BATCH, SEQ, -1), vc.reshape(BATCH, CTX_LEN, -1)], axis=1)
        at = attention(q4, k_all, v_all, nq=SEQ, nk=SEQ + CTX_LEN)
        if last:
            ntok = T_LAT
        else:
            atc = attention(qc, kc, vc.reshape(BATCH, CTX_LEN, -1), nq=CTX_LEN, nk=CTX_LEN)
            at = jnp.concatenate([at, atc], axis=0)
            cv = jnp.concatenate([cv, cvc], axis=0)
            ff = jnp.concatenate([ff, ffc], axis=0)
            po = jnp.concatenate([po, poc], axis=0)
            ntok = T_ALL
        xa = merge(xa, mod, at, cv, ff, po, p, wo4, w_out[l].astype(BF16), ntiles=ntok // 256)

        wr_t = w_router[l].T
        wr_hi = wr_t.astype(BF16)
        wr_lo = (wr_t - wr_hi.astype(F32)).astype(BF16)
        tokens, eidx, wk = norm_router(xa, g_norm2[l], mod, wr_hi, wr_lo, router_bias[l], ntiles=ntok // 256)
        dest, pends, blk_e, nused = _routing_plan(eidx)
        xs = dispatch(tokens, _tile_major(dest, 256), pends, ntiles=ntok // 256)
        ys = experts(xs, blk_e, nused, w_exp_gate[l], w_exp_up[l], w_exp_down[l])
        sh = shared_expert(tokens, w_sh_gate[l].astype(BF16), w_sh_up[l].astype(BF16),
                           w_sh_down[l].astype(BF16))
        xa = combine(xa, sh, mod, wk.T, g_final, _tile_major(dest, 128), ys, ntiles=ntok // 128, final=last)
    return xa.reshape(BATCH, SEQ, D_MODEL)
```

```python
import functools

import jax
import jax.numpy as jnp
import numpy as np
from jax import lax
from jax.experimental import pallas as pl
from jax.experimental.pallas import tpu as pltpu

F32 = jnp.float32
BF16 = jnp.bfloat16
I32 = jnp.int32

D_MODEL = 2048
BATCH = 4
SEQ = 4096
DEPTH = 2
GRID_W = 64
CTX_LEN = 256
MLA_HEADS = 8
QK_NOPE = 64
QK_ROPE = 32
V_HEAD = 64
Q_LORA = 512
KV_LORA = 256
ROPE_BASE = 10000.0
CONV_CH = 512
CONV_WIDTH = 31
FFT_CH = 512
FFT_GROUPS = 4
POOL_CH = 512
POOL_WINDOWS = (2, 4, 8, 16)
POOL_GROUP = POOL_CH // len(POOL_WINDOWS)
N_BRANCHES = 4
N_EXPERTS = 64
N_EXPERT_GROUPS = 8
TOPK_GROUPS = 4
TOP_K = 8
EXPERT_FF = 512
SHARED_FF = 512
ROUTED_SCALE = 2.5
EPS = 1e-6

MLA_IN = Q_LORA + KV_LORA + QK_ROPE
T_LAT = BATCH * SEQ
T_CTX = BATCH * CTX_LEN
T_ALL = T_LAT + T_CTX

LANES = 128
SUBLANES = 8
VMEM_LIMIT_BYTES = 56 * 1024 * 1024

P_GATE = 0
P_MLA = N_BRANCHES * D_MODEL
P_MLA_W = 1024
P_CONV = P_MLA + P_MLA_W
P_FFT = P_CONV + 2 * CONV_CH
P_POOL = P_FFT + FFT_CH
P_COLS = P_POOL + POOL_CH
HEAD_PAD = 128
ATT_SCALE = (QK_NOPE + QK_ROPE) ** -0.5
Q_SCALE = ATT_SCALE * float(np.log2(np.e))

ROW_TILE = 1024
EXPERT_ROWS = 256
N_BLOCKS = -(-(T_ALL * TOP_K) // EXPERT_ROWS) + N_EXPERTS


def _cparams(sem, vmem=VMEM_LIMIT_BYTES):
    return pltpu.CompilerParams(dimension_semantics=sem, vmem_limit_bytes=vmem)


def _sigmoid(x):
    return 1.0 / (1.0 + jnp.exp(-x))


def _silu(x):
    return x * _sigmoid(x)


def _mod_row(tile, tiles_per_seq):
    return jnp.minimum(tile // tiles_per_seq, BATCH)


def _modulation_kernel(c_ref, w_ref, b_ref, o_ref):
    c = c_ref[...]
    s = _silu(c).astype(BF16)
    o_ref[0] = jnp.dot(s, w_ref[0].astype(BF16), preferred_element_type=F32) + b_ref[0]


def modulation_all(cvec, w_mod, b_mod):
    tn = 1024
    n = 6 * D_MODEL
    return pl.pallas_call(
        _modulation_kernel,
        out_shape=jax.ShapeDtypeStruct((DEPTH, SUBLANES, n), F32),
        grid=(DEPTH, n // tn),
        in_specs=[
            pl.BlockSpec((SUBLANES, D_MODEL), lambda l, j: (0, 0)),
            pl.BlockSpec((1, D_MODEL, tn), lambda l, j: (l, 0, j)),
            pl.BlockSpec((1, 1, tn), lambda l, j: (l, 0, j)),
        ],
        out_specs=pl.BlockSpec((1, SUBLANES, tn), lambda l, j: (l, 0, j)),
        compiler_params=_cparams(("arbitrary", "arbitrary")),
        name="modulation",
    )(cvec, w_mod, b_mod.reshape(DEPTH, 1, n))


def _adaln(x, g, shift, scale):
    y = x * lax.rsqrt(jnp.mean(x * x, axis=-1, keepdims=True) + EPS)
    return (y * g) * (1.0 + scale) + shift


def _normproj_kernel(x_ref, g_ref, mod_ref, w_ref, o_ref, h_ref, *, shift_row, chunk):
    @pl.when(pl.program_id(1) == 0)
    def _():
        shift = mod_ref[0, shift_row:shift_row + 1, :]
        scale = mod_ref[0, shift_row + 1:shift_row + 2, :]
        for r in range(0, x_ref.shape[0], chunk):
            h = _adaln(x_ref[r:r + chunk, :], g_ref[...], shift, scale)
            h_ref[r:r + chunk, :] = h.astype(BF16)

    o_ref[...] = jnp.dot(h_ref[...], w_ref[...], preferred_element_type=F32).astype(o_ref.dtype)


def normproj(x, g, mod, w, *, tile0, ntiles, col0, ncols, shift_row=0):
    tm, tn = ROW_TILE, 1024
    tps = SEQ // tm
    cb0 = col0 // tn
    return pl.pallas_call(
        functools.partial(_normproj_kernel, shift_row=shift_row, chunk=256),
        out_shape=jax.ShapeDtypeStruct((ntiles * tm, ncols), BF16),
        grid=(ntiles, ncols // tn),
        in_specs=[
            pl.BlockSpec((tm, D_MODEL), lambda i, j: (i + tile0, 0)),
            pl.BlockSpec((1, D_MODEL), lambda i, j: (0, 0)),
            pl.BlockSpec((1, 6, D_MODEL), lambda i, j: (_mod_row(i + tile0, tps), 0, 0)),
            pl.BlockSpec((D_MODEL, tn), lambda i, j: (0, j + cb0)),
        ],
        out_specs=pl.BlockSpec((tm, tn), lambda i, j: (i, j)),
        scratch_shapes=[pltpu.VMEM((tm, D_MODEL), BF16)],
        compiler_params=_cparams(("arbitrary", "arbitrary")),
        name="normproj",
    )(x, g.reshape(1, D_MODEL), mod, w)


def _rms_gain(x, g):
    y = x * lax.rsqrt(jnp.mean(x * x, axis=-1, keepdims=True) + EPS)
    return (y * g).astype(BF16)


_NT = (((1,), (1,)), ((), ()))


def _mla_prep_kernel(p_ref, gq_ref, gkv_ref, wqt_ref, wk_ref, wvt_ref, pk_ref, cos_ref, sin_ref,
                     cost_ref, sint_ref, qt_ref, k_ref, vt_ref):
    hw = MLA_HEADS * HEAD_PAD
    cqn = _rms_gain(p_ref[:, :Q_LORA].astype(F32), gq_ref[...])
    ckvn = _rms_gain(p_ref[:, Q_LORA:Q_LORA + KV_LORA].astype(F32), gkv_ref[...])
    kr = p_ref[:, Q_LORA + KV_LORA:Q_LORA + KV_LORA + LANES]
    q2t = lax.dot_general(wqt_ref[...], cqn, _NT, preferred_element_type=F32)
    vt_ref[0] = lax.dot_general(wvt_ref[...], ckvn, _NT, preferred_element_type=F32).astype(BF16)
    kk = jnp.dot(ckvn, wk_ref[...], preferred_element_type=F32)
    kr2 = jnp.dot(kr, pk_ref[...], preferred_element_type=F32)
    cos, sin = cos_ref[...], sin_ref[...]
    cost, sint = cost_ref[...], sint_ref[...]
    for h in range(MLA_HEADS):
        a = slice(h * HEAD_PAD, (h + 1) * HEAD_PAD)
        b = slice(hw + h * HEAD_PAD, hw + (h + 1) * HEAD_PAD)
        qt_ref[0, h] = ((q2t[a, :] * cost + q2t[b, :] * sint) * Q_SCALE).astype(BF16)
        k_ref[0, h] = (kk[:, a] + kr2[:, a] * cos + kr2[:, b] * sin).astype(BF16)


def mla_prep(p, wts, tables, *, row0, col0, nseq, n):
    tm = 256
    tps = n // tm
    hw = MLA_HEADS * HEAD_PAD
    vw = MLA_HEADS * V_HEAD
    blk0 = row0 // tm
    cblk = col0 // P_MLA_W
    cos_t, sin_t = tables
    const = lambda i: (0, 0)
    return pl.pallas_call(
        _mla_prep_kernel,
        out_shape=(jax.ShapeDtypeStruct((nseq, MLA_HEADS, HEAD_PAD, n), BF16),
                   jax.ShapeDtypeStruct((nseq, MLA_HEADS, n, HEAD_PAD), BF16),
                   jax.ShapeDtypeStruct((nseq, vw, n), BF16)),
        grid=(nseq * tps,),
        in_specs=[
            pl.BlockSpec((tm, P_MLA_W), lambda i: (i + blk0, cblk)),
            pl.BlockSpec((1, Q_LORA), const),
            pl.BlockSpec((1, KV_LORA), const),
            pl.BlockSpec((2 * hw, Q_LORA), const),
            pl.BlockSpec((KV_LORA, hw), const),
            pl.BlockSpec((vw, KV_LORA), const),
            pl.BlockSpec((LANES, 2 * hw), const),
            pl.BlockSpec((tm, HEAD_PAD), lambda i: (i % tps, 0)),
            pl.BlockSpec((tm, HEAD_PAD), lambda i: (i % tps, 0)),
            pl.BlockSpec((HEAD_PAD, tm), lambda i: (0, i % tps)),
            pl.BlockSpec((HEAD_PAD, tm), lambda i: (0, i % tps)),
        ],
        out_specs=(pl.BlockSpec((1, MLA_HEADS, HEAD_PAD, tm), lambda i: (i // tps, 0, 0, i % tps)),
                   pl.BlockSpec((1, MLA_HEADS, tm, HEAD_PAD), lambda i: (i // tps, 0, i % tps, 0)),
                   pl.BlockSpec((1, vw, tm), lambda i: (i // tps, 0, i % tps))),
        compiler_params=_cparams(("arbitrary",)),
        name="mla_prep",
    )(p, wts["g_cq"], wts["g_ckv"], wts["wq2t"], wts["wk"], wts["wvt"], wts["pk"],
      cos_t, sin_t, cos_t.T, sin_t.T)


def _attn_kernel(qt_ref, k_ref, vt_ref, o_ref, s_ref, p_ref, *, nk, ck):
    tq = qt_ref.shape[3]
    chunks = [slice(c, c + ck) for c in range(0, nk, ck)]
    outs = []
    for h in range(2):
        qt = qt_ref[0, h]
        m = None
        for c in chunks:
            s = jnp.dot(k_ref[0, h, c, :], qt, preferred_element_type=F32)
            s_ref[h, c, :] = s
            mc = jnp.max(s, axis=0, keepdims=True)
            m = mc if m is None else jnp.maximum(m, mc)
        l = jnp.zeros((1, tq), F32)
        for c in chunks:
            p = jnp.exp2(s_ref[h, c, :] - m)
            l = l + jnp.sum(p, axis=0, keepdims=True)
            p_ref[h, c, :] = p.astype(BF16)
        ot = jnp.dot(vt_ref[0, h * V_HEAD:(h + 1) * V_HEAD, :], p_ref[h], preferred_element_type=F32)
        outs.append(ot / l)
    o_ref[...] = jnp.concatenate(outs, axis=0).T.astype(o_ref.dtype)


def attention(qt4, k4, vt3, *, nq, nk):
    tq = 256
    ck = 544 if nk % 544 == 0 else nk
    nqt = nq // tq
    return pl.pallas_call(
        functools.partial(_attn_kernel, nk=nk, ck=ck),
        out_shape=jax.ShapeDtypeStruct((BATCH * nq, MLA_HEADS * V_HEAD), BF16),
        grid=(BATCH, MLA_HEADS // 2, nqt),
        in_specs=[
            pl.BlockSpec((1, 2, HEAD_PAD, tq), lambda b, hp, i: (b, hp, 0, i)),
            pl.BlockSpec((1, 2, nk, HEAD_PAD), lambda b, hp, i: (b, hp, 0, 0)),
            pl.BlockSpec((1, 2 * V_HEAD, nk), lambda b, hp, i: (b, hp, 0)),
        ],
        out_specs=pl.BlockSpec((tq, 2 * V_HEAD), lambda b, hp, i: (b * nqt + i, hp)),
        scratch_shapes=[pltpu.VMEM((2, nk, tq), F32), pltpu.VMEM((2, nk, tq), BF16)],
        compiler_params=_cparams(("arbitrary", "arbitrary", "arbitrary")),
        name="attention",
    )(qt4, k4, vt3)


CONV_PAD = 16


def _conv_kernel(p_ref, w_ref, cb_ref, lg_ref, lb_ref, o_ref, u_ref, *, n, tr):
    zeros = jnp.zeros((CONV_PAD, CONV_CH), F32)
    u_ref[0:CONV_PAD, :] = zeros
    u_ref[CONV_PAD + n:2 * CONV_PAD + n, :] = zeros

    def glu(i, carry):
        r = pl.multiple_of(i * tr, tr)
        a = p_ref[pl.ds(r, tr), 0:CONV_CH].astype(F32)
        b = p_ref[pl.ds(r, tr), CONV_CH:2 * CONV_CH].astype(F32)
        u_ref[pl.ds(CONV_PAD + r, tr), :] = a * _sigmoid(b)
        return carry

    lax.fori_loop(0, n // tr, glu, 0)
    off = CONV_PAD - CONV_WIDTH // 2
    win_rows = tr + 2 * CONV_PAD

    def conv(i, carry):
        r = pl.multiple_of(i * tr, tr)
        strips = []
        for c0 in range(0, CONV_CH, LANES):
            win = u_ref[pl.ds(r, win_rows), c0:c0 + LANES]
            acc = jnp.zeros((tr, LANES), F32)
            for b in range(SUBLANES):
                wb = pltpu.roll(win, win_rows - (off + b), axis=0)
                for k in range(b, CONV_WIDTH, SUBLANES):
                    acc = acc + wb[k - b:k - b + tr, :] * w_ref[k:k + 1, c0:c0 + LANES]
            strips.append(acc)
        acc = jnp.concatenate(strips, axis=-1) + cb_ref[...]
        mu = jnp.mean(acc, axis=-1, keepdims=True)
        d = acc - mu
        var = jnp.mean(d * d, axis=-1, keepdims=True)
        y = (d * lax.rsqrt(var + EPS)) * lg_ref[...] + lb_ref[...]
        o_ref[pl.ds(r, tr), :] = _silu(y).astype(o_ref.dtype)
        return carry

    lax.fori_loop(0, n // tr, conv, 0)


def conv_branch(p, conv_w, conv_b, ln_g, ln_b, *, row0, nseq, n):
    tr = 128
    blk0 = row0 // n
    const = lambda s: (0, 0)
    return pl.pallas_call(
        functools.partial(_conv_kernel, n=n, tr=tr),
        out_shape=jax.ShapeDtypeStruct((nseq * n, CONV_CH), BF16),
        grid=(nseq,),
        in_specs=[
            pl.BlockSpec((n, 2 * CONV_CH), lambda s: (s + blk0, P_CONV // (2 * CONV_CH))),
            pl.BlockSpec((CONV_WIDTH, CONV_CH), const),
            pl.BlockSpec((1, CONV_CH), const),
            pl.BlockSpec((1, CONV_CH), const),
            pl.BlockSpec((1, CONV_CH), const),
        ],
        out_specs=pl.BlockSpec((n, CONV_CH), lambda s: (s, 0)),
        scratch_shapes=[pltpu.VMEM((n + 2 * CONV_PAD, CONV_CH), F32)],
        compiler_params=_cparams(("arbitrary",)),
        name="conv_branch",
    )(p, conv_w, conv_b.reshape(1, -1), ln_g.reshape(1, -1), ln_b.reshape(1, -1))


def _fft_kernel(z_ref, cs_ref, m_ref, o_ref, ab_ref, *, n, tc):
    gw = FFT_CH // FFT_GROUPS

    @pl.when(pl.program_id(1) == 0)
    def _():
        def chan(i, carry):
            r = pl.multiple_of(i * tc, tc)
            for g in range(FFT_GROUPS):
                zg = z_ref[pl.ds(r, tc), g * gw:(g + 1) * gw]
                ab = jnp.dot(zg, cs_ref[...], preferred_element_type=F32)
                ab_ref[pl.ds(r, tc), g * gw:(g + 1) * gw] = ab[:, :gw].astype(BF16)
                ab_ref[pl.ds(n + r, tc), g * gw:(g + 1) * gw] = ab[:, gw:].astype(BF16)
            return carry

        lax.fori_loop(0, n // tc, chan, 0)

    norm = 1.0 / float(np.sqrt(n * gw))
    o_ref[...] = (jnp.dot(m_ref[...], ab_ref[...], preferred_element_type=F32) * norm).astype(o_ref.dtype)


def fft_branch(p, cs, mseq, *, row0, nseq, n):
    tm = min(n, 512)
    tc = min(n, 512)
    blk0 = row0 // n
    nt = n // tm
    return pl.pallas_call(
        functools.partial(_fft_kernel, n=n, tc=tc),
        out_shape=jax.ShapeDtypeStruct((nseq * n, FFT_CH), BF16),
        grid=(nseq, nt),
        in_specs=[
            pl.BlockSpec((n, FFT_CH), lambda s, i: (s + blk0, P_FFT // FFT_CH)),
            pl.BlockSpec((FFT_CH // FFT_GROUPS, 2 * FFT_CH // FFT_GROUPS), lambda s, i: (0, 0)),
            pl.BlockSpec((tm, 2 * n), lambda s, i: (i, 0)),
        ],
        out_specs=pl.BlockSpec((tm, FFT_CH), lambda s, i: (s * nt + i, 0)),
        scratch_shapes=[pltpu.VMEM((2 * n, FFT_CH), BF16)],
        compiler_params=_cparams(("arbitrary", "arbitrary")),
        name="fft_branch",
    )(p, cs, mseq)


POOL_PAD = 16


def _pool_kernel(z_ref, pw_ref, ps_ref, o_ref, zp_ref, d_ref, *, n, tr, tc):
    zeros = jnp.zeros((POOL_PAD, POOL_CH), F32)
    zp_ref[0:POOL_PAD, :] = zeros
    zp_ref[POOL_PAD + n:2 * POOL_PAD + n, :] = zeros

    def fill(i, carry):
        r = pl.multiple_of(i * tc, tc)
        zp_ref[pl.ds(POOL_PAD + r, tc), :] = z_ref[pl.ds(r, tc), :].astype(F32)
        return carry

    lax.fori_loop(0, n // tc, fill, 0)

    win_rows = tr + 2 * POOL_PAD

    def pool(i, carry):
        r = pl.multiple_of(i * tr, tr)
        t = (r + lax.broadcasted_iota(I32, (tr, POOL_GROUP), 0)).astype(F32)
        for gi, w in enumerate(POOL_WINDOWS):
            cols = slice(gi * POOL_GROUP, (gi + 1) * POOL_GROUP)
            win = zp_ref[pl.ds(r, win_rows), cols]
            s = jnp.zeros((tr, POOL_GROUP), F32)
            for j in range(w):
                start = POOL_PAD - w // 2 + j
                if start % SUBLANES == 0:
                    s = s + win[start:start + tr, :]
                else:
                    s = s + pltpu.roll(win, win_rows - start, axis=0)[0:tr, :]
            lo = jnp.maximum(t - (w // 2), 0.0)
            hi = jnp.minimum(t - (w // 2) + w, float(n))
            z = win[POOL_PAD:POOL_PAD + tr, :]
            d_ref[pl.ds(r, tr), cols] = (s / (hi - lo) - z).astype(BF16)
        return carry

    lax.fori_loop(0, n // tr, pool, 0)

    def proj(i, carry):
        r = pl.multiple_of(i * tc, tc)
        for gi in range(len(POOL_WINDOWS)):
            cols = slice(gi * POOL_GROUP, (gi + 1) * POOL_GROUP)
            y = jnp.dot(d_ref[pl.ds(r, tc), cols], pw_ref[gi], preferred_element_type=F32)
            o_ref[pl.ds(r, tc), cols] = (y * ps_ref[:, cols]).astype(o_ref.dtype)
        return carry

    lax.fori_loop(0, n // tc, proj, 0)


def pool_branch(p, pool_w, pool_scale, *, row0, nseq, n):
    tr = 128
    tc = min(n, 512)
    blk0 = row0 // n
    return pl.pallas_call(
        functools.partial(_pool_kernel, n=n, tr=tr, tc=tc),
        out_shape=jax.ShapeDtypeStruct((nseq * n, POOL_CH), BF16),
        grid=(nseq,),
        in_specs=[
            pl.BlockSpec((n, POOL_CH), lambda s: (s + blk0, P_POOL // POOL_CH)),
            pl.BlockSpec((len(POOL_WINDOWS), POOL_GROUP, POOL_GROUP), lambda s: (0, 0, 0)),
            pl.BlockSpec((1, POOL_CH), lambda s: (0, 0)),
        ],
        out_specs=pl.BlockSpec((n, POOL_CH), lambda s: (s, 0)),
        scratch_shapes=[pltpu.VMEM((n + 2 * POOL_PAD, POOL_CH), F32),
                        pltpu.VMEM((n, POOL_CH), BF16)],
        compiler_params=_cparams(("arbitrary",)),
        name="pool_branch",
    )(p, pool_w, pool_scale.reshape(1, -1))


def _merge_kernel(x_ref, mod_ref, at_ref, cv_ref, ff_ref, po_ref, g_ref, wo_ref, wout_ref, o_ref):
    mix = None
    for b, br in enumerate((at_ref, cv_ref, ff_ref, po_ref)):
        y = jnp.dot(br[...], wo_ref[b], preferred_element_type=F32)
        gate = _sigmoid(g_ref[:, b * D_MODEL:(b + 1) * D_MODEL].astype(F32))
        mix = gate * y if mix is None else mix + gate * y
    out = jnp.dot(mix.astype(BF16), wout_ref[...], preferred_element_type=F32)
    o_ref[...] = x_ref[...] + mod_ref[0, 2:3, :] * out


def merge(x, mod, attn, conv, fft, pool, p, wo4, w_out, *, ntiles):
    tm = 256
    tps = SEQ // tm
    br_spec = pl.BlockSpec((tm, 512), lambda i: (i, 0))
    return pl.pallas_call(
        _merge_kernel,
        out_shape=jax.ShapeDtypeStruct((ntiles * tm, D_MODEL), F32),
        grid=(ntiles,),
        in_specs=[
            pl.BlockSpec((tm, D_MODEL), lambda i: (i, 0)),
            pl.BlockSpec((1, 6, D_MODEL), lambda i: (_mod_row(i, tps), 0, 0)),
            br_spec, br_spec, br_spec, br_spec,
            pl.BlockSpec((tm, N_BRANCHES * D_MODEL), lambda i: (i, P_GATE // (N_BRANCHES * D_MODEL))),
            pl.BlockSpec((N_BRANCHES, 512, D_MODEL), lambda i: (0, 0, 0), pipeline_mode=pl.Buffered(1)),
            pl.BlockSpec((D_MODEL, D_MODEL), lambda i: (0, 0), pipeline_mode=pl.Buffered(1)),
        ],
        out_specs=pl.BlockSpec((tm, D_MODEL), lambda i: (i, 0)),
        compiler_params=_cparams(("arbitrary",)),
        name="merge",
    )(x, mod, attn, conv, fft, pool, p, wo4, w_out)


def _first_index(hit_value, cand, ids, big):
    return jnp.min(jnp.where(cand == hit_value, ids, big), axis=0, keepdims=True)


def _router_kernel(x_ref, g_ref, mod_ref, wrh_ref, wrl_ref, rb_ref, tok_ref, eidx_ref, wk_ref):
    tm = x_ref.shape[0]
    h = _adaln(x_ref[...], g_ref[...], mod_ref[0, 3:4, :], mod_ref[0, 4:5, :])
    tok_ref[...] = h
    hh = h.astype(BF16)
    hl = (h - hh.astype(F32)).astype(BF16)
    nt = (((1,), (1,)), ((), ()))
    logits = (lax.dot_general(wrh_ref[...], hh, nt, preferred_element_type=F32)
              + lax.dot_general(wrh_ref[...], hl, nt, preferred_element_type=F32)
              + lax.dot_general(wrl_ref[...], hh, nt, preferred_element_type=F32))
    scores = _sigmoid(logits)
    sel = scores + rb_ref[...]
    per = N_EXPERTS // N_EXPERT_GROUPS
    assert per == SUBLANES and N_EXPERT_GROUPS == SUBLANES and TOP_K == SUBLANES
    neg = -jnp.inf
    sub = lax.broadcasted_iota(I32, (SUBLANES, tm), 0).astype(F32)
    sg = [sel[g * per:(g + 1) * per, :] for g in range(N_EXPERT_GROUPS)]
    sc = [scores[g * per:(g + 1) * per, :] for g in range(N_EXPERT_GROUPS)]
    gsc = jnp.zeros((SUBLANES, tm), F32)
    for g in range(N_EXPERT_GROUPS):
        m1 = jnp.max(sg[g], axis=0, keepdims=True)
        i1 = _first_index(m1, sg[g], sub, float(per))
        m2 = jnp.max(jnp.where(sub == i1, neg, sg[g]), axis=0, keepdims=True)
        gsc = jnp.where(sub == float(g), m1 + m2, gsc)
    gsel = jnp.zeros((SUBLANES, tm), F32)
    for _ in range(TOPK_GROUPS):
        m = jnp.max(gsc, axis=0, keepdims=True)
        hit = sub == _first_index(m, gsc, sub, float(N_EXPERT_GROUPS))
        gsel = jnp.where(hit, 1.0, gsel)
        gsc = jnp.where(hit, neg, gsc)
    cand = []
    for g in range(N_EXPERT_GROUPS):
        allowed = jnp.max(jnp.where(sub == float(g), gsel, 0.0), axis=0, keepdims=True)
        cand.append(jnp.where(allowed > 0.0, sg[g], neg))
    eid = [sub + float(g * per) for g in range(N_EXPERT_GROUPS)]
    idxs = jnp.zeros((SUBLANES, tm), F32)
    vals = jnp.zeros((SUBLANES, tm), F32)
    for k in range(TOP_K):
        m = functools.reduce(jnp.maximum, [jnp.max(c, axis=0, keepdims=True) for c in cand])
        idx = functools.reduce(
            jnp.minimum, [_first_index(m, cand[g], eid[g], float(N_EXPERTS)) for g in range(N_EXPERT_GROUPS)])
        val = jnp.zeros((1, tm), F32)
        for g in range(N_EXPERT_GROUPS):
            hit = eid[g] == idx
            val = val + jnp.sum(jnp.where(hit, sc[g], 0.0), axis=0, keepdims=True)
            cand[g] = jnp.where(hit, neg, cand[g])
        idxs = jnp.where(sub == float(k), idx, idxs)
        vals = jnp.where(sub == float(k), val, vals)
    eidx_ref[...] = idxs.astype(I32)
    wk_ref[...] = vals / jnp.sum(vals, axis=0, keepdims=True) * ROUTED_SCALE


def norm_router(x, g, mod, wr_hi, wr_lo, rbias, *, ntiles):
    tm = 256
    tps = SEQ // tm
    t = ntiles * tm
    return pl.pallas_call(
        _router_kernel,
        out_shape=(jax.ShapeDtypeStruct((t, D_MODEL), F32),
                   jax.ShapeDtypeStruct((TOP_K, t), I32),
                   jax.ShapeDtypeStruct((TOP_K, t), F32)),
        grid=(ntiles,),
        in_specs=[
            pl.BlockSpec((tm, D_MODEL), lambda i: (i, 0)),
            pl.BlockSpec((1, D_MODEL), lambda i: (0, 0)),
            pl.BlockSpec((1, 6, D_MODEL), lambda i: (_mod_row(i, tps), 0, 0)),
            pl.BlockSpec((N_EXPERTS, D_MODEL), lambda i: (0, 0)),
            pl.BlockSpec((N_EXPERTS, D_MODEL), lambda i: (0, 0)),
            pl.BlockSpec((N_EXPERTS, 1), lambda i: (0, 0)),
        ],
        out_specs=(pl.BlockSpec((tm, D_MODEL), lambda i: (i, 0)),
                   pl.BlockSpec((TOP_K, tm), lambda i: (0, i)),
                   pl.BlockSpec((TOP_K, tm), lambda i: (0, i))),
        compiler_params=_cparams(("arbitrary",)),
        name="norm_router",
    )(x, g.reshape(1, D_MODEL), mod, wr_hi, wr_lo, rbias.reshape(N_EXPERTS, 1))


def _row_copy(src, src_row, dst, dst_row, sem):
    return pltpu.make_async_copy(src.at[src_row], dst.at[dst_row], sem)


def _dispatch_kernel(pends_ref, tok_ref, dest_hbm, xs_hbm, dest_smem, zero_ref, sem_idx, sem_z, sem_s):
    i = pl.program_id(0)
    tm = tok_ref.shape[0]
    idx_copy = pltpu.make_async_copy(dest_hbm.at[i], dest_smem, sem_idx)
    idx_copy.start()

    def pad_copy(e):
        row = pl.multiple_of(pends_ref[e] - EXPERT_ROWS, EXPERT_ROWS)
        return pltpu.make_async_copy(zero_ref, xs_hbm.at[pl.ds(row, EXPERT_ROWS)], sem_z)

    @pl.when(i == 0)
    def _():
        zero_ref[...] = jnp.zeros_like(zero_ref)

        def start(e, prev):
            @pl.when(pends_ref[e] > prev)
            def _():
                pad_copy(e).start()
            return pends_ref[e]

        lax.fori_loop(0, N_EXPERTS, start, 0)

        def wait(e, prev):
            @pl.when(pends_ref[e] > prev)
            def _():
                pad_copy(e).wait()
            return pends_ref[e]

        lax.fori_loop(0, N_EXPERTS, wait, 0)

    idx_copy.wait()

    def scatter(r, carry):
        for k in range(TOP_K):
            _row_copy(tok_ref, r, xs_hbm, dest_smem[k * tm + r], sem_s).start()
        return carry

    lax.fori_loop(0, tm, scatter, 0)

    def drain(r, carry):
        for k in range(TOP_K):
            _row_copy(tok_ref, r, xs_hbm, dest_smem[k * tm + r], sem_s).wait()
        return carry

    lax.fori_loop(0, tm, drain, 0)


def dispatch(tokens, dest_tiles, pends, *, ntiles):
    tm = tokens.shape[0] // ntiles
    return pl.pallas_call(
        _dispatch_kernel,
        out_shape=jax.ShapeDtypeStruct((N_BLOCKS * EXPERT_ROWS, D_MODEL), F32),
        grid_spec=pltpu.PrefetchScalarGridSpec(
            num_scalar_prefetch=1,
            grid=(ntiles,),
            in_specs=[
                pl.BlockSpec((tm, D_MODEL), lambda i, pends: (i, 0)),
                pl.BlockSpec(memory_space=pl.ANY),
            ],
            out_specs=pl.BlockSpec(memory_space=pl.ANY),
            scratch_shapes=[
                pltpu.SMEM((TOP_K * tm,), I32),
                pltpu.VMEM((EXPERT_ROWS, D_MODEL), F32),
                pltpu.SemaphoreType.DMA,
                pltpu.SemaphoreType.DMA,
                pltpu.SemaphoreType.DMA,
            ],
        ),
        compiler_params=_cparams(("arbitrary",)),
        name="moe_dispatch",
    )(pends, tokens, dest_tiles)


def _experts_kernel(blk_e_ref, nused_ref, xs_ref, wg_ref, wu_ref, wd_ref, y_ref, wgb, wub, wdb):
    i = pl.program_id(0)
    e = blk_e_ref[i]
    prev = blk_e_ref[jnp.maximum(i - 1, 0)]

    @pl.when(i < nused_ref[0])
    def _():
        @pl.when((i == 0) | (e != prev))
        def _():
            wgb[...] = wg_ref[0].astype(BF16)
            wub[...] = wu_ref[0].astype(BF16)
            wdb[...] = wd_ref[0].astype(BF16)

        x = xs_ref[...].astype(BF16)
        g = jnp.dot(x, wgb[...], preferred_element_type=F32)
        u = jnp.dot(x, wub[...], preferred_element_type=F32)
        hb = (_silu(g) * u).astype(BF16)
        y_ref[...] = jnp.dot(hb, wdb[...], preferred_element_type=F32)


def experts(xs, blk_e, nused, w_gate, w_up, w_down, *, layer):
    w_gate = w_gate.reshape(DEPTH * N_EXPERTS, D_MODEL, EXPERT_FF)
    w_up = w_up.reshape(DEPTH * N_EXPERTS, D_MODEL, EXPERT_FF)
    w_down = w_down.reshape(DEPTH * N_EXPERTS, EXPERT_FF, D_MODEL)

    def row_map(i, blk_e, nused):
        return (jnp.minimum(i, nused[0] - 1), 0)

    def w_map(i, blk_e, nused):
        return (layer * N_EXPERTS + blk_e[jnp.minimum(i, nused[0] - 1)], 0, 0)

    return pl.pallas_call(
        _experts_kernel,
        out_shape=jax.ShapeDtypeStruct((N_BLOCKS * EXPERT_ROWS, D_MODEL), F32),
        grid_spec=pltpu.PrefetchScalarGridSpec(
            num_scalar_prefetch=2,
            grid=(N_BLOCKS,),
            in_specs=[
                pl.BlockSpec((EXPERT_ROWS, D_MODEL), row_map),
                pl.BlockSpec((1, D_MODEL, EXPERT_FF), w_map),
                pl.BlockSpec((1, D_MODEL, EXPERT_FF), w_map),
                pl.BlockSpec((1, EXPERT_FF, D_MODEL), w_map),
            ],
            out_specs=pl.BlockSpec((EXPERT_ROWS, D_MODEL), row_map),
            scratch_shapes=[
                pltpu.VMEM((D_MODEL, EXPERT_FF), BF16),
                pltpu.VMEM((D_MODEL, EXPERT_FF), BF16),
                pltpu.VMEM((EXPERT_FF, D_MODEL), BF16),
            ],
        ),
        compiler_params=_cparams(("arbitrary",)),
        name="moe_experts",
    )(blk_e, nused, xs, w_gate, w_up, w_down)


def _shared_kernel(t_ref, wg_ref, wu_ref, wd_ref, o_ref):
    x = t_ref[...].astype(BF16)
    g = jnp.dot(x, wg_ref[...], preferred_element_type=F32)
    u = jnp.dot(x, wu_ref[...], preferred_element_type=F32)
    o_ref[...] = jnp.dot((_silu(g) * u).astype(BF16), wd_ref[...], preferred_element_type=F32)


def shared_expert(tokens, wg, wu, wd):
    tm = 512
    t = tokens.shape[0]
    return pl.pallas_call(
        _shared_kernel,
        out_shape=jax.ShapeDtypeStruct((t, D_MODEL), F32),
        grid=(t // tm,),
        in_specs=[
            pl.BlockSpec((tm, D_MODEL), lambda i: (i, 0)),
            pl.BlockSpec((D_MODEL, SHARED_FF), lambda i: (0, 0)),
            pl.BlockSpec((D_MODEL, SHARED_FF), lambda i: (0, 0)),
            pl.BlockSpec((SHARED_FF, D_MODEL), lambda i: (0, 0)),
        ],
        out_specs=pl.BlockSpec((tm, D_MODEL), lambda i: (i, 0)),
        compiler_params=_cparams(("arbitrary",)),
        name="shared_expert",
    )(tokens, wg, wu, wd)


def _combine_kernel(x_ref, sh_ref, mod_ref, wk_ref, gf_ref, dest_hbm, ys_hbm, o_ref,
                    dest_smem, buf_ref, sem_idx, sem_g, *, final):
    i = pl.program_id(0)
    tm = x_ref.shape[0]
    idx_copy = pltpu.make_async_copy(dest_hbm.at[i], dest_smem, sem_idx)
    idx_copy.start()
    idx_copy.wait()

    def gather(r, carry):
        for k in range(TOP_K):
            _row_copy(ys_hbm, dest_smem[k * tm + r], buf_ref.at[k], r, sem_g).start()
        return carry

    lax.fori_loop(0, tm, gather, 0)

    def drain(r, carry):
        for k in range(TOP_K):
            _row_copy(ys_hbm, dest_smem[k * tm + r], buf_ref.at[k], r, sem_g).wait()
        return carry

    lax.fori_loop(0, tm, drain, 0)
    f = sh_ref[...]
    for k in range(TOP_K):
        f = f + buf_ref[k] * wk_ref[:, k:k + 1]
    out = x_ref[...] + mod_ref[0, 5:6, :] * f
    if final:
        y = out * lax.rsqrt(jnp.mean(out * out, axis=-1, keepdims=True) + EPS)
        out = y * gf_ref[...]
    o_ref[...] = out


def combine(x, shared, mod, wk_t, g_final, dest_tiles, ys, *, ntiles, final):
    tm = 128
    tps = SEQ // tm
    return pl.pallas_call(
        functools.partial(_combine_kernel, final=final),
        out_shape=jax.ShapeDtypeStruct((ntiles * tm, D_MODEL), F32),
        grid=(ntiles,),
        in_specs=[
            pl.BlockSpec((tm, D_MODEL), lambda i: (i, 0)),
            pl.BlockSpec((tm, D_MODEL), lambda i: (i, 0)),
            pl.BlockSpec((1, 6, D_MODEL), lambda i: (_mod_row(i, tps), 0, 0)),
            pl.BlockSpec((tm, TOP_K), lambda i: (i, 0)),
            pl.BlockSpec((1, D_MODEL), lambda i: (0, 0)),
            pl.BlockSpec(memory_space=pl.ANY),
            pl.BlockSpec(memory_space=pl.ANY),
        ],
        out_specs=pl.BlockSpec((tm, D_MODEL), lambda i: (i, 0)),
        scratch_shapes=[
            pltpu.SMEM((TOP_K * tm,), I32),
            pltpu.VMEM((TOP_K, tm, D_MODEL), F32),
            pltpu.SemaphoreType.DMA,
            pltpu.SemaphoreType.DMA,
        ],
        compiler_params=_cparams(("arbitrary",)),
        name="moe_combine",
    )(x, shared, mod, wk_t, g_final.reshape(1, D_MODEL), dest_tiles, ys)


def _routing_plan(eidx):
    t = eidx.shape[1]
    onehot = (eidx[None, :, :] == jnp.arange(N_EXPERTS, dtype=I32)[:, None, None])
    mask = jnp.any(onehot, axis=1).astype(I32)
    incl = jnp.cumsum(mask, axis=1)
    counts = incl[:, -1]
    rank = incl - mask
    pcounts = (counts + EXPERT_ROWS - 1) // EXPERT_ROWS * EXPERT_ROWS
    pends = jnp.cumsum(pcounts)
    pstarts = pends - pcounts
    slot = pstarts[:, None] + rank
    dest = jnp.sum(jnp.where(onehot, slot[:, None, :], 0), axis=0)
    nused = (pends[-1] // EXPERT_ROWS).astype(I32).reshape(1)
    first_row = jnp.arange(N_BLOCKS, dtype=I32) * EXPERT_ROWS
    blk_e = jnp.minimum(jnp.sum((pends[None, :] <= first_row[:, None]).astype(I32), axis=1),
                        N_EXPERTS - 1).astype(I32)
    return dest.astype(I32), pends.astype(I32), blk_e, nused


def _tile_major(dest, tm):
    k, t = dest.shape
    return dest.reshape(k, t // tm, tm).transpose(1, 0, 2).reshape(t // tm, k * tm)


def _rope_tables(n_rows):
    row = jnp.repeat(jnp.arange(n_rows, dtype=F32), GRID_W)
    col = jnp.tile(jnp.arange(GRID_W, dtype=F32), n_rows)
    half = QK_ROPE // 2
    inv = ROPE_BASE ** (-jnp.arange(0, half, 2, dtype=F32) / half)
    ang_r = row[:, None] * inv
    ang_c = col[:, None] * inv
    ang = jnp.concatenate([ang_r, ang_r, ang_c, ang_c], axis=-1)
    n = ang.shape[0]
    ones = jnp.ones((n, QK_NOPE), F32)
    zeros = jnp.zeros((n, QK_NOPE), F32)
    tail = jnp.zeros((n, HEAD_PAD - QK_NOPE - QK_ROPE), F32)
    cos = jnp.concatenate([ones, jnp.cos(ang), tail], axis=-1)
    sin = jnp.concatenate([zeros, jnp.sin(ang), tail], axis=-1)
    return cos, sin


def _identity_tables(n):
    cos = jnp.concatenate([jnp.ones((n, QK_NOPE + QK_ROPE), F32),
                           jnp.zeros((n, HEAD_PAD - QK_NOPE - QK_ROPE), F32)], axis=-1)
    return cos, jnp.zeros((n, HEAD_PAD), F32)


def _rotate_cols(w):
    i = np.arange(QK_ROPE)
    first = (i % (QK_ROPE // 2)) < (QK_ROPE // 4)
    perm = np.where(first, i + QK_ROPE // 4, i - QK_ROPE // 4)
    sign = np.where(first, -1.0, 1.0).astype(np.float32)
    return w[:, perm] * sign


def _mla_weights(g_cq, w_uq, g_ckv, w_ukv):
    dk = QK_NOPE + QK_ROPE
    wq = w_uq.reshape(Q_LORA, MLA_HEADS, dk)
    pad = jnp.zeros((Q_LORA, MLA_HEADS, HEAD_PAD - dk), F32)
    zero_nope = jnp.zeros((Q_LORA, MLA_HEADS, QK_NOPE), F32)
    wq_rot = _rotate_cols(wq[..., QK_NOPE:].reshape(Q_LORA * MLA_HEADS, QK_ROPE)).reshape(
        Q_LORA, MLA_HEADS, QK_ROPE)
    wq_a = jnp.concatenate([wq, pad], axis=-1).reshape(Q_LORA, -1)
    wq_b = jnp.concatenate([zero_nope, wq_rot, pad], axis=-1).reshape(Q_LORA, -1)
    wkv = w_ukv.reshape(KV_LORA, MLA_HEADS, QK_NOPE + V_HEAD)
    wk = jnp.concatenate([wkv[..., :QK_NOPE],
                          jnp.zeros((KV_LORA, MLA_HEADS, HEAD_PAD - QK_NOPE), F32)], axis=-1)
    wv = wkv[..., QK_NOPE:]
    eye = jnp.eye(QK_ROPE, dtype=F32)
    place = jnp.zeros((LANES, MLA_HEADS, HEAD_PAD), F32)
    place_a = place.at[:QK_ROPE, :, QK_NOPE:dk].set(jnp.broadcast_to(eye[:, None, :], (QK_ROPE, MLA_HEADS, QK_ROPE)))
    rot = _rotate_cols(eye)
    place_b = place.at[:QK_ROPE, :, QK_NOPE:dk].set(jnp.broadcast_to(rot[:, None, :], (QK_ROPE, MLA_HEADS, QK_ROPE)))
    return {
        "g_cq": g_cq.reshape(1, -1), "g_ckv": g_ckv.reshape(1, -1),
        "wq2t": jnp.concatenate([wq_a, wq_b], axis=-1).T.astype(BF16),
        "wk": wk.reshape(KV_LORA, -1).astype(BF16),
        "wvt": wv.reshape(KV_LORA, -1).T.astype(BF16),
        "pk": jnp.concatenate([place_a.reshape(LANES, -1), place_b.reshape(LANES, -1)], axis=-1).astype(BF16),
    }


def _pack_w_in(w):
    gates0 = MLA_IN + 2 * CONV_CH + FFT_CH + POOL_CH
    pad = jnp.zeros((D_MODEL, P_MLA_W - MLA_IN), w.dtype)
    return jnp.concatenate([w[:, gates0:], w[:, :MLA_IN], pad, w[:, MLA_IN:gates0]], axis=-1).astype(BF16)


def _dft_tables(n):
    gw = FFT_CH // FFT_GROUPS

    def angles(m):
        j = jnp.arange(m, dtype=I32)
        return (j[:, None] * j[None, :] % m).astype(F32) * (2.0 * np.pi / m)

    ac = angles(gw)
    an = angles(n)
    cs = jnp.concatenate([jnp.cos(ac), jnp.sin(ac)], axis=-1).astype(BF16)
    mseq = jnp.concatenate([jnp.cos(an), -jnp.sin(an)], axis=-1).astype(BF16)
    return cs, mseq


def _mixers(p, row0, nseq, n, wts, tables, conv_args, pool_args, dft):
    q4, k4, v = mla_prep(p, wts, tables, row0=row0, col0=P_MLA, nseq=nseq, n=n)
    cv = conv_branch(p, *conv_args, row0=row0, nseq=nseq, n=n)
    ff = fft_branch(p, dft[0], dft[1], row0=row0, nseq=nseq, n=n)
    po = pool_branch(p, *pool_args, row0=row0, nseq=nseq, n=n)
    return q4, k4, v, cv, ff, po


def kernel(x, c, ctx, c_ctx, w_mod, b_mod, g_norm1, g_norm2, w_in, g_cq, w_uq, g_ckv, w_ukv, w_o_mla, conv_w, conv_b, conv_ln_g, conv_ln_b, w_o_conv, w_o_fft, pool_w, pool_scale, w_o_pool, w_out, w_router, router_bias, w_exp_gate, w_exp_up, w_exp_down, w_sh_gate, w_sh_up, w_sh_down, g_final):
    xa = jnp.concatenate([x.reshape(T_LAT, D_MODEL), ctx.reshape(T_CTX, D_MODEL)], axis=0)
    cvec = jnp.concatenate([c, c_ctx[None, :], jnp.zeros((SUBLANES - BATCH - 1, D_MODEL), F32)], axis=0)
    mod_all = modulation_all(cvec, w_mod, b_mod).reshape(DEPTH, SUBLANES, 6, D_MODEL)
    rope_lat = _rope_tables(SEQ // GRID_W)
    rope_ctx = _identity_tables(CTX_LEN)
    dft_lat = _dft_tables(SEQ)
    dft_ctx = _dft_tables(CTX_LEN)
    lat_tiles = T_LAT // ROW_TILE
    all_tiles = T_ALL // ROW_TILE

    for l in range(DEPTH):
        last = l == DEPTH - 1
        mod = mod_all[l]
        w_in_p = _pack_w_in(w_in[l])
        wts = _mla_weights(g_cq[l], w_uq[l], g_ckv[l], w_ukv[l])
        conv_args = (conv_w[l], conv_b[l], conv_ln_g[l], conv_ln_b[l])
        pool_args = (pool_w[l].astype(BF16), pool_scale[l])
        wo4 = jnp.stack([w_o_mla[l], w_o_conv[l], w_o_fft[l], w_o_pool[l]], axis=0).astype(BF16)

        if last:
            p = normproj(xa, g_norm1[l], mod, w_in_p, tile0=0, ntiles=lat_tiles, col0=0, ncols=P_COLS)
            p_ctx = normproj(xa, g_norm1[l], mod, w_in_p, tile0=lat_tiles, ntiles=all_tiles - lat_tiles,
                             col0=P_MLA, ncols=P_MLA_W)
        else:
            p = normproj(xa, g_norm1[l], mod, w_in_p, tile0=0, ntiles=all_tiles, col0=0, ncols=P_COLS)
        q4, k4, v, cv, ff, po = _mixers(p, 0, BATCH, SEQ, wts, rope_lat, conv_args, pool_args, dft_lat)
        if last:
            qc, kc, vc = mla_prep(p_ctx, wts, rope_ctx, row0=0, col0=0, nseq=BATCH, n=CTX_LEN)
        else:
            qc, kc, vc, cvc, ffc, poc = _mixers(p, T_LAT, BATCH, CTX_LEN, wts, rope_ctx,
                                                conv_args, pool_args, dft_ctx)
        k_all = jnp.concatenate([k4, kc], axis=2)
        v_all = jnp.concatenate([v, vc], axis=2)
        at = attention(q4, k_all, v_all, nq=SEQ, nk=SEQ + CTX_LEN)
        if last:
            ntok = T_LAT
        else:
            atc = attention(qc, kc, vc, nq=CTX_LEN, nk=CTX_LEN)
            at = jnp.concatenate([at, atc], axis=0)
            cv = jnp.concatenate([cv, cvc], axis=0)
            ff = jnp.concatenate([ff, ffc], axis=0)
            po = jnp.concatenate([po, poc], axis=0)
            ntok = T_ALL
        xa = merge(xa, mod, at, cv, ff, po, p, wo4, w_out[l].astype(BF16), ntiles=ntok // 256)

        wr_t = w_router[l].T
        wr_hi = wr_t.astype(BF16)
        wr_lo = (wr_t - wr_hi.astype(F32)).astype(BF16)
        tokens, eidx, wk = norm_router(xa, g_norm2[l], mod, wr_hi, wr_lo, router_bias[l], ntiles=ntok // 256)
        dest, pends, blk_e, nused = _routing_plan(eidx)
        xs = dispatch(tokens, _tile_major(dest, 256), pends, ntiles=ntok // 256)
        ys = experts(xs, blk_e, nused, w_exp_gate, w_exp_up, w_exp_down, layer=l)
        sh = shared_expert(tokens, w_sh_gate[l].astype(BF16), w_sh_up[l].astype(BF16),
                           w_sh_down[l].astype(BF16))
        xa = combine(xa, sh, mod, wk.T, g_final, _tile_major(dest, 128), ys, ntiles=ntok // 128, final=last)
    return xa.reshape(BATCH, SEQ, D_MODEL)
```

```python
import functools

import jax
import jax.numpy as jnp
import numpy as np
from jax import lax
from jax.experimental import pallas as pl
from jax.experimental.pallas import tpu as pltpu

F32 = jnp.float32
BF16 = jnp.bfloat16
I32 = jnp.int32

D_MODEL = 2048
BATCH = 4
SEQ = 4096
DEPTH = 2
GRID_W = 64
CTX_LEN = 256
MLA_HEADS = 8
QK_NOPE = 64
QK_ROPE = 32
V_HEAD = 64
Q_LORA = 512
KV_LORA = 256
ROPE_BASE = 10000.0
CONV_CH = 512
CONV_WIDTH = 31
FFT_CH = 512
FFT_GROUPS = 4
POOL_CH = 512
POOL_WINDOWS = (2, 4, 8, 16)
POOL_GROUP = POOL_CH // len(POOL_WINDOWS)
N_BRANCHES = 4
N_EXPERTS = 64
N_EXPERT_GROUPS = 8
TOPK_GROUPS = 4
TOP_K = 8
EXPERT_FF = 512
SHARED_FF = 512
ROUTED_SCALE = 2.5
EPS = 1e-6

MLA_IN = Q_LORA + KV_LORA + QK_ROPE
T_LAT = BATCH * SEQ
T_CTX = BATCH * CTX_LEN
T_ALL = T_LAT + T_CTX

LANES = 128
SUBLANES = 8
VMEM_LIMIT_BYTES = 56 * 1024 * 1024

P_GATE = 0
P_MLA = N_BRANCHES * D_MODEL
P_MLA_W = 1024
P_CONV = P_MLA + P_MLA_W
P_FFT = P_CONV + 2 * CONV_CH
P_POOL = P_FFT + FFT_CH
P_COLS = P_POOL + POOL_CH
HEAD_PAD = 128
ATT_SCALE = (QK_NOPE + QK_ROPE) ** -0.5
Q_SCALE = ATT_SCALE * float(np.log2(np.e))

ROW_TILE = 1024
EXPERT_ROWS = 256
N_BLOCKS = -(-(T_ALL * TOP_K) // EXPERT_ROWS) + N_EXPERTS


def _cparams(sem, vmem=VMEM_LIMIT_BYTES):
    return pltpu.CompilerParams(dimension_semantics=sem, vmem_limit_bytes=vmem)


def _sigmoid(x):
    return 1.0 / (1.0 + jnp.exp(-x))


def _silu(x):
    return x * _sigmoid(x)


def _mod_row(tile, tiles_per_seq):
    return jnp.minimum(tile // tiles_per_seq, BATCH)


def _modulation_kernel(c_ref, w_ref, b_ref, o_ref):
    c = c_ref[...]
    s = _silu(c).astype(BF16)
    o_ref[0] = jnp.dot(s, w_ref[0].astype(BF16), preferred_element_type=F32) + b_ref[0]


def modulation_all(cvec, w_mod, b_mod):
    tn = 1024
    n = 6 * D_MODEL
    return pl.pallas_call(
        _modulation_kernel,
        out_shape=jax.ShapeDtypeStruct((DEPTH, SUBLANES, n), F32),
        grid=(DEPTH, n // tn),
        in_specs=[
            pl.BlockSpec((SUBLANES, D_MODEL), lambda l, j: (0, 0)),
            pl.BlockSpec((1, D_MODEL, tn), lambda l, j: (l, 0, j)),
            pl.BlockSpec((1, 1, tn), lambda l, j: (l, 0, j)),
        ],
        out_specs=pl.BlockSpec((1, SUBLANES, tn), lambda l, j: (l, 0, j)),
        compiler_params=_cparams(("arbitrary", "arbitrary")),
        name="modulation",
    )(cvec, w_mod, b_mod.reshape(DEPTH, 1, n))


def _adaln(x, g, shift, scale):
    y = x * lax.rsqrt(jnp.mean(x * x, axis=-1, keepdims=True) + EPS)
    return (y * g) * (1.0 + scale) + shift


def _normproj_kernel(x_ref, g_ref, mod_ref, w_ref, o_ref, h_ref, *, shift_row, chunk):
    @pl.when(pl.program_id(1) == 0)
    def _():
        shift = mod_ref[0, shift_row:shift_row + 1, :]
        scale = mod_ref[0, shift_row + 1:shift_row + 2, :]
        for r in range(0, x_ref.shape[0], chunk):
            h = _adaln(x_ref[r:r + chunk, :], g_ref[...], shift, scale)
            h_ref[r:r + chunk, :] = h.astype(BF16)

    o_ref[...] = jnp.dot(h_ref[...], w_ref[...], preferred_element_type=F32).astype(o_ref.dtype)


def normproj(x, g, mod, w, *, tile0, ntiles, col0, ncols, shift_row=0):
    tm, tn = ROW_TILE, 1024
    tps = SEQ // tm
    cb0 = col0 // tn
    return pl.pallas_call(
        functools.partial(_normproj_kernel, shift_row=shift_row, chunk=256),
        out_shape=jax.ShapeDtypeStruct((ntiles * tm, ncols), BF16),
        grid=(ntiles, ncols // tn),
        in_specs=[
            pl.BlockSpec((tm, D_MODEL), lambda i, j: (i + tile0, 0)),
            pl.BlockSpec((1, D_MODEL), lambda i, j: (0, 0)),
            pl.BlockSpec((1, 6, D_MODEL), lambda i, j: (_mod_row(i + tile0, tps), 0, 0)),
            pl.BlockSpec((D_MODEL, tn), lambda i, j: (0, j + cb0)),
        ],
        out_specs=pl.BlockSpec((tm, tn), lambda i, j: (i, j)),
        scratch_shapes=[pltpu.VMEM((tm, D_MODEL), BF16)],
        compiler_params=_cparams(("arbitrary", "arbitrary")),
        name="normproj",
    )(x, g.reshape(1, D_MODEL), mod, w)


def _rms_gain(x, g):
    y = x * lax.rsqrt(jnp.mean(x * x, axis=-1, keepdims=True) + EPS)
    return (y * g).astype(BF16)


_NT = (((1,), (1,)), ((), ()))


def _mla_prep_kernel(p_ref, gq_ref, gkv_ref, wqt_ref, wk_ref, wvt_ref, pk_ref, cos_ref, sin_ref,
                     cost_ref, sint_ref, qt_ref, k_ref, vt_ref):
    hw = MLA_HEADS * HEAD_PAD
    cqn = _rms_gain(p_ref[:, :Q_LORA].astype(F32), gq_ref[...])
    ckvn = _rms_gain(p_ref[:, Q_LORA:Q_LORA + KV_LORA].astype(F32), gkv_ref[...])
    kr = p_ref[:, Q_LORA + KV_LORA:Q_LORA + KV_LORA + LANES]
    q2t = lax.dot_general(wqt_ref[...], cqn, _NT, preferred_element_type=F32)
    vt_ref[0] = lax.dot_general(wvt_ref[...], ckvn, _NT, preferred_element_type=F32).astype(BF16)
    kk = jnp.dot(ckvn, wk_ref[...], preferred_element_type=F32)
    kr2 = jnp.dot(kr, pk_ref[...], preferred_element_type=F32)
    cos, sin = cos_ref[...], sin_ref[...]
    cost, sint = cost_ref[...], sint_ref[...]
    for h in range(MLA_HEADS):
        a = slice(h * HEAD_PAD, (h + 1) * HEAD_PAD)
        b = slice(hw + h * HEAD_PAD, hw + (h + 1) * HEAD_PAD)
        qt_ref[0, h] = ((q2t[a, :] * cost + q2t[b, :] * sint) * Q_SCALE).astype(BF16)
        k_ref[0, h] = (kk[:, a] + kr2[:, a] * cos + kr2[:, b] * sin).astype(BF16)


def mla_prep(p, wts, tables, *, row0, col0, nseq, n):
    tm = 256
    tps = n // tm
    hw = MLA_HEADS * HEAD_PAD
    vw = MLA_HEADS * V_HEAD
    blk0 = row0 // tm
    cblk = col0 // P_MLA_W
    cos_t, sin_t = tables
    const = lambda i: (0, 0)
    return pl.pallas_call(
        _mla_prep_kernel,
        out_shape=(jax.ShapeDtypeStruct((nseq, MLA_HEADS, HEAD_PAD, n), BF16),
                   jax.ShapeDtypeStruct((nseq, MLA_HEADS, n, HEAD_PAD), BF16),
                   jax.ShapeDtypeStruct((nseq, vw, n), BF16)),
        grid=(nseq * tps,),
        in_specs=[
            pl.BlockSpec((tm, P_MLA_W), lambda i: (i + blk0, cblk)),
            pl.BlockSpec((1, Q_LORA), const),
            pl.BlockSpec((1, KV_LORA), const),
            pl.BlockSpec((2 * hw, Q_LORA), const),
            pl.BlockSpec((KV_LORA, hw), const),
            pl.BlockSpec((vw, KV_LORA), const),
            pl.BlockSpec((LANES, 2 * hw), const),
            pl.BlockSpec((tm, HEAD_PAD), lambda i: (i % tps, 0)),
            pl.BlockSpec((tm, HEAD_PAD), lambda i: (i % tps, 0)),
            pl.BlockSpec((HEAD_PAD, tm), lambda i: (0, i % tps)),
            pl.BlockSpec((HEAD_PAD, tm), lambda i: (0, i % tps)),
        ],
        out_specs=(pl.BlockSpec((1, MLA_HEADS, HEAD_PAD, tm), lambda i: (i // tps, 0, 0, i % tps)),
                   pl.BlockSpec((1, MLA_HEADS, tm, HEAD_PAD), lambda i: (i // tps, 0, i % tps, 0)),
                   pl.BlockSpec((1, vw, tm), lambda i: (i // tps, 0, i % tps))),
        compiler_params=_cparams(("arbitrary",)),
        name="mla_prep",
    )(p, wts["g_cq"], wts["g_ckv"], wts["wq2t"], wts["wk"], wts["wvt"], wts["pk"],
      cos_t, sin_t, cos_t.T, sin_t.T)


def _attn_kernel(qt_ref, k_ref, vt_ref, o_ref, s_ref, p_ref, *, nk, ck):
    tq = qt_ref.shape[3]
    chunks = [slice(c, c + ck) for c in range(0, nk, ck)]
    outs = []
    for h in range(2):
        qt = qt_ref[0, h]
        m = None
        for c in chunks:
            s = jnp.dot(k_ref[0, h, c, :], qt, preferred_element_type=F32)
            s_ref[h, c, :] = s
            mc = jnp.max(s, axis=0, keepdims=True)
            m = mc if m is None else jnp.maximum(m, mc)
        l = jnp.zeros((1, tq), F32)
        for c in chunks:
            p = jnp.exp2(s_ref[h, c, :] - m)
            l = l + jnp.sum(p, axis=0, keepdims=True)
            p_ref[h, c, :] = p.astype(BF16)
        ot = jnp.dot(vt_ref[0, h * V_HEAD:(h + 1) * V_HEAD, :], p_ref[h], preferred_element_type=F32)
        outs.append(ot / l)
    o_ref[...] = jnp.concatenate(outs, axis=0).T.astype(o_ref.dtype)


def attention(qt4, k4, vt3, *, nq, nk):
    tq = 256
    ck = 544 if nk % 544 == 0 else nk
    nqt = nq // tq
    return pl.pallas_call(
        functools.partial(_attn_kernel, nk=nk, ck=ck),
        out_shape=jax.ShapeDtypeStruct((BATCH * nq, MLA_HEADS * V_HEAD), BF16),
        grid=(BATCH, MLA_HEADS // 2, nqt),
        in_specs=[
            pl.BlockSpec((1, 2, HEAD_PAD, tq), lambda b, hp, i: (b, hp, 0, i)),
            pl.BlockSpec((1, 2, nk, HEAD_PAD), lambda b, hp, i: (b, hp, 0, 0)),
            pl.BlockSpec((1, 2 * V_HEAD, nk), lambda b, hp, i: (b, hp, 0)),
        ],
        out_specs=pl.BlockSpec((tq, 2 * V_HEAD), lambda b, hp, i: (b * nqt + i, hp)),
        scratch_shapes=[pltpu.VMEM((2, nk, tq), F32), pltpu.VMEM((2, nk, tq), BF16)],
        compiler_params=_cparams(("arbitrary", "arbitrary", "arbitrary")),
        name="attention",
    )(qt4, k4, vt3)


CONV_PAD = 16


def _conv_kernel(p_ref, w_ref, cb_ref, lg_ref, lb_ref, o_ref, u_ref, *, n, tr):
    zeros = jnp.zeros((CONV_PAD, CONV_CH), F32)
    u_ref[0:CONV_PAD, :] = zeros
    u_ref[CONV_PAD + n:2 * CONV_PAD + n, :] = zeros

    def glu(i, carry):
        r = pl.multiple_of(i * tr, tr)
        a = p_ref[pl.ds(r, tr), 0:CONV_CH].astype(F32)
        b = p_ref[pl.ds(r, tr), CONV_CH:2 * CONV_CH].astype(F32)
        u_ref[pl.ds(CONV_PAD + r, tr), :] = a * _sigmoid(b)
        return carry

    lax.fori_loop(0, n // tr, glu, 0)
    off = CONV_PAD - CONV_WIDTH // 2
    win_rows = tr + 2 * CONV_PAD

    def conv(i, carry):
        r = pl.multiple_of(i * tr, tr)
        strips = []
        for c0 in range(0, CONV_CH, LANES):
            win = u_ref[pl.ds(r, win_rows), c0:c0 + LANES]
            acc = jnp.zeros((tr, LANES), F32)
            for b in range(SUBLANES):
                wb = pltpu.roll(win, win_rows - (off + b), axis=0)
                for k in range(b, CONV_WIDTH, SUBLANES):
                    acc = acc + wb[k - b:k - b + tr, :] * w_ref[k:k + 1, c0:c0 + LANES]
            strips.append(acc)
        acc = jnp.concatenate(strips, axis=-1) + cb_ref[...]
        mu = jnp.mean(acc, axis=-1, keepdims=True)
        d = acc - mu
        var = jnp.mean(d * d, axis=-1, keepdims=True)
        y = (d * lax.rsqrt(var + EPS)) * lg_ref[...] + lb_ref[...]
        o_ref[pl.ds(r, tr), :] = _silu(y).astype(o_ref.dtype)
        return carry

    lax.fori_loop(0, n // tr, conv, 0)


def conv_branch(p, conv_w, conv_b, ln_g, ln_b, *, row0, nseq, n):
    tr = 128
    blk0 = row0 // n
    const = lambda s: (0, 0)
    return pl.pallas_call(
        functools.partial(_conv_kernel, n=n, tr=tr),
        out_shape=jax.ShapeDtypeStruct((nseq * n, CONV_CH), BF16),
        grid=(nseq,),
        in_specs=[
            pl.BlockSpec((n, 2 * CONV_CH), lambda s: (s + blk0, P_CONV // (2 * CONV_CH))),
            pl.BlockSpec((CONV_WIDTH, CONV_CH), const),
            pl.BlockSpec((1, CONV_CH), const),
            pl.BlockSpec((1, CONV_CH), const),
            pl.BlockSpec((1, CONV_CH), const),
        ],
        out_specs=pl.BlockSpec((n, CONV_CH), lambda s: (s, 0)),
        scratch_shapes=[pltpu.VMEM((n + 2 * CONV_PAD, CONV_CH), F32)],
        compiler_params=_cparams(("arbitrary",)),
        name="conv_branch",
    )(p, conv_w, conv_b.reshape(1, -1), ln_g.reshape(1, -1), ln_b.reshape(1, -1))


def _fft_kernel(z_ref, cs_ref, m_ref, o_ref, ab_ref, *, n, tc):
    gw = FFT_CH // FFT_GROUPS

    @pl.when(pl.program_id(1) == 0)
    def _():
        def chan(i, carry):
            r = pl.multiple_of(i * tc, tc)
            for g in range(FFT_GROUPS):
                zg = z_ref[pl.ds(r, tc), g * gw:(g + 1) * gw]
                ab = jnp.dot(zg, cs_ref[...], preferred_element_type=F32)
                ab_ref[pl.ds(r, tc), g * gw:(g + 1) * gw] = ab[:, :gw].astype(BF16)
                ab_ref[pl.ds(n + r, tc), g * gw:(g + 1) * gw] = ab[:, gw:].astype(BF16)
            return carry

        lax.fori_loop(0, n // tc, chan, 0)

    norm = 1.0 / float(np.sqrt(n * gw))
    o_ref[...] = (jnp.dot(m_ref[...], ab_ref[...], preferred_element_type=F32) * norm).astype(o_ref.dtype)


def fft_branch(p, cs, mseq, *, row0, nseq, n):
    tm = min(n, 512)
    tc = min(n, 512)
    blk0 = row0 // n
    nt = n // tm
    return pl.pallas_call(
        functools.partial(_fft_kernel, n=n, tc=tc),
        out_shape=jax.ShapeDtypeStruct((nseq * n, FFT_CH), BF16),
        grid=(nseq, nt),
        in_specs=[
            pl.BlockSpec((n, FFT_CH), lambda s, i: (s + blk0, P_FFT // FFT_CH)),
            pl.BlockSpec((FFT_CH // FFT_GROUPS, 2 * FFT_CH // FFT_GROUPS), lambda s, i: (0, 0)),
            pl.BlockSpec((tm, 2 * n), lambda s, i: (i, 0)),
        ],
        out_specs=pl.BlockSpec((tm, FFT_CH), lambda s, i: (s * nt + i, 0)),
        scratch_shapes=[pltpu.VMEM((2 * n, FFT_CH), BF16)],
        compiler_params=_cparams(("arbitrary", "arbitrary")),
        name="fft_branch",
    )(p, cs, mseq)


POOL_PAD = 16


def _pool_kernel(z_ref, pw_ref, ps_ref, o_ref, zp_ref, d_ref, *, n, tr, tc):
    zeros = jnp.zeros((POOL_PAD, POOL_CH), F32)
    zp_ref[0:POOL_PAD, :] = zeros
    zp_ref[POOL_PAD + n:2 * POOL_PAD + n, :] = zeros

    def fill(i, carry):
        r = pl.multiple_of(i * tc, tc)
        zp_ref[pl.ds(POOL_PAD + r, tc), :] = z_ref[pl.ds(r, tc), :].astype(F32)
        return carry

    lax.fori_loop(0, n // tc, fill, 0)

    win_rows = tr + 2 * POOL_PAD

    def pool(i, carry):
        r = pl.multiple_of(i * tr, tr)
        t = (r + lax.broadcasted_iota(I32, (tr, POOL_GROUP), 0)).astype(F32)
        for gi, w in enumerate(POOL_WINDOWS):
            cols = slice(gi * POOL_GROUP, (gi + 1) * POOL_GROUP)
            win = zp_ref[pl.ds(r, win_rows), cols]
            s = jnp.zeros((tr, POOL_GROUP), F32)
            for j in range(w):
                start = POOL_PAD - w // 2 + j
                if start % SUBLANES == 0:
                    s = s + win[start:start + tr, :]
                else:
                    s = s + pltpu.roll(win, win_rows - start, axis=0)[0:tr, :]
            lo = jnp.maximum(t - (w // 2), 0.0)
            hi = jnp.minimum(t - (w // 2) + w, float(n))
            z = win[POOL_PAD:POOL_PAD + tr, :]
            d_ref[pl.ds(r, tr), cols] = (s / (hi - lo) - z).astype(BF16)
        return carry

    lax.fori_loop(0, n // tr, pool, 0)

    def proj(i, carry):
        r = pl.multiple_of(i * tc, tc)
        for gi in range(len(POOL_WINDOWS)):
            cols = slice(gi * POOL_GROUP, (gi + 1) * POOL_GROUP)
            y = jnp.dot(d_ref[pl.ds(r, tc), cols], pw_ref[gi], preferred_element_type=F32)
            o_ref[pl.ds(r, tc), cols] = (y * ps_ref[:, cols]).astype(o_ref.dtype)
        return carry

    lax.fori_loop(0, n // tc, proj, 0)


def pool_branch(p, pool_w, pool_scale, *, row0, nseq, n):
    tr = 128
    tc = min(n, 512)
    blk0 = row0 // n
    return pl.pallas_call(
        functools.partial(_pool_kernel, n=n, tr=tr, tc=tc),
        out_shape=jax.ShapeDtypeStruct((nseq * n, POOL_CH), BF16),
        grid=(nseq,),
        in_specs=[
            pl.BlockSpec((n, POOL_CH), lambda s: (s + blk0, P_POOL // POOL_CH)),
            pl.BlockSpec((len(POOL_WINDOWS), POOL_GROUP, POOL_GROUP), lambda s: (0, 0, 0)),
            pl.BlockSpec((1, POOL_CH), lambda s: (0, 0)),
        ],
        out_specs=pl.BlockSpec((n, POOL_CH), lambda s: (s, 0)),
        scratch_shapes=[pltpu.VMEM((n + 2 * POOL_PAD, POOL_CH), F32),
                        pltpu.VMEM((n, POOL_CH), BF16)],
        compiler_params=_cparams(("arbitrary",)),
        name="pool_branch",
    )(p, pool_w, pool_scale.reshape(1, -1))


def _merge_kernel(x_ref, mod_ref, at_ref, cv_ref, ff_ref, po_ref, g_ref, wo_ref, wout_ref, o_ref):
    mix = None
    for b, br in enumerate((at_ref, cv_ref, ff_ref, po_ref)):
        y = jnp.dot(br[...], wo_ref[b], preferred_element_type=F32)
        gate = _sigmoid(g_ref[:, b * D_MODEL:(b + 1) * D_MODEL].astype(F32))
        mix = gate * y if mix is None else mix + gate * y
    out = jnp.dot(mix.astype(BF16), wout_ref[...], preferred_element_type=F32)
    o_ref[...] = x_ref[...] + mod_ref[0, 2:3, :] * out


def merge(x, mod, attn, conv, fft, pool, p, wo4, w_out, *, ntiles):
    tm = 256
    tps = SEQ // tm
    br_spec = pl.BlockSpec((tm, 512), lambda i: (i, 0))
    return pl.pallas_call(
        _merge_kernel,
        out_shape=jax.ShapeDtypeStruct((ntiles * tm, D_MODEL), F32),
        grid=(ntiles,),
        in_specs=[
            pl.BlockSpec((tm, D_MODEL), lambda i: (i, 0)),
            pl.BlockSpec((1, 6, D_MODEL), lambda i: (_mod_row(i, tps), 0, 0)),
            br_spec, br_spec, br_spec, br_spec,
            pl.BlockSpec((tm, N_BRANCHES * D_MODEL), lambda i: (i, P_GATE // (N_BRANCHES * D_MODEL))),
            pl.BlockSpec((N_BRANCHES, 512, D_MODEL), lambda i: (0, 0, 0), pipeline_mode=pl.Buffered(1)),
            pl.BlockSpec((D_MODEL, D_MODEL), lambda i: (0, 0), pipeline_mode=pl.Buffered(1)),
        ],
        out_specs=pl.BlockSpec((tm, D_MODEL), lambda i: (i, 0)),
        compiler_params=_cparams(("arbitrary",)),
        name="merge",
    )(x, mod, attn, conv, fft, pool, p, wo4, w_out)


def _first_index(hit_value, cand, ids, big):
    return jnp.min(jnp.where(cand == hit_value, ids, big), axis=0, keepdims=True)


HALF = D_MODEL // 2
ROW_WORDS = HALF // LANES


def _round_bf16_bits(x):
    u = lax.bitcast_convert_type(x, I32)
    odd = lax.shift_right_logical(u, 16) & 1
    return (u + 0x7FFF + odd) & jnp.int32(-65536)


def _pack_words(lo, hi):
    return lax.shift_right_logical(_round_bf16_bits(lo), 16) | _round_bf16_bits(hi)


def _unpack_words(w):
    lo = lax.bitcast_convert_type(lax.shift_left(w, 16), F32)
    hi = lax.bitcast_convert_type(w & jnp.int32(-65536), F32)
    return lo, hi


def _store_packed(ref, x):
    tm = x.shape[0]
    for s in range(ROW_WORDS):
        w = _pack_words(x[:, s * LANES:(s + 1) * LANES], x[:, HALF + s * LANES:HALF + (s + 1) * LANES])
        ref[pl.ds(s, tm, stride=ROW_WORDS), :] = w


def _load_packed(ref, tm, dtype):
    los, his = [], []
    for s in range(ROW_WORDS):
        lo, hi = _unpack_words(ref[pl.ds(s, tm, stride=ROW_WORDS), :])
        los.append(lo.astype(dtype))
        his.append(hi.astype(dtype))
    return jnp.concatenate(los + his, axis=-1)


def _router_kernel(x_ref, g_ref, mod_ref, wrh_ref, wrl_ref, rb_ref, tok_ref, eidx_ref, wk_ref):
    tm = x_ref.shape[0]
    h = _adaln(x_ref[...], g_ref[...], mod_ref[0, 3:4, :], mod_ref[0, 4:5, :])
    _store_packed(tok_ref, h)
    hh = h.astype(BF16)
    hl = (h - hh.astype(F32)).astype(BF16)
    nt = (((1,), (1,)), ((), ()))
    logits = (lax.dot_general(wrh_ref[...], hh, nt, preferred_element_type=F32)
              + lax.dot_general(wrh_ref[...], hl, nt, preferred_element_type=F32)
              + lax.dot_general(wrl_ref[...], hh, nt, preferred_element_type=F32))
    scores = _sigmoid(logits)
    sel = scores + rb_ref[...]
    per = N_EXPERTS // N_EXPERT_GROUPS
    assert per == SUBLANES and N_EXPERT_GROUPS == SUBLANES and TOP_K == SUBLANES
    neg = -jnp.inf
    sub = lax.broadcasted_iota(I32, (SUBLANES, tm), 0).astype(F32)
    sg = [sel[g * per:(g + 1) * per, :] for g in range(N_EXPERT_GROUPS)]
    sc = [scores[g * per:(g + 1) * per, :] for g in range(N_EXPERT_GROUPS)]
    gsc = jnp.zeros((SUBLANES, tm), F32)
    for g in range(N_EXPERT_GROUPS):
        m1 = jnp.max(sg[g], axis=0, keepdims=True)
        i1 = _first_index(m1, sg[g], sub, float(per))
        m2 = jnp.max(jnp.where(sub == i1, neg, sg[g]), axis=0, keepdims=True)
        gsc = jnp.where(sub == float(g), m1 + m2, gsc)
    gsel = jnp.zeros((SUBLANES, tm), F32)
    for _ in range(TOPK_GROUPS):
        m = jnp.max(gsc, axis=0, keepdims=True)
        hit = sub == _first_index(m, gsc, sub, float(N_EXPERT_GROUPS))
        gsel = jnp.where(hit, 1.0, gsel)
        gsc = jnp.where(hit, neg, gsc)
    cand = []
    for g in range(N_EXPERT_GROUPS):
        allowed = jnp.max(jnp.where(sub == float(g), gsel, 0.0), axis=0, keepdims=True)
        cand.append(jnp.where(allowed > 0.0, sg[g], neg))
    eid = [sub + float(g * per) for g in range(N_EXPERT_GROUPS)]
    idxs = jnp.zeros((SUBLANES, tm), F32)
    vals = jnp.zeros((SUBLANES, tm), F32)
    for k in range(TOP_K):
        m = functools.reduce(jnp.maximum, [jnp.max(c, axis=0, keepdims=True) for c in cand])
        idx = functools.reduce(
            jnp.minimum, [_first_index(m, cand[g], eid[g], float(N_EXPERTS)) for g in range(N_EXPERT_GROUPS)])
        val = jnp.zeros((1, tm), F32)
        for g in range(N_EXPERT_GROUPS):
            hit = eid[g] == idx
            val = val + jnp.sum(jnp.where(hit, sc[g], 0.0), axis=0, keepdims=True)
            cand[g] = jnp.where(hit, neg, cand[g])
        idxs = jnp.where(sub == float(k), idx, idxs)
        vals = jnp.where(sub == float(k), val, vals)
    eidx_ref[...] = idxs.astype(I32)
    wk_ref[...] = vals / jnp.sum(vals, axis=0, keepdims=True) * ROUTED_SCALE


def norm_router(x, g, mod, wr_hi, wr_lo, rbias, *, ntiles):
    tm = 256
    tps = SEQ // tm
    t = ntiles * tm
    return pl.pallas_call(
        _router_kernel,
        out_shape=(jax.ShapeDtypeStruct((t * ROW_WORDS, LANES), I32),
                   jax.ShapeDtypeStruct((TOP_K, t), I32),
                   jax.ShapeDtypeStruct((TOP_K, t), F32)),
        grid=(ntiles,),
        in_specs=[
            pl.BlockSpec((tm, D_MODEL), lambda i: (i, 0)),
            pl.BlockSpec((1, D_MODEL), lambda i: (0, 0)),
            pl.BlockSpec((1, 6, D_MODEL), lambda i: (_mod_row(i, tps), 0, 0)),
            pl.BlockSpec((N_EXPERTS, D_MODEL), lambda i: (0, 0)),
            pl.BlockSpec((N_EXPERTS, D_MODEL), lambda i: (0, 0)),
            pl.BlockSpec((N_EXPERTS, 1), lambda i: (0, 0)),
        ],
        out_specs=(pl.BlockSpec((tm * ROW_WORDS, LANES), lambda i: (i, 0)),
                   pl.BlockSpec((TOP_K, tm), lambda i: (0, i)),
                   pl.BlockSpec((TOP_K, tm), lambda i: (0, i))),
        compiler_params=_cparams(("arbitrary",)),
        name="norm_router",
    )(x, g.reshape(1, D_MODEL), mod, wr_hi, wr_lo, rbias.reshape(N_EXPERTS, 1))


def _row_copy(src, src_tok, dst, dst_tok, sem):
    s = pl.multiple_of(src_tok * ROW_WORDS, ROW_WORDS)
    d = pl.multiple_of(dst_tok * ROW_WORDS, ROW_WORDS)
    return pltpu.make_async_copy(src.at[pl.ds(s, ROW_WORDS)], dst.at[pl.ds(d, ROW_WORDS)], sem)


def _dispatch_kernel(pends_ref, tok_ref, dest_hbm, xs_hbm, dest_smem, zero_ref, sem_idx, sem_z, sem_s):
    i = pl.program_id(0)
    tm = tok_ref.shape[0] // ROW_WORDS
    idx_copy = pltpu.make_async_copy(dest_hbm.at[i], dest_smem, sem_idx)
    idx_copy.start()
    blk = EXPERT_ROWS * ROW_WORDS

    def pad_copy(e):
        row = pl.multiple_of((pends_ref[e] - EXPERT_ROWS) * ROW_WORDS, blk)
        return pltpu.make_async_copy(zero_ref, xs_hbm.at[pl.ds(row, blk)], sem_z)

    @pl.when(i == 0)
    def _():
        zero_ref[...] = jnp.zeros_like(zero_ref)

        def start(e, prev):
            @pl.when(pends_ref[e] > prev)
            def _():
                pad_copy(e).start()
            return pends_ref[e]

        lax.fori_loop(0, N_EXPERTS, start, 0)

        def wait(e, prev):
            @pl.when(pends_ref[e] > prev)
            def _():
                pad_copy(e).wait()
            return pends_ref[e]

        lax.fori_loop(0, N_EXPERTS, wait, 0)

    idx_copy.wait()

    def scatter(r, carry):
        for k in range(TOP_K):
            _row_copy(tok_ref, r, xs_hbm, dest_smem[k * tm + r], sem_s).start()
        return carry

    lax.fori_loop(0, tm, scatter, 0)

    def drain(r, carry):
        for k in range(TOP_K):
            _row_copy(tok_ref, r, xs_hbm, dest_smem[k * tm + r], sem_s).wait()
        return carry

    lax.fori_loop(0, tm, drain, 0)


def dispatch(tokens, dest_tiles, pends, *, ntiles):
    tm = tokens.shape[0] // ROW_WORDS // ntiles
    return pl.pallas_call(
        _dispatch_kernel,
        out_shape=jax.ShapeDtypeStruct((N_BLOCKS * EXPERT_ROWS * ROW_WORDS, LANES), I32),
        grid_spec=pltpu.PrefetchScalarGridSpec(
            num_scalar_prefetch=1,
            grid=(ntiles,),
            in_specs=[
                pl.BlockSpec((tm * ROW_WORDS, LANES), lambda i, pends: (i, 0)),
                pl.BlockSpec(memory_space=pl.ANY),
            ],
            out_specs=pl.BlockSpec(memory_space=pl.ANY),
            scratch_shapes=[
                pltpu.SMEM((TOP_K * tm,), I32),
                pltpu.VMEM((EXPERT_ROWS * ROW_WORDS, LANES), I32),
                pltpu.SemaphoreType.DMA,
                pltpu.SemaphoreType.DMA,
                pltpu.SemaphoreType.DMA,
            ],
        ),
        compiler_params=_cparams(("arbitrary",)),
        name="moe_dispatch",
    )(pends, tokens, dest_tiles)


def _experts_kernel(blk_e_ref, nused_ref, xs_ref, wg_ref, wu_ref, wd_ref, y_ref, wgb, wub, wdb):
    i = pl.program_id(0)
    e = blk_e_ref[i]
    prev = blk_e_ref[jnp.maximum(i - 1, 0)]

    @pl.when(i < nused_ref[0])
    def _():
        @pl.when((i == 0) | (e != prev))
        def _():
            wgb[...] = wg_ref[0].astype(BF16)
            wub[...] = wu_ref[0].astype(BF16)
            wdb[...] = wd_ref[0].astype(BF16)

        x = _load_packed(xs_ref, EXPERT_ROWS, BF16)
        g = jnp.dot(x, wgb[...], preferred_element_type=F32)
        u = jnp.dot(x, wub[...], preferred_element_type=F32)
        hb = (_silu(g) * u).astype(BF16)
        _store_packed(y_ref, jnp.dot(hb, wdb[...], preferred_element_type=F32))


def experts(xs, blk_e, nused, w_gate, w_up, w_down, *, layer):
    w_gate = w_gate.reshape(DEPTH * N_EXPERTS, D_MODEL, EXPERT_FF)
    w_up = w_up.reshape(DEPTH * N_EXPERTS, D_MODEL, EXPERT_FF)
    w_down = w_down.reshape(DEPTH * N_EXPERTS, EXPERT_FF, D_MODEL)

    def row_map(i, blk_e, nused):
        return (jnp.minimum(i, nused[0] - 1), 0)

    def w_map(i, blk_e, nused):
        return (layer * N_EXPERTS + blk_e[jnp.minimum(i, nused[0] - 1)], 0, 0)

    return pl.pallas_call(
        _experts_kernel,
        out_shape=jax.ShapeDtypeStruct((N_BLOCKS * EXPERT_ROWS * ROW_WORDS, LANES), I32),
        grid_spec=pltpu.PrefetchScalarGridSpec(
            num_scalar_prefetch=2,
            grid=(N_BLOCKS,),
            in_specs=[
                pl.BlockSpec((EXPERT_ROWS * ROW_WORDS, LANES), row_map),
                pl.BlockSpec((1, D_MODEL, EXPERT_FF), w_map),
                pl.BlockSpec((1, D_MODEL, EXPERT_FF), w_map),
                pl.BlockSpec((1, EXPERT_FF, D_MODEL), w_map),
            ],
            out_specs=pl.BlockSpec((EXPERT_ROWS * ROW_WORDS, LANES), row_map),
            scratch_shapes=[
                pltpu.VMEM((D_MODEL, EXPERT_FF), BF16),
                pltpu.VMEM((D_MODEL, EXPERT_FF), BF16),
                pltpu.VMEM((EXPERT_FF, D_MODEL), BF16),
            ],
        ),
        compiler_params=_cparams(("arbitrary",)),
        name="moe_experts",
    )(blk_e, nused, xs, w_gate, w_up, w_down)


def _shared_kernel(t_ref, wg_ref, wu_ref, wd_ref, o_ref):
    x = _load_packed(t_ref, o_ref.shape[0], BF16)
    g = jnp.dot(x, wg_ref[...], preferred_element_type=F32)
    u = jnp.dot(x, wu_ref[...], preferred_element_type=F32)
    o_ref[...] = jnp.dot((_silu(g) * u).astype(BF16), wd_ref[...], preferred_element_type=F32)


def shared_expert(tokens, wg, wu, wd):
    tm = 512
    t = tokens.shape[0] // ROW_WORDS
    return pl.pallas_call(
        _shared_kernel,
        out_shape=jax.ShapeDtypeStruct((t, D_MODEL), F32),
        grid=(t // tm,),
        in_specs=[
            pl.BlockSpec((tm * ROW_WORDS, LANES), lambda i: (i, 0)),
            pl.BlockSpec((D_MODEL, SHARED_FF), lambda i: (0, 0)),
            pl.BlockSpec((D_MODEL, SHARED_FF), lambda i: (0, 0)),
            pl.BlockSpec((SHARED_FF, D_MODEL), lambda i: (0, 0)),
        ],
        out_specs=pl.BlockSpec((tm, D_MODEL), lambda i: (i, 0)),
        compiler_params=_cparams(("arbitrary",)),
        name="shared_expert",
    )(tokens, wg, wu, wd)


def _combine_kernel(x_ref, sh_ref, mod_ref, wk_ref, gf_ref, dest_hbm, ys_hbm, o_ref,
                    dest_smem, buf_ref, sem_idx, sem_g, *, final):
    i = pl.program_id(0)
    tm = x_ref.shape[0]
    idx_copy = pltpu.make_async_copy(dest_hbm.at[i], dest_smem, sem_idx)
    idx_copy.start()
    idx_copy.wait()

    def gather(r, carry):
        for k in range(TOP_K):
            _row_copy(ys_hbm, dest_smem[k * tm + r], buf_ref.at[k], r, sem_g).start()
        return carry

    lax.fori_loop(0, tm, gather, 0)

    def drain(r, carry):
        for k in range(TOP_K):
            _row_copy(ys_hbm, dest_smem[k * tm + r], buf_ref.at[k], r, sem_g).wait()
        return carry

    lax.fori_loop(0, tm, drain, 0)
    los, his = [], []
    for s in range(ROW_WORDS):
        acc_lo = sh_ref[:, s * LANES:(s + 1) * LANES]
        acc_hi = sh_ref[:, HALF + s * LANES:HALF + (s + 1) * LANES]
        for k in range(TOP_K):
            lo, hi = _unpack_words(buf_ref[k, pl.ds(s, tm, stride=ROW_WORDS), :])
            w = wk_ref[:, k:k + 1]
            acc_lo = acc_lo + lo * w
            acc_hi = acc_hi + hi * w
        los.append(acc_lo)
        his.append(acc_hi)
    f = jnp.concatenate(los + his, axis=-1)
    out = x_ref[...] + mod_ref[0, 5:6, :] * f
    if final:
        y = out * lax.rsqrt(jnp.mean(out * out, axis=-1, keepdims=True) + EPS)
        out = y * gf_ref[...]
    o_ref[...] = out


def combine(x, shared, mod, wk_t, g_final, dest_tiles, ys, *, ntiles, final):
    tm = 128
    tps = SEQ // tm
    return pl.pallas_call(
        functools.partial(_combine_kernel, final=final),
        out_shape=jax.ShapeDtypeStruct((ntiles * tm, D_MODEL), F32),
        grid=(ntiles,),
        in_specs=[
            pl.BlockSpec((tm, D_MODEL), lambda i: (i, 0)),
            pl.BlockSpec((tm, D_MODEL), lambda i: (i, 0)),
            pl.BlockSpec((1, 6, D_MODEL), lambda i: (_mod_row(i, tps), 0, 0)),
            pl.BlockSpec((tm, TOP_K), lambda i: (i, 0)),
            pl.BlockSpec((1, D_MODEL), lambda i: (0, 0)),
            pl.BlockSpec(memory_space=pl.ANY),
            pl.BlockSpec(memory_space=pl.ANY),
        ],
        out_specs=pl.BlockSpec((tm, D_MODEL), lambda i: (i, 0)),
        scratch_shapes=[
            pltpu.SMEM((TOP_K * tm,), I32),
            pltpu.VMEM((TOP_K, tm * ROW_WORDS, LANES), I32),
            pltpu.SemaphoreType.DMA,
            pltpu.SemaphoreType.DMA,
        ],
        compiler_params=_cparams(("arbitrary",)),
        name="moe_combine",
    )(x, shared, mod, wk_t, g_final.reshape(1, D_MODEL), dest_tiles, ys)


def _routing_plan(eidx):
    t = eidx.shape[1]
    onehot = (eidx[None, :, :] == jnp.arange(N_EXPERTS, dtype=I32)[:, None, None])
    mask = jnp.any(onehot, axis=1).astype(I32)
    incl = jnp.cumsum(mask, axis=1)
    counts = incl[:, -1]
    rank = incl - mask
    pcounts = (counts + EXPERT_ROWS - 1) // EXPERT_ROWS * EXPERT_ROWS
    pends = jnp.cumsum(pcounts)
    pstarts = pends - pcounts
    slot = pstarts[:, None] + rank
    dest = jnp.sum(jnp.where(onehot, slot[:, None, :], 0), axis=0)
    nused = (pends[-1] // EXPERT_ROWS).astype(I32).reshape(1)
    first_row = jnp.arange(N_BLOCKS, dtype=I32) * EXPERT_ROWS
    blk_e = jnp.minimum(jnp.sum((pends[None, :] <= first_row[:, None]).astype(I32), axis=1),
                        N_EXPERTS - 1).astype(I32)
    return dest.astype(I32), pends.astype(I32), blk_e, nused


def _tile_major(dest, tm):
    k, t = dest.shape
    return dest.reshape(k, t // tm, tm).transpose(1, 0, 2).reshape(t // tm, k * tm)


def _rope_tables(n_rows):
    row = jnp.repeat(jnp.arange(n_rows, dtype=F32), GRID_W)
    col = jnp.tile(jnp.arange(GRID_W, dtype=F32), n_rows)
    half = QK_ROPE // 2
    inv = ROPE_BASE ** (-jnp.arange(0, half, 2, dtype=F32) / half)
    ang_r = row[:, None] * inv
    ang_c = col[:, None] * inv
    ang = jnp.concatenate([ang_r, ang_r, ang_c, ang_c], axis=-1)
    n = ang.shape[0]
    ones = jnp.ones((n, QK_NOPE), F32)
    zeros = jnp.zeros((n, QK_NOPE), F32)
    tail = jnp.zeros((n, HEAD_PAD - QK_NOPE - QK_ROPE), F32)
    cos = jnp.concatenate([ones, jnp.cos(ang), tail], axis=-1)
    sin = jnp.concatenate([zeros, jnp.sin(ang), tail], axis=-1)
    return cos, sin


def _identity_tables(n):
    cos = jnp.concatenate([jnp.ones((n, QK_NOPE + QK_ROPE), F32),
                           jnp.zeros((n, HEAD_PAD - QK_NOPE - QK_ROPE), F32)], axis=-1)
    return cos, jnp.zeros((n, HEAD_PAD), F32)


def _rotate_cols(w):
    i = np.arange(QK_ROPE)
    first = (i % (QK_ROPE // 2)) < (QK_ROPE // 4)
    perm = np.where(first, i + QK_ROPE // 4, i - QK_ROPE // 4)
    sign = np.where(first, -1.0, 1.0).astype(np.float32)
    return w[:, perm] * sign


def _mla_weights(g_cq, w_uq, g_ckv, w_ukv):
    dk = QK_NOPE + QK_ROPE
    wq = w_uq.reshape(Q_LORA, MLA_HEADS, dk)
    pad = jnp.zeros((Q_LORA, MLA_HEADS, HEAD_PAD - dk), F32)
    zero_nope = jnp.zeros((Q_LORA, MLA_HEADS, QK_NOPE), F32)
    wq_rot = _rotate_cols(wq[..., QK_NOPE:].reshape(Q_LORA * MLA_HEADS, QK_ROPE)).reshape(
        Q_LORA, MLA_HEADS, QK_ROPE)
    wq_a = jnp.concatenate([wq, pad], axis=-1).reshape(Q_LORA, -1)
    wq_b = jnp.concatenate([zero_nope, wq_rot, pad], axis=-1).reshape(Q_LORA, -1)
    wkv = w_ukv.reshape(KV_LORA, MLA_HEADS, QK_NOPE + V_HEAD)
    wk = jnp.concatenate([wkv[..., :QK_NOPE],
                          jnp.zeros((KV_LORA, MLA_HEADS, HEAD_PAD - QK_NOPE), F32)], axis=-1)
    wv = wkv[..., QK_NOPE:]
    eye = jnp.eye(QK_ROPE, dtype=F32)
    place = jnp.zeros((LANES, MLA_HEADS, HEAD_PAD), F32)
    place_a = place.at[:QK_ROPE, :, QK_NOPE:dk].set(jnp.broadcast_to(eye[:, None, :], (QK_ROPE, MLA_HEADS, QK_ROPE)))
    rot = _rotate_cols(eye)
    place_b = place.at[:QK_ROPE, :, QK_NOPE:dk].set(jnp.broadcast_to(rot[:, None, :], (QK_ROPE, MLA_HEADS, QK_ROPE)))
    return {
        "g_cq": g_cq.reshape(1, -1), "g_ckv": g_ckv.reshape(1, -1),
        "wq2t": jnp.concatenate([wq_a, wq_b], axis=-1).T.astype(BF16),
        "wk": wk.reshape(KV_LORA, -1).astype(BF16),
        "wvt": wv.reshape(KV_LORA, -1).T.astype(BF16),
        "pk": jnp.concatenate([place_a.reshape(LANES, -1), place_b.reshape(LANES, -1)], axis=-1).astype(BF16),
    }


def _pack_w_in(w):
    gates0 = MLA_IN + 2 * CONV_CH + FFT_CH + POOL_CH
    pad = jnp.zeros((D_MODEL, P_MLA_W - MLA_IN), w.dtype)
    return jnp.concatenate([w[:, gates0:], w[:, :MLA_IN], pad, w[:, MLA_IN:gates0]], axis=-1).astype(BF16)


def _dft_tables(n):
    gw = FFT_CH // FFT_GROUPS

    def angles(m):
        j = jnp.arange(m, dtype=I32)
        return (j[:, None] * j[None, :] % m).astype(F32) * (2.0 * np.pi / m)

    ac = angles(gw)
    an = angles(n)
    cs = jnp.concatenate([jnp.cos(ac), jnp.sin(ac)], axis=-1).astype(BF16)
    mseq = jnp.concatenate([jnp.cos(an), -jnp.sin(an)], axis=-1).astype(BF16)
    return cs, mseq


def _mixers(p, row0, nseq, n, wts, tables, conv_args, pool_args, dft):
    q4, k4, v = mla_prep(p, wts, tables, row0=row0, col0=P_MLA, nseq=nseq, n=n)
    cv = conv_branch(p, *conv_args, row0=row0, nseq=nseq, n=n)
    ff = fft_branch(p, dft[0], dft[1], row0=row0, nseq=nseq, n=n)
    po = pool_branch(p, *pool_args, row0=row0, nseq=nseq, n=n)
    return q4, k4, v, cv, ff, po


def kernel(x, c, ctx, c_ctx, w_mod, b_mod, g_norm1, g_norm2, w_in, g_cq, w_uq, g_ckv, w_ukv, w_o_mla, conv_w, conv_b, conv_ln_g, conv_ln_b, w_o_conv, w_o_fft, pool_w, pool_scale, w_o_pool, w_out, w_router, router_bias, w_exp_gate, w_exp_up, w_exp_down, w_sh_gate, w_sh_up, w_sh_down, g_final):
    xa = jnp.concatenate([x.reshape(T_LAT, D_MODEL), ctx.reshape(T_CTX, D_MODEL)], axis=0)
    cvec = jnp.concatenate([c, c_ctx[None, :], jnp.zeros((SUBLANES - BATCH - 1, D_MODEL), F32)], axis=0)
    mod_all = modulation_all(cvec, w_mod, b_mod).reshape(DEPTH, SUBLANES, 6, D_MODEL)
    rope_lat = _rope_tables(SEQ // GRID_W)
    rope_ctx = _identity_tables(CTX_LEN)
    dft_lat = _dft_tables(SEQ)
    dft_ctx = _dft_tables(CTX_LEN)
    lat_tiles = T_LAT // ROW_TILE
    all_tiles = T_ALL // ROW_TILE

    for l in range(DEPTH):
        last = l == DEPTH - 1
        mod = mod_all[l]
        w_in_p = _pack_w_in(w_in[l])
        wts = _mla_weights(g_cq[l], w_uq[l], g_ckv[l], w_ukv[l])
        conv_args = (conv_w[l], conv_b[l], conv_ln_g[l], conv_ln_b[l])
        pool_args = (pool_w[l].astype(BF16), pool_scale[l])
        wo4 = jnp.stack([w_o_mla[l], w_o_conv[l], w_o_fft[l], w_o_pool[l]], axis=0).astype(BF16)

        if last:
            p = normproj(xa, g_norm1[l], mod, w_in_p, tile0=0, ntiles=lat_tiles, col0=0, ncols=P_COLS)
            p_ctx = normproj(xa, g_norm1[l], mod, w_in_p, tile0=lat_tiles, ntiles=all_tiles - lat_tiles,
                             col0=P_MLA, ncols=P_MLA_W)
        else:
            p = normproj(xa, g_norm1[l], mod, w_in_p, tile0=0, ntiles=all_tiles, col0=0, ncols=P_COLS)
        q4, k4, v, cv, ff, po = _mixers(p, 0, BATCH, SEQ, wts, rope_lat, conv_args, pool_args, dft_lat)
        if last:
            qc, kc, vc = mla_prep(p_ctx, wts, rope_ctx, row0=0, col0=0, nseq=BATCH, n=CTX_LEN)
        else:
            qc, kc, vc, cvc, ffc, poc = _mixers(p, T_LAT, BATCH, CTX_LEN, wts, rope_ctx,
                                                conv_args, pool_args, dft_ctx)
        k_all = jnp.concatenate([k4, kc], axis=2)
        v_all = jnp.concatenate([v, vc], axis=2)
        at = attention(q4, k_all, v_all, nq=SEQ, nk=SEQ + CTX_LEN)
        if last:
            ntok = T_LAT
        else:
            atc = attention(qc, kc, vc, nq=CTX_LEN, nk=CTX_LEN)
            at = jnp.concatenate([at, atc], axis=0)
            cv = jnp.concatenate([cv, cvc], axis=0)
            ff = jnp.concatenate([ff, ffc], axis=0)
            po = jnp.concatenate([po, poc], axis=0)
            ntok = T_ALL
        xa = merge(xa, mod, at, cv, ff, po, p, wo4, w_out[l].astype(BF16), ntiles=ntok // 256)

        wr_t = w_router[l].T
        wr_hi = wr_t.astype(BF16)
        wr_lo = (wr_t - wr_hi.astype(F32)).astype(BF16)
        tokens, eidx, wk = norm_router(xa, g_norm2[l], mod, wr_hi, wr_lo, router_bias[l], ntiles=ntok // 256)
        dest, pends, blk_e, nused = _routing_plan(eidx)
        xs = dispatch(tokens, _tile_major(dest, 256), pends, ntiles=ntok // 256)
        ys = experts(xs, blk_e, nused, w_exp_gate, w_exp_up, w_exp_down, layer=l)
        sh = shared_expert(tokens, w_sh_gate[l].astype(BF16), w_sh_up[l].astype(BF16),
                           w_sh_down[l].astype(BF16))
        xa = combine(xa, sh, mod, wk.T, g_final, _tile_major(dest, 128), ys, ntiles=ntok // 128, final=last)
    return xa.reshape(BATCH, SEQ, D_MODEL)
```

```python
import functools

import jax
import jax.numpy as jnp
import numpy as np
from jax import lax
from jax.experimental import pallas as pl
from jax.experimental.pallas import tpu as pltpu

F32 = jnp.float32
BF16 = jnp.bfloat16
I32 = jnp.int32

D_MODEL = 2048
BATCH = 4
SEQ = 4096
DEPTH = 2
GRID_W = 64
CTX_LEN = 256
MLA_HEADS = 8
QK_NOPE = 64
QK_ROPE = 32
V_HEAD = 64
Q_LORA = 512
KV_LORA = 256
ROPE_BASE = 10000.0
CONV_CH = 512
CONV_WIDTH = 31
FFT_CH = 512
FFT_GROUPS = 4
POOL_CH = 512
POOL_WINDOWS = (2, 4, 8, 16)
POOL_GROUP = POOL_CH // len(POOL_WINDOWS)
N_BRANCHES = 4
N_EXPERTS = 64
N_EXPERT_GROUPS = 8
TOPK_GROUPS = 4
TOP_K = 8
EXPERT_FF = 512
SHARED_FF = 512
ROUTED_SCALE = 2.5
EPS = 1e-6

MLA_IN = Q_LORA + KV_LORA + QK_ROPE
T_LAT = BATCH * SEQ
T_CTX = BATCH * CTX_LEN
T_ALL = T_LAT + T_CTX

LANES = 128
SUBLANES = 8
VMEM_LIMIT_BYTES = 56 * 1024 * 1024

P_GATE = 0
P_MLA = N_BRANCHES * D_MODEL
P_MLA_W = 1024
P_CONV = P_MLA + P_MLA_W
P_FFT = P_CONV + 2 * CONV_CH
P_POOL = P_FFT + FFT_CH
P_COLS = P_POOL + POOL_CH
HEAD_PAD = 128
ATT_SCALE = (QK_NOPE + QK_ROPE) ** -0.5
Q_SCALE = ATT_SCALE * float(np.log2(np.e))

ROW_TILE = 1024
EXPERT_ROWS = 512
N_BLOCKS = -(-(T_ALL * TOP_K) // EXPERT_ROWS) + N_EXPERTS


def _cparams(sem, vmem=VMEM_LIMIT_BYTES):
    return pltpu.CompilerParams(dimension_semantics=sem, vmem_limit_bytes=vmem)


def _sigmoid(x):
    return 1.0 / (1.0 + jnp.exp(-x))


def _silu(x):
    return x * _sigmoid(x)


def _mod_row(tile, tiles_per_seq):
    return jnp.minimum(tile // tiles_per_seq, BATCH)


def _modulation_kernel(c_ref, w_ref, b_ref, o_ref):
    c = c_ref[...]
    s = _silu(c).astype(BF16)
    o_ref[0] = jnp.dot(s, w_ref[0].astype(BF16), preferred_element_type=F32) + b_ref[0]


def modulation_all(cvec, w_mod, b_mod):
    tn = 1024
    n = 6 * D_MODEL
    return pl.pallas_call(
        _modulation_kernel,
        out_shape=jax.ShapeDtypeStruct((DEPTH, SUBLANES, n), F32),
        grid=(DEPTH, n // tn),
        in_specs=[
            pl.BlockSpec((SUBLANES, D_MODEL), lambda l, j: (0, 0)),
            pl.BlockSpec((1, D_MODEL, tn), lambda l, j: (l, 0, j)),
            pl.BlockSpec((1, 1, tn), lambda l, j: (l, 0, j)),
        ],
        out_specs=pl.BlockSpec((1, SUBLANES, tn), lambda l, j: (l, 0, j)),
        compiler_params=_cparams(("arbitrary", "arbitrary")),
        name="modulation",
    )(cvec, w_mod, b_mod.reshape(DEPTH, 1, n))


def _adaln(x, g, shift, scale):
    y = x * lax.rsqrt(jnp.mean(x * x, axis=-1, keepdims=True) + EPS)
    return (y * g) * (1.0 + scale) + shift


def _normproj_kernel(x_ref, g_ref, mod_ref, w_ref, o_ref, h_ref, *, shift_row, chunk):
    @pl.when(pl.program_id(1) == 0)
    def _():
        shift = mod_ref[0, shift_row:shift_row + 1, :]
        scale = mod_ref[0, shift_row + 1:shift_row + 2, :]
        for r in range(0, x_ref.shape[0], chunk):
            h = _adaln(x_ref[r:r + chunk, :], g_ref[...], shift, scale)
            h_ref[r:r + chunk, :] = h.astype(BF16)

    o_ref[...] = jnp.dot(h_ref[...], w_ref[...], preferred_element_type=F32).astype(o_ref.dtype)


def normproj(x, g, mod, w, *, tile0, ntiles, col0, ncols, shift_row=0):
    tm, tn = ROW_TILE, 1024
    tps = SEQ // tm
    cb0 = col0 // tn
    return pl.pallas_call(
        functools.partial(_normproj_kernel, shift_row=shift_row, chunk=256),
        out_shape=jax.ShapeDtypeStruct((ntiles * tm, ncols), BF16),
        grid=(ntiles, ncols // tn),
        in_specs=[
            pl.BlockSpec((tm, D_MODEL), lambda i, j: (i + tile0, 0)),
            pl.BlockSpec((1, D_MODEL), lambda i, j: (0, 0)),
            pl.BlockSpec((1, 6, D_MODEL), lambda i, j: (_mod_row(i + tile0, tps), 0, 0)),
            pl.BlockSpec((D_MODEL, tn), lambda i, j: (0, j + cb0)),
        ],
        out_specs=pl.BlockSpec((tm, tn), lambda i, j: (i, j)),
        scratch_shapes=[pltpu.VMEM((tm, D_MODEL), BF16)],
        compiler_params=_cparams(("arbitrary", "arbitrary")),
        name="normproj",
    )(x, g.reshape(1, D_MODEL), mod, w)


def _rms_gain(x, g):
    y = x * lax.rsqrt(jnp.mean(x * x, axis=-1, keepdims=True) + EPS)
    return (y * g).astype(BF16)


_NT = (((1,), (1,)), ((), ()))


def _mla_prep_kernel(p_ref, gq_ref, gkv_ref, wqt_ref, wk_ref, wvt_ref, pk_ref, cos_ref, sin_ref,
                     cost_ref, sint_ref, qt_ref, k_ref, vt_ref):
    hw = MLA_HEADS * HEAD_PAD
    cqn = _rms_gain(p_ref[:, :Q_LORA].astype(F32), gq_ref[...])
    ckvn = _rms_gain(p_ref[:, Q_LORA:Q_LORA + KV_LORA].astype(F32), gkv_ref[...])
    kr = p_ref[:, Q_LORA + KV_LORA:Q_LORA + KV_LORA + LANES]
    q2t = lax.dot_general(wqt_ref[...], cqn, _NT, preferred_element_type=F32)
    vt_ref[0] = lax.dot_general(wvt_ref[...], ckvn, _NT, preferred_element_type=F32).astype(BF16)
    kk = jnp.dot(ckvn, wk_ref[...], preferred_element_type=F32)
    kr2 = jnp.dot(kr, pk_ref[...], preferred_element_type=F32)
    cos, sin = cos_ref[...], sin_ref[...]
    cost, sint = cost_ref[...], sint_ref[...]
    for h in range(MLA_HEADS):
        a = slice(h * HEAD_PAD, (h + 1) * HEAD_PAD)
        b = slice(hw + h * HEAD_PAD, hw + (h + 1) * HEAD_PAD)
        qt_ref[0, h] = ((q2t[a, :] * cost + q2t[b, :] * sint) * Q_SCALE).astype(BF16)
        k_ref[0, h] = (kk[:, a] + kr2[:, a] * cos + kr2[:, b] * sin).astype(BF16)


def mla_prep(p, wts, tables, *, row0, col0, nseq, n):
    tm = 256
    tps = n // tm
    hw = MLA_HEADS * HEAD_PAD
    vw = MLA_HEADS * V_HEAD
    blk0 = row0 // tm
    cblk = col0 // P_MLA_W
    cos_t, sin_t = tables
    const = lambda i: (0, 0)
    return pl.pallas_call(
        _mla_prep_kernel,
        out_shape=(jax.ShapeDtypeStruct((nseq, MLA_HEADS, HEAD_PAD, n), BF16),
                   jax.ShapeDtypeStruct((nseq, MLA_HEADS, n, HEAD_PAD), BF16),
                   jax.ShapeDtypeStruct((nseq, vw, n), BF16)),
        grid=(nseq * tps,),
        in_specs=[
            pl.BlockSpec((tm, P_MLA_W), lambda i: (i + blk0, cblk)),
            pl.BlockSpec((1, Q_LORA), const),
            pl.BlockSpec((1, KV_LORA), const),
            pl.BlockSpec((2 * hw, Q_LORA), const),
            pl.BlockSpec((KV_LORA, hw), const),
            pl.BlockSpec((vw, KV_LORA), const),
            pl.BlockSpec((LANES, 2 * hw), const),
            pl.BlockSpec((tm, HEAD_PAD), lambda i: (i % tps, 0)),
            pl.BlockSpec((tm, HEAD_PAD), lambda i: (i % tps, 0)),
            pl.BlockSpec((HEAD_PAD, tm), lambda i: (0, i % tps)),
            pl.BlockSpec((HEAD_PAD, tm), lambda i: (0, i % tps)),
        ],
        out_specs=(pl.BlockSpec((1, MLA_HEADS, HEAD_PAD, tm), lambda i: (i // tps, 0, 0, i % tps)),
                   pl.BlockSpec((1, MLA_HEADS, tm, HEAD_PAD), lambda i: (i // tps, 0, i % tps, 0)),
                   pl.BlockSpec((1, vw, tm), lambda i: (i // tps, 0, i % tps))),
        compiler_params=_cparams(("arbitrary",)),
        name="mla_prep",
    )(p, wts["g_cq"], wts["g_ckv"], wts["wq2t"], wts["wk"], wts["wvt"], wts["pk"],
      cos_t, sin_t, cos_t.T, sin_t.T)


def _attn_kernel(qt_ref, k_ref, vt_ref, o_ref, s_ref, p_ref, *, nk, ck):
    tq = qt_ref.shape[3]
    chunks = [slice(c, c + ck) for c in range(0, nk, ck)]
    outs = []
    for h in range(2):
        qt = qt_ref[0, h]
        m = None
        for c in chunks:
            s = jnp.dot(k_ref[0, h, c, :], qt, preferred_element_type=F32)
            s_ref[h, c, :] = s
            mc = jnp.max(s, axis=0, keepdims=True)
            m = mc if m is None else jnp.maximum(m, mc)
        l = jnp.zeros((1, tq), F32)
        for c in chunks:
            p = jnp.exp2(s_ref[h, c, :] - m)
            l = l + jnp.sum(p, axis=0, keepdims=True)
            p_ref[h, c, :] = p.astype(BF16)
        ot = jnp.dot(vt_ref[0, h * V_HEAD:(h + 1) * V_HEAD, :], p_ref[h], preferred_element_type=F32)
        outs.append(ot / l)
    o_ref[...] = jnp.concatenate(outs, axis=0).T.astype(o_ref.dtype)


def attention(qt4, k4, vt3, *, nq, nk):
    tq = 256
    ck = 544 if nk % 544 == 0 else nk
    nqt = nq // tq
    return pl.pallas_call(
        functools.partial(_attn_kernel, nk=nk, ck=ck),
        out_shape=jax.ShapeDtypeStruct((BATCH * nq, MLA_HEADS * V_HEAD), BF16),
        grid=(BATCH, MLA_HEADS // 2, nqt),
        in_specs=[
            pl.BlockSpec((1, 2, HEAD_PAD, tq), lambda b, hp, i: (b, hp, 0, i)),
            pl.BlockSpec((1, 2, nk, HEAD_PAD), lambda b, hp, i: (b, hp, 0, 0)),
            pl.BlockSpec((1, 2 * V_HEAD, nk), lambda b, hp, i: (b, hp, 0)),
        ],
        out_specs=pl.BlockSpec((tq, 2 * V_HEAD), lambda b, hp, i: (b * nqt + i, hp)),
        scratch_shapes=[pltpu.VMEM((2, nk, tq), F32), pltpu.VMEM((2, nk, tq), BF16)],
        compiler_params=_cparams(("arbitrary", "arbitrary", "arbitrary")),
        name="attention",
    )(qt4, k4, vt3)


CONV_PAD = 16


def _conv_kernel(p_ref, w_ref, cb_ref, lg_ref, lb_ref, o_ref, u_ref, *, n, tr):
    zeros = jnp.zeros((CONV_PAD, CONV_CH), F32)
    u_ref[0:CONV_PAD, :] = zeros
    u_ref[CONV_PAD + n:2 * CONV_PAD + n, :] = zeros

    def glu(i, carry):
        r = pl.multiple_of(i * tr, tr)
        a = p_ref[pl.ds(r, tr), 0:CONV_CH].astype(F32)
        b = p_ref[pl.ds(r, tr), CONV_CH:2 * CONV_CH].astype(F32)
        u_ref[pl.ds(CONV_PAD + r, tr), :] = a * _sigmoid(b)
        return carry

    lax.fori_loop(0, n // tr, glu, 0)
    off = CONV_PAD - CONV_WIDTH // 2
    win_rows = tr + 2 * CONV_PAD

    def conv(i, carry):
        r = pl.multiple_of(i * tr, tr)
        strips = []
        for c0 in range(0, CONV_CH, LANES):
            win = u_ref[pl.ds(r, win_rows), c0:c0 + LANES]
            acc = jnp.zeros((tr, LANES), F32)
            for b in range(SUBLANES):
                wb = pltpu.roll(win, win_rows - (off + b), axis=0)
                for k in range(b, CONV_WIDTH, SUBLANES):
                    acc = acc + wb[k - b:k - b + tr, :] * w_ref[k:k + 1, c0:c0 + LANES]
            strips.append(acc)
        acc = jnp.concatenate(strips, axis=-1) + cb_ref[...]
        mu = jnp.mean(acc, axis=-1, keepdims=True)
        d = acc - mu
        var = jnp.mean(d * d, axis=-1, keepdims=True)
        y = (d * lax.rsqrt(var + EPS)) * lg_ref[...] + lb_ref[...]
        o_ref[pl.ds(r, tr), :] = _silu(y).astype(o_ref.dtype)
        return carry

    lax.fori_loop(0, n // tr, conv, 0)


def conv_branch(p, conv_w, conv_b, ln_g, ln_b, *, row0, nseq, n):
    tr = 128
    blk0 = row0 // n
    const = lambda s: (0, 0)
    return pl.pallas_call(
        functools.partial(_conv_kernel, n=n, tr=tr),
        out_shape=jax.ShapeDtypeStruct((nseq * n, CONV_CH), BF16),
        grid=(nseq,),
        in_specs=[
            pl.BlockSpec((n, 2 * CONV_CH), lambda s: (s + blk0, P_CONV // (2 * CONV_CH))),
            pl.BlockSpec((CONV_WIDTH, CONV_CH), const),
            pl.BlockSpec((1, CONV_CH), const),
            pl.BlockSpec((1, CONV_CH), const),
            pl.BlockSpec((1, CONV_CH), const),
        ],
        out_specs=pl.BlockSpec((n, CONV_CH), lambda s: (s, 0)),
        scratch_shapes=[pltpu.VMEM((n + 2 * CONV_PAD, CONV_CH), F32)],
        compiler_params=_cparams(("arbitrary",)),
        name="conv_branch",
    )(p, conv_w, conv_b.reshape(1, -1), ln_g.reshape(1, -1), ln_b.reshape(1, -1))


def _fft_kernel(z_ref, cs_ref, m_ref, o_ref, ab_ref, *, n, tc):
    gw = FFT_CH // FFT_GROUPS

    @pl.when(pl.program_id(1) == 0)
    def _():
        def chan(i, carry):
            r = pl.multiple_of(i * tc, tc)
            for g in range(FFT_GROUPS):
                zg = z_ref[pl.ds(r, tc), g * gw:(g + 1) * gw]
                ab = jnp.dot(zg, cs_ref[...], preferred_element_type=F32)
                ab_ref[pl.ds(r, tc), g * gw:(g + 1) * gw] = ab[:, :gw].astype(BF16)
                ab_ref[pl.ds(n + r, tc), g * gw:(g + 1) * gw] = ab[:, gw:].astype(BF16)
            return carry

        lax.fori_loop(0, n // tc, chan, 0)

    norm = 1.0 / float(np.sqrt(n * gw))
    o_ref[...] = (jnp.dot(m_ref[...], ab_ref[...], preferred_element_type=F32) * norm).astype(o_ref.dtype)


def fft_branch(p, cs, mseq, *, row0, nseq, n):
    tm = min(n, 512)
    tc = min(n, 512)
    blk0 = row0 // n
    nt = n // tm
    return pl.pallas_call(
        functools.partial(_fft_kernel, n=n, tc=tc),
        out_shape=jax.ShapeDtypeStruct((nseq * n, FFT_CH), BF16),
        grid=(nseq, nt),
        in_specs=[
            pl.BlockSpec((n, FFT_CH), lambda s, i: (s + blk0, P_FFT // FFT_CH)),
            pl.BlockSpec((FFT_CH // FFT_GROUPS, 2 * FFT_CH // FFT_GROUPS), lambda s, i: (0, 0)),
            pl.BlockSpec((tm, 2 * n), lambda s, i: (i, 0)),
        ],
        out_specs=pl.BlockSpec((tm, FFT_CH), lambda s, i: (s * nt + i, 0)),
        scratch_shapes=[pltpu.VMEM((2 * n, FFT_CH), BF16)],
        compiler_params=_cparams(("arbitrary", "arbitrary")),
        name="fft_branch",
    )(p, cs, mseq)


POOL_PAD = 16


def _pool_kernel(z_ref, pw_ref, ps_ref, o_ref, zp_ref, d_ref, *, n, tr, tc):
    zeros = jnp.zeros((POOL_PAD, POOL_CH), F32)
    zp_ref[0:POOL_PAD, :] = zeros
    zp_ref[POOL_PAD + n:2 * POOL_PAD + n, :] = zeros

    def fill(i, carry):
        r = pl.multiple_of(i * tc, tc)
        zp_ref[pl.ds(POOL_PAD + r, tc), :] = z_ref[pl.ds(r, tc), :].astype(F32)
        return carry

    lax.fori_loop(0, n // tc, fill, 0)

    win_rows = tr + 2 * POOL_PAD

    def pool(i, carry):
        r = pl.multiple_of(i * tr, tr)
        t = (r + lax.broadcasted_iota(I32, (tr, POOL_GROUP), 0)).astype(F32)
        for gi, w in enumerate(POOL_WINDOWS):
            cols = slice(gi * POOL_GROUP, (gi + 1) * POOL_GROUP)
            win = zp_ref[pl.ds(r, win_rows), cols]
            s = jnp.zeros((tr, POOL_GROUP), F32)
            for j in range(w):
                start = POOL_PAD - w // 2 + j
                if start % SUBLANES == 0:
                    s = s + win[start:start + tr, :]
                else:
                    s = s + pltpu.roll(win, win_rows - start, axis=0)[0:tr, :]
            lo = jnp.maximum(t - (w // 2), 0.0)
            hi = jnp.minimum(t - (w // 2) + w, float(n))
            z = win[POOL_PAD:POOL_PAD + tr, :]
            d_ref[pl.ds(r, tr), cols] = (s / (hi - lo) - z).astype(BF16)
        return carry

    lax.fori_loop(0, n // tr, pool, 0)

    def proj(i, carry):
        r = pl.multiple_of(i * tc, tc)
        for gi in range(len(POOL_WINDOWS)):
            cols = slice(gi * POOL_GROUP, (gi + 1) * POOL_GROUP)
            y = jnp.dot(d_ref[pl.ds(r, tc), cols], pw_ref[gi], preferred_element_type=F32)
            o_ref[pl.ds(r, tc), cols] = (y * ps_ref[:, cols]).astype(o_ref.dtype)
        return carry

    lax.fori_loop(0, n // tc, proj, 0)


def pool_branch(p, pool_w, pool_scale, *, row0, nseq, n):
    tr = 128
    tc = min(n, 512)
    blk0 = row0 // n
    return pl.pallas_call(
        functools.partial(_pool_kernel, n=n, tr=tr, tc=tc),
        out_shape=jax.ShapeDtypeStruct((nseq * n, POOL_CH), BF16),
        grid=(nseq,),
        in_specs=[
            pl.BlockSpec((n, POOL_CH), lambda s: (s + blk0, P_POOL // POOL_CH)),
            pl.BlockSpec((len(POOL_WINDOWS), POOL_GROUP, POOL_GROUP), lambda s: (0, 0, 0)),
            pl.BlockSpec((1, POOL_CH), lambda s: (0, 0)),
        ],
        out_specs=pl.BlockSpec((n, POOL_CH), lambda s: (s, 0)),
        scratch_shapes=[pltpu.VMEM((n + 2 * POOL_PAD, POOL_CH), F32),
                        pltpu.VMEM((n, POOL_CH), BF16)],
        compiler_params=_cparams(("arbitrary",)),
        name="pool_branch",
    )(p, pool_w, pool_scale.reshape(1, -1))


def _merge_kernel(x_ref, mod_ref, at_ref, cv_ref, ff_ref, po_ref, g_ref, wo_ref, wout_ref, o_ref):
    mix = None
    for b, br in enumerate((at_ref, cv_ref, ff_ref, po_ref)):
        y = jnp.dot(br[...], wo_ref[b], preferred_element_type=F32)
        gate = _sigmoid(g_ref[:, b * D_MODEL:(b + 1) * D_MODEL].astype(F32))
        mix = gate * y if mix is None else mix + gate * y
    out = jnp.dot(mix.astype(BF16), wout_ref[...], preferred_element_type=F32)
    o_ref[...] = x_ref[...] + mod_ref[0, 2:3, :] * out


def merge(x, mod, attn, conv, fft, pool, p, wo4, w_out, *, ntiles):
    tm = 256
    tps = SEQ // tm
    br_spec = pl.BlockSpec((tm, 512), lambda i: (i, 0))
    return pl.pallas_call(
        _merge_kernel,
        out_shape=jax.ShapeDtypeStruct((ntiles * tm, D_MODEL), F32),
        grid=(ntiles,),
        in_specs=[
            pl.BlockSpec((tm, D_MODEL), lambda i: (i, 0)),
            pl.BlockSpec((1, 6, D_MODEL), lambda i: (_mod_row(i, tps), 0, 0)),
            br_spec, br_spec, br_spec, br_spec,
            pl.BlockSpec((tm, N_BRANCHES * D_MODEL), lambda i: (i, P_GATE // (N_BRANCHES * D_MODEL))),
            pl.BlockSpec((N_BRANCHES, 512, D_MODEL), lambda i: (0, 0, 0), pipeline_mode=pl.Buffered(1)),
            pl.BlockSpec((D_MODEL, D_MODEL), lambda i: (0, 0), pipeline_mode=pl.Buffered(1)),
        ],
        out_specs=pl.BlockSpec((tm, D_MODEL), lambda i: (i, 0)),
        compiler_params=_cparams(("arbitrary",)),
        name="merge",
    )(x, mod, attn, conv, fft, pool, p, wo4, w_out)


def _first_index(hit_value, cand, ids, big):
    return jnp.min(jnp.where(cand == hit_value, ids, big), axis=0, keepdims=True)


HALF = D_MODEL // 2
ROW_WORDS = HALF // LANES


def _round_bf16_bits(x):
    u = lax.bitcast_convert_type(x, I32)
    odd = lax.shift_right_logical(u, 16) & 1
    return (u + 0x7FFF + odd) & jnp.int32(-65536)


def _pack_words(lo, hi):
    return lax.shift_right_logical(_round_bf16_bits(lo), 16) | _round_bf16_bits(hi)


def _unpack_words(w):
    lo = lax.bitcast_convert_type(lax.shift_left(w, 16), F32)
    hi = lax.bitcast_convert_type(w & jnp.int32(-65536), F32)
    return lo, hi


def _store_packed(ref, x):
    tm = x.shape[0]
    for s in range(ROW_WORDS):
        w = _pack_words(x[:, s * LANES:(s + 1) * LANES], x[:, HALF + s * LANES:HALF + (s + 1) * LANES])
        ref[pl.ds(s, tm, stride=ROW_WORDS), :] = w


def _load_packed(ref, tm, dtype):
    los, his = [], []
    for s in range(ROW_WORDS):
        lo, hi = _unpack_words(ref[pl.ds(s, tm, stride=ROW_WORDS), :])
        los.append(lo.astype(dtype))
        his.append(hi.astype(dtype))
    return jnp.concatenate(los + his, axis=-1)


def _router_kernel(x_ref, g_ref, mod_ref, wrh_ref, wrl_ref, rb_ref, tok_ref, eidx_ref, wk_ref):
    tm = x_ref.shape[0]
    h = _adaln(x_ref[...], g_ref[...], mod_ref[0, 3:4, :], mod_ref[0, 4:5, :])
    _store_packed(tok_ref, h)
    hh = h.astype(BF16)
    hl = (h - hh.astype(F32)).astype(BF16)
    nt = (((1,), (1,)), ((), ()))
    logits = (lax.dot_general(wrh_ref[...], hh, nt, preferred_element_type=F32)
              + lax.dot_general(wrh_ref[...], hl, nt, preferred_element_type=F32)
              + lax.dot_general(wrl_ref[...], hh, nt, preferred_element_type=F32))
    scores = _sigmoid(logits)
    sel = scores + rb_ref[...]
    per = N_EXPERTS // N_EXPERT_GROUPS
    assert per == SUBLANES and N_EXPERT_GROUPS == SUBLANES and TOP_K == SUBLANES
    neg = -jnp.inf
    sub = lax.broadcasted_iota(I32, (SUBLANES, tm), 0).astype(F32)
    sg = [sel[g * per:(g + 1) * per, :] for g in range(N_EXPERT_GROUPS)]
    sc = [scores[g * per:(g + 1) * per, :] for g in range(N_EXPERT_GROUPS)]
    gsc = jnp.zeros((SUBLANES, tm), F32)
    for g in range(N_EXPERT_GROUPS):
        m1 = jnp.max(sg[g], axis=0, keepdims=True)
        i1 = _first_index(m1, sg[g], sub, float(per))
        m2 = jnp.max(jnp.where(sub == i1, neg, sg[g]), axis=0, keepdims=True)
        gsc = jnp.where(sub == float(g), m1 + m2, gsc)
    gsel = jnp.zeros((SUBLANES, tm), F32)
    for _ in range(TOPK_GROUPS):
        m = jnp.max(gsc, axis=0, keepdims=True)
        hit = sub == _first_index(m, gsc, sub, float(N_EXPERT_GROUPS))
        gsel = jnp.where(hit, 1.0, gsel)
        gsc = jnp.where(hit, neg, gsc)
    cand = []
    for g in range(N_EXPERT_GROUPS):
        allowed = jnp.max(jnp.where(sub == float(g), gsel, 0.0), axis=0, keepdims=True)
        cand.append(jnp.where(allowed > 0.0, sg[g], neg))
    eid = [sub + float(g * per) for g in range(N_EXPERT_GROUPS)]
    idxs = jnp.zeros((SUBLANES, tm), F32)
    vals = jnp.zeros((SUBLANES, tm), F32)
    for k in range(TOP_K):
        m = functools.reduce(jnp.maximum, [jnp.max(c, axis=0, keepdims=True) for c in cand])
        idx = functools.reduce(
            jnp.minimum, [_first_index(m, cand[g], eid[g], float(N_EXPERTS)) for g in range(N_EXPERT_GROUPS)])
        val = jnp.zeros((1, tm), F32)
        for g in range(N_EXPERT_GROUPS):
            hit = eid[g] == idx
            val = val + jnp.sum(jnp.where(hit, sc[g], 0.0), axis=0, keepdims=True)
            cand[g] = jnp.where(hit, neg, cand[g])
        idxs = jnp.where(sub == float(k), idx, idxs)
        vals = jnp.where(sub == float(k), val, vals)
    eidx_ref[...] = idxs.astype(I32)
    wk_ref[...] = vals / jnp.sum(vals, axis=0, keepdims=True) * ROUTED_SCALE


def norm_router(x, g, mod, wr_hi, wr_lo, rbias, *, ntiles):
    tm = 256
    tps = SEQ // tm
    t = ntiles * tm
    return pl.pallas_call(
        _router_kernel,
        out_shape=(jax.ShapeDtypeStruct((t * ROW_WORDS, LANES), I32),
                   jax.ShapeDtypeStruct((TOP_K, t), I32),
                   jax.ShapeDtypeStruct((TOP_K, t), F32)),
        grid=(ntiles,),
        in_specs=[
            pl.BlockSpec((tm, D_MODEL), lambda i: (i, 0)),
            pl.BlockSpec((1, D_MODEL), lambda i: (0, 0)),
            pl.BlockSpec((1, 6, D_MODEL), lambda i: (_mod_row(i, tps), 0, 0)),
            pl.BlockSpec((N_EXPERTS, D_MODEL), lambda i: (0, 0)),
            pl.BlockSpec((N_EXPERTS, D_MODEL), lambda i: (0, 0)),
            pl.BlockSpec((N_EXPERTS, 1), lambda i: (0, 0)),
        ],
        out_specs=(pl.BlockSpec((tm * ROW_WORDS, LANES), lambda i: (i, 0)),
                   pl.BlockSpec((TOP_K, tm), lambda i: (0, i)),
                   pl.BlockSpec((TOP_K, tm), lambda i: (0, i))),
        compiler_params=_cparams(("arbitrary",)),
        name="norm_router",
    )(x, g.reshape(1, D_MODEL), mod, wr_hi, wr_lo, rbias.reshape(N_EXPERTS, 1))


def _row_copy(src, src_tok, dst, dst_tok, sem):
    s = pl.multiple_of(src_tok * ROW_WORDS, ROW_WORDS)
    d = pl.multiple_of(dst_tok * ROW_WORDS, ROW_WORDS)
    return pltpu.make_async_copy(src.at[pl.ds(s, ROW_WORDS)], dst.at[pl.ds(d, ROW_WORDS)], sem)


def _dispatch_kernel(pends_ref, tok_ref, dest_hbm, xs_hbm, dest_smem, zero_ref, sem_idx, sem_z, sem_s):
    i = pl.program_id(0)
    tm = tok_ref.shape[0] // ROW_WORDS
    idx_copy = pltpu.make_async_copy(dest_hbm.at[i], dest_smem, sem_idx)
    idx_copy.start()
    blk = EXPERT_ROWS * ROW_WORDS

    def pad_copy(e):
        row = pl.multiple_of((pends_ref[e] - EXPERT_ROWS) * ROW_WORDS, blk)
        return pltpu.make_async_copy(zero_ref, xs_hbm.at[pl.ds(row, blk)], sem_z)

    @pl.when(i == 0)
    def _():
        zero_ref[...] = jnp.zeros_like(zero_ref)

        def start(e, prev):
            @pl.when(pends_ref[e] > prev)
            def _():
                pad_copy(e).start()
            return pends_ref[e]

        lax.fori_loop(0, N_EXPERTS, start, 0)

        def wait(e, prev):
            @pl.when(pends_ref[e] > prev)
            def _():
                pad_copy(e).wait()
            return pends_ref[e]

        lax.fori_loop(0, N_EXPERTS, wait, 0)

    idx_copy.wait()

    def scatter(r, carry):
        for k in range(TOP_K):
            _row_copy(tok_ref, r, xs_hbm, dest_smem[k * tm + r], sem_s).start(priority=k % 2)
        return carry

    lax.fori_loop(0, tm, scatter, 0)

    def drain(r, carry):
        for k in range(TOP_K):
            _row_copy(tok_ref, r, xs_hbm, dest_smem[k * tm + r], sem_s).wait()
        return carry

    lax.fori_loop(0, tm, drain, 0)


def dispatch(tokens, dest_tiles, pends, *, ntiles):
    tm = tokens.shape[0] // ROW_WORDS // ntiles
    return pl.pallas_call(
        _dispatch_kernel,
        out_shape=jax.ShapeDtypeStruct((N_BLOCKS * EXPERT_ROWS * ROW_WORDS, LANES), I32),
        grid_spec=pltpu.PrefetchScalarGridSpec(
            num_scalar_prefetch=1,
            grid=(ntiles,),
            in_specs=[
                pl.BlockSpec((tm * ROW_WORDS, LANES), lambda i, pends: (i, 0)),
                pl.BlockSpec(memory_space=pl.ANY),
            ],
            out_specs=pl.BlockSpec(memory_space=pl.ANY),
            scratch_shapes=[
                pltpu.SMEM((TOP_K * tm,), I32),
                pltpu.VMEM((EXPERT_ROWS * ROW_WORDS, LANES), I32),
                pltpu.SemaphoreType.DMA,
                pltpu.SemaphoreType.DMA,
                pltpu.SemaphoreType.DMA,
            ],
        ),
        compiler_params=_cparams(("arbitrary",)),
        name="moe_dispatch",
    )(pends, tokens, dest_tiles)


def _experts_kernel(blk_e_ref, nused_ref, xs_ref, wg_ref, wu_ref, wd_ref, y_ref, wgb, wub, wdb):
    i = pl.program_id(0)
    e = blk_e_ref[i]
    prev = blk_e_ref[jnp.maximum(i - 1, 0)]

    @pl.when(i < nused_ref[0])
    def _():
        @pl.when((i == 0) | (e != prev))
        def _():
            wgb[...] = wg_ref[0].astype(BF16)
            wub[...] = wu_ref[0].astype(BF16)
            wdb[...] = wd_ref[0].astype(BF16)

        x = _load_packed(xs_ref, EXPERT_ROWS, BF16)
        g = jnp.dot(x, wgb[...], preferred_element_type=F32)
        u = jnp.dot(x, wub[...], preferred_element_type=F32)
        hb = (_silu(g) * u).astype(BF16)
        _store_packed(y_ref, jnp.dot(hb, wdb[...], preferred_element_type=F32))


def experts(xs, blk_e, nused, w_gate, w_up, w_down, *, layer):
    w_gate = w_gate.reshape(DEPTH * N_EXPERTS, D_MODEL, EXPERT_FF)
    w_up = w_up.reshape(DEPTH * N_EXPERTS, D_MODEL, EXPERT_FF)
    w_down = w_down.reshape(DEPTH * N_EXPERTS, EXPERT_FF, D_MODEL)

    def row_map(i, blk_e, nused):
        return (jnp.minimum(i, nused[0] - 1), 0)

    def w_map(i, blk_e, nused):
        return (layer * N_EXPERTS + blk_e[jnp.minimum(i, nused[0] - 1)], 0, 0)

    return pl.pallas_call(
        _experts_kernel,
        out_shape=jax.ShapeDtypeStruct((N_BLOCKS * EXPERT_ROWS * ROW_WORDS, LANES), I32),
        grid_spec=pltpu.PrefetchScalarGridSpec(
            num_scalar_prefetch=2,
            grid=(N_BLOCKS,),
            in_specs=[
                pl.BlockSpec((EXPERT_ROWS * ROW_WORDS, LANES), row_map),
                pl.BlockSpec((1, D_MODEL, EXPERT_FF), w_map),
                pl.BlockSpec((1, D_MODEL, EXPERT_FF), w_map),
                pl.BlockSpec((1, EXPERT_FF, D_MODEL), w_map),
            ],
            out_specs=pl.BlockSpec((EXPERT_ROWS * ROW_WORDS, LANES), row_map),
            scratch_shapes=[
                pltpu.VMEM((D_MODEL, EXPERT_FF), BF16),
                pltpu.VMEM((D_MODEL, EXPERT_FF), BF16),
                pltpu.VMEM((EXPERT_FF, D_MODEL), BF16),
            ],
        ),
        compiler_params=_cparams(("arbitrary",)),
        name="moe_experts",
    )(blk_e, nused, xs, w_gate, w_up, w_down)


def _shared_kernel(t_ref, wg_ref, wu_ref, wd_ref, o_ref):
    x = _load_packed(t_ref, o_ref.shape[0], BF16)
    g = jnp.dot(x, wg_ref[...], preferred_element_type=F32)
    u = jnp.dot(x, wu_ref[...], preferred_element_type=F32)
    o_ref[...] = jnp.dot((_silu(g) * u).astype(BF16), wd_ref[...], preferred_element_type=F32)


def shared_expert(tokens, wg, wu, wd):
    tm = 512
    t = tokens.shape[0] // ROW_WORDS
    return pl.pallas_call(
        _shared_kernel,
        out_shape=jax.ShapeDtypeStruct((t, D_MODEL), F32),
        grid=(t // tm,),
        in_specs=[
            pl.BlockSpec((tm * ROW_WORDS, LANES), lambda i: (i, 0)),
            pl.BlockSpec((D_MODEL, SHARED_FF), lambda i: (0, 0)),
            pl.BlockSpec((D_MODEL, SHARED_FF), lambda i: (0, 0)),
            pl.BlockSpec((SHARED_FF, D_MODEL), lambda i: (0, 0)),
        ],
        out_specs=pl.BlockSpec((tm, D_MODEL), lambda i: (i, 0)),
        compiler_params=_cparams(("arbitrary",)),
        name="shared_expert",
    )(tokens, wg, wu, wd)


def _combine_kernel(x_ref, sh_ref, mod_ref, wk_ref, gf_ref, dest_hbm, ys_hbm, o_ref,
                    dest_smem, buf_ref, sem_idx, sem_g, *, final):
    i = pl.program_id(0)
    tm = x_ref.shape[0]
    idx_copy = pltpu.make_async_copy(dest_hbm.at[i], dest_smem, sem_idx)
    idx_copy.start()
    idx_copy.wait()

    def gather(r, carry):
        for k in range(TOP_K):
            _row_copy(ys_hbm, dest_smem[k * tm + r], buf_ref.at[k], r, sem_g).start(priority=k % 2)
        return carry

    lax.fori_loop(0, tm, gather, 0)

    def drain(r, carry):
        for k in range(TOP_K):
            _row_copy(ys_hbm, dest_smem[k * tm + r], buf_ref.at[k], r, sem_g).wait()
        return carry

    lax.fori_loop(0, tm, drain, 0)
    los, his = [], []
    for s in range(ROW_WORDS):
        acc_lo = sh_ref[:, s * LANES:(s + 1) * LANES]
        acc_hi = sh_ref[:, HALF + s * LANES:HALF + (s + 1) * LANES]
        for k in range(TOP_K):
            lo, hi = _unpack_words(buf_ref[k, pl.ds(s, tm, stride=ROW_WORDS), :])
            w = wk_ref[:, k:k + 1]
            acc_lo = acc_lo + lo * w
            acc_hi = acc_hi + hi * w
        los.append(acc_lo)
        his.append(acc_hi)
    f = jnp.concatenate(los + his, axis=-1)
    out = x_ref[...] + mod_ref[0, 5:6, :] * f
    if final:
        y = out * lax.rsqrt(jnp.mean(out * out, axis=-1, keepdims=True) + EPS)
        out = y * gf_ref[...]
    o_ref[...] = out


def combine(x, shared, mod, wk_t, g_final, dest_tiles, ys, *, ntiles, final):
    tm = 128
    tps = SEQ // tm
    return pl.pallas_call(
        functools.partial(_combine_kernel, final=final),
        out_shape=jax.ShapeDtypeStruct((ntiles * tm, D_MODEL), F32),
        grid=(ntiles,),
        in_specs=[
            pl.BlockSpec((tm, D_MODEL), lambda i: (i, 0)),
            pl.BlockSpec((tm, D_MODEL), lambda i: (i, 0)),
            pl.BlockSpec((1, 6, D_MODEL), lambda i: (_mod_row(i, tps), 0, 0)),
            pl.BlockSpec((tm, TOP_K), lambda i: (i, 0)),
            pl.BlockSpec((1, D_MODEL), lambda i: (0, 0)),
            pl.BlockSpec(memory_space=pl.ANY),
            pl.BlockSpec(memory_space=pl.ANY),
        ],
        out_specs=pl.BlockSpec((tm, D_MODEL), lambda i: (i, 0)),
        scratch_shapes=[
            pltpu.SMEM((TOP_K * tm,), I32),
            pltpu.VMEM((TOP_K, tm * ROW_WORDS, LANES), I32),
            pltpu.SemaphoreType.DMA,
            pltpu.SemaphoreType.DMA,
        ],
        compiler_params=_cparams(("arbitrary",)),
        name="moe_combine",
    )(x, shared, mod, wk_t, g_final.reshape(1, D_MODEL), dest_tiles, ys)


def _routing_plan(eidx):
    t = eidx.shape[1]
    onehot = (eidx[None, :, :] == jnp.arange(N_EXPERTS, dtype=I32)[:, None, None])
    mask = jnp.any(onehot, axis=1).astype(I32)
    incl = jnp.cumsum(mask, axis=1)
    counts = incl[:, -1]
    rank = incl - mask
    pcounts = (counts + EXPERT_ROWS - 1) // EXPERT_ROWS * EXPERT_ROWS
    pends = jnp.cumsum(pcounts)
    pstarts = pends - pcounts
    slot = pstarts[:, None] + rank
    dest = jnp.sum(jnp.where(onehot, slot[:, None, :], 0), axis=0)
    nused = (pends[-1] // EXPERT_ROWS).astype(I32).reshape(1)
    first_row = jnp.arange(N_BLOCKS, dtype=I32) * EXPERT_ROWS
    blk_e = jnp.minimum(jnp.sum((pends[None, :] <= first_row[:, None]).astype(I32), axis=1),
                        N_EXPERTS - 1).astype(I32)
    return dest.astype(I32), pends.astype(I32), blk_e, nused


def _tile_major(dest, tm):
    k, t = dest.shape
    return dest.reshape(k, t // tm, tm).transpose(1, 0, 2).reshape(t // tm, k * tm)


def _rope_tables(n_rows):
    row = jnp.repeat(jnp.arange(n_rows, dtype=F32), GRID_W)
    col = jnp.tile(jnp.arange(GRID_W, dtype=F32), n_rows)
    half = QK_ROPE // 2
    inv = ROPE_BASE ** (-jnp.arange(0, half, 2, dtype=F32) / half)
    ang_r = row[:, None] * inv
    ang_c = col[:, None] * inv
    ang = jnp.concatenate([ang_r, ang_r, ang_c, ang_c], axis=-1)
    n = ang.shape[0]
    ones = jnp.ones((n, QK_NOPE), F32)
    zeros = jnp.zeros((n, QK_NOPE), F32)
    tail = jnp.zeros((n, HEAD_PAD - QK_NOPE - QK_ROPE), F32)
    cos = jnp.concatenate([ones, jnp.cos(ang), tail], axis=-1)
    sin = jnp.concatenate([zeros, jnp.sin(ang), tail], axis=-1)
    return cos, sin


def _identity_tables(n):
    cos = jnp.concatenate([jnp.ones((n, QK_NOPE + QK_ROPE), F32),
                           jnp.zeros((n, HEAD_PAD - QK_NOPE - QK_ROPE), F32)], axis=-1)
    return cos, jnp.zeros((n, HEAD_PAD), F32)


def _rotate_cols(w):
    i = np.arange(QK_ROPE)
    first = (i % (QK_ROPE // 2)) < (QK_ROPE // 4)
    perm = np.where(first, i + QK_ROPE // 4, i - QK_ROPE // 4)
    sign = np.where(first, -1.0, 1.0).astype(np.float32)
    return w[:, perm] * sign


def _mla_weights(g_cq, w_uq, g_ckv, w_ukv):
    dk = QK_NOPE + QK_ROPE
    wq = w_uq.reshape(Q_LORA, MLA_HEADS, dk)
    pad = jnp.zeros((Q_LORA, MLA_HEADS, HEAD_PAD - dk), F32)
    zero_nope = jnp.zeros((Q_LORA, MLA_HEADS, QK_NOPE), F32)
    wq_rot = _rotate_cols(wq[..., QK_NOPE:].reshape(Q_LORA * MLA_HEADS, QK_ROPE)).reshape(
        Q_LORA, MLA_HEADS, QK_ROPE)
    wq_a = jnp.concatenate([wq, pad], axis=-1).reshape(Q_LORA, -1)
    wq_b = jnp.concatenate([zero_nope, wq_rot, pad], axis=-1).reshape(Q_LORA, -1)
    wkv = w_ukv.reshape(KV_LORA, MLA_HEADS, QK_NOPE + V_HEAD)
    wk = jnp.concatenate([wkv[..., :QK_NOPE],
                          jnp.zeros((KV_LORA, MLA_HEADS, HEAD_PAD - QK_NOPE), F32)], axis=-1)
    wv = wkv[..., QK_NOPE:]
    eye = jnp.eye(QK_ROPE, dtype=F32)
    place = jnp.zeros((LANES, MLA_HEADS, HEAD_PAD), F32)
    place_a = place.at[:QK_ROPE, :, QK_NOPE:dk].set(jnp.broadcast_to(eye[:, None, :], (QK_ROPE, MLA_HEADS, QK_ROPE)))
    rot = _rotate_cols(eye)
    place_b = place.at[:QK_ROPE, :, QK_NOPE:dk].set(jnp.broadcast_to(rot[:, None, :], (QK_ROPE, MLA_HEADS, QK_ROPE)))
    return {
        "g_cq": g_cq.reshape(1, -1), "g_ckv": g_ckv.reshape(1, -1),
        "wq2t": jnp.concatenate([wq_a, wq_b], axis=-1).T.astype(BF16),
        "wk": wk.reshape(KV_LORA, -1).astype(BF16),
        "wvt": wv.reshape(KV_LORA, -1).T.astype(BF16),
        "pk": jnp.concatenate([place_a.reshape(LANES, -1), place_b.reshape(LANES, -1)], axis=-1).astype(BF16),
    }


def _pack_w_in(w):
    gates0 = MLA_IN + 2 * CONV_CH + FFT_CH + POOL_CH
    pad = jnp.zeros((D_MODEL, P_MLA_W - MLA_IN), w.dtype)
    return jnp.concatenate([w[:, gates0:], w[:, :MLA_IN], pad, w[:, MLA_IN:gates0]], axis=-1).astype(BF16)


def _dft_tables(n):
    gw = FFT_CH // FFT_GROUPS

    def angles(m):
        j = jnp.arange(m, dtype=I32)
        return (j[:, None] * j[None, :] % m).astype(F32) * (2.0 * np.pi / m)

    ac = angles(gw)
    an = angles(n)
    cs = jnp.concatenate([jnp.cos(ac), jnp.sin(ac)], axis=-1).astype(BF16)
    mseq = jnp.concatenate([jnp.cos(an), -jnp.sin(an)], axis=-1).astype(BF16)
    return cs, mseq


def _mixers(p, row0, nseq, n, wts, tables, conv_args, pool_args, dft):
    q4, k4, v = mla_prep(p, wts, tables, row0=row0, col0=P_MLA, nseq=nseq, n=n)
    cv = conv_branch(p, *conv_args, row0=row0, nseq=nseq, n=n)
    ff = fft_branch(p, dft[0], dft[1], row0=row0, nseq=nseq, n=n)
    po = pool_branch(p, *pool_args, row0=row0, nseq=nseq, n=n)
    return q4, k4, v, cv, ff, po


def kernel(x, c, ctx, c_ctx, w_mod, b_mod, g_norm1, g_norm2, w_in, g_cq, w_uq, g_ckv, w_ukv, w_o_mla, conv_w, conv_b, conv_ln_g, conv_ln_b, w_o_conv, w_o_fft, pool_w, pool_scale, w_o_pool, w_out, w_router, router_bias, w_exp_gate, w_exp_up, w_exp_down, w_sh_gate, w_sh_up, w_sh_down, g_final):
    xa = jnp.concatenate([x.reshape(T_LAT, D_MODEL), ctx.reshape(T_CTX, D_MODEL)], axis=0)
    cvec = jnp.concatenate([c, c_ctx[None, :], jnp.zeros((SUBLANES - BATCH - 1, D_MODEL), F32)], axis=0)
    mod_all = modulation_all(cvec, w_mod, b_mod).reshape(DEPTH, SUBLANES, 6, D_MODEL)
    rope_lat = _rope_tables(SEQ // GRID_W)
    rope_ctx = _identity_tables(CTX_LEN)
    dft_lat = _dft_tables(SEQ)
    dft_ctx = _dft_tables(CTX_LEN)
    lat_tiles = T_LAT // ROW_TILE
    all_tiles = T_ALL // ROW_TILE

    for l in range(DEPTH):
        last = l == DEPTH - 1
        mod = mod_all[l]
        w_in_p = _pack_w_in(w_in[l])
        wts = _mla_weights(g_cq[l], w_uq[l], g_ckv[l], w_ukv[l])
        conv_args = (conv_w[l], conv_b[l], conv_ln_g[l], conv_ln_b[l])
        pool_args = (pool_w[l].astype(BF16), pool_scale[l])
        wo4 = jnp.stack([w_o_mla[l], w_o_conv[l], w_o_fft[l], w_o_pool[l]], axis=0).astype(BF16)

        if last:
            p = normproj(xa, g_norm1[l], mod, w_in_p, tile0=0, ntiles=lat_tiles, col0=0, ncols=P_COLS)
            p_ctx = normproj(xa, g_norm1[l], mod, w_in_p, tile0=lat_tiles, ntiles=all_tiles - lat_tiles,
                             col0=P_MLA, ncols=P_MLA_W)
        else:
            p = normproj(xa, g_norm1[l], mod, w_in_p, tile0=0, ntiles=all_tiles, col0=0, ncols=P_COLS)
        q4, k4, v, cv, ff, po = _mixers(p, 0, BATCH, SEQ, wts, rope_lat, conv_args, pool_args, dft_lat)
        if last:
            qc, kc, vc = mla_prep(p_ctx, wts, rope_ctx, row0=0, col0=0, nseq=BATCH, n=CTX_LEN)
        else:
            qc, kc, vc, cvc, ffc, poc = _mixers(p, T_LAT, BATCH, CTX_LEN, wts, rope_ctx,
                                                conv_args, pool_args, dft_ctx)
        k_all = jnp.concatenate([k4, kc], axis=2)
        v_all = jnp.concatenate([v, vc], axis=2)
        at = attention(q4, k_all, v_all, nq=SEQ, nk=SEQ + CTX_LEN)
        if last:
            ntok = T_LAT
        else:
            atc = attention(qc, kc, vc, nq=CTX_LEN, nk=CTX_LEN)
            at = jnp.concatenate([at, atc], axis=0)
            cv = jnp.concatenate([cv, cvc], axis=0)
            ff = jnp.concatenate([ff, ffc], axis=0)
            po = jnp.concatenate([po, poc], axis=0)
            ntok = T_ALL
        xa = merge(xa, mod, at, cv, ff, po, p, wo4, w_out[l].astype(BF16), ntiles=ntok // 256)

        wr_t = w_router[l].T
        wr_hi = wr_t.astype(BF16)
        wr_lo = (wr_t - wr_hi.astype(F32)).astype(BF16)
        tokens, eidx, wk = norm_router(xa, g_norm2[l], mod, wr_hi, wr_lo, router_bias[l], ntiles=ntok // 256)
        dest, pends, blk_e, nused = _routing_plan(eidx)
        xs = dispatch(tokens, _tile_major(dest, 256), pends, ntiles=ntok // 256)
        ys = experts(xs, blk_e, nused, w_exp_gate, w_exp_up, w_exp_down, layer=l)
        sh = shared_expert(tokens, w_sh_gate[l].astype(BF16), w_sh_up[l].astype(BF16),
                           w_sh_down[l].astype(BF16))
        xa = combine(xa, sh, mod, wk.T, g_final, _tile_major(dest, 128), ys, ntiles=ntok // 128, final=last)
    return xa.reshape(BATCH, SEQ, D_MODEL)
```

```python
import functools

import jax
import jax.numpy as jnp
import numpy as np
from jax import lax
from jax.experimental import pallas as pl
from jax.experimental.pallas import tpu as pltpu

F32 = jnp.float32
BF16 = jnp.bfloat16
I32 = jnp.int32

D_MODEL = 2048
BATCH = 4
SEQ = 4096
DEPTH = 2
GRID_W = 64
CTX_LEN = 256
MLA_HEADS = 8
QK_NOPE = 64
QK_ROPE = 32
V_HEAD = 64
Q_LORA = 512
KV_LORA = 256
ROPE_BASE = 10000.0
CONV_CH = 512
CONV_WIDTH = 31
FFT_CH = 512
FFT_GROUPS = 4
POOL_CH = 512
POOL_WINDOWS = (2, 4, 8, 16)
POOL_GROUP = POOL_CH // len(POOL_WINDOWS)
N_BRANCHES = 4
N_EXPERTS = 64
N_EXPERT_GROUPS = 8
TOPK_GROUPS = 4
TOP_K = 8
EXPERT_FF = 512
SHARED_FF = 512
ROUTED_SCALE = 2.5
EPS = 1e-6

MLA_IN = Q_LORA + KV_LORA + QK_ROPE
T_LAT = BATCH * SEQ
T_CTX = BATCH * CTX_LEN
T_ALL = T_LAT + T_CTX

LANES = 128
SUBLANES = 8
VMEM_LIMIT_BYTES = 56 * 1024 * 1024

P_GATE = 0
P_MLA = N_BRANCHES * D_MODEL
P_MLA_W = 1024
P_CONV = P_MLA + P_MLA_W
P_FFT = P_CONV + 2 * CONV_CH
P_POOL = P_FFT + FFT_CH
P_COLS = P_POOL + POOL_CH
HEAD_PAD = 128
ATT_SCALE = (QK_NOPE + QK_ROPE) ** -0.5
Q_SCALE = ATT_SCALE * float(np.log2(np.e))

ROW_TILE = 1024
EXPERT_ROWS = 256
EXPERT_SUBS = 4
STEP_ROWS = EXPERT_ROWS * EXPERT_SUBS
N_BLOCKS = -(-(T_ALL * TOP_K) // STEP_ROWS) + N_EXPERTS


def _cparams(sem, vmem=VMEM_LIMIT_BYTES):
    return pltpu.CompilerParams(dimension_semantics=sem, vmem_limit_bytes=vmem)


def _sigmoid(x):
    return 1.0 / (1.0 + jnp.exp(-x))


def _silu(x):
    return x * _sigmoid(x)


def _mod_row(tile, tiles_per_seq):
    return jnp.minimum(tile // tiles_per_seq, BATCH)


def _modulation_kernel(c_ref, w_ref, b_ref, o_ref):
    c = c_ref[...]
    s = _silu(c).astype(BF16)
    o_ref[0] = jnp.dot(s, w_ref[0].astype(BF16), preferred_element_type=F32) + b_ref[0]


def modulation_all(cvec, w_mod, b_mod):
    tn = 1024
    n = 6 * D_MODEL
    return pl.pallas_call(
        _modulation_kernel,
        out_shape=jax.ShapeDtypeStruct((DEPTH, SUBLANES, n), F32),
        grid=(DEPTH, n // tn),
        in_specs=[
            pl.BlockSpec((SUBLANES, D_MODEL), lambda l, j: (0, 0)),
            pl.BlockSpec((1, D_MODEL, tn), lambda l, j: (l, 0, j)),
            pl.BlockSpec((1, 1, tn), lambda l, j: (l, 0, j)),
        ],
        out_specs=pl.BlockSpec((1, SUBLANES, tn), lambda l, j: (l, 0, j)),
        compiler_params=_cparams(("arbitrary", "arbitrary")),
        name="modulation",
    )(cvec, w_mod, b_mod.reshape(DEPTH, 1, n))


def _adaln(x, g, shift, scale):
    y = x * lax.rsqrt(jnp.mean(x * x, axis=-1, keepdims=True) + EPS)
    return (y * g) * (1.0 + scale) + shift


def _normproj_kernel(x_ref, g_ref, mod_ref, w_ref, o_ref, h_ref, *, shift_row, chunk):
    @pl.when(pl.program_id(1) == 0)
    def _():
        shift = mod_ref[0, shift_row:shift_row + 1, :]
        scale = mod_ref[0, shift_row + 1:shift_row + 2, :]
        for r in range(0, x_ref.shape[0], chunk):
            h = _adaln(x_ref[r:r + chunk, :], g_ref[...], shift, scale)
            h_ref[r:r + chunk, :] = h.astype(BF16)

    o_ref[...] = jnp.dot(h_ref[...], w_ref[...], preferred_element_type=F32).astype(o_ref.dtype)


def normproj(x, g, mod, w, *, tile0, ntiles, col0, ncols, shift_row=0):
    tm, tn = ROW_TILE, 1024
    tps = SEQ // tm
    cb0 = col0 // tn
    return pl.pallas_call(
        functools.partial(_normproj_kernel, shift_row=shift_row, chunk=256),
        out_shape=jax.ShapeDtypeStruct((ntiles * tm, ncols), BF16),
        grid=(ntiles, ncols // tn),
        in_specs=[
            pl.BlockSpec((tm, D_MODEL), lambda i, j: (i + tile0, 0)),
            pl.BlockSpec((1, D_MODEL), lambda i, j: (0, 0)),
            pl.BlockSpec((1, 6, D_MODEL), lambda i, j: (_mod_row(i + tile0, tps), 0, 0)),
            pl.BlockSpec((D_MODEL, tn), lambda i, j: (0, j + cb0)),
        ],
        out_specs=pl.BlockSpec((tm, tn), lambda i, j: (i, j)),
        scratch_shapes=[pltpu.VMEM((tm, D_MODEL), BF16)],
        compiler_params=_cparams(("arbitrary", "arbitrary")),
        name="normproj",
    )(x, g.reshape(1, D_MODEL), mod, w)


def _rms_gain(x, g):
    y = x * lax.rsqrt(jnp.mean(x * x, axis=-1, keepdims=True) + EPS)
    return (y * g).astype(BF16)


_NT = (((1,), (1,)), ((), ()))


def _mla_prep_kernel(p_ref, gq_ref, gkv_ref, wqt_ref, wk_ref, wvt_ref, pk_ref, cos_ref, sin_ref,
                     cost_ref, sint_ref, qt_ref, k_ref, vt_ref):
    hw = MLA_HEADS * HEAD_PAD
    cqn = _rms_gain(p_ref[:, :Q_LORA].astype(F32), gq_ref[...])
    ckvn = _rms_gain(p_ref[:, Q_LORA:Q_LORA + KV_LORA].astype(F32), gkv_ref[...])
    kr = p_ref[:, Q_LORA + KV_LORA:Q_LORA + KV_LORA + LANES]
    q2t = lax.dot_general(wqt_ref[...], cqn, _NT, preferred_element_type=F32)
    vt_ref[0] = lax.dot_general(wvt_ref[...], ckvn, _NT, preferred_element_type=F32).astype(BF16)
    kk = jnp.dot(ckvn, wk_ref[...], preferred_element_type=F32)
    kr2 = jnp.dot(kr, pk_ref[...], preferred_element_type=F32)
    cos, sin = cos_ref[...], sin_ref[...]
    cost, sint = cost_ref[...], sint_ref[...]
    for h in range(MLA_HEADS):
        a = slice(h * HEAD_PAD, (h + 1) * HEAD_PAD)
        b = slice(hw + h * HEAD_PAD, hw + (h + 1) * HEAD_PAD)
        qt_ref[0, h] = ((q2t[a, :] * cost + q2t[b, :] * sint) * Q_SCALE).astype(BF16)
        k_ref[0, h] = (kk[:, a] + kr2[:, a] * cos + kr2[:, b] * sin).astype(BF16)


def mla_prep(p, wts, tables, *, row0, col0, nseq, n):
    tm = 256
    tps = n // tm
    hw = MLA_HEADS * HEAD_PAD
    vw = MLA_HEADS * V_HEAD
    blk0 = row0 // tm
    cblk = col0 // P_MLA_W
    cos_t, sin_t = tables
    const = lambda i: (0, 0)
    return pl.pallas_call(
        _mla_prep_kernel,
        out_shape=(jax.ShapeDtypeStruct((nseq, MLA_HEADS, HEAD_PAD, n), BF16),
                   jax.ShapeDtypeStruct((nseq, MLA_HEADS, n, HEAD_PAD), BF16),
                   jax.ShapeDtypeStruct((nseq, vw, n), BF16)),
        grid=(nseq * tps,),
        in_specs=[
            pl.BlockSpec((tm, P_MLA_W), lambda i: (i + blk0, cblk)),
            pl.BlockSpec((1, Q_LORA), const),
            pl.BlockSpec((1, KV_LORA), const),
            pl.BlockSpec((2 * hw, Q_LORA), const),
            pl.BlockSpec((KV_LORA, hw), const),
            pl.BlockSpec((vw, KV_LORA), const),
            pl.BlockSpec((LANES, 2 * hw), const),
            pl.BlockSpec((tm, HEAD_PAD), lambda i: (i % tps, 0)),
            pl.BlockSpec((tm, HEAD_PAD), lambda i: (i % tps, 0)),
            pl.BlockSpec((HEAD_PAD, tm), lambda i: (0, i % tps)),
            pl.BlockSpec((HEAD_PAD, tm), lambda i: (0, i % tps)),
        ],
        out_specs=(pl.BlockSpec((1, MLA_HEADS, HEAD_PAD, tm), lambda i: (i // tps, 0, 0, i % tps)),
                   pl.BlockSpec((1, MLA_HEADS, tm, HEAD_PAD), lambda i: (i // tps, 0, i % tps, 0)),
                   pl.BlockSpec((1, vw, tm), lambda i: (i // tps, 0, i % tps))),
        compiler_params=_cparams(("arbitrary",)),
        name="mla_prep",
    )(p, wts["g_cq"], wts["g_ckv"], wts["wq2t"], wts["wk"], wts["wvt"], wts["pk"],
      cos_t, sin_t, cos_t.T, sin_t.T)


ATT_HEADS = 4


def _attn_kernel(qt_ref, k_ref, vt_ref, o_ref, s_ref, p_ref, *, nk, ck):
    tq = qt_ref.shape[3]
    chunks = [slice(c, c + ck) for c in range(0, nk, ck)]
    nh = ATT_HEADS
    m = [None] * nh
    l = [jnp.zeros((1, tq), F32) for _ in range(nh)]
    ot = [jnp.zeros((V_HEAD, tq), F32) for _ in range(nh)]
    for step in range(nh + 2):
        ha, hb, hc = step, step - 1, step - 2
        for c in chunks:
            if 0 <= ha < nh:
                s = jnp.dot(k_ref[0, ha, c, :], qt_ref[0, ha], preferred_element_type=F32)
                s_ref[ha % 2, c, :] = s
                mc = jnp.max(s, axis=0, keepdims=True)
                m[ha] = mc if m[ha] is None else jnp.maximum(m[ha], mc)
            if 0 <= hb < nh:
                p = jnp.exp2(s_ref[hb % 2, c, :] - m[hb])
                l[hb] = l[hb] + jnp.sum(p, axis=0, keepdims=True)
                p_ref[hb % 2, c, :] = p.astype(BF16)
            if 0 <= hc < nh:
                ot[hc] = ot[hc] + jnp.dot(vt_ref[0, hc * V_HEAD:(hc + 1) * V_HEAD, c], p_ref[hc % 2, c, :],
                                          preferred_element_type=F32)
    outs = [ot[h] / l[h] for h in range(nh)]
    o_ref[...] = jnp.concatenate(outs, axis=0).T.astype(o_ref.dtype)


def attention(qt4, k4, vt3, *, nq, nk):
    tq = 256
    ck = 256
    nqt = nq // tq
    hs = ATT_HEADS
    return pl.pallas_call(
        functools.partial(_attn_kernel, nk=nk, ck=ck),
        out_shape=jax.ShapeDtypeStruct((BATCH * nq, MLA_HEADS * V_HEAD), BF16),
        grid=(BATCH, MLA_HEADS // hs, nqt),
        in_specs=[
            pl.BlockSpec((1, hs, HEAD_PAD, tq), lambda b, hg, i: (b, hg, 0, i)),
            pl.BlockSpec((1, hs, nk, HEAD_PAD), lambda b, hg, i: (b, hg, 0, 0)),
            pl.BlockSpec((1, hs * V_HEAD, nk), lambda b, hg, i: (b, hg, 0)),
        ],
        out_specs=pl.BlockSpec((tq, hs * V_HEAD), lambda b, hg, i: (b * nqt + i, hg)),
        scratch_shapes=[pltpu.VMEM((2, nk, tq), F32), pltpu.VMEM((2, nk, tq), BF16)],
        compiler_params=_cparams(("arbitrary", "arbitrary", "arbitrary")),
        name="attention",
    )(qt4, k4, vt3)


CONV_PAD = 16


def _conv_kernel(p_ref, w_ref, cb_ref, lg_ref, lb_ref, o_ref, u_ref, *, n, tr):
    zeros = jnp.zeros((CONV_PAD, CONV_CH), F32)
    u_ref[0:CONV_PAD, :] = zeros
    u_ref[CONV_PAD + n:2 * CONV_PAD + n, :] = zeros

    def glu(i, carry):
        r = pl.multiple_of(i * tr, tr)
        a = p_ref[pl.ds(r, tr), 0:CONV_CH].astype(F32)
        b = p_ref[pl.ds(r, tr), CONV_CH:2 * CONV_CH].astype(F32)
        u_ref[pl.ds(CONV_PAD + r, tr), :] = a * _sigmoid(b)
        return carry

    lax.fori_loop(0, n // tr, glu, 0)
    off = CONV_PAD - CONV_WIDTH // 2
    win_rows = tr + 2 * CONV_PAD

    def conv(i, carry):
        r = pl.multiple_of(i * tr, tr)
        strips = []
        for c0 in range(0, CONV_CH, LANES):
            win = u_ref[pl.ds(r, win_rows), c0:c0 + LANES]
            acc = jnp.zeros((tr, LANES), F32)
            for b in range(SUBLANES):
                wb = pltpu.roll(win, win_rows - (off + b), axis=0)
                for k in range(b, CONV_WIDTH, SUBLANES):
                    acc = acc + wb[k - b:k - b + tr, :] * w_ref[k:k + 1, c0:c0 + LANES]
            strips.append(acc)
        acc = jnp.concatenate(strips, axis=-1) + cb_ref[...]
        mu = jnp.mean(acc, axis=-1, keepdims=True)
        d = acc - mu
        var = jnp.mean(d * d, axis=-1, keepdims=True)
        y = (d * lax.rsqrt(var + EPS)) * lg_ref[...] + lb_ref[...]
        o_ref[pl.ds(r, tr), :] = _silu(y).astype(o_ref.dtype)
        return carry

    lax.fori_loop(0, n // tr, conv, 0)


def conv_branch(p, conv_w, conv_b, ln_g, ln_b, *, row0, nseq, n):
    tr = 128
    blk0 = row0 // n
    const = lambda s: (0, 0)
    return pl.pallas_call(
        functools.partial(_conv_kernel, n=n, tr=tr),
        out_shape=jax.ShapeDtypeStruct((nseq * n, CONV_CH), BF16),
        grid=(nseq,),
        in_specs=[
            pl.BlockSpec((n, 2 * CONV_CH), lambda s: (s + blk0, P_CONV // (2 * CONV_CH))),
            pl.BlockSpec((CONV_WIDTH, CONV_CH), const),
            pl.BlockSpec((1, CONV_CH), const),
            pl.BlockSpec((1, CONV_CH), const),
            pl.BlockSpec((1, CONV_CH), const),
        ],
        out_specs=pl.BlockSpec((n, CONV_CH), lambda s: (s, 0)),
        scratch_shapes=[pltpu.VMEM((n + 2 * CONV_PAD, CONV_CH), F32)],
        compiler_params=_cparams(("arbitrary",)),
        name="conv_branch",
    )(p, conv_w, conv_b.reshape(1, -1), ln_g.reshape(1, -1), ln_b.reshape(1, -1))


def _fft_kernel(z_ref, cs_ref, m_ref, o_ref, ab_ref, *, n, tc):
    gw = FFT_CH // FFT_GROUPS

    @pl.when(pl.program_id(1) == 0)
    def _():
        def chan(i, carry):
            r = pl.multiple_of(i * tc, tc)
            for g in range(FFT_GROUPS):
                zg = z_ref[pl.ds(r, tc), g * gw:(g + 1) * gw]
                ab = jnp.dot(zg, cs_ref[...], preferred_element_type=F32)
                ab_ref[pl.ds(r, tc), g * gw:(g + 1) * gw] = ab[:, :gw].astype(BF16)
                ab_ref[pl.ds(n + r, tc), g * gw:(g + 1) * gw] = ab[:, gw:].astype(BF16)
            return carry

        lax.fori_loop(0, n // tc, chan, 0)

    norm = 1.0 / float(np.sqrt(n * gw))
    o_ref[...] = (jnp.dot(m_ref[...], ab_ref[...], preferred_element_type=F32) * norm).astype(o_ref.dtype)


def fft_branch(p, cs, mseq, *, row0, nseq, n):
    tm = min(n, 512)
    tc = min(n, 512)
    blk0 = row0 // n
    nt = n // tm
    return pl.pallas_call(
        functools.partial(_fft_kernel, n=n, tc=tc),
        out_shape=jax.ShapeDtypeStruct((nseq * n, FFT_CH), BF16),
        grid=(nseq, nt),
        in_specs=[
            pl.BlockSpec((n, FFT_CH), lambda s, i: (s + blk0, P_FFT // FFT_CH)),
            pl.BlockSpec((FFT_CH // FFT_GROUPS, 2 * FFT_CH // FFT_GROUPS), lambda s, i: (0, 0)),
            pl.BlockSpec((tm, 2 * n), lambda s, i: (i, 0)),
        ],
        out_specs=pl.BlockSpec((tm, FFT_CH), lambda s, i: (s * nt + i, 0)),
        scratch_shapes=[pltpu.VMEM((2 * n, FFT_CH), BF16)],
        compiler_params=_cparams(("arbitrary", "arbitrary")),
        name="fft_branch",
    )(p, cs, mseq)


POOL_PAD = 16


def _pool_kernel(z_ref, pw_ref, ps_ref, o_ref, zp_ref, d_ref, *, n, tr, tc):
    zeros = jnp.zeros((POOL_PAD, POOL_CH), F32)
    zp_ref[0:POOL_PAD, :] = zeros
    zp_ref[POOL_PAD + n:2 * POOL_PAD + n, :] = zeros

    def fill(i, carry):
        r = pl.multiple_of(i * tc, tc)
        zp_ref[pl.ds(POOL_PAD + r, tc), :] = z_ref[pl.ds(r, tc), :].astype(F32)
        return carry

    lax.fori_loop(0, n // tc, fill, 0)

    win_rows = tr + 2 * POOL_PAD

    def pool(i, carry):
        r = pl.multiple_of(i * tr, tr)
        t = (r + lax.broadcasted_iota(I32, (tr, POOL_GROUP), 0)).astype(F32)
        for gi, w in enumerate(POOL_WINDOWS):
            cols = slice(gi * POOL_GROUP, (gi + 1) * POOL_GROUP)
            win = zp_ref[pl.ds(r, win_rows), cols]
            s = jnp.zeros((tr, POOL_GROUP), F32)
            for j in range(w):
                start = POOL_PAD - w // 2 + j
                if start % SUBLANES == 0:
                    s = s + win[start:start + tr, :]
                else:
                    s = s + pltpu.roll(win, win_rows - start, axis=0)[0:tr, :]
            lo = jnp.maximum(t - (w // 2), 0.0)
            hi = jnp.minimum(t - (w // 2) + w, float(n))
            z = win[POOL_PAD:POOL_PAD + tr, :]
            d_ref[pl.ds(r, tr), cols] = (s / (hi - lo) - z).astype(BF16)
        return carry

    lax.fori_loop(0, n // tr, pool, 0)

    def proj(i, carry):
        r = pl.multiple_of(i * tc, tc)
        for gi in range(len(POOL_WINDOWS)):
            cols = slice(gi * POOL_GROUP, (gi + 1) * POOL_GROUP)
            y = jnp.dot(d_ref[pl.ds(r, tc), cols], pw_ref[gi], preferred_element_type=F32)
            o_ref[pl.ds(r, tc), cols] = (y * ps_ref[:, cols]).astype(o_ref.dtype)
        return carry

    lax.fori_loop(0, n // tc, proj, 0)


def pool_branch(p, pool_w, pool_scale, *, row0, nseq, n):
    tr = 128
    tc = min(n, 512)
    blk0 = row0 // n
    return pl.pallas_call(
        functools.partial(_pool_kernel, n=n, tr=tr, tc=tc),
        out_shape=jax.ShapeDtypeStruct((nseq * n, POOL_CH), BF16),
        grid=(nseq,),
        in_specs=[
            pl.BlockSpec((n, POOL_CH), lambda s: (s + blk0, P_POOL // POOL_CH)),
            pl.BlockSpec((len(POOL_WINDOWS), POOL_GROUP, POOL_GROUP), lambda s: (0, 0, 0)),
            pl.BlockSpec((1, POOL_CH), lambda s: (0, 0)),
        ],
        out_specs=pl.BlockSpec((n, POOL_CH), lambda s: (s, 0)),
        scratch_shapes=[pltpu.VMEM((n + 2 * POOL_PAD, POOL_CH), F32),
                        pltpu.VMEM((n, POOL_CH), BF16)],
        compiler_params=_cparams(("arbitrary",)),
        name="pool_branch",
    )(p, pool_w, pool_scale.reshape(1, -1))


def _merge_kernel(x_ref, mod_ref, at_ref, cv_ref, ff_ref, po_ref, g_ref, wo_ref, wout_ref, o_ref):
    mix = None
    for b, br in enumerate((at_ref, cv_ref, ff_ref, po_ref)):
        y = jnp.dot(br[...], wo_ref[b], preferred_element_type=F32)
        gate = _sigmoid(g_ref[:, b * D_MODEL:(b + 1) * D_MODEL].astype(F32))
        mix = gate * y if mix is None else mix + gate * y
    out = jnp.dot(mix.astype(BF16), wout_ref[...], preferred_element_type=F32)
    o_ref[...] = x_ref[...] + mod_ref[0, 2:3, :] * out


def merge(x, mod, attn, conv, fft, pool, p, wo4, w_out, *, ntiles):
    tm = 256
    tps = SEQ // tm
    br_spec = pl.BlockSpec((tm, 512), lambda i: (i, 0))
    return pl.pallas_call(
        _merge_kernel,
        out_shape=jax.ShapeDtypeStruct((ntiles * tm, D_MODEL), F32),
        grid=(ntiles,),
        in_specs=[
            pl.BlockSpec((tm, D_MODEL), lambda i: (i, 0)),
            pl.BlockSpec((1, 6, D_MODEL), lambda i: (_mod_row(i, tps), 0, 0)),
            br_spec, br_spec, br_spec, br_spec,
            pl.BlockSpec((tm, N_BRANCHES * D_MODEL), lambda i: (i, P_GATE // (N_BRANCHES * D_MODEL))),
            pl.BlockSpec((N_BRANCHES, 512, D_MODEL), lambda i: (0, 0, 0), pipeline_mode=pl.Buffered(1)),
            pl.BlockSpec((D_MODEL, D_MODEL), lambda i: (0, 0), pipeline_mode=pl.Buffered(1)),
        ],
        out_specs=pl.BlockSpec((tm, D_MODEL), lambda i: (i, 0)),
        compiler_params=_cparams(("arbitrary",)),
        name="merge",
    )(x, mod, attn, conv, fft, pool, p, wo4, w_out)


def _first_index(hit_value, cand, ids, big):
    return jnp.min(jnp.where(cand == hit_value, ids, big), axis=0, keepdims=True)


HALF = D_MODEL // 2
ROW_WORDS = HALF // LANES


def _round_bf16_bits(x):
    u = lax.bitcast_convert_type(x, I32)
    odd = lax.shift_right_logical(u, 16) & 1
    return (u + 0x7FFF + odd) & jnp.int32(-65536)


def _pack_words(lo, hi):
    return lax.shift_right_logical(_round_bf16_bits(lo), 16) | _round_bf16_bits(hi)


def _unpack_words(w):
    lo = lax.bitcast_convert_type(lax.shift_left(w, 16), F32)
    hi = lax.bitcast_convert_type(w & jnp.int32(-65536), F32)
    return lo, hi


def _store_packed(ref, x, tok0=0):
    tm = x.shape[0]
    for s in range(ROW_WORDS):
        w = _pack_words(x[:, s * LANES:(s + 1) * LANES], x[:, HALF + s * LANES:HALF + (s + 1) * LANES])
        ref[pl.ds(tok0 * ROW_WORDS + s, tm, stride=ROW_WORDS), :] = w


def _load_packed(ref, tm, dtype, tok0=0):
    los, his = [], []
    for s in range(ROW_WORDS):
        lo, hi = _unpack_words(ref[pl.ds(tok0 * ROW_WORDS + s, tm, stride=ROW_WORDS), :])
        los.append(lo.astype(dtype))
        his.append(hi.astype(dtype))
    return jnp.concatenate(los + his, axis=-1)


def _router_kernel(x_ref, g_ref, mod_ref, wrh_ref, wrl_ref, rb_ref, tok_ref, eidx_ref, wk_ref):
    tm = x_ref.shape[0]
    h = _adaln(x_ref[...], g_ref[...], mod_ref[0, 3:4, :], mod_ref[0, 4:5, :])
    _store_packed(tok_ref, h)
    hh = h.astype(BF16)
    hl = (h - hh.astype(F32)).astype(BF16)
    nt = (((1,), (1,)), ((), ()))
    logits = (lax.dot_general(wrh_ref[...], hh, nt, preferred_element_type=F32)
              + lax.dot_general(wrh_ref[...], hl, nt, preferred_element_type=F32)
              + lax.dot_general(wrl_ref[...], hh, nt, preferred_element_type=F32))
    scores = _sigmoid(logits)
    sel = scores + rb_ref[...]
    per = N_EXPERTS // N_EXPERT_GROUPS
    assert per == SUBLANES and N_EXPERT_GROUPS == SUBLANES and TOP_K == SUBLANES
    neg = -jnp.inf
    sub = lax.broadcasted_iota(I32, (SUBLANES, tm), 0).astype(F32)
    sg = [sel[g * per:(g + 1) * per, :] for g in range(N_EXPERT_GROUPS)]
    sc = [scores[g * per:(g + 1) * per, :] for g in range(N_EXPERT_GROUPS)]
    gsc = jnp.zeros((SUBLANES, tm), F32)
    for g in range(N_EXPERT_GROUPS):
        m1 = jnp.max(sg[g], axis=0, keepdims=True)
        i1 = _first_index(m1, sg[g], sub, float(per))
        m2 = jnp.max(jnp.where(sub == i1, neg, sg[g]), axis=0, keepdims=True)
        gsc = jnp.where(sub == float(g), m1 + m2, gsc)
    gsel = jnp.zeros((SUBLANES, tm), F32)
    for _ in range(TOPK_GROUPS):
        m = jnp.max(gsc, axis=0, keepdims=True)
        hit = sub == _first_index(m, gsc, sub, float(N_EXPERT_GROUPS))
        gsel = jnp.where(hit, 1.0, gsel)
        gsc = jnp.where(hit, neg, gsc)
    cand = []
    for g in range(N_EXPERT_GROUPS):
        allowed = jnp.max(jnp.where(sub == float(g), gsel, 0.0), axis=0, keepdims=True)
        cand.append(jnp.where(allowed > 0.0, sg[g], neg))
    eid = [sub + float(g * per) for g in range(N_EXPERT_GROUPS)]
    idxs = jnp.zeros((SUBLANES, tm), F32)
    vals = jnp.zeros((SUBLANES, tm), F32)
    for k in range(TOP_K):
        m = functools.reduce(jnp.maximum, [jnp.max(c, axis=0, keepdims=True) for c in cand])
        idx = functools.reduce(
            jnp.minimum, [_first_index(m, cand[g], eid[g], float(N_EXPERTS)) for g in range(N_EXPERT_GROUPS)])
        val = jnp.zeros((1, tm), F32)
        for g in range(N_EXPERT_GROUPS):
            hit = eid[g] == idx
            val = val + jnp.sum(jnp.where(hit, sc[g], 0.0), axis=0, keepdims=True)
            cand[g] = jnp.where(hit, neg, cand[g])
        idxs = jnp.where(sub == float(k), idx, idxs)
        vals = jnp.where(sub == float(k), val, vals)
    eidx_ref[...] = idxs.astype(I32)
    wk_ref[...] = vals / jnp.sum(vals, axis=0, keepdims=True) * ROUTED_SCALE


def norm_router(x, g, mod, wr_hi, wr_lo, rbias, *, ntiles):
    tm = 256
    tps = SEQ // tm
    t = ntiles * tm
    return pl.pallas_call(
        _router_kernel,
        out_shape=(jax.ShapeDtypeStruct((t * ROW_WORDS, LANES), I32),
                   jax.ShapeDtypeStruct((TOP_K, t), I32),
                   jax.ShapeDtypeStruct((TOP_K, t), F32)),
        grid=(ntiles,),
        in_specs=[
            pl.BlockSpec((tm, D_MODEL), lambda i: (i, 0)),
            pl.BlockSpec((1, D_MODEL), lambda i: (0, 0)),
            pl.BlockSpec((1, 6, D_MODEL), lambda i: (_mod_row(i, tps), 0, 0)),
            pl.BlockSpec((N_EXPERTS, D_MODEL), lambda i: (0, 0)),
            pl.BlockSpec((N_EXPERTS, D_MODEL), lambda i: (0, 0)),
            pl.BlockSpec((N_EXPERTS, 1), lambda i: (0, 0)),
        ],
        out_specs=(pl.BlockSpec((tm * ROW_WORDS, LANES), lambda i: (i, 0)),
                   pl.BlockSpec((TOP_K, tm), lambda i: (0, i)),
                   pl.BlockSpec((TOP_K, tm), lambda i: (0, i))),
        compiler_params=_cparams(("arbitrary",)),
        name="norm_router",
    )(x, g.reshape(1, D_MODEL), mod, wr_hi, wr_lo, rbias.reshape(N_EXPERTS, 1))


def _row_copy(src, src_tok, dst, dst_tok, sem):
    s = pl.multiple_of(src_tok * ROW_WORDS, ROW_WORDS)
    d = pl.multiple_of(dst_tok * ROW_WORDS, ROW_WORDS)
    return pltpu.make_async_copy(src.at[pl.ds(s, ROW_WORDS)], dst.at[pl.ds(d, ROW_WORDS)], sem)


def _dispatch_kernel(zends_ref, tok_ref, dest_hbm, xs_hbm, dest_smem, zero_ref, sem_idx, sem_z, sem_s):
    i = pl.program_id(0)
    tm = tok_ref.shape[0] // ROW_WORDS
    idx_copy = pltpu.make_async_copy(dest_hbm.at[i], dest_smem, sem_idx)
    idx_copy.start()
    blk = EXPERT_ROWS * ROW_WORDS

    def pad_copy(e):
        row = pl.multiple_of((zends_ref[e] - EXPERT_ROWS) * ROW_WORDS, blk)
        return pltpu.make_async_copy(zero_ref, xs_hbm.at[pl.ds(row, blk)], sem_z)

    @pl.when(i == 0)
    def _():
        zero_ref[...] = jnp.zeros_like(zero_ref)

        def start(e, carry):
            @pl.when(zends_ref[e] > 0)
            def _():
                pad_copy(e).start()
            return carry

        lax.fori_loop(0, N_EXPERTS, start, 0)

        def wait(e, carry):
            @pl.when(zends_ref[e] > 0)
            def _():
                pad_copy(e).wait()
            return carry

        lax.fori_loop(0, N_EXPERTS, wait, 0)

    idx_copy.wait()

    def scatter(r, carry):
        for k in range(TOP_K):
            _row_copy(tok_ref, r, xs_hbm, dest_smem[k * tm + r], sem_s).start(priority=k % 2)
        return carry

    lax.fori_loop(0, tm, scatter, 0)

    def drain(r, carry):
        for k in range(TOP_K):
            _row_copy(tok_ref, r, xs_hbm, dest_smem[k * tm + r], sem_s).wait()
        return carry

    lax.fori_loop(0, tm, drain, 0)


def dispatch(tokens, dest_tiles, zends, *, ntiles):
    tm = tokens.shape[0] // ROW_WORDS // ntiles
    return pl.pallas_call(
        _dispatch_kernel,
        out_shape=jax.ShapeDtypeStruct((N_BLOCKS * STEP_ROWS * ROW_WORDS, LANES), I32),
        grid_spec=pltpu.PrefetchScalarGridSpec(
            num_scalar_prefetch=1,
            grid=(ntiles,),
            in_specs=[
                pl.BlockSpec((tm * ROW_WORDS, LANES), lambda i, zends: (i, 0)),
                pl.BlockSpec(memory_space=pl.ANY),
            ],
            out_specs=pl.BlockSpec(memory_space=pl.ANY),
            scratch_shapes=[
                pltpu.SMEM((TOP_K * tm,), I32),
                pltpu.VMEM((EXPERT_ROWS * ROW_WORDS, LANES), I32),
                pltpu.SemaphoreType.DMA,
                pltpu.SemaphoreType.DMA,
                pltpu.SemaphoreType.DMA,
            ],
        ),
        compiler_params=_cparams(("arbitrary",)),
        name="moe_dispatch",
    )(zends, tokens, dest_tiles)


def _experts_kernel(blk_e_ref, nsub_ref, nused_ref, xs_ref, wg_ref, wu_ref, wd_ref, y_ref, wgb, wub, wdb):
    i = pl.program_id(0)
    e = blk_e_ref[i]
    prev = blk_e_ref[jnp.maximum(i - 1, 0)]

    @pl.when(i < nused_ref[0])
    def _():
        @pl.when((i == 0) | (e != prev))
        def _():
            wgb[...] = wg_ref[0].astype(BF16)
            wub[...] = wu_ref[0].astype(BF16)
            wdb[...] = wd_ref[0].astype(BF16)

        for sb in range(EXPERT_SUBS):
            tok0 = sb * EXPERT_ROWS

            @pl.when(sb < nsub_ref[i])
            def _():
                x = _load_packed(xs_ref, EXPERT_ROWS, BF16, tok0)
                g = jnp.dot(x, wgb[...], preferred_element_type=F32)
                u = jnp.dot(x, wub[...], preferred_element_type=F32)
                hb = (_silu(g) * u).astype(BF16)
                _store_packed(y_ref, jnp.dot(hb, wdb[...], preferred_element_type=F32), tok0)

            @pl.when(sb >= nsub_ref[i])
            def _():
                rows = EXPERT_ROWS * ROW_WORDS
                y_ref[tok0 * ROW_WORDS:tok0 * ROW_WORDS + rows, :] = jnp.zeros((rows, LANES), I32)


def experts(xs, blk_e, nsub, nused, w_gate, w_up, w_down, *, layer):
    w_gate = w_gate.reshape(DEPTH * N_EXPERTS, D_MODEL, EXPERT_FF)
    w_up = w_up.reshape(DEPTH * N_EXPERTS, D_MODEL, EXPERT_FF)
    w_down = w_down.reshape(DEPTH * N_EXPERTS, EXPERT_FF, D_MODEL)

    def row_map(i, blk_e, nsub, nused):
        return (jnp.minimum(i, nused[0] - 1), 0)

    def w_map(i, blk_e, nsub, nused):
        return (layer * N_EXPERTS + blk_e[jnp.minimum(i, nused[0] - 1)], 0, 0)

    return pl.pallas_call(
        _experts_kernel,
        out_shape=jax.ShapeDtypeStruct((N_BLOCKS * STEP_ROWS * ROW_WORDS, LANES), I32),
        grid_spec=pltpu.PrefetchScalarGridSpec(
            num_scalar_prefetch=3,
            grid=(N_BLOCKS,),
            in_specs=[
                pl.BlockSpec((STEP_ROWS * ROW_WORDS, LANES), row_map),
                pl.BlockSpec((1, D_MODEL, EXPERT_FF), w_map),
                pl.BlockSpec((1, D_MODEL, EXPERT_FF), w_map),
                pl.BlockSpec((1, EXPERT_FF, D_MODEL), w_map),
            ],
            out_specs=pl.BlockSpec((STEP_ROWS * ROW_WORDS, LANES), row_map),
            scratch_shapes=[
                pltpu.VMEM((D_MODEL, EXPERT_FF), BF16),
                pltpu.VMEM((D_MODEL, EXPERT_FF), BF16),
                pltpu.VMEM((EXPERT_FF, D_MODEL), BF16),
            ],
        ),
        compiler_params=_cparams(("arbitrary",)),
        name="moe_experts",
    )(blk_e, nsub, nused, xs, w_gate, w_up, w_down)


def _shared_kernel(t_ref, wg_ref, wu_ref, wd_ref, o_ref):
    x = _load_packed(t_ref, o_ref.shape[0], BF16)
    g = jnp.dot(x, wg_ref[...], preferred_element_type=F32)
    u = jnp.dot(x, wu_ref[...], preferred_element_type=F32)
    o_ref[...] = jnp.dot((_silu(g) * u).astype(BF16), wd_ref[...], preferred_element_type=F32)


def shared_expert(tokens, wg, wu, wd):
    tm = 512
    t = tokens.shape[0] // ROW_WORDS
    return pl.pallas_call(
        _shared_kernel,
        out_shape=jax.ShapeDtypeStruct((t, D_MODEL), F32),
        grid=(t // tm,),
        in_specs=[
            pl.BlockSpec((tm * ROW_WORDS, LANES), lambda i: (i, 0)),
            pl.BlockSpec((D_MODEL, SHARED_FF), lambda i: (0, 0)),
            pl.BlockSpec((D_MODEL, SHARED_FF), lambda i: (0, 0)),
            pl.BlockSpec((SHARED_FF, D_MODEL), lambda i: (0, 0)),
        ],
        out_specs=pl.BlockSpec((tm, D_MODEL), lambda i: (i, 0)),
        compiler_params=_cparams(("arbitrary",)),
        name="shared_expert",
    )(tokens, wg, wu, wd)


def _combine_kernel(x_ref, sh_ref, mod_ref, wk_ref, gf_ref, dest_hbm, ys_hbm, o_ref,
                    idx0, idx1, buf0, buf1, sem_idx, sem_g, *, final):
    i = pl.program_id(0)
    n = pl.num_programs(0)
    tm = x_ref.shape[0]
    idx = (idx0, idx1)
    buf = (buf0, buf1)

    def start_gather(tile, slot):
        idx_copy = pltpu.make_async_copy(dest_hbm.at[tile], idx[slot], sem_idx)
        idx_copy.start()
        idx_copy.wait()

        def gather(r, carry):
            for k in range(TOP_K):
                _row_copy(ys_hbm, idx[slot][k * tm + r], buf[slot].at[k], r, sem_g.at[slot]).start(priority=k % 2)
            return carry

        lax.fori_loop(0, tm, gather, 0)

    def finish(slot):
        @pl.when(i + 1 < n)
        def _():
            start_gather(i + 1, 1 - slot)

        def drain(r, carry):
            for k in range(TOP_K):
                _row_copy(ys_hbm, 0, buf[slot].at[k], r, sem_g.at[slot]).wait()
            return carry

        lax.fori_loop(0, tm, drain, 0)
        los, his = [], []
        for s in range(ROW_WORDS):
            acc_lo = sh_ref[:, s * LANES:(s + 1) * LANES]
            acc_hi = sh_ref[:, HALF + s * LANES:HALF + (s + 1) * LANES]
            for k in range(TOP_K):
                lo, hi = _unpack_words(buf[slot][k, pl.ds(s, tm, stride=ROW_WORDS), :])
                w = wk_ref[:, k:k + 1]
                acc_lo = acc_lo + lo * w
                acc_hi = acc_hi + hi * w
            los.append(acc_lo)
            his.append(acc_hi)
        f = jnp.concatenate(los + his, axis=-1)
        out = x_ref[...] + mod_ref[0, 5:6, :] * f
        if final:
            y = out * lax.rsqrt(jnp.mean(out * out, axis=-1, keepdims=True) + EPS)
            out = y * gf_ref[...]
        o_ref[...] = out

    @pl.when(i == 0)
    def _():
        start_gather(0, 0)

    for slot in range(2):
        @pl.when(i % 2 == slot)
        def _(slot=slot):
            finish(slot)


def combine(x, shared, mod, wk_t, g_final, dest_tiles, ys, *, ntiles, final):
    tm = 128
    tps = SEQ // tm
    return pl.pallas_call(
        functools.partial(_combine_kernel, final=final),
        out_shape=jax.ShapeDtypeStruct((ntiles * tm, D_MODEL), F32),
        grid=(ntiles,),
        in_specs=[
            pl.BlockSpec((tm, D_MODEL), lambda i: (i, 0)),
            pl.BlockSpec((tm, D_MODEL), lambda i: (i, 0)),
            pl.BlockSpec((1, 6, D_MODEL), lambda i: (_mod_row(i, tps), 0, 0)),
            pl.BlockSpec((tm, TOP_K), lambda i: (i, 0)),
            pl.BlockSpec((1, D_MODEL), lambda i: (0, 0)),
            pl.BlockSpec(memory_space=pl.ANY),
            pl.BlockSpec(memory_space=pl.ANY),
        ],
        out_specs=pl.BlockSpec((tm, D_MODEL), lambda i: (i, 0)),
        scratch_shapes=[
            pltpu.SMEM((TOP_K * tm,), I32),
            pltpu.SMEM((TOP_K * tm,), I32),
            pltpu.VMEM((TOP_K, tm * ROW_WORDS, LANES), I32),
            pltpu.VMEM((TOP_K, tm * ROW_WORDS, LANES), I32),
            pltpu.SemaphoreType.DMA,
            pltpu.SemaphoreType.DMA((2,)),
        ],
        compiler_params=_cparams(("arbitrary",)),
        name="moe_combine",
    )(x, shared, mod, wk_t, g_final.reshape(1, D_MODEL), dest_tiles, ys)


def _routing_plan(eidx):
    t = eidx.shape[1]
    onehot = (eidx[None, :, :] == jnp.arange(N_EXPERTS, dtype=I32)[:, None, None])
    mask = jnp.any(onehot, axis=1).astype(I32)
    incl = jnp.cumsum(mask, axis=1)
    counts = incl[:, -1]
    rank = incl - mask
    pcounts = (counts + STEP_ROWS - 1) // STEP_ROWS * STEP_ROWS
    pends = jnp.cumsum(pcounts)
    pstarts = pends - pcounts
    slot = pstarts[:, None] + rank
    dest = jnp.sum(jnp.where(onehot, slot[:, None, :], 0), axis=0)
    nused = (pends[-1] // STEP_ROWS).astype(I32).reshape(1)
    blocks = jnp.arange(N_BLOCKS, dtype=I32)
    blk_e = jnp.minimum(jnp.sum((pends[None, :] <= (blocks * STEP_ROWS)[:, None]).astype(I32), axis=1),
                        N_EXPERTS - 1).astype(I32)
    nsub_e = (counts + EXPERT_ROWS - 1) // EXPERT_ROWS
    zends = jnp.where(counts > 0, pstarts + nsub_e * EXPERT_ROWS, 0)
    done = (blocks - pstarts[blk_e] // STEP_ROWS) * EXPERT_SUBS
    nsub = jnp.clip(nsub_e[blk_e] - done, 0, EXPERT_SUBS)
    return dest.astype(I32), zends.astype(I32), blk_e, nsub.astype(I32), nused


def _tile_major(dest, tm):
    k, t = dest.shape
    return dest.reshape(k, t // tm, tm).transpose(1, 0, 2).reshape(t // tm, k * tm)


def _rope_tables(n_rows):
    row = jnp.repeat(jnp.arange(n_rows, dtype=F32), GRID_W)
    col = jnp.tile(jnp.arange(GRID_W, dtype=F32), n_rows)
    half = QK_ROPE // 2
    inv = ROPE_BASE ** (-jnp.arange(0, half, 2, dtype=F32) / half)
    ang_r = row[:, None] * inv
    ang_c = col[:, None] * inv
    ang = jnp.concatenate([ang_r, ang_r, ang_c, ang_c], axis=-1)
    n = ang.shape[0]
    ones = jnp.ones((n, QK_NOPE), F32)
    zeros = jnp.zeros((n, QK_NOPE), F32)
    tail = jnp.zeros((n, HEAD_PAD - QK_NOPE - QK_ROPE), F32)
    cos = jnp.concatenate([ones, jnp.cos(ang), tail], axis=-1)
    sin = jnp.concatenate([zeros, jnp.sin(ang), tail], axis=-1)
    return cos, sin


def _identity_tables(n):
    cos = jnp.concatenate([jnp.ones((n, QK_NOPE + QK_ROPE), F32),
                           jnp.zeros((n, HEAD_PAD - QK_NOPE - QK_ROPE), F32)], axis=-1)
    return cos, jnp.zeros((n, HEAD_PAD), F32)


def _rotate_cols(w):
    i = np.arange(QK_ROPE)
    first = (i % (QK_ROPE // 2)) < (QK_ROPE // 4)
    perm = np.where(first, i + QK_ROPE // 4, i - QK_ROPE // 4)
    sign = np.where(first, -1.0, 1.0).astype(np.float32)
    return w[:, perm] * sign


def _mla_weights(g_cq, w_uq, g_ckv, w_ukv):
    dk = QK_NOPE + QK_ROPE
    wq = w_uq.reshape(Q_LORA, MLA_HEADS, dk)
    pad = jnp.zeros((Q_LORA, MLA_HEADS, HEAD_PAD - dk), F32)
    zero_nope = jnp.zeros((Q_LORA, MLA_HEADS, QK_NOPE), F32)
    wq_rot = _rotate_cols(wq[..., QK_NOPE:].reshape(Q_LORA * MLA_HEADS, QK_ROPE)).reshape(
        Q_LORA, MLA_HEADS, QK_ROPE)
    wq_a = jnp.concatenate([wq, pad], axis=-1).reshape(Q_LORA, -1)
    wq_b = jnp.concatenate([zero_nope, wq_rot, pad], axis=-1).reshape(Q_LORA, -1)
    wkv = w_ukv.reshape(KV_LORA, MLA_HEADS, QK_NOPE + V_HEAD)
    wk = jnp.concatenate([wkv[..., :QK_NOPE],
                          jnp.zeros((KV_LORA, MLA_HEADS, HEAD_PAD - QK_NOPE), F32)], axis=-1)
    wv = wkv[..., QK_NOPE:]
    eye = jnp.eye(QK_ROPE, dtype=F32)
    place = jnp.zeros((LANES, MLA_HEADS, HEAD_PAD), F32)
    place_a = place.at[:QK_ROPE, :, QK_NOPE:dk].set(jnp.broadcast_to(eye[:, None, :], (QK_ROPE, MLA_HEADS, QK_ROPE)))
    rot = _rotate_cols(eye)
    place_b = place.at[:QK_ROPE, :, QK_NOPE:dk].set(jnp.broadcast_to(rot[:, None, :], (QK_ROPE, MLA_HEADS, QK_ROPE)))
    return {
        "g_cq": g_cq.reshape(1, -1), "g_ckv": g_ckv.reshape(1, -1),
        "wq2t": jnp.concatenate([wq_a, wq_b], axis=-1).T.astype(BF16),
        "wk": wk.reshape(KV_LORA, -1).astype(BF16),
        "wvt": wv.reshape(KV_LORA, -1).T.astype(BF16),
        "pk": jnp.concatenate([place_a.reshape(LANES, -1), place_b.reshape(LANES, -1)], axis=-1).astype(BF16),
    }


def _pack_w_in(w):
    gates0 = MLA_IN + 2 * CONV_CH + FFT_CH + POOL_CH
    pad = jnp.zeros((D_MODEL, P_MLA_W - MLA_IN), w.dtype)
    return jnp.concatenate([w[:, gates0:], w[:, :MLA_IN], pad, w[:, MLA_IN:gates0]], axis=-1).astype(BF16)


def _dft_tables(n):
    gw = FFT_CH // FFT_GROUPS

    def angles(m):
        j = jnp.arange(m, dtype=I32)
        return (j[:, None] * j[None, :] % m).astype(F32) * (2.0 * np.pi / m)

    ac = angles(gw)
    an = angles(n)
    cs = jnp.concatenate([jnp.cos(ac), jnp.sin(ac)], axis=-1).astype(BF16)
    mseq = jnp.concatenate([jnp.cos(an), -jnp.sin(an)], axis=-1).astype(BF16)
    return cs, mseq


def _mixers(p, row0, nseq, n, wts, tables, conv_args, pool_args, dft):
    q4, k4, v = mla_prep(p, wts, tables, row0=row0, col0=P_MLA, nseq=nseq, n=n)
    cv = conv_branch(p, *conv_args, row0=row0, nseq=nseq, n=n)
    ff = fft_branch(p, dft[0], dft[1], row0=row0, nseq=nseq, n=n)
    po = pool_branch(p, *pool_args, row0=row0, nseq=nseq, n=n)
    return q4, k4, v, cv, ff, po


def kernel(x, c, ctx, c_ctx, w_mod, b_mod, g_norm1, g_norm2, w_in, g_cq, w_uq, g_ckv, w_ukv, w_o_mla, conv_w, conv_b, conv_ln_g, conv_ln_b, w_o_conv, w_o_fft, pool_w, pool_scale, w_o_pool, w_out, w_router, router_bias, w_exp_gate, w_exp_up, w_exp_down, w_sh_gate, w_sh_up, w_sh_down, g_final):
    xa = jnp.concatenate([x.reshape(T_LAT, D_MODEL), ctx.reshape(T_CTX, D_MODEL)], axis=0)
    cvec = jnp.concatenate([c, c_ctx[None, :], jnp.zeros((SUBLANES - BATCH - 1, D_MODEL), F32)], axis=0)
    mod_all = modulation_all(cvec, w_mod, b_mod).reshape(DEPTH, SUBLANES, 6, D_MODEL)
    rope_lat = _rope_tables(SEQ // GRID_W)
    rope_ctx = _identity_tables(CTX_LEN)
    dft_lat = _dft_tables(SEQ)
    dft_ctx = _dft_tables(CTX_LEN)
    lat_tiles = T_LAT // ROW_TILE
    all_tiles = T_ALL // ROW_TILE

    for l in range(DEPTH):
        last = l == DEPTH - 1
        mod = mod_all[l]
        w_in_p = _pack_w_in(w_in[l])
        wts = _mla_weights(g_cq[l], w_uq[l], g_ckv[l], w_ukv[l])
        conv_args = (conv_w[l], conv_b[l], conv_ln_g[l], conv_ln_b[l])
        pool_args = (pool_w[l].astype(BF16), pool_scale[l])
        wo4 = jnp.stack([w_o_mla[l], w_o_conv[l], w_o_fft[l], w_o_pool[l]], axis=0).astype(BF16)

        if last:
            p = normproj(xa, g_norm1[l], mod, w_in_p, tile0=0, ntiles=lat_tiles, col0=0, ncols=P_COLS)
            p_ctx = normproj(xa, g_norm1[l], mod, w_in_p, tile0=lat_tiles, ntiles=all_tiles - lat_tiles,
                             col0=P_MLA, ncols=P_MLA_W)
        else:
            p = normproj(xa, g_norm1[l], mod, w_in_p, tile0=0, ntiles=all_tiles, col0=0, ncols=P_COLS)
        q4, k4, v, cv, ff, po = _mixers(p, 0, BATCH, SEQ, wts, rope_lat, conv_args, pool_args, dft_lat)
        if last:
            qc, kc, vc = mla_prep(p_ctx, wts, rope_ctx, row0=0, col0=0, nseq=BATCH, n=CTX_LEN)
        else:
            qc, kc, vc, cvc, ffc, poc = _mixers(p, T_LAT, BATCH, CTX_LEN, wts, rope_ctx,
                                                conv_args, pool_args, dft_ctx)
        k_all = jnp.concatenate([k4, kc], axis=2)
        v_all = jnp.concatenate([v, vc], axis=2)
        at = attention(q4, k_all, v_all, nq=SEQ, nk=SEQ + CTX_LEN)
        if last:
            ntok = T_LAT
        else:
            atc = attention(qc, kc, vc, nq=CTX_LEN, nk=CTX_LEN)
            at = jnp.concatenate([at, atc], axis=0)
            cv = jnp.concatenate([cv, cvc], axis=0)
            ff = jnp.concatenate([ff, ffc], axis=0)
            po = jnp.concatenate([po, poc], axis=0)
            ntok = T_ALL
        xa = merge(xa, mod, at, cv, ff, po, p, wo4, w_out[l].astype(BF16), ntiles=ntok // 256)

        wr_t = w_router[l].T
        wr_hi = wr_t.astype(BF16)
        wr_lo = (wr_t - wr_hi.astype(F32)).astype(BF16)
        tokens, eidx, wk = norm_router(xa, g_norm2[l], mod, wr_hi, wr_lo, router_bias[l], ntiles=ntok // 256)
        dest, zends, blk_e, nsub, nused = _routing_plan(eidx)
        xs = dispatch(tokens, _tile_major(dest, 256), zends, ntiles=ntok // 256)
        ys = experts(xs, blk_e, nsub, nused, w_exp_gate, w_exp_up, w_exp_down, layer=l)
        sh = shared_expert(tokens, w_sh_gate[l].astype(BF16), w_sh_up[l].astype(BF16),
                           w_sh_down[l].astype(BF16))
        xa = combine(xa, sh, mod, wk.T, g_final, _tile_major(dest, 128), ys, ntiles=ntok // 128, final=last)
    return xa.reshape(BATCH, SEQ, D_MODEL)
```

```python
import functools

import jax
import jax.numpy as jnp
import numpy as np
from jax import lax
from jax.experimental import pallas as pl
from jax.experimental.pallas import tpu as pltpu

F32 = jnp.float32
BF16 = jnp.bfloat16
I32 = jnp.int32

D_MODEL = 2048
BATCH = 4
SEQ = 4096
DEPTH = 2
GRID_W = 64
CTX_LEN = 256
MLA_HEADS = 8
QK_NOPE = 64
QK_ROPE = 32
V_HEAD = 64
Q_LORA = 512
KV_LORA = 256
ROPE_BASE = 10000.0
CONV_CH = 512
CONV_WIDTH = 31
FFT_CH = 512
FFT_GROUPS = 4
POOL_CH = 512
POOL_WINDOWS = (2, 4, 8, 16)
POOL_GROUP = POOL_CH // len(POOL_WINDOWS)
N_BRANCHES = 4
N_EXPERTS = 64
N_EXPERT_GROUPS = 8
TOPK_GROUPS = 4
TOP_K = 8
EXPERT_FF = 512
SHARED_FF = 512
ROUTED_SCALE = 2.5
EPS = 1e-6

MLA_IN = Q_LORA + KV_LORA + QK_ROPE
T_LAT = BATCH * SEQ
T_CTX = BATCH * CTX_LEN
T_ALL = T_LAT + T_CTX

LANES = 128
SUBLANES = 8
VMEM_LIMIT_BYTES = 56 * 1024 * 1024

P_GATE = 0
P_MLA = N_BRANCHES * D_MODEL
P_MLA_W = 1024
P_CONV = P_MLA + P_MLA_W
P_FFT = P_CONV + 2 * CONV_CH
P_POOL = P_FFT + FFT_CH
P_COLS = P_POOL + POOL_CH
HEAD_PAD = 128
ATT_SCALE = (QK_NOPE + QK_ROPE) ** -0.5
Q_SCALE = ATT_SCALE * float(np.log2(np.e))

ROW_TILE = 1024
EXPERT_ROWS = 256
EXPERT_SUBS = 4
STEP_ROWS = EXPERT_ROWS * EXPERT_SUBS
N_BLOCKS = -(-(T_ALL * TOP_K) // STEP_ROWS) + N_EXPERTS


def _cparams(sem, vmem=VMEM_LIMIT_BYTES):
    return pltpu.CompilerParams(dimension_semantics=sem, vmem_limit_bytes=vmem)


def _sigmoid(x):
    return 1.0 / (1.0 + jnp.exp(-x))


def _silu(x):
    return x * _sigmoid(x)


def _mod_row(tile, tiles_per_seq):
    return jnp.minimum(tile // tiles_per_seq, BATCH)


def _modulation_kernel(c_ref, w_ref, b_ref, o_ref):
    c = c_ref[...]
    s = _silu(c).astype(BF16)
    o_ref[0] = jnp.dot(s, w_ref[0].astype(BF16), preferred_element_type=F32) + b_ref[0]


def modulation_all(cvec, w_mod, b_mod):
    tn = 1024
    n = 6 * D_MODEL
    return pl.pallas_call(
        _modulation_kernel,
        out_shape=jax.ShapeDtypeStruct((DEPTH, SUBLANES, n), F32),
        grid=(DEPTH, n // tn),
        in_specs=[
            pl.BlockSpec((SUBLANES, D_MODEL), lambda l, j: (0, 0)),
            pl.BlockSpec((1, D_MODEL, tn), lambda l, j: (l, 0, j)),
            pl.BlockSpec((1, 1, tn), lambda l, j: (l, 0, j)),
        ],
        out_specs=pl.BlockSpec((1, SUBLANES, tn), lambda l, j: (l, 0, j)),
        compiler_params=_cparams(("arbitrary", "arbitrary")),
        name="modulation",
    )(cvec, w_mod, b_mod.reshape(DEPTH, 1, n))


def _adaln(x, g, shift, scale):
    y = x * lax.rsqrt(jnp.mean(x * x, axis=-1, keepdims=True) + EPS)
    return (y * g) * (1.0 + scale) + shift


def _normproj_kernel(x_ref, g_ref, mod_ref, w_ref, o_ref, h_ref, *, shift_row, chunk):
    @pl.when(pl.program_id(1) == 0)
    def _():
        shift = mod_ref[0, shift_row:shift_row + 1, :]
        scale = mod_ref[0, shift_row + 1:shift_row + 2, :]
        for r in range(0, x_ref.shape[0], chunk):
            h = _adaln(x_ref[r:r + chunk, :], g_ref[...], shift, scale)
            h_ref[r:r + chunk, :] = h.astype(BF16)

    o_ref[...] = jnp.dot(h_ref[...], w_ref[...], preferred_element_type=F32).astype(o_ref.dtype)


def normproj(x, g, mod, w, *, tile0, ntiles, col0, ncols, shift_row=0):
    tm, tn = ROW_TILE, 1024
    tps = SEQ // tm
    cb0 = col0 // tn
    return pl.pallas_call(
        functools.partial(_normproj_kernel, shift_row=shift_row, chunk=256),
        out_shape=jax.ShapeDtypeStruct((ntiles * tm, ncols), BF16),
        grid=(ntiles, ncols // tn),
        in_specs=[
            pl.BlockSpec((tm, D_MODEL), lambda i, j: (i + tile0, 0)),
            pl.BlockSpec((1, D_MODEL), lambda i, j: (0, 0)),
            pl.BlockSpec((1, 6, D_MODEL), lambda i, j: (_mod_row(i + tile0, tps), 0, 0)),
            pl.BlockSpec((D_MODEL, tn), lambda i, j: (0, j + cb0)),
        ],
        out_specs=pl.BlockSpec((tm, tn), lambda i, j: (i, j)),
        scratch_shapes=[pltpu.VMEM((tm, D_MODEL), BF16)],
        compiler_params=_cparams(("arbitrary", "arbitrary")),
        name="normproj",
    )(x, g.reshape(1, D_MODEL), mod, w)


def _rms_gain(x, g):
    y = x * lax.rsqrt(jnp.mean(x * x, axis=-1, keepdims=True) + EPS)
    return (y * g).astype(BF16)


_NT = (((1,), (1,)), ((), ()))


def _mla_prep_kernel(p_ref, gq_ref, gkv_ref, wqt_ref, wk_ref, wvt_ref, pk_ref, cos_ref, sin_ref,
                     cost_ref, sint_ref, qt_ref, k_ref, vt_ref):
    hw = MLA_HEADS * HEAD_PAD
    cqn = _rms_gain(p_ref[:, :Q_LORA].astype(F32), gq_ref[...])
    ckvn = _rms_gain(p_ref[:, Q_LORA:Q_LORA + KV_LORA].astype(F32), gkv_ref[...])
    kr = p_ref[:, Q_LORA + KV_LORA:Q_LORA + KV_LORA + LANES]
    q2t = lax.dot_general(wqt_ref[...], cqn, _NT, preferred_element_type=F32)
    vt_ref[0] = lax.dot_general(wvt_ref[...], ckvn, _NT, preferred_element_type=F32).astype(BF16)
    kk = jnp.dot(ckvn, wk_ref[...], preferred_element_type=F32)
    kr2 = jnp.dot(kr, pk_ref[...], preferred_element_type=F32)
    cos, sin = cos_ref[...], sin_ref[...]
    cost, sint = cost_ref[...], sint_ref[...]
    for h in range(MLA_HEADS):
        a = slice(h * HEAD_PAD, (h + 1) * HEAD_PAD)
        b = slice(hw + h * HEAD_PAD, hw + (h + 1) * HEAD_PAD)
        qt_ref[0, h] = ((q2t[a, :] * cost + q2t[b, :] * sint) * Q_SCALE).astype(BF16)
        k_ref[0, h] = (kk[:, a] + kr2[:, a] * cos + kr2[:, b] * sin).astype(BF16)


def mla_prep(p, wts, tables, *, row0, col0, nseq, n):
    tm = 256
    tps = n // tm
    hw = MLA_HEADS * HEAD_PAD
    vw = MLA_HEADS * V_HEAD
    blk0 = row0 // tm
    cblk = col0 // P_MLA_W
    cos_t, sin_t = tables
    const = lambda i: (0, 0)
    return pl.pallas_call(
        _mla_prep_kernel,
        out_shape=(jax.ShapeDtypeStruct((nseq, MLA_HEADS, HEAD_PAD, n), BF16),
                   jax.ShapeDtypeStruct((nseq, MLA_HEADS, n, HEAD_PAD), BF16),
                   jax.ShapeDtypeStruct((nseq, vw, n), BF16)),
        grid=(nseq * tps,),
        in_specs=[
            pl.BlockSpec((tm, P_MLA_W), lambda i: (i + blk0, cblk)),
            pl.BlockSpec((1, Q_LORA), const),
            pl.BlockSpec((1, KV_LORA), const),
            pl.BlockSpec((2 * hw, Q_LORA), const),
            pl.BlockSpec((KV_LORA, hw), const),
            pl.BlockSpec((vw, KV_LORA), const),
            pl.BlockSpec((LANES, 2 * hw), const),
            pl.BlockSpec((tm, HEAD_PAD), lambda i: (i % tps, 0)),
            pl.BlockSpec((tm, HEAD_PAD), lambda i: (i % tps, 0)),
            pl.BlockSpec((HEAD_PAD, tm), lambda i: (0, i % tps)),
            pl.BlockSpec((HEAD_PAD, tm), lambda i: (0, i % tps)),
        ],
        out_specs=(pl.BlockSpec((1, MLA_HEADS, HEAD_PAD, tm), lambda i: (i // tps, 0, 0, i % tps)),
                   pl.BlockSpec((1, MLA_HEADS, tm, HEAD_PAD), lambda i: (i // tps, 0, i % tps, 0)),
                   pl.BlockSpec((1, vw, tm), lambda i: (i // tps, 0, i % tps))),
        compiler_params=_cparams(("arbitrary",)),
        name="mla_prep",
    )(p, wts["g_cq"], wts["g_ckv"], wts["wq2t"], wts["wk"], wts["wvt"], wts["pk"],
      cos_t, sin_t, cos_t.T, sin_t.T)


ATT_HEADS = 4


def _attn_kernel(qt_ref, k_ref, vt_ref, o_ref, s_ref, p_ref, *, nk, ck):
    tq = qt_ref.shape[3]
    chunks = [slice(c, c + ck) for c in range(0, nk, ck)]
    nh = ATT_HEADS
    m = [None] * nh
    l = [jnp.zeros((1, tq), F32) for _ in range(nh)]
    ot = [jnp.zeros((V_HEAD, tq), F32) for _ in range(nh)]
    for step in range(nh + 2):
        ha, hb, hc = step, step - 1, step - 2
        for c in chunks:
            if 0 <= ha < nh:
                s = jnp.dot(k_ref[0, ha, c, :], qt_ref[0, ha], preferred_element_type=F32)
                s_ref[ha % 2, c, :] = s
                mc = jnp.max(s, axis=0, keepdims=True)
                m[ha] = mc if m[ha] is None else jnp.maximum(m[ha], mc)
            if 0 <= hb < nh:
                p = jnp.exp2(s_ref[hb % 2, c, :] - m[hb])
                l[hb] = l[hb] + jnp.sum(p, axis=0, keepdims=True)
                p_ref[hb % 2, c, :] = p.astype(BF16)
            if 0 <= hc < nh:
                ot[hc] = ot[hc] + jnp.dot(vt_ref[0, hc * V_HEAD:(hc + 1) * V_HEAD, c], p_ref[hc % 2, c, :],
                                          preferred_element_type=F32)
    outs = [ot[h] / l[h] for h in range(nh)]
    o_ref[...] = jnp.concatenate(outs, axis=0).T.astype(o_ref.dtype)


def attention(qt4, k4, vt3, *, nq, nk):
    tq = 256
    ck = 256
    nqt = nq // tq
    hs = ATT_HEADS
    return pl.pallas_call(
        functools.partial(_attn_kernel, nk=nk, ck=ck),
        out_shape=jax.ShapeDtypeStruct((BATCH * nq, MLA_HEADS * V_HEAD), BF16),
        grid=(BATCH, MLA_HEADS // hs, nqt),
        in_specs=[
            pl.BlockSpec((1, hs, HEAD_PAD, tq), lambda b, hg, i: (b, hg, 0, i)),
            pl.BlockSpec((1, hs, nk, HEAD_PAD), lambda b, hg, i: (b, hg, 0, 0)),
            pl.BlockSpec((1, hs * V_HEAD, nk), lambda b, hg, i: (b, hg, 0)),
        ],
        out_specs=pl.BlockSpec((tq, hs * V_HEAD), lambda b, hg, i: (b * nqt + i, hg)),
        scratch_shapes=[pltpu.VMEM((2, nk, tq), F32), pltpu.VMEM((2, nk, tq), BF16)],
        compiler_params=_cparams(("arbitrary", "arbitrary", "arbitrary")),
        name="attention",
    )(qt4, k4, vt3)


CONV_PAD = 16


def _conv_kernel(p_ref, w_ref, cb_ref, lg_ref, lb_ref, o_ref, u_ref, *, n, tr):
    zeros = jnp.zeros((CONV_PAD, CONV_CH), F32)
    u_ref[0:CONV_PAD, :] = zeros
    u_ref[CONV_PAD + n:2 * CONV_PAD + n, :] = zeros

    def glu(i, carry):
        r = pl.multiple_of(i * tr, tr)
        a = p_ref[pl.ds(r, tr), 0:CONV_CH].astype(F32)
        b = p_ref[pl.ds(r, tr), CONV_CH:2 * CONV_CH].astype(F32)
        u_ref[pl.ds(CONV_PAD + r, tr), :] = a * _sigmoid(b)
        return carry

    lax.fori_loop(0, n // tr, glu, 0)
    off = CONV_PAD - CONV_WIDTH // 2
    win_rows = tr + 2 * CONV_PAD

    def conv(i, carry):
        r = pl.multiple_of(i * tr, tr)
        strips = []
        for c0 in range(0, CONV_CH, LANES):
            win = u_ref[pl.ds(r, win_rows), c0:c0 + LANES]
            acc = jnp.zeros((tr, LANES), F32)
            for b in range(SUBLANES):
                wb = pltpu.roll(win, win_rows - (off + b), axis=0)
                for k in range(b, CONV_WIDTH, SUBLANES):
                    acc = acc + wb[k - b:k - b + tr, :] * w_ref[k:k + 1, c0:c0 + LANES]
            strips.append(acc)
        acc = jnp.concatenate(strips, axis=-1) + cb_ref[...]
        mu = jnp.mean(acc, axis=-1, keepdims=True)
        d = acc - mu
        var = jnp.mean(d * d, axis=-1, keepdims=True)
        y = (d * lax.rsqrt(var + EPS)) * lg_ref[...] + lb_ref[...]
        o_ref[pl.ds(r, tr), :] = _silu(y).astype(o_ref.dtype)
        return carry

    lax.fori_loop(0, n // tr, conv, 0)


def conv_branch(p, conv_w, conv_b, ln_g, ln_b, *, row0, nseq, n):
    tr = 128
    blk0 = row0 // n
    const = lambda s: (0, 0)
    return pl.pallas_call(
        functools.partial(_conv_kernel, n=n, tr=tr),
        out_shape=jax.ShapeDtypeStruct((nseq * n, CONV_CH), BF16),
        grid=(nseq,),
        in_specs=[
            pl.BlockSpec((n, 2 * CONV_CH), lambda s: (s + blk0, P_CONV // (2 * CONV_CH))),
            pl.BlockSpec((CONV_WIDTH, CONV_CH), const),
            pl.BlockSpec((1, CONV_CH), const),
            pl.BlockSpec((1, CONV_CH), const),
            pl.BlockSpec((1, CONV_CH), const),
        ],
        out_specs=pl.BlockSpec((n, CONV_CH), lambda s: (s, 0)),
        scratch_shapes=[pltpu.VMEM((n + 2 * CONV_PAD, CONV_CH), F32)],
        compiler_params=_cparams(("arbitrary",)),
        name="conv_branch",
    )(p, conv_w, conv_b.reshape(1, -1), ln_g.reshape(1, -1), ln_b.reshape(1, -1))


def _fft_kernel(z_ref, cs_ref, m_ref, o_ref, ab_ref, *, n, tc):
    gw = FFT_CH // FFT_GROUPS

    @pl.when(pl.program_id(1) == 0)
    def _():
        def chan(i, carry):
            r = pl.multiple_of(i * tc, tc)
            for g in range(FFT_GROUPS):
                zg = z_ref[pl.ds(r, tc), g * gw:(g + 1) * gw]
                ab = jnp.dot(zg, cs_ref[...], preferred_element_type=F32)
                ab_ref[pl.ds(r, tc), g * gw:(g + 1) * gw] = ab[:, :gw].astype(BF16)
                ab_ref[pl.ds(n + r, tc), g * gw:(g + 1) * gw] = ab[:, gw:].astype(BF16)
            return carry

        lax.fori_loop(0, n // tc, chan, 0)

    norm = 1.0 / float(np.sqrt(n * gw))
    o_ref[...] = (jnp.dot(m_ref[...], ab_ref[...], preferred_element_type=F32) * norm).astype(o_ref.dtype)


def fft_branch(p, cs, mseq, *, row0, nseq, n):
    tm = min(n, 512)
    tc = min(n, 512)
    blk0 = row0 // n
    nt = n // tm
    return pl.pallas_call(
        functools.partial(_fft_kernel, n=n, tc=tc),
        out_shape=jax.ShapeDtypeStruct((nseq * n, FFT_CH), BF16),
        grid=(nseq, nt),
        in_specs=[
            pl.BlockSpec((n, FFT_CH), lambda s, i: (s + blk0, P_FFT // FFT_CH)),
            pl.BlockSpec((FFT_CH // FFT_GROUPS, 2 * FFT_CH // FFT_GROUPS), lambda s, i: (0, 0)),
            pl.BlockSpec((tm, 2 * n), lambda s, i: (i, 0)),
        ],
        out_specs=pl.BlockSpec((tm, FFT_CH), lambda s, i: (s * nt + i, 0)),
        scratch_shapes=[pltpu.VMEM((2 * n, FFT_CH), BF16)],
        compiler_params=_cparams(("arbitrary", "arbitrary")),
        name="fft_branch",
    )(p, cs, mseq)


POOL_PAD = 16


def _pool_kernel(z_ref, pw_ref, ps_ref, o_ref, zp_ref, d_ref, *, n, tr, tc):
    zeros = jnp.zeros((POOL_PAD, POOL_CH), F32)
    zp_ref[0:POOL_PAD, :] = zeros
    zp_ref[POOL_PAD + n:2 * POOL_PAD + n, :] = zeros

    def fill(i, carry):
        r = pl.multiple_of(i * tc, tc)
        zp_ref[pl.ds(POOL_PAD + r, tc), :] = z_ref[pl.ds(r, tc), :].astype(F32)
        return carry

    lax.fori_loop(0, n // tc, fill, 0)

    win_rows = tr + 2 * POOL_PAD

    def pool(i, carry):
        r = pl.multiple_of(i * tr, tr)
        t = (r + lax.broadcasted_iota(I32, (tr, POOL_GROUP), 0)).astype(F32)
        for gi, w in enumerate(POOL_WINDOWS):
            cols = slice(gi * POOL_GROUP, (gi + 1) * POOL_GROUP)
            win = zp_ref[pl.ds(r, win_rows), cols]
            s = jnp.zeros((tr, POOL_GROUP), F32)
            for j in range(w):
                start = POOL_PAD - w // 2 + j
                if start % SUBLANES == 0:
                    s = s + win[start:start + tr, :]
                else:
                    s = s + pltpu.roll(win, win_rows - start, axis=0)[0:tr, :]
            lo = jnp.maximum(t - (w // 2), 0.0)
            hi = jnp.minimum(t - (w // 2) + w, float(n))
            z = win[POOL_PAD:POOL_PAD + tr, :]
            d_ref[pl.ds(r, tr), cols] = (s / (hi - lo) - z).astype(BF16)
        return carry

    lax.fori_loop(0, n // tr, pool, 0)

    def proj(i, carry):
        r = pl.multiple_of(i * tc, tc)
        for gi in range(len(POOL_WINDOWS)):
            cols = slice(gi * POOL_GROUP, (gi + 1) * POOL_GROUP)
            y = jnp.dot(d_ref[pl.ds(r, tc), cols], pw_ref[gi], preferred_element_type=F32)
            o_ref[pl.ds(r, tc), cols] = (y * ps_ref[:, cols]).astype(o_ref.dtype)
        return carry

    lax.fori_loop(0, n // tc, proj, 0)


def pool_branch(p, pool_w, pool_scale, *, row0, nseq, n):
    tr = 128
    tc = min(n, 512)
    blk0 = row0 // n
    return pl.pallas_call(
        functools.partial(_pool_kernel, n=n, tr=tr, tc=tc),
        out_shape=jax.ShapeDtypeStruct((nseq * n, POOL_CH), BF16),
        grid=(nseq,),
        in_specs=[
            pl.BlockSpec((n, POOL_CH), lambda s: (s + blk0, P_POOL // POOL_CH)),
            pl.BlockSpec((len(POOL_WINDOWS), POOL_GROUP, POOL_GROUP), lambda s: (0, 0, 0)),
            pl.BlockSpec((1, POOL_CH), lambda s: (0, 0)),
        ],
        out_specs=pl.BlockSpec((n, POOL_CH), lambda s: (s, 0)),
        scratch_shapes=[pltpu.VMEM((n + 2 * POOL_PAD, POOL_CH), F32),
                        pltpu.VMEM((n, POOL_CH), BF16)],
        compiler_params=_cparams(("arbitrary",)),
        name="pool_branch",
    )(p, pool_w, pool_scale.reshape(1, -1))


def _merge_kernel(x_ref, mod_ref, at_ref, cv_ref, ff_ref, po_ref, g_ref, wo_ref, wout_ref, o_ref):
    mix = None
    for b, br in enumerate((at_ref, cv_ref, ff_ref, po_ref)):
        y = jnp.dot(br[...], wo_ref[b], preferred_element_type=F32)
        gate = _sigmoid(g_ref[:, b * D_MODEL:(b + 1) * D_MODEL].astype(F32))
        mix = gate * y if mix is None else mix + gate * y
    out = jnp.dot(mix.astype(BF16), wout_ref[...], preferred_element_type=F32)
    o_ref[...] = x_ref[...] + mod_ref[0, 2:3, :] * out


def merge(x, mod, attn, conv, fft, pool, p, wo4, w_out, *, ntiles):
    tm = 256
    tps = SEQ // tm
    br_spec = pl.BlockSpec((tm, 512), lambda i: (i, 0))
    return pl.pallas_call(
        _merge_kernel,
        out_shape=jax.ShapeDtypeStruct((ntiles * tm, D_MODEL), F32),
        grid=(ntiles,),
        in_specs=[
            pl.BlockSpec((tm, D_MODEL), lambda i: (i, 0)),
            pl.BlockSpec((1, 6, D_MODEL), lambda i: (_mod_row(i, tps), 0, 0)),
            br_spec, br_spec, br_spec, br_spec,
            pl.BlockSpec((tm, N_BRANCHES * D_MODEL), lambda i: (i, P_GATE // (N_BRANCHES * D_MODEL))),
            pl.BlockSpec((N_BRANCHES, 512, D_MODEL), lambda i: (0, 0, 0), pipeline_mode=pl.Buffered(1)),
            pl.BlockSpec((D_MODEL, D_MODEL), lambda i: (0, 0), pipeline_mode=pl.Buffered(1)),
        ],
        out_specs=pl.BlockSpec((tm, D_MODEL), lambda i: (i, 0)),
        compiler_params=_cparams(("arbitrary",)),
        name="merge",
    )(x, mod, attn, conv, fft, pool, p, wo4, w_out)


def _first_index(hit_value, cand, ids, big):
    return jnp.min(jnp.where(cand == hit_value, ids, big), axis=0, keepdims=True)


HALF = D_MODEL // 2
ROW_WORDS = HALF // LANES


def _round_bf16_bits(x):
    u = lax.bitcast_convert_type(x, I32)
    odd = lax.shift_right_logical(u, 16) & 1
    return (u + 0x7FFF + odd) & jnp.int32(-65536)


def _pack_words(lo, hi):
    return lax.shift_right_logical(_round_bf16_bits(lo), 16) | _round_bf16_bits(hi)


def _unpack_words(w):
    lo = lax.bitcast_convert_type(lax.shift_left(w, 16), F32)
    hi = lax.bitcast_convert_type(w & jnp.int32(-65536), F32)
    return lo, hi


def _store_packed(ref, x, tok0=0):
    tm = x.shape[0]
    for s in range(ROW_WORDS):
        w = _pack_words(x[:, s * LANES:(s + 1) * LANES], x[:, HALF + s * LANES:HALF + (s + 1) * LANES])
        ref[pl.ds(tok0 * ROW_WORDS + s, tm, stride=ROW_WORDS), :] = w


def _load_packed(ref, tm, dtype, tok0=0):
    los, his = [], []
    for s in range(ROW_WORDS):
        lo, hi = _unpack_words(ref[pl.ds(tok0 * ROW_WORDS + s, tm, stride=ROW_WORDS), :])
        los.append(lo.astype(dtype))
        his.append(hi.astype(dtype))
    return jnp.concatenate(los + his, axis=-1)


def _router_kernel(x_ref, g_ref, mod_ref, wrh_ref, wrl_ref, rb_ref, tok_ref, eidx_ref, wk_ref):
    tm = x_ref.shape[0]
    h = _adaln(x_ref[...], g_ref[...], mod_ref[0, 3:4, :], mod_ref[0, 4:5, :])
    _store_packed(tok_ref, h)
    hh = h.astype(BF16)
    hl = (h - hh.astype(F32)).astype(BF16)
    nt = (((1,), (1,)), ((), ()))
    logits = (lax.dot_general(wrh_ref[...], hh, nt, preferred_element_type=F32)
              + lax.dot_general(wrh_ref[...], hl, nt, preferred_element_type=F32)
              + lax.dot_general(wrl_ref[...], hh, nt, preferred_element_type=F32))
    scores = _sigmoid(logits)
    sel = scores + rb_ref[...]
    per = N_EXPERTS // N_EXPERT_GROUPS
    assert per == SUBLANES and N_EXPERT_GROUPS == SUBLANES and TOP_K == SUBLANES
    neg = -jnp.inf
    sub = lax.broadcasted_iota(I32, (SUBLANES, tm), 0).astype(F32)
    sg = [sel[g * per:(g + 1) * per, :] for g in range(N_EXPERT_GROUPS)]
    sc = [scores[g * per:(g + 1) * per, :] for g in range(N_EXPERT_GROUPS)]
    gsc = jnp.zeros((SUBLANES, tm), F32)
    for g in range(N_EXPERT_GROUPS):
        m1 = jnp.max(sg[g], axis=0, keepdims=True)
        i1 = _first_index(m1, sg[g], sub, float(per))
        m2 = jnp.max(jnp.where(sub == i1, neg, sg[g]), axis=0, keepdims=True)
        gsc = jnp.where(sub == float(g), m1 + m2, gsc)
    gsel = jnp.zeros((SUBLANES, tm), F32)
    for _ in range(TOPK_GROUPS):
        m = jnp.max(gsc, axis=0, keepdims=True)
        hit = sub == _first_index(m, gsc, sub, float(N_EXPERT_GROUPS))
        gsel = jnp.where(hit, 1.0, gsel)
        gsc = jnp.where(hit, neg, gsc)
    cand = []
    for g in range(N_EXPERT_GROUPS):
        allowed = jnp.max(jnp.where(sub == float(g), gsel, 0.0), axis=0, keepdims=True)
        cand.append(jnp.where(allowed > 0.0, sg[g], neg))
    eid = [sub + float(g * per) for g in range(N_EXPERT_GROUPS)]
    idxs = jnp.zeros((SUBLANES, tm), F32)
    vals = jnp.zeros((SUBLANES, tm), F32)
    for k in range(TOP_K):
        m = functools.reduce(jnp.maximum, [jnp.max(c, axis=0, keepdims=True) for c in cand])
        idx = functools.reduce(
            jnp.minimum, [_first_index(m, cand[g], eid[g], float(N_EXPERTS)) for g in range(N_EXPERT_GROUPS)])
        val = jnp.zeros((1, tm), F32)
        for g in range(N_EXPERT_GROUPS):
            hit = eid[g] == idx
            val = val + jnp.sum(jnp.where(hit, sc[g], 0.0), axis=0, keepdims=True)
            cand[g] = jnp.where(hit, neg, cand[g])
        idxs = jnp.where(sub == float(k), idx, idxs)
        vals = jnp.where(sub == float(k), val, vals)
    eidx_ref[...] = idxs.astype(I32)
    wk_ref[...] = vals / jnp.sum(vals, axis=0, keepdims=True) * ROUTED_SCALE


def norm_router(x, g, mod, wr_hi, wr_lo, rbias, *, ntiles):
    tm = 256
    tps = SEQ // tm
    t = ntiles * tm
    return pl.pallas_call(
        _router_kernel,
        out_shape=(jax.ShapeDtypeStruct((t * ROW_WORDS, LANES), I32),
                   jax.ShapeDtypeStruct((TOP_K, t), I32),
                   jax.ShapeDtypeStruct((TOP_K, t), F32)),
        grid=(ntiles,),
        in_specs=[
            pl.BlockSpec((tm, D_MODEL), lambda i: (i, 0)),
            pl.BlockSpec((1, D_MODEL), lambda i: (0, 0)),
            pl.BlockSpec((1, 6, D_MODEL), lambda i: (_mod_row(i, tps), 0, 0)),
            pl.BlockSpec((N_EXPERTS, D_MODEL), lambda i: (0, 0)),
            pl.BlockSpec((N_EXPERTS, D_MODEL), lambda i: (0, 0)),
            pl.BlockSpec((N_EXPERTS, 1), lambda i: (0, 0)),
        ],
        out_specs=(pl.BlockSpec((tm * ROW_WORDS, LANES), lambda i: (i, 0)),
                   pl.BlockSpec((TOP_K, tm), lambda i: (0, i)),
                   pl.BlockSpec((TOP_K, tm), lambda i: (0, i))),
        compiler_params=_cparams(("arbitrary",)),
        name="norm_router",
    )(x, g.reshape(1, D_MODEL), mod, wr_hi, wr_lo, rbias.reshape(N_EXPERTS, 1))


def _row_copy(src, src_tok, dst, dst_tok, sem):
    s = pl.multiple_of(src_tok * ROW_WORDS, ROW_WORDS)
    d = pl.multiple_of(dst_tok * ROW_WORDS, ROW_WORDS)
    return pltpu.make_async_copy(src.at[pl.ds(s, ROW_WORDS)], dst.at[pl.ds(d, ROW_WORDS)], sem)


def _dispatch_kernel(zends_ref, tok_ref, dest_hbm, wg_ref, wu_ref, wd_ref, xs_hbm, sh_ref,
                     dest_smem, zero_ref, sem_idx, sem_z, sem_s):
    i = pl.program_id(0)
    tm = tok_ref.shape[0] // ROW_WORDS
    idx_copy = pltpu.make_async_copy(dest_hbm.at[i], dest_smem, sem_idx)
    idx_copy.start()
    blk = EXPERT_ROWS * ROW_WORDS

    def pad_copy(e):
        row = pl.multiple_of((zends_ref[e] - EXPERT_ROWS) * ROW_WORDS, blk)
        return pltpu.make_async_copy(zero_ref, xs_hbm.at[pl.ds(row, blk)], sem_z)

    @pl.when(i == 0)
    def _():
        zero_ref[...] = jnp.zeros_like(zero_ref)

        def start(e, carry):
            @pl.when(zends_ref[e] > 0)
            def _():
                pad_copy(e).start()
            return carry

        lax.fori_loop(0, N_EXPERTS, start, 0)

        def wait(e, carry):
            @pl.when(zends_ref[e] > 0)
            def _():
                pad_copy(e).wait()
            return carry

        lax.fori_loop(0, N_EXPERTS, wait, 0)

    idx_copy.wait()

    def scatter(r, carry):
        for k in range(TOP_K):
            _row_copy(tok_ref, r, xs_hbm, dest_smem[k * tm + r], sem_s).start(priority=k % 2)
        return carry

    lax.fori_loop(0, tm, scatter, 0)

    x = _load_packed(tok_ref, tm, BF16)
    g = jnp.dot(x, wg_ref[...], preferred_element_type=F32)
    u = jnp.dot(x, wu_ref[...], preferred_element_type=F32)
    sh_ref[...] = jnp.dot((_silu(g) * u).astype(BF16), wd_ref[...], preferred_element_type=F32)

    def drain(r, carry):
        for k in range(TOP_K):
            _row_copy(tok_ref, r, xs_hbm, dest_smem[k * tm + r], sem_s).wait()
        return carry

    lax.fori_loop(0, tm, drain, 0)


def dispatch_shared(tokens, dest_tiles, zends, wg, wu, wd, *, ntiles):
    tm = tokens.shape[0] // ROW_WORDS // ntiles
    const = lambda i, zends: (0, 0)
    return pl.pallas_call(
        _dispatch_kernel,
        out_shape=(jax.ShapeDtypeStruct((N_BLOCKS * STEP_ROWS * ROW_WORDS, LANES), I32),
                   jax.ShapeDtypeStruct((ntiles * tm, D_MODEL), F32)),
        grid_spec=pltpu.PrefetchScalarGridSpec(
            num_scalar_prefetch=1,
            grid=(ntiles,),
            in_specs=[
                pl.BlockSpec((tm * ROW_WORDS, LANES), lambda i, zends: (i, 0)),
                pl.BlockSpec(memory_space=pl.ANY),
                pl.BlockSpec((D_MODEL, SHARED_FF), const),
                pl.BlockSpec((D_MODEL, SHARED_FF), const),
                pl.BlockSpec((SHARED_FF, D_MODEL), const),
            ],
            out_specs=(pl.BlockSpec(memory_space=pl.ANY),
                       pl.BlockSpec((tm, D_MODEL), lambda i, zends: (i, 0))),
            scratch_shapes=[
                pltpu.SMEM((TOP_K * tm,), I32),
                pltpu.VMEM((EXPERT_ROWS * ROW_WORDS, LANES), I32),
                pltpu.SemaphoreType.DMA,
                pltpu.SemaphoreType.DMA,
                pltpu.SemaphoreType.DMA,
            ],
        ),
        compiler_params=_cparams(("arbitrary",)),
        name="moe_dispatch",
    )(zends, tokens, dest_tiles, wg, wu, wd)


def _experts_kernel(blk_e_ref, nsub_ref, next_e_ref, nused_ref, xs_ref, wg_hbm, wu_hbm, wd_hbm, y_ref,
                    wgs, wus, wds, wgb, wub, wdb, sem_w, *, layer):
    i = pl.program_id(0)
    e = blk_e_ref[i]
    prev = blk_e_ref[jnp.maximum(i - 1, 0)]

    def weight_copies(expert):
        row = layer * N_EXPERTS + expert
        return (pltpu.make_async_copy(wg_hbm.at[row], wgs, sem_w.at[0]),
                pltpu.make_async_copy(wu_hbm.at[row], wus, sem_w.at[1]),
                pltpu.make_async_copy(wd_hbm.at[row], wds, sem_w.at[2]))

    @pl.when(i < nused_ref[0])
    def _():
        @pl.when(i == 0)
        def _():
            for cp in weight_copies(e):
                cp.start()

        @pl.when((i == 0) | (e != prev))
        def _():
            for cp in weight_copies(e):
                cp.wait()
            wgb[...] = wgs[...].astype(BF16)
            wub[...] = wus[...].astype(BF16)
            wdb[...] = wds[...].astype(BF16)

            @pl.when(next_e_ref[i] >= 0)
            def _():
                for cp in weight_copies(next_e_ref[i]):
                    cp.start()

        for sb in range(EXPERT_SUBS):
            tok0 = sb * EXPERT_ROWS

            @pl.when(sb < nsub_ref[i])
            def _():
                x = _load_packed(xs_ref, EXPERT_ROWS, BF16, tok0)
                g = jnp.dot(x, wgb[...], preferred_element_type=F32)
                u = jnp.dot(x, wub[...], preferred_element_type=F32)
                hb = (_silu(g) * u).astype(BF16)
                _store_packed(y_ref, jnp.dot(hb, wdb[...], preferred_element_type=F32), tok0)

            @pl.when(sb >= nsub_ref[i])
            def _():
                rows = EXPERT_ROWS * ROW_WORDS
                y_ref[tok0 * ROW_WORDS:tok0 * ROW_WORDS + rows, :] = jnp.zeros((rows, LANES), I32)


def experts(xs, blk_e, nsub, next_e, nused, w_gate, w_up, w_down, *, layer):
    w_gate = w_gate.reshape(DEPTH * N_EXPERTS, D_MODEL, EXPERT_FF)
    w_up = w_up.reshape(DEPTH * N_EXPERTS, D_MODEL, EXPERT_FF)
    w_down = w_down.reshape(DEPTH * N_EXPERTS, EXPERT_FF, D_MODEL)

    def row_map(i, blk_e, nsub, next_e, nused):
        return (jnp.minimum(i, nused[0] - 1), 0)

    return pl.pallas_call(
        functools.partial(_experts_kernel, layer=layer),
        out_shape=jax.ShapeDtypeStruct((N_BLOCKS * STEP_ROWS * ROW_WORDS, LANES), I32),
        grid_spec=pltpu.PrefetchScalarGridSpec(
            num_scalar_prefetch=4,
            grid=(N_BLOCKS,),
            in_specs=[
                pl.BlockSpec((STEP_ROWS * ROW_WORDS, LANES), row_map),
                pl.BlockSpec(memory_space=pl.ANY),
                pl.BlockSpec(memory_space=pl.ANY),
                pl.BlockSpec(memory_space=pl.ANY),
            ],
            out_specs=pl.BlockSpec((STEP_ROWS * ROW_WORDS, LANES), row_map),
            scratch_shapes=[
                pltpu.VMEM((D_MODEL, EXPERT_FF), F32),
                pltpu.VMEM((D_MODEL, EXPERT_FF), F32),
                pltpu.VMEM((EXPERT_FF, D_MODEL), F32),
                pltpu.VMEM((D_MODEL, EXPERT_FF), BF16),
                pltpu.VMEM((D_MODEL, EXPERT_FF), BF16),
                pltpu.VMEM((EXPERT_FF, D_MODEL), BF16),
                pltpu.SemaphoreType.DMA((3,)),
            ],
        ),
        compiler_params=_cparams(("arbitrary",)),
        name="moe_experts",
    )(blk_e, nsub, next_e, nused, xs, w_gate, w_up, w_down)


def _combine_kernel(x_ref, sh_ref, mod_ref, wk_ref, gf_ref, dest_hbm, ys_hbm, o_ref,
                    idx0, idx1, buf0, buf1, sem_idx, sem_g, *, final):
    i = pl.program_id(0)
    n = pl.num_programs(0)
    tm = x_ref.shape[0]
    idx = (idx0, idx1)
    buf = (buf0, buf1)

    def start_gather(tile, slot):
        idx_copy = pltpu.make_async_copy(dest_hbm.at[tile], idx[slot], sem_idx)
        idx_copy.start()
        idx_copy.wait()

        def gather(r, carry):
            for k in range(TOP_K):
                _row_copy(ys_hbm, idx[slot][k * tm + r], buf[slot].at[k], r, sem_g.at[slot]).start(priority=k % 2)
            return carry

        lax.fori_loop(0, tm, gather, 0)

    def finish(slot):
        @pl.when(i + 1 < n)
        def _():
            start_gather(i + 1, 1 - slot)

        def drain(r, carry):
            for k in range(TOP_K):
                _row_copy(ys_hbm, 0, buf[slot].at[k], r, sem_g.at[slot]).wait()
            return carry

        lax.fori_loop(0, tm, drain, 0)
        los, his = [], []
        for s in range(ROW_WORDS):
            acc_lo = sh_ref[:, s * LANES:(s + 1) * LANES]
            acc_hi = sh_ref[:, HALF + s * LANES:HALF + (s + 1) * LANES]
            for k in range(TOP_K):
                lo, hi = _unpack_words(buf[slot][k, pl.ds(s, tm, stride=ROW_WORDS), :])
                w = wk_ref[:, k:k + 1]
                acc_lo = acc_lo + lo * w
                acc_hi = acc_hi + hi * w
            los.append(acc_lo)
            his.append(acc_hi)
        f = jnp.concatenate(los + his, axis=-1)
        out = x_ref[...] + mod_ref[0, 5:6, :] * f
        if final:
            y = out * lax.rsqrt(jnp.mean(out * out, axis=-1, keepdims=True) + EPS)
            out = y * gf_ref[...]
        o_ref[...] = out

    @pl.when(i == 0)
    def _():
        start_gather(0, 0)

    for slot in range(2):
        @pl.when(i % 2 == slot)
        def _(slot=slot):
            finish(slot)


def combine(x, shared, mod, wk_t, g_final, dest_tiles, ys, *, ntiles, final):
    tm = 128
    tps = SEQ // tm
    return pl.pallas_call(
        functools.partial(_combine_kernel, final=final),
        out_shape=jax.ShapeDtypeStruct((ntiles * tm, D_MODEL), F32),
        grid=(ntiles,),
        in_specs=[
            pl.BlockSpec((tm, D_MODEL), lambda i: (i, 0)),
            pl.BlockSpec((tm, D_MODEL), lambda i: (i, 0)),
            pl.BlockSpec((1, 6, D_MODEL), lambda i: (_mod_row(i, tps), 0, 0)),
            pl.BlockSpec((tm, TOP_K), lambda i: (i, 0)),
            pl.BlockSpec((1, D_MODEL), lambda i: (0, 0)),
            pl.BlockSpec(memory_space=pl.ANY),
            pl.BlockSpec(memory_space=pl.ANY),
        ],
        out_specs=pl.BlockSpec((tm, D_MODEL), lambda i: (i, 0)),
        scratch_shapes=[
            pltpu.SMEM((TOP_K * tm,), I32),
            pltpu.SMEM((TOP_K * tm,), I32),
            pltpu.VMEM((TOP_K, tm * ROW_WORDS, LANES), I32),
            pltpu.VMEM((TOP_K, tm * ROW_WORDS, LANES), I32),
            pltpu.SemaphoreType.DMA,
            pltpu.SemaphoreType.DMA((2,)),
        ],
        compiler_params=_cparams(("arbitrary",)),
        name="moe_combine",
    )(x, shared, mod, wk_t, g_final.reshape(1, D_MODEL), dest_tiles, ys)


def _routing_plan(eidx):
    t = eidx.shape[1]
    onehot = (eidx[None, :, :] == jnp.arange(N_EXPERTS, dtype=I32)[:, None, None])
    mask = jnp.any(onehot, axis=1).astype(I32)
    incl = jnp.cumsum(mask, axis=1)
    counts = incl[:, -1]
    rank = incl - mask
    pcounts = (counts + STEP_ROWS - 1) // STEP_ROWS * STEP_ROWS
    pends = jnp.cumsum(pcounts)
    pstarts = pends - pcounts
    slot = pstarts[:, None] + rank
    dest = jnp.sum(jnp.where(onehot, slot[:, None, :], 0), axis=0)
    nused = (pends[-1] // STEP_ROWS).astype(I32).reshape(1)
    blocks = jnp.arange(N_BLOCKS, dtype=I32)
    blk_e = jnp.minimum(jnp.sum((pends[None, :] <= (blocks * STEP_ROWS)[:, None]).astype(I32), axis=1),
                        N_EXPERTS - 1).astype(I32)
    nsub_e = (counts + EXPERT_ROWS - 1) // EXPERT_ROWS
    zends = jnp.where(counts > 0, pstarts + nsub_e * EXPERT_ROWS, 0)
    done = (blocks - pstarts[blk_e] // STEP_ROWS) * EXPERT_SUBS
    nsub = jnp.clip(nsub_e[blk_e] - done, 0, EXPERT_SUBS)
    ids = jnp.where(counts > 0, jnp.arange(N_EXPERTS, dtype=I32), N_EXPERTS)
    later = jnp.concatenate([lax.cummin(ids[::-1])[::-1][1:], jnp.full((1,), N_EXPERTS, I32)])
    next_e = jnp.where(later < N_EXPERTS, later, -1)[blk_e]
    return dest.astype(I32), zends.astype(I32), blk_e, nsub.astype(I32), next_e.astype(I32), nused


def _tile_major(dest, tm):
    k, t = dest.shape
    return dest.reshape(k, t // tm, tm).transpose(1, 0, 2).reshape(t // tm, k * tm)


def _rope_tables(n_rows):
    row = jnp.repeat(jnp.arange(n_rows, dtype=F32), GRID_W)
    col = jnp.tile(jnp.arange(GRID_W, dtype=F32), n_rows)
    half = QK_ROPE // 2
    inv = ROPE_BASE ** (-jnp.arange(0, half, 2, dtype=F32) / half)
    ang_r = row[:, None] * inv
    ang_c = col[:, None] * inv
    ang = jnp.concatenate([ang_r, ang_r, ang_c, ang_c], axis=-1)
    n = ang.shape[0]
    ones = jnp.ones((n, QK_NOPE), F32)
    zeros = jnp.zeros((n, QK_NOPE), F32)
    tail = jnp.zeros((n, HEAD_PAD - QK_NOPE - QK_ROPE), F32)
    cos = jnp.concatenate([ones, jnp.cos(ang), tail], axis=-1)
    sin = jnp.concatenate([zeros, jnp.sin(ang), tail], axis=-1)
    return cos, sin


def _identity_tables(n):
    cos = jnp.concatenate([jnp.ones((n, QK_NOPE + QK_ROPE), F32),
                           jnp.zeros((n, HEAD_PAD - QK_NOPE - QK_ROPE), F32)], axis=-1)
    return cos, jnp.zeros((n, HEAD_PAD), F32)


def _rotate_cols(w):
    i = np.arange(QK_ROPE)
    first = (i % (QK_ROPE // 2)) < (QK_ROPE // 4)
    perm = np.where(first, i + QK_ROPE // 4, i - QK_ROPE // 4)
    sign = np.where(first, -1.0, 1.0).astype(np.float32)
    return w[:, perm] * sign


def _mla_weights(g_cq, w_uq, g_ckv, w_ukv):
    dk = QK_NOPE + QK_ROPE
    wq = w_uq.reshape(Q_LORA, MLA_HEADS, dk)
    pad = jnp.zeros((Q_LORA, MLA_HEADS, HEAD_PAD - dk), F32)
    zero_nope = jnp.zeros((Q_LORA, MLA_HEADS, QK_NOPE), F32)
    wq_rot = _rotate_cols(wq[..., QK_NOPE:].reshape(Q_LORA * MLA_HEADS, QK_ROPE)).reshape(
        Q_LORA, MLA_HEADS, QK_ROPE)
    wq_a = jnp.concatenate([wq, pad], axis=-1).reshape(Q_LORA, -1)
    wq_b = jnp.concatenate([zero_nope, wq_rot, pad], axis=-1).reshape(Q_LORA, -1)
    wkv = w_ukv.reshape(KV_LORA, MLA_HEADS, QK_NOPE + V_HEAD)
    wk = jnp.concatenate([wkv[..., :QK_NOPE],
                          jnp.zeros((KV_LORA, MLA_HEADS, HEAD_PAD - QK_NOPE), F32)], axis=-1)
    wv = wkv[..., QK_NOPE:]
    eye = jnp.eye(QK_ROPE, dtype=F32)
    place = jnp.zeros((LANES, MLA_HEADS, HEAD_PAD), F32)
    place_a = place.at[:QK_ROPE, :, QK_NOPE:dk].set(jnp.broadcast_to(eye[:, None, :], (QK_ROPE, MLA_HEADS, QK_ROPE)))
    rot = _rotate_cols(eye)
    place_b = place.at[:QK_ROPE, :, QK_NOPE:dk].set(jnp.broadcast_to(rot[:, None, :], (QK_ROPE, MLA_HEADS, QK_ROPE)))
    return {
        "g_cq": g_cq.reshape(1, -1), "g_ckv": g_ckv.reshape(1, -1),
        "wq2t": jnp.concatenate([wq_a, wq_b], axis=-1).T.astype(BF16),
        "wk": wk.reshape(KV_LORA, -1).astype(BF16),
        "wvt": wv.reshape(KV_LORA, -1).T.astype(BF16),
        "pk": jnp.concatenate([place_a.reshape(LANES, -1), place_b.reshape(LANES, -1)], axis=-1).astype(BF16),
    }


def _pack_w_in(w):
    gates0 = MLA_IN + 2 * CONV_CH + FFT_CH + POOL_CH
    pad = jnp.zeros((D_MODEL, P_MLA_W - MLA_IN), w.dtype)
    return jnp.concatenate([w[:, gates0:], w[:, :MLA_IN], pad, w[:, MLA_IN:gates0]], axis=-1).astype(BF16)


def _dft_tables(n):
    gw = FFT_CH // FFT_GROUPS

    def angles(m):
        j = jnp.arange(m, dtype=I32)
        return (j[:, None] * j[None, :] % m).astype(F32) * (2.0 * np.pi / m)

    ac = angles(gw)
    an = angles(n)
    cs = jnp.concatenate([jnp.cos(ac), jnp.sin(ac)], axis=-1).astype(BF16)
    mseq = jnp.concatenate([jnp.cos(an), -jnp.sin(an)], axis=-1).astype(BF16)
    return cs, mseq


def _mixers(p, row0, nseq, n, wts, tables, conv_args, pool_args, dft):
    q4, k4, v = mla_prep(p, wts, tables, row0=row0, col0=P_MLA, nseq=nseq, n=n)
    cv = conv_branch(p, *conv_args, row0=row0, nseq=nseq, n=n)
    ff = fft_branch(p, dft[0], dft[1], row0=row0, nseq=nseq, n=n)
    po = pool_branch(p, *pool_args, row0=row0, nseq=nseq, n=n)
    return q4, k4, v, cv, ff, po


def kernel(x, c, ctx, c_ctx, w_mod, b_mod, g_norm1, g_norm2, w_in, g_cq, w_uq, g_ckv, w_ukv, w_o_mla, conv_w, conv_b, conv_ln_g, conv_ln_b, w_o_conv, w_o_fft, pool_w, pool_scale, w_o_pool, w_out, w_router, router_bias, w_exp_gate, w_exp_up, w_exp_down, w_sh_gate, w_sh_up, w_sh_down, g_final):
    xa = jnp.concatenate([x.reshape(T_LAT, D_MODEL), ctx.reshape(T_CTX, D_MODEL)], axis=0)
    cvec = jnp.concatenate([c, c_ctx[None, :], jnp.zeros((SUBLANES - BATCH - 1, D_MODEL), F32)], axis=0)
    mod_all = modulation_all(cvec, w_mod, b_mod).reshape(DEPTH, SUBLANES, 6, D_MODEL)
    rope_lat = _rope_tables(SEQ // GRID_W)
    rope_ctx = _identity_tables(CTX_LEN)
    dft_lat = _dft_tables(SEQ)
    dft_ctx = _dft_tables(CTX_LEN)
    lat_tiles = T_LAT // ROW_TILE
    all_tiles = T_ALL // ROW_TILE

    for l in range(DEPTH):
        last = l == DEPTH - 1
        mod = mod_all[l]
        w_in_p = _pack_w_in(w_in[l])
        wts = _mla_weights(g_cq[l], w_uq[l], g_ckv[l], w_ukv[l])
        conv_args = (conv_w[l], conv_b[l], conv_ln_g[l], conv_ln_b[l])
        pool_args = (pool_w[l].astype(BF16), pool_scale[l])
        wo4 = jnp.stack([w_o_mla[l], w_o_conv[l], w_o_fft[l], w_o_pool[l]], axis=0).astype(BF16)

        if last:
            p = normproj(xa, g_norm1[l], mod, w_in_p, tile0=0, ntiles=lat_tiles, col0=0, ncols=P_COLS)
            p_ctx = normproj(xa, g_norm1[l], mod, w_in_p, tile0=lat_tiles, ntiles=all_tiles - lat_tiles,
                             col0=P_MLA, ncols=P_MLA_W)
        else:
            p = normproj(xa, g_norm1[l], mod, w_in_p, tile0=0, ntiles=all_tiles, col0=0, ncols=P_COLS)
        q4, k4, v, cv, ff, po = _mixers(p, 0, BATCH, SEQ, wts, rope_lat, conv_args, pool_args, dft_lat)
        if last:
            qc, kc, vc = mla_prep(p_ctx, wts, rope_ctx, row0=0, col0=0, nseq=BATCH, n=CTX_LEN)
        else:
            qc, kc, vc, cvc, ffc, poc = _mixers(p, T_LAT, BATCH, CTX_LEN, wts, rope_ctx,
                                                conv_args, pool_args, dft_ctx)
        k_all = jnp.concatenate([k4, kc], axis=2)
        v_all = jnp.concatenate([v, vc], axis=2)
        at = attention(q4, k_all, v_all, nq=SEQ, nk=SEQ + CTX_LEN)
        if last:
            ntok = T_LAT
        else:
            atc = attention(qc, kc, vc, nq=CTX_LEN, nk=CTX_LEN)
            at = jnp.concatenate([at, atc], axis=0)
            cv = jnp.concatenate([cv, cvc], axis=0)
            ff = jnp.concatenate([ff, ffc], axis=0)
            po = jnp.concatenate([po, poc], axis=0)
            ntok = T_ALL
        xa = merge(xa, mod, at, cv, ff, po, p, wo4, w_out[l].astype(BF16), ntiles=ntok // 256)

        wr_t = w_router[l].T
        wr_hi = wr_t.astype(BF16)
        wr_lo = (wr_t - wr_hi.astype(F32)).astype(BF16)
        tokens, eidx, wk = norm_router(xa, g_norm2[l], mod, wr_hi, wr_lo, router_bias[l], ntiles=ntok // 256)
        dest, zends, blk_e, nsub, next_e, nused = _routing_plan(eidx)
        xs, sh = dispatch_shared(tokens, _tile_major(dest, 256), zends, w_sh_gate[l].astype(BF16),
                                 w_sh_up[l].astype(BF16), w_sh_down[l].astype(BF16), ntiles=ntok // 256)
        ys = experts(xs, blk_e, nsub, next_e, nused, w_exp_gate, w_exp_up, w_exp_down, layer=l)
        xa = combine(xa, sh, mod, wk.T, g_final, _tile_major(dest, 128), ys, ntiles=ntok // 128, final=last)
    return xa.reshape(BATCH, SEQ, D_MODEL)
```

```python
import functools

import jax
import jax.numpy as jnp
import numpy as np
from jax import lax
from jax.experimental import pallas as pl
from jax.experimental.pallas import tpu as pltpu

F32 = jnp.float32
BF16 = jnp.bfloat16
I32 = jnp.int32

D_MODEL = 2048
BATCH = 4
SEQ = 4096
DEPTH = 2
GRID_W = 64
CTX_LEN = 256
MLA_HEADS = 8
QK_NOPE = 64
QK_ROPE = 32
V_HEAD = 64
Q_LORA = 512
KV_LORA = 256
ROPE_BASE = 10000.0
CONV_CH = 512
CONV_WIDTH = 31
FFT_CH = 512
FFT_GROUPS = 4
POOL_CH = 512
POOL_WINDOWS = (2, 4, 8, 16)
POOL_GROUP = POOL_CH // len(POOL_WINDOWS)
N_BRANCHES = 4
N_EXPERTS = 64
N_EXPERT_GROUPS = 8
TOPK_GROUPS = 4
TOP_K = 8
EXPERT_FF = 512
SHARED_FF = 512
ROUTED_SCALE = 2.5
EPS = 1e-6

MLA_IN = Q_LORA + KV_LORA + QK_ROPE
T_LAT = BATCH * SEQ
T_CTX = BATCH * CTX_LEN
T_ALL = T_LAT + T_CTX

LANES = 128
SUBLANES = 8
VMEM_LIMIT_BYTES = 56 * 1024 * 1024

P_GATE = 0
P_MLA = N_BRANCHES * D_MODEL
P_MLA_W = 1024
P_CONV = P_MLA + P_MLA_W
P_FFT = P_CONV + 2 * CONV_CH
P_POOL = P_FFT + FFT_CH
P_COLS = P_POOL + POOL_CH
HEAD_PAD = 128
ATT_SCALE = (QK_NOPE + QK_ROPE) ** -0.5
Q_SCALE = ATT_SCALE * float(np.log2(np.e))

ROW_TILE = 1024
EXPERT_ROWS = 256
EXPERT_SUBS = 4
STEP_ROWS = EXPERT_ROWS * EXPERT_SUBS
N_BLOCKS = -(-(T_ALL * TOP_K) // STEP_ROWS) + N_EXPERTS


def _cparams(sem, vmem=VMEM_LIMIT_BYTES):
    return pltpu.CompilerParams(dimension_semantics=sem, vmem_limit_bytes=vmem)


def _sigmoid(x):
    return 1.0 / (1.0 + jnp.exp(-x))


def _silu(x):
    return x * _sigmoid(x)


def _mod_row(tile, tiles_per_seq):
    return jnp.minimum(tile // tiles_per_seq, BATCH)


def _modulation_kernel(c_ref, w_ref, b_ref, o_ref):
    c = c_ref[...]
    s = _silu(c).astype(BF16)
    o_ref[0] = jnp.dot(s, w_ref[0].astype(BF16), preferred_element_type=F32) + b_ref[0]


def modulation_all(cvec, w_mod, b_mod):
    tn = 1024
    n = 6 * D_MODEL
    return pl.pallas_call(
        _modulation_kernel,
        out_shape=jax.ShapeDtypeStruct((DEPTH, SUBLANES, n), F32),
        grid=(DEPTH, n // tn),
        in_specs=[
            pl.BlockSpec((SUBLANES, D_MODEL), lambda l, j: (0, 0)),
            pl.BlockSpec((1, D_MODEL, tn), lambda l, j: (l, 0, j)),
            pl.BlockSpec((1, 1, tn), lambda l, j: (l, 0, j)),
        ],
        out_specs=pl.BlockSpec((1, SUBLANES, tn), lambda l, j: (l, 0, j)),
        compiler_params=_cparams(("arbitrary", "arbitrary")),
        name="modulation",
    )(cvec, w_mod, b_mod.reshape(DEPTH, 1, n))


def _adaln(x, g, shift, scale):
    y = x * lax.rsqrt(jnp.mean(x * x, axis=-1, keepdims=True) + EPS)
    return (y * g) * (1.0 + scale) + shift


def _normproj_kernel(x_ref, g_ref, mod_ref, w_ref, o_ref, h_ref, *, shift_row, chunk):
    @pl.when(pl.program_id(1) == 0)
    def _():
        shift = mod_ref[0, shift_row:shift_row + 1, :]
        scale = mod_ref[0, shift_row + 1:shift_row + 2, :]
        for r in range(0, x_ref.shape[0], chunk):
            h = _adaln(x_ref[r:r + chunk, :], g_ref[...], shift, scale)
            h_ref[r:r + chunk, :] = h.astype(BF16)

    o_ref[...] = jnp.dot(h_ref[...], w_ref[...], preferred_element_type=F32).astype(o_ref.dtype)


def normproj(x, g, mod, w, *, tile0, ntiles, col0, ncols, shift_row=0):
    tm, tn = ROW_TILE, 1024
    tps = SEQ // tm
    cb0 = col0 // tn
    return pl.pallas_call(
        functools.partial(_normproj_kernel, shift_row=shift_row, chunk=256),
        out_shape=jax.ShapeDtypeStruct((ntiles * tm, ncols), BF16),
        grid=(ntiles, ncols // tn),
        in_specs=[
            pl.BlockSpec((tm, D_MODEL), lambda i, j: (i + tile0, 0)),
            pl.BlockSpec((1, D_MODEL), lambda i, j: (0, 0)),
            pl.BlockSpec((1, 6, D_MODEL), lambda i, j: (_mod_row(i + tile0, tps), 0, 0)),
            pl.BlockSpec((D_MODEL, tn), lambda i, j: (0, j + cb0)),
        ],
        out_specs=pl.BlockSpec((tm, tn), lambda i, j: (i, j)),
        scratch_shapes=[pltpu.VMEM((tm, D_MODEL), BF16)],
        compiler_params=_cparams(("arbitrary", "arbitrary")),
        name="normproj",
    )(x, g.reshape(1, D_MODEL), mod, w)


def _rms_gain(x, g):
    y = x * lax.rsqrt(jnp.mean(x * x, axis=-1, keepdims=True) + EPS)
    return (y * g).astype(BF16)


_NT = (((1,), (1,)), ((), ()))


def _mla_prep_kernel(p_ref, gq_ref, gkv_ref, wqt_ref, wk_ref, wvt_ref, pk_ref, cos_ref, sin_ref,
                     cost_ref, sint_ref, qt_ref, k_ref, vt_ref):
    hw = MLA_HEADS * HEAD_PAD
    cqn = _rms_gain(p_ref[:, :Q_LORA].astype(F32), gq_ref[...])
    ckvn = _rms_gain(p_ref[:, Q_LORA:Q_LORA + KV_LORA].astype(F32), gkv_ref[...])
    kr = p_ref[:, Q_LORA + KV_LORA:Q_LORA + KV_LORA + LANES]
    q2t = lax.dot_general(wqt_ref[...], cqn, _NT, preferred_element_type=F32)
    vt_ref[0] = lax.dot_general(wvt_ref[...], ckvn, _NT, preferred_element_type=F32).astype(BF16)
    kk = jnp.dot(ckvn, wk_ref[...], preferred_element_type=F32)
    kr2 = jnp.dot(kr, pk_ref[...], preferred_element_type=F32)
    cos, sin = cos_ref[...], sin_ref[...]
    cost, sint = cost_ref[...], sint_ref[...]
    for h in range(MLA_HEADS):
        a = slice(h * HEAD_PAD, (h + 1) * HEAD_PAD)
        b = slice(hw + h * HEAD_PAD, hw + (h + 1) * HEAD_PAD)
        qt_ref[0, h] = ((q2t[a, :] * cost + q2t[b, :] * sint) * Q_SCALE).astype(BF16)
        k_ref[0, h] = (kk[:, a] + kr2[:, a] * cos + kr2[:, b] * sin).astype(BF16)


def mla_prep(p, wts, tables, *, row0, col0, nseq, n):
    tm = 256
    tps = n // tm
    hw = MLA_HEADS * HEAD_PAD
    vw = MLA_HEADS * V_HEAD
    blk0 = row0 // tm
    cblk = col0 // P_MLA_W
    cos_t, sin_t = tables
    const = lambda i: (0, 0)
    return pl.pallas_call(
        _mla_prep_kernel,
        out_shape=(jax.ShapeDtypeStruct((nseq, MLA_HEADS, HEAD_PAD, n), BF16),
                   jax.ShapeDtypeStruct((nseq, MLA_HEADS, n, HEAD_PAD), BF16),
                   jax.ShapeDtypeStruct((nseq, vw, n), BF16)),
        grid=(nseq * tps,),
        in_specs=[
            pl.BlockSpec((tm, P_MLA_W), lambda i: (i + blk0, cblk)),
            pl.BlockSpec((1, Q_LORA), const),
            pl.BlockSpec((1, KV_LORA), const),
            pl.BlockSpec((2 * hw, Q_LORA), const),
            pl.BlockSpec((KV_LORA, hw), const),
            pl.BlockSpec((vw, KV_LORA), const),
            pl.BlockSpec((LANES, 2 * hw), const),
            pl.BlockSpec((tm, HEAD_PAD), lambda i: (i % tps, 0)),
            pl.BlockSpec((tm, HEAD_PAD), lambda i: (i % tps, 0)),
            pl.BlockSpec((HEAD_PAD, tm), lambda i: (0, i % tps)),
            pl.BlockSpec((HEAD_PAD, tm), lambda i: (0, i % tps)),
        ],
        out_specs=(pl.BlockSpec((1, MLA_HEADS, HEAD_PAD, tm), lambda i: (i // tps, 0, 0, i % tps)),
                   pl.BlockSpec((1, MLA_HEADS, tm, HEAD_PAD), lambda i: (i // tps, 0, i % tps, 0)),
                   pl.BlockSpec((1, vw, tm), lambda i: (i // tps, 0, i % tps))),
        compiler_params=_cparams(("arbitrary",)),
        name="mla_prep",
    )(p, wts["g_cq"], wts["g_ckv"], wts["wq2t"], wts["wk"], wts["wvt"], wts["pk"],
      cos_t, sin_t, cos_t.T, sin_t.T)


ATT_HEADS = 4


def _attn_kernel(qt_ref, k_ref, vt_ref, o_ref, s_ref, p_ref, *, nk, ck):
    tq = qt_ref.shape[3]
    chunks = [slice(c, c + ck) for c in range(0, nk, ck)]
    nh = ATT_HEADS
    m = [None] * nh
    l = [jnp.zeros((1, tq), F32) for _ in range(nh)]
    ot = [jnp.zeros((V_HEAD, tq), F32) for _ in range(nh)]
    for step in range(nh + 2):
        ha, hb, hc = step, step - 1, step - 2
        for c in chunks:
            if 0 <= ha < nh:
                s = jnp.dot(k_ref[0, ha, c, :], qt_ref[0, ha], preferred_element_type=F32)
                s_ref[ha % 2, c, :] = s
                mc = jnp.max(s, axis=0, keepdims=True)
                m[ha] = mc if m[ha] is None else jnp.maximum(m[ha], mc)
            if 0 <= hb < nh:
                p = jnp.exp2(s_ref[hb % 2, c, :] - m[hb])
                l[hb] = l[hb] + jnp.sum(p, axis=0, keepdims=True)
                p_ref[hb % 2, c, :] = p.astype(BF16)
            if 0 <= hc < nh:
                ot[hc] = ot[hc] + jnp.dot(vt_ref[0, hc * V_HEAD:(hc + 1) * V_HEAD, c], p_ref[hc % 2, c, :],
                                          preferred_element_type=F32)
    outs = [ot[h] / l[h] for h in range(nh)]
    o_ref[...] = jnp.concatenate(outs, axis=0).T.astype(o_ref.dtype)


def attention(qt4, k4, vt3, *, nq, nk):
    tq = 256
    ck = 256
    nqt = nq // tq
    hs = ATT_HEADS
    return pl.pallas_call(
        functools.partial(_attn_kernel, nk=nk, ck=ck),
        out_shape=jax.ShapeDtypeStruct((BATCH * nq, MLA_HEADS * V_HEAD), BF16),
        grid=(BATCH, MLA_HEADS // hs, nqt),
        in_specs=[
            pl.BlockSpec((1, hs, HEAD_PAD, tq), lambda b, hg, i: (b, hg, 0, i)),
            pl.BlockSpec((1, hs, nk, HEAD_PAD), lambda b, hg, i: (b, hg, 0, 0)),
            pl.BlockSpec((1, hs * V_HEAD, nk), lambda b, hg, i: (b, hg, 0)),
        ],
        out_specs=pl.BlockSpec((tq, hs * V_HEAD), lambda b, hg, i: (b * nqt + i, hg)),
        scratch_shapes=[pltpu.VMEM((2, nk, tq), F32), pltpu.VMEM((2, nk, tq), BF16)],
        compiler_params=_cparams(("arbitrary", "arbitrary", "arbitrary")),
        name="attention",
    )(qt4, k4, vt3)


CONV_PAD = 16


def _conv_kernel(p_ref, w_ref, cb_ref, lg_ref, lb_ref, o_ref, u_ref, *, n, tr):
    zeros = jnp.zeros((CONV_PAD, CONV_CH), F32)
    u_ref[0:CONV_PAD, :] = zeros
    u_ref[CONV_PAD + n:2 * CONV_PAD + n, :] = zeros

    def glu(i, carry):
        r = pl.multiple_of(i * tr, tr)
        a = p_ref[pl.ds(r, tr), 0:CONV_CH].astype(F32)
        b = p_ref[pl.ds(r, tr), CONV_CH:2 * CONV_CH].astype(F32)
        u_ref[pl.ds(CONV_PAD + r, tr), :] = a * _sigmoid(b)
        return carry

    lax.fori_loop(0, n // tr, glu, 0)
    off = CONV_PAD - CONV_WIDTH // 2
    win_rows = tr + 2 * CONV_PAD

    def conv(i, carry):
        r = pl.multiple_of(i * tr, tr)
        strips = []
        for c0 in range(0, CONV_CH, LANES):
            win = u_ref[pl.ds(r, win_rows), c0:c0 + LANES]
            acc = jnp.zeros((tr, LANES), F32)
            for b in range(SUBLANES):
                wb = pltpu.roll(win, win_rows - (off + b), axis=0)
                for k in range(b, CONV_WIDTH, SUBLANES):
                    acc = acc + wb[k - b:k - b + tr, :] * w_ref[k:k + 1, c0:c0 + LANES]
            strips.append(acc)
        acc = jnp.concatenate(strips, axis=-1) + cb_ref[...]
        mu = jnp.mean(acc, axis=-1, keepdims=True)
        d = acc - mu
        var = jnp.mean(d * d, axis=-1, keepdims=True)
        y = (d * lax.rsqrt(var + EPS)) * lg_ref[...] + lb_ref[...]
        o_ref[pl.ds(r, tr), :] = _silu(y).astype(o_ref.dtype)
        return carry

    lax.fori_loop(0, n // tr, conv, 0)


def conv_branch(p, conv_w, conv_b, ln_g, ln_b, *, row0, nseq, n):
    tr = 128
    blk0 = row0 // n
    const = lambda s: (0, 0)
    return pl.pallas_call(
        functools.partial(_conv_kernel, n=n, tr=tr),
        out_shape=jax.ShapeDtypeStruct((nseq * n, CONV_CH), BF16),
        grid=(nseq,),
        in_specs=[
            pl.BlockSpec((n, 2 * CONV_CH), lambda s: (s + blk0, P_CONV // (2 * CONV_CH))),
            pl.BlockSpec((CONV_WIDTH, CONV_CH), const),
            pl.BlockSpec((1, CONV_CH), const),
            pl.BlockSpec((1, CONV_CH), const),
            pl.BlockSpec((1, CONV_CH), const),
        ],
        out_specs=pl.BlockSpec((n, CONV_CH), lambda s: (s, 0)),
        scratch_shapes=[pltpu.VMEM((n + 2 * CONV_PAD, CONV_CH), F32)],
        compiler_params=_cparams(("arbitrary",)),
        name="conv_branch",
    )(p, conv_w, conv_b.reshape(1, -1), ln_g.reshape(1, -1), ln_b.reshape(1, -1))


def _fft_kernel(z_ref, cs_ref, m_ref, o_ref, ab_ref, *, n, tc):
    gw = FFT_CH // FFT_GROUPS

    @pl.when(pl.program_id(1) == 0)
    def _():
        def chan(i, carry):
            r = pl.multiple_of(i * tc, tc)
            for g in range(FFT_GROUPS):
                zg = z_ref[pl.ds(r, tc), g * gw:(g + 1) * gw]
                ab = jnp.dot(zg, cs_ref[...], preferred_element_type=F32)
                ab_ref[pl.ds(r, tc), g * gw:(g + 1) * gw] = ab[:, :gw].astype(BF16)
                ab_ref[pl.ds(n + r, tc), g * gw:(g + 1) * gw] = ab[:, gw:].astype(BF16)
            return carry

        lax.fori_loop(0, n // tc, chan, 0)

    norm = 1.0 / float(np.sqrt(n * gw))
    o_ref[...] = (jnp.dot(m_ref[...], ab_ref[...], preferred_element_type=F32) * norm).astype(o_ref.dtype)


def fft_branch(p, cs, mseq, *, row0, nseq, n):
    tm = min(n, 512)
    tc = min(n, 512)
    blk0 = row0 // n
    nt = n // tm
    return pl.pallas_call(
        functools.partial(_fft_kernel, n=n, tc=tc),
        out_shape=jax.ShapeDtypeStruct((nseq * n, FFT_CH), BF16),
        grid=(nseq, nt),
        in_specs=[
            pl.BlockSpec((n, FFT_CH), lambda s, i: (s + blk0, P_FFT // FFT_CH)),
            pl.BlockSpec((FFT_CH // FFT_GROUPS, 2 * FFT_CH // FFT_GROUPS), lambda s, i: (0, 0)),
            pl.BlockSpec((tm, 2 * n), lambda s, i: (i, 0)),
        ],
        out_specs=pl.BlockSpec((tm, FFT_CH), lambda s, i: (s * nt + i, 0)),
        scratch_shapes=[pltpu.VMEM((2 * n, FFT_CH), BF16)],
        compiler_params=_cparams(("arbitrary", "arbitrary")),
        name="fft_branch",
    )(p, cs, mseq)


POOL_PAD = 16


def _pool_kernel(z_ref, pw_ref, ps_ref, o_ref, zp_ref, d_ref, *, n, tr, tc):
    zeros = jnp.zeros((POOL_PAD, POOL_CH), F32)
    zp_ref[0:POOL_PAD, :] = zeros
    zp_ref[POOL_PAD + n:2 * POOL_PAD + n, :] = zeros

    def fill(i, carry):
        r = pl.multiple_of(i * tc, tc)
        zp_ref[pl.ds(POOL_PAD + r, tc), :] = z_ref[pl.ds(r, tc), :].astype(F32)
        return carry

    lax.fori_loop(0, n // tc, fill, 0)

    win_rows = tr + 2 * POOL_PAD

    def pool(i, carry):
        r = pl.multiple_of(i * tr, tr)
        t = (r + lax.broadcasted_iota(I32, (tr, POOL_GROUP), 0)).astype(F32)
        for gi, w in enumerate(POOL_WINDOWS):
            cols = slice(gi * POOL_GROUP, (gi + 1) * POOL_GROUP)
            win = zp_ref[pl.ds(r, win_rows), cols]
            s = jnp.zeros((tr, POOL_GROUP), F32)
            for j in range(w):
                start = POOL_PAD - w // 2 + j
                if start % SUBLANES == 0:
                    s = s + win[start:start + tr, :]
                else:
                    s = s + pltpu.roll(win, win_rows - start, axis=0)[0:tr, :]
            lo = jnp.maximum(t - (w // 2), 0.0)
            hi = jnp.minimum(t - (w // 2) + w, float(n))
            z = win[POOL_PAD:POOL_PAD + tr, :]
            d_ref[pl.ds(r, tr), cols] = (s / (hi - lo) - z).astype(BF16)
        return carry

    lax.fori_loop(0, n // tr, pool, 0)

    def proj(i, carry):
        r = pl.multiple_of(i * tc, tc)
        for gi in range(len(POOL_WINDOWS)):
            cols = slice(gi * POOL_GROUP, (gi + 1) * POOL_GROUP)
            y = jnp.dot(d_ref[pl.ds(r, tc), cols], pw_ref[gi], preferred_element_type=F32)
            o_ref[pl.ds(r, tc), cols] = (y * ps_ref[:, cols]).astype(o_ref.dtype)
        return carry

    lax.fori_loop(0, n // tc, proj, 0)


def pool_branch(p, pool_w, pool_scale, *, row0, nseq, n):
    tr = 128
    tc = min(n, 512)
    blk0 = row0 // n
    return pl.pallas_call(
        functools.partial(_pool_kernel, n=n, tr=tr, tc=tc),
        out_shape=jax.ShapeDtypeStruct((nseq * n, POOL_CH), BF16),
        grid=(nseq,),
        in_specs=[
            pl.BlockSpec((n, POOL_CH), lambda s: (s + blk0, P_POOL // POOL_CH)),
            pl.BlockSpec((len(POOL_WINDOWS), POOL_GROUP, POOL_GROUP), lambda s: (0, 0, 0)),
            pl.BlockSpec((1, POOL_CH), lambda s: (0, 0)),
        ],
        out_specs=pl.BlockSpec((n, POOL_CH), lambda s: (s, 0)),
        scratch_shapes=[pltpu.VMEM((n + 2 * POOL_PAD, POOL_CH), F32),
                        pltpu.VMEM((n, POOL_CH), BF16)],
        compiler_params=_cparams(("arbitrary",)),
        name="pool_branch",
    )(p, pool_w, pool_scale.reshape(1, -1))


def _merge_kernel(x_ref, mod_ref, at_ref, cv_ref, ff_ref, po_ref, g_ref, wo_ref, wout_ref, o_ref):
    mix = None
    for b, br in enumerate((at_ref, cv_ref, ff_ref, po_ref)):
        y = jnp.dot(br[...], wo_ref[b], preferred_element_type=F32)
        gate = _sigmoid(g_ref[:, b * D_MODEL:(b + 1) * D_MODEL].astype(F32))
        mix = gate * y if mix is None else mix + gate * y
    out = jnp.dot(mix.astype(BF16), wout_ref[...], preferred_element_type=F32)
    o_ref[...] = x_ref[...] + mod_ref[0, 2:3, :] * out


def merge(x, mod, attn, conv, fft, pool, p, wo4, w_out, *, ntiles):
    tm = 256
    tps = SEQ // tm
    br_spec = pl.BlockSpec((tm, 512), lambda i: (i, 0))
    return pl.pallas_call(
        _merge_kernel,
        out_shape=jax.ShapeDtypeStruct((ntiles * tm, D_MODEL), F32),
        grid=(ntiles,),
        in_specs=[
            pl.BlockSpec((tm, D_MODEL), lambda i: (i, 0)),
            pl.BlockSpec((1, 6, D_MODEL), lambda i: (_mod_row(i, tps), 0, 0)),
            br_spec, br_spec, br_spec, br_spec,
            pl.BlockSpec((tm, N_BRANCHES * D_MODEL), lambda i: (i, P_GATE // (N_BRANCHES * D_MODEL))),
            pl.BlockSpec((N_BRANCHES, 512, D_MODEL), lambda i: (0, 0, 0), pipeline_mode=pl.Buffered(1)),
            pl.BlockSpec((D_MODEL, D_MODEL), lambda i: (0, 0), pipeline_mode=pl.Buffered(1)),
        ],
        out_specs=pl.BlockSpec((tm, D_MODEL), lambda i: (i, 0)),
        compiler_params=_cparams(("arbitrary",)),
        name="merge",
    )(x, mod, attn, conv, fft, pool, p, wo4, w_out)


def _first_index(hit_value, cand, ids, big):
    return jnp.min(jnp.where(cand == hit_value, ids, big), axis=0, keepdims=True)


HALF = D_MODEL // 2
ROW_WORDS = HALF // LANES


def _round_bf16_bits(x):
    u = lax.bitcast_convert_type(x, I32)
    odd = lax.shift_right_logical(u, 16) & 1
    return (u + 0x7FFF + odd) & jnp.int32(-65536)


def _pack_words(lo, hi):
    return lax.shift_right_logical(_round_bf16_bits(lo), 16) | _round_bf16_bits(hi)


def _unpack_words(w):
    lo = lax.bitcast_convert_type(lax.shift_left(w, 16), F32)
    hi = lax.bitcast_convert_type(w & jnp.int32(-65536), F32)
    return lo, hi


def _store_packed(ref, x, tok0=0):
    tm = x.shape[0]
    for s in range(ROW_WORDS):
        w = _pack_words(x[:, s * LANES:(s + 1) * LANES], x[:, HALF + s * LANES:HALF + (s + 1) * LANES])
        ref[pl.ds(tok0 * ROW_WORDS + s, tm, stride=ROW_WORDS), :] = w


def _load_packed(ref, tm, dtype, tok0=0):
    los, his = [], []
    for s in range(ROW_WORDS):
        lo, hi = _unpack_words(ref[pl.ds(tok0 * ROW_WORDS + s, tm, stride=ROW_WORDS), :])
        los.append(lo.astype(dtype))
        his.append(hi.astype(dtype))
    return jnp.concatenate(los + his, axis=-1)


def _router_kernel(x_ref, g_ref, mod_ref, wrh_ref, wrl_ref, rb_ref, tok_ref, eidx_ref, wk_ref):
    tm = x_ref.shape[0]
    h = _adaln(x_ref[...], g_ref[...], mod_ref[0, 3:4, :], mod_ref[0, 4:5, :])
    _store_packed(tok_ref, h)
    hh = h.astype(BF16)
    hl = (h - hh.astype(F32)).astype(BF16)
    nt = (((1,), (1,)), ((), ()))
    logits = (lax.dot_general(wrh_ref[...], hh, nt, preferred_element_type=F32)
              + lax.dot_general(wrh_ref[...], hl, nt, preferred_element_type=F32)
              + lax.dot_general(wrl_ref[...], hh, nt, preferred_element_type=F32))
    scores = _sigmoid(logits)
    sel = scores + rb_ref[...]
    per = N_EXPERTS // N_EXPERT_GROUPS
    assert per == SUBLANES and N_EXPERT_GROUPS == SUBLANES and TOP_K == SUBLANES
    neg = -jnp.inf
    sub = lax.broadcasted_iota(I32, (SUBLANES, tm), 0).astype(F32)
    sg = [sel[g * per:(g + 1) * per, :] for g in range(N_EXPERT_GROUPS)]
    sc = [scores[g * per:(g + 1) * per, :] for g in range(N_EXPERT_GROUPS)]
    gsc = jnp.zeros((SUBLANES, tm), F32)
    for g in range(N_EXPERT_GROUPS):
        m1 = jnp.max(sg[g], axis=0, keepdims=True)
        i1 = _first_index(m1, sg[g], sub, float(per))
        m2 = jnp.max(jnp.where(sub == i1, neg, sg[g]), axis=0, keepdims=True)
        gsc = jnp.where(sub == float(g), m1 + m2, gsc)
    gsel = jnp.zeros((SUBLANES, tm), F32)
    for _ in range(TOPK_GROUPS):
        m = jnp.max(gsc, axis=0, keepdims=True)
        hit = sub == _first_index(m, gsc, sub, float(N_EXPERT_GROUPS))
        gsel = jnp.where(hit, 1.0, gsel)
        gsc = jnp.where(hit, neg, gsc)
    cand = []
    for g in range(N_EXPERT_GROUPS):
        allowed = jnp.max(jnp.where(sub == float(g), gsel, 0.0), axis=0, keepdims=True)
        cand.append(jnp.where(allowed > 0.0, sg[g], neg))
    eid = [sub + float(g * per) for g in range(N_EXPERT_GROUPS)]
    idxs = jnp.zeros((SUBLANES, tm), F32)
    vals = jnp.zeros((SUBLANES, tm), F32)
    for k in range(TOP_K):
        m = functools.reduce(jnp.maximum, [jnp.max(c, axis=0, keepdims=True) for c in cand])
        idx = functools.reduce(
            jnp.minimum, [_first_index(m, cand[g], eid[g], float(N_EXPERTS)) for g in range(N_EXPERT_GROUPS)])
        val = jnp.zeros((1, tm), F32)
        for g in range(N_EXPERT_GROUPS):
            hit = eid[g] == idx
            val = val + jnp.sum(jnp.where(hit, sc[g], 0.0), axis=0, keepdims=True)
            cand[g] = jnp.where(hit, neg, cand[g])
        idxs = jnp.where(sub == float(k), idx, idxs)
        vals = jnp.where(sub == float(k), val, vals)
    eidx_ref[...] = idxs.astype(I32)
    wk_ref[...] = vals / jnp.sum(vals, axis=0, keepdims=True) * ROUTED_SCALE


def norm_router(x, g, mod, wr_hi, wr_lo, rbias, *, ntiles):
    tm = 256
    tps = SEQ // tm
    t = ntiles * tm
    return pl.pallas_call(
        _router_kernel,
        out_shape=(jax.ShapeDtypeStruct((t * ROW_WORDS, LANES), I32),
                   jax.ShapeDtypeStruct((TOP_K, t), I32),
                   jax.ShapeDtypeStruct((TOP_K, t), F32)),
        grid=(ntiles,),
        in_specs=[
            pl.BlockSpec((tm, D_MODEL), lambda i: (i, 0)),
            pl.BlockSpec((1, D_MODEL), lambda i: (0, 0)),
            pl.BlockSpec((1, 6, D_MODEL), lambda i: (_mod_row(i, tps), 0, 0)),
            pl.BlockSpec((N_EXPERTS, D_MODEL), lambda i: (0, 0)),
            pl.BlockSpec((N_EXPERTS, D_MODEL), lambda i: (0, 0)),
            pl.BlockSpec((N_EXPERTS, 1), lambda i: (0, 0)),
        ],
        out_specs=(pl.BlockSpec((tm * ROW_WORDS, LANES), lambda i: (i, 0)),
                   pl.BlockSpec((TOP_K, tm), lambda i: (0, i)),
                   pl.BlockSpec((TOP_K, tm), lambda i: (0, i))),
        compiler_params=_cparams(("arbitrary",)),
        name="norm_router",
    )(x, g.reshape(1, D_MODEL), mod, wr_hi, wr_lo, rbias.reshape(N_EXPERTS, 1))


def _row_copy(src, src_tok, dst, dst_tok, sem):
    s = pl.multiple_of(src_tok * ROW_WORDS, ROW_WORDS)
    d = pl.multiple_of(dst_tok * ROW_WORDS, ROW_WORDS)
    return pltpu.make_async_copy(src.at[pl.ds(s, ROW_WORDS)], dst.at[pl.ds(d, ROW_WORDS)], sem)


def _dispatch_kernel(zends_ref, tok_ref, dest_hbm, wg_ref, wu_ref, wd_ref, xs_hbm, sh_ref,
                     dest_smem, zero_ref, sem_idx, sem_z, sem_s):
    i = pl.program_id(0)
    tm = tok_ref.shape[0] // ROW_WORDS
    idx_copy = pltpu.make_async_copy(dest_hbm.at[i], dest_smem, sem_idx)
    idx_copy.start()
    blk = EXPERT_ROWS * ROW_WORDS

    def pad_copy(e):
        row = pl.multiple_of((zends_ref[e] - EXPERT_ROWS) * ROW_WORDS, blk)
        return pltpu.make_async_copy(zero_ref, xs_hbm.at[pl.ds(row, blk)], sem_z)

    @pl.when(i == 0)
    def _():
        zero_ref[...] = jnp.zeros_like(zero_ref)

        def start(e, carry):
            @pl.when(zends_ref[e] > 0)
            def _():
                pad_copy(e).start()
            return carry

        lax.fori_loop(0, N_EXPERTS, start, 0)

        def wait(e, carry):
            @pl.when(zends_ref[e] > 0)
            def _():
                pad_copy(e).wait()
            return carry

        lax.fori_loop(0, N_EXPERTS, wait, 0)

    idx_copy.wait()

    x = _load_packed(tok_ref, tm, BF16)
    pieces = SHARED_FF // LANES
    per = tm // pieces
    acc = None
    for p in range(pieces):
        for r in range(p * per, (p + 1) * per):
            for k in range(TOP_K):
                _row_copy(tok_ref, r, xs_hbm, dest_smem[k * tm + r], sem_s).start(priority=k % 2)
        c = slice(p * LANES, (p + 1) * LANES)
        g = jnp.dot(x, wg_ref[:, c], preferred_element_type=F32)
        u = jnp.dot(x, wu_ref[:, c], preferred_element_type=F32)
        part = jnp.dot((_silu(g) * u).astype(BF16), wd_ref[c, :], preferred_element_type=F32)
        acc = part if acc is None else acc + part
    sh_ref[...] = acc

    def drain(r, carry):
        for k in range(TOP_K):
            _row_copy(tok_ref, r, xs_hbm, dest_smem[k * tm + r], sem_s).wait()
        return carry

    lax.fori_loop(0, tm, drain, 0)


def dispatch_shared(tokens, dest_tiles, zends, wg, wu, wd, *, ntiles):
    tm = tokens.shape[0] // ROW_WORDS // ntiles
    const = lambda i, zends: (0, 0)
    return pl.pallas_call(
        _dispatch_kernel,
        out_shape=(jax.ShapeDtypeStruct((N_BLOCKS * STEP_ROWS * ROW_WORDS, LANES), I32),
                   jax.ShapeDtypeStruct((ntiles * tm, D_MODEL), F32)),
        grid_spec=pltpu.PrefetchScalarGridSpec(
            num_scalar_prefetch=1,
            grid=(ntiles,),
            in_specs=[
                pl.BlockSpec((tm * ROW_WORDS, LANES), lambda i, zends: (i, 0)),
                pl.BlockSpec(memory_space=pl.ANY),
                pl.BlockSpec((D_MODEL, SHARED_FF), const),
                pl.BlockSpec((D_MODEL, SHARED_FF), const),
                pl.BlockSpec((SHARED_FF, D_MODEL), const),
            ],
            out_specs=(pl.BlockSpec(memory_space=pl.ANY),
                       pl.BlockSpec((tm, D_MODEL), lambda i, zends: (i, 0))),
            scratch_shapes=[
                pltpu.SMEM((TOP_K * tm,), I32),
                pltpu.VMEM((EXPERT_ROWS * ROW_WORDS, LANES), I32),
                pltpu.SemaphoreType.DMA,
                pltpu.SemaphoreType.DMA,
                pltpu.SemaphoreType.DMA,
            ],
        ),
        compiler_params=_cparams(("arbitrary",)),
        name="moe_dispatch",
    )(zends, tokens, dest_tiles, wg, wu, wd)


def _experts_kernel(blk_e_ref, nsub_ref, next_e_ref, nused_ref, xs_ref, wg_hbm, wu_hbm, wd_hbm, y_ref,
                    wgs, wus, wds, wgb, wub, wdb, sem_w, *, layer):
    i = pl.program_id(0)
    e = blk_e_ref[i]
    prev = blk_e_ref[jnp.maximum(i - 1, 0)]

    def weight_copies(expert):
        row = layer * N_EXPERTS + expert
        return (pltpu.make_async_copy(wg_hbm.at[row], wgs, sem_w.at[0]),
                pltpu.make_async_copy(wu_hbm.at[row], wus, sem_w.at[1]),
                pltpu.make_async_copy(wd_hbm.at[row], wds, sem_w.at[2]))

    @pl.when(i < nused_ref[0])
    def _():
        @pl.when(i == 0)
        def _():
            for cp in weight_copies(e):
                cp.start()

        @pl.when((i == 0) | (e != prev))
        def _():
            for cp in weight_copies(e):
                cp.wait()
            wgb[...] = wgs[...].astype(BF16)
            wub[...] = wus[...].astype(BF16)
            wdb[...] = wds[...].astype(BF16)

            @pl.when(next_e_ref[i] >= 0)
            def _():
                for cp in weight_copies(next_e_ref[i]):
                    cp.start()

        for sb in range(EXPERT_SUBS):
            tok0 = sb * EXPERT_ROWS

            @pl.when(sb < nsub_ref[i])
            def _():
                x = _load_packed(xs_ref, EXPERT_ROWS, BF16, tok0)
                g = jnp.dot(x, wgb[...], preferred_element_type=F32)
                u = jnp.dot(x, wub[...], preferred_element_type=F32)
                hb = (_silu(g) * u).astype(BF16)
                _store_packed(y_ref, jnp.dot(hb, wdb[...], preferred_element_type=F32), tok0)

            @pl.when(sb >= nsub_ref[i])
            def _():
                rows = EXPERT_ROWS * ROW_WORDS
                y_ref[tok0 * ROW_WORDS:tok0 * ROW_WORDS + rows, :] = jnp.zeros((rows, LANES), I32)


def experts(xs, blk_e, nsub, next_e, nused, w_gate, w_up, w_down, *, layer):
    w_gate = w_gate.reshape(DEPTH * N_EXPERTS, D_MODEL, EXPERT_FF)
    w_up = w_up.reshape(DEPTH * N_EXPERTS, D_MODEL, EXPERT_FF)
    w_down = w_down.reshape(DEPTH * N_EXPERTS, EXPERT_FF, D_MODEL)

    def row_map(i, blk_e, nsub, next_e, nused):
        return (jnp.minimum(i, nused[0] - 1), 0)

    return pl.pallas_call(
        functools.partial(_experts_kernel, layer=layer),
        out_shape=jax.ShapeDtypeStruct((N_BLOCKS * STEP_ROWS * ROW_WORDS, LANES), I32),
        grid_spec=pltpu.PrefetchScalarGridSpec(
            num_scalar_prefetch=4,
            grid=(N_BLOCKS,),
            in_specs=[
                pl.BlockSpec((STEP_ROWS * ROW_WORDS, LANES), row_map),
                pl.BlockSpec(memory_space=pl.ANY),
                pl.BlockSpec(memory_space=pl.ANY),
                pl.BlockSpec(memory_space=pl.ANY),
            ],
            out_specs=pl.BlockSpec((STEP_ROWS * ROW_WORDS, LANES), row_map),
            scratch_shapes=[
                pltpu.VMEM((D_MODEL, EXPERT_FF), F32),
                pltpu.VMEM((D_MODEL, EXPERT_FF), F32),
                pltpu.VMEM((EXPERT_FF, D_MODEL), F32),
                pltpu.VMEM((D_MODEL, EXPERT_FF), BF16),
                pltpu.VMEM((D_MODEL, EXPERT_FF), BF16),
                pltpu.VMEM((EXPERT_FF, D_MODEL), BF16),
                pltpu.SemaphoreType.DMA((3,)),
            ],
        ),
        compiler_params=_cparams(("arbitrary",)),
        name="moe_experts",
    )(blk_e, nsub, next_e, nused, xs, w_gate, w_up, w_down)


def _combine_kernel(x_ref, sh_ref, mod_ref, wk_ref, gf_ref, dest_hbm, ys_hbm, o_ref,
                    idx0, idx1, buf0, buf1, sem_idx, sem_g, *, final):
    i = pl.program_id(0)
    n = pl.num_programs(0)
    tm = x_ref.shape[0]
    idx = (idx0, idx1)
    buf = (buf0, buf1)

    def fetch_idx(tile, slot):
        idx_copy = pltpu.make_async_copy(dest_hbm.at[tile], idx[slot], sem_idx)
        idx_copy.start()
        idx_copy.wait()

    def start_rows(slot, r):
        for k in range(TOP_K):
            _row_copy(ys_hbm, idx[slot][k * tm + r], buf[slot].at[k], r, sem_g.at[slot]).start(priority=k % 2)

    def drain(slot):
        def wait_rows(r, carry):
            for k in range(TOP_K):
                _row_copy(ys_hbm, 0, buf[slot].at[k], r, sem_g.at[slot]).wait()
            return carry

        lax.fori_loop(0, tm, wait_rows, 0)

    def finish(slot):
        fetch_idx(jnp.minimum(i + 1, n - 1), 1 - slot)
        drain(slot)
        per = tm // ROW_WORDS
        los, his = [], []
        for s in range(ROW_WORDS):
            for r in range(s * per, (s + 1) * per):
                start_rows(1 - slot, r)
            acc_lo = sh_ref[:, s * LANES:(s + 1) * LANES]
            acc_hi = sh_ref[:, HALF + s * LANES:HALF + (s + 1) * LANES]
            for k in range(TOP_K):
                lo, hi = _unpack_words(buf[slot][k, pl.ds(s, tm, stride=ROW_WORDS), :])
                w = wk_ref[:, k:k + 1]
                acc_lo = acc_lo + lo * w
                acc_hi = acc_hi + hi * w
            los.append(acc_lo)
            his.append(acc_hi)
        f = jnp.concatenate(los + his, axis=-1)
        out = x_ref[...] + mod_ref[0, 5:6, :] * f
        if final:
            y = out * lax.rsqrt(jnp.mean(out * out, axis=-1, keepdims=True) + EPS)
            out = y * gf_ref[...]
        o_ref[...] = out

        @pl.when(i == n - 1)
        def _():
            drain(1 - slot)

    @pl.when(i == 0)
    def _():
        fetch_idx(0, 0)

        def first(r, carry):
            start_rows(0, r)
            return carry

        lax.fori_loop(0, tm, first, 0)

    for slot in range(2):
        @pl.when(i % 2 == slot)
        def _(slot=slot):
            finish(slot)


def combine(x, shared, mod, wk_t, g_final, dest_tiles, ys, *, ntiles, final):
    tm = 128
    tps = SEQ // tm
    return pl.pallas_call(
        functools.partial(_combine_kernel, final=final),
        out_shape=jax.ShapeDtypeStruct((ntiles * tm, D_MODEL), F32),
        grid=(ntiles,),
        in_specs=[
            pl.BlockSpec((tm, D_MODEL), lambda i: (i, 0)),
            pl.BlockSpec((tm, D_MODEL), lambda i: (i, 0)),
            pl.BlockSpec((1, 6, D_MODEL), lambda i: (_mod_row(i, tps), 0, 0)),
            pl.BlockSpec((tm, TOP_K), lambda i: (i, 0)),
            pl.BlockSpec((1, D_MODEL), lambda i: (0, 0)),
            pl.BlockSpec(memory_space=pl.ANY),
            pl.BlockSpec(memory_space=pl.ANY),
        ],
        out_specs=pl.BlockSpec((tm, D_MODEL), lambda i: (i, 0)),
        scratch_shapes=[
            pltpu.SMEM((TOP_K * tm,), I32),
            pltpu.SMEM((TOP_K * tm,), I32),
            pltpu.VMEM((TOP_K, tm * ROW_WORDS, LANES), I32),
            pltpu.VMEM((TOP_K, tm * ROW_WORDS, LANES), I32),
            pltpu.SemaphoreType.DMA,
            pltpu.SemaphoreType.DMA((2,)),
        ],
        compiler_params=_cparams(("arbitrary",)),
        name="moe_combine",
    )(x, shared, mod, wk_t, g_final.reshape(1, D_MODEL), dest_tiles, ys)


def _routing_plan(eidx):
    t = eidx.shape[1]
    onehot = (eidx[None, :, :] == jnp.arange(N_EXPERTS, dtype=I32)[:, None, None])
    mask = jnp.any(onehot, axis=1).astype(I32)
    incl = jnp.cumsum(mask, axis=1)
    counts = incl[:, -1]
    rank = incl - mask
    pcounts = (counts + STEP_ROWS - 1) // STEP_ROWS * STEP_ROWS
    pends = jnp.cumsum(pcounts)
    pstarts = pends - pcounts
    slot = pstarts[:, None] + rank
    dest = jnp.sum(jnp.where(onehot, slot[:, None, :], 0), axis=0)
    nused = (pends[-1] // STEP_ROWS).astype(I32).reshape(1)
    blocks = jnp.arange(N_BLOCKS, dtype=I32)
    blk_e = jnp.minimum(jnp.sum((pends[None, :] <= (blocks * STEP_ROWS)[:, None]).astype(I32), axis=1),
                        N_EXPERTS - 1).astype(I32)
    nsub_e = (counts + EXPERT_ROWS - 1) // EXPERT_ROWS
    zends = jnp.where(counts > 0, pstarts + nsub_e * EXPERT_ROWS, 0)
    done = (blocks - pstarts[blk_e] // STEP_ROWS) * EXPERT_SUBS
    nsub = jnp.clip(nsub_e[blk_e] - done, 0, EXPERT_SUBS)
    ids = jnp.where(counts > 0, jnp.arange(N_EXPERTS, dtype=I32), N_EXPERTS)
    later = jnp.concatenate([lax.cummin(ids[::-1])[::-1][1:], jnp.full((1,), N_EXPERTS, I32)])
    next_e = jnp.where(later < N_EXPERTS, later, -1)[blk_e]
    return dest.astype(I32), zends.astype(I32), blk_e, nsub.astype(I32), next_e.astype(I32), nused


def _tile_major(dest, tm):
    k, t = dest.shape
    return dest.reshape(k, t // tm, tm).transpose(1, 0, 2).reshape(t // tm, k * tm)


def _rope_tables(n_rows):
    row = jnp.repeat(jnp.arange(n_rows, dtype=F32), GRID_W)
    col = jnp.tile(jnp.arange(GRID_W, dtype=F32), n_rows)
    half = QK_ROPE // 2
    inv = ROPE_BASE ** (-jnp.arange(0, half, 2, dtype=F32) / half)
    ang_r = row[:, None] * inv
    ang_c = col[:, None] * inv
    ang = jnp.concatenate([ang_r, ang_r, ang_c, ang_c], axis=-1)
    n = ang.shape[0]
    ones = jnp.ones((n, QK_NOPE), F32)
    zeros = jnp.zeros((n, QK_NOPE), F32)
    tail = jnp.zeros((n, HEAD_PAD - QK_NOPE - QK_ROPE), F32)
    cos = jnp.concatenate([ones, jnp.cos(ang), tail], axis=-1)
    sin = jnp.concatenate([zeros, jnp.sin(ang), tail], axis=-1)
    return cos, sin


def _identity_tables(n):
    cos = jnp.concatenate([jnp.ones((n, QK_NOPE + QK_ROPE), F32),
                           jnp.zeros((n, HEAD_PAD - QK_NOPE - QK_ROPE), F32)], axis=-1)
    return cos, jnp.zeros((n, HEAD_PAD), F32)


def _rotate_cols(w):
    i = np.arange(QK_ROPE)
    first = (i % (QK_ROPE // 2)) < (QK_ROPE // 4)
    perm = np.where(first, i + QK_ROPE // 4, i - QK_ROPE // 4)
    sign = np.where(first, -1.0, 1.0).astype(np.float32)
    return w[:, perm] * sign


def _mla_weights(g_cq, w_uq, g_ckv, w_ukv):
    dk = QK_NOPE + QK_ROPE
    wq = w_uq.reshape(Q_LORA, MLA_HEADS, dk)
    pad = jnp.zeros((Q_LORA, MLA_HEADS, HEAD_PAD - dk), F32)
    zero_nope = jnp.zeros((Q_LORA, MLA_HEADS, QK_NOPE), F32)
    wq_rot = _rotate_cols(wq[..., QK_NOPE:].reshape(Q_LORA * MLA_HEADS, QK_ROPE)).reshape(
        Q_LORA, MLA_HEADS, QK_ROPE)
    wq_a = jnp.concatenate([wq, pad], axis=-1).reshape(Q_LORA, -1)
    wq_b = jnp.concatenate([zero_nope, wq_rot, pad], axis=-1).reshape(Q_LORA, -1)
    wkv = w_ukv.reshape(KV_LORA, MLA_HEADS, QK_NOPE + V_HEAD)
    wk = jnp.concatenate([wkv[..., :QK_NOPE],
                          jnp.zeros((KV_LORA, MLA_HEADS, HEAD_PAD - QK_NOPE), F32)], axis=-1)
    wv = wkv[..., QK_NOPE:]
    eye = jnp.eye(QK_ROPE, dtype=F32)
    place = jnp.zeros((LANES, MLA_HEADS, HEAD_PAD), F32)
    place_a = place.at[:QK_ROPE, :, QK_NOPE:dk].set(jnp.broadcast_to(eye[:, None, :], (QK_ROPE, MLA_HEADS, QK_ROPE)))
    rot = _rotate_cols(eye)
    place_b = place.at[:QK_ROPE, :, QK_NOPE:dk].set(jnp.broadcast_to(rot[:, None, :], (QK_ROPE, MLA_HEADS, QK_ROPE)))
    return {
        "g_cq": g_cq.reshape(1, -1), "g_ckv": g_ckv.reshape(1, -1),
        "wq2t": jnp.concatenate([wq_a, wq_b], axis=-1).T.astype(BF16),
        "wk": wk.reshape(KV_LORA, -1).astype(BF16),
        "wvt": wv.reshape(KV_LORA, -1).T.astype(BF16),
        "pk": jnp.concatenate([place_a.reshape(LANES, -1), place_b.reshape(LANES, -1)], axis=-1).astype(BF16),
    }


def _pack_w_in(w):
    gates0 = MLA_IN + 2 * CONV_CH + FFT_CH + POOL_CH
    pad = jnp.zeros((D_MODEL, P_MLA_W - MLA_IN), w.dtype)
    return jnp.concatenate([w[:, gates0:], w[:, :MLA_IN], pad, w[:, MLA_IN:gates0]], axis=-1).astype(BF16)


def _dft_tables(n):
    gw = FFT_CH // FFT_GROUPS

    def angles(m):
        j = jnp.arange(m, dtype=I32)
        return (j[:, None] * j[None, :] % m).astype(F32) * (2.0 * np.pi / m)

    ac = angles(gw)
    an = angles(n)
    cs = jnp.concatenate([jnp.cos(ac), jnp.sin(ac)], axis=-1).astype(BF16)
    mseq = jnp.concatenate([jnp.cos(an), -jnp.sin(an)], axis=-1).astype(BF16)
    return cs, mseq


def _mixers(p, row0, nseq, n, wts, tables, conv_args, pool_args, dft):
    q4, k4, v = mla_prep(p, wts, tables, row0=row0, col0=P_MLA, nseq=nseq, n=n)
    cv = conv_branch(p, *conv_args, row0=row0, nseq=nseq, n=n)
    ff = fft_branch(p, dft[0], dft[1], row0=row0, nseq=nseq, n=n)
    po = pool_branch(p, *pool_args, row0=row0, nseq=nseq, n=n)
    return q4, k4, v, cv, ff, po


def kernel(x, c, ctx, c_ctx, w_mod, b_mod, g_norm1, g_norm2, w_in, g_cq, w_uq, g_ckv, w_ukv, w_o_mla, conv_w, conv_b, conv_ln_g, conv_ln_b, w_o_conv, w_o_fft, pool_w, pool_scale, w_o_pool, w_out, w_router, router_bias, w_exp_gate, w_exp_up, w_exp_down, w_sh_gate, w_sh_up, w_sh_down, g_final):
    xa = jnp.concatenate([x.reshape(T_LAT, D_MODEL), ctx.reshape(T_CTX, D_MODEL)], axis=0)
    cvec = jnp.concatenate([c, c_ctx[None, :], jnp.zeros((SUBLANES - BATCH - 1, D_MODEL), F32)], axis=0)
    mod_all = modulation_all(cvec, w_mod, b_mod).reshape(DEPTH, SUBLANES, 6, D_MODEL)
    rope_lat = _rope_tables(SEQ // GRID_W)
    rope_ctx = _identity_tables(CTX_LEN)
    dft_lat = _dft_tables(SEQ)
    dft_ctx = _dft_tables(CTX_LEN)
    lat_tiles = T_LAT // ROW_TILE
    all_tiles = T_ALL // ROW_TILE

    for l in range(DEPTH):
        last = l == DEPTH - 1
        mod = mod_all[l]
        w_in_p = _pack_w_in(w_in[l])
        wts = _mla_weights(g_cq[l], w_uq[l], g_ckv[l], w_ukv[l])
        conv_args = (conv_w[l], conv_b[l], conv_ln_g[l], conv_ln_b[l])
        pool_args = (pool_w[l].astype(BF16), pool_scale[l])
        wo4 = jnp.stack([w_o_mla[l], w_o_conv[l], w_o_fft[l], w_o_pool[l]], axis=0).astype(BF16)

        if last:
            p = normproj(xa, g_norm1[l], mod, w_in_p, tile0=0, ntiles=lat_tiles, col0=0, ncols=P_COLS)
            p_ctx = normproj(xa, g_norm1[l], mod, w_in_p, tile0=lat_tiles, ntiles=all_tiles - lat_tiles,
                             col0=P_MLA, ncols=P_MLA_W)
        else:
            p = normproj(xa, g_norm1[l], mod, w_in_p, tile0=0, ntiles=all_tiles, col0=0, ncols=P_COLS)
        q4, k4, v, cv, ff, po = _mixers(p, 0, BATCH, SEQ, wts, rope_lat, conv_args, pool_args, dft_lat)
        if last:
            qc, kc, vc = mla_prep(p_ctx, wts, rope_ctx, row0=0, col0=0, nseq=BATCH, n=CTX_LEN)
        else:
            qc, kc, vc, cvc, ffc, poc = _mixers(p, T_LAT, BATCH, CTX_LEN, wts, rope_ctx,
                                                conv_args, pool_args, dft_ctx)
        k_all = jnp.concatenate([k4, kc], axis=2)
        v_all = jnp.concatenate([v, vc], axis=2)
        at = attention(q4, k_all, v_all, nq=SEQ, nk=SEQ + CTX_LEN)
        if last:
            ntok = T_LAT
        else:
            atc = attention(qc, kc, vc, nq=CTX_LEN, nk=CTX_LEN)
            at = jnp.concatenate([at, atc], axis=0)
            cv = jnp.concatenate([cv, cvc], axis=0)
            ff = jnp.concatenate([ff, ffc], axis=0)
            po = jnp.concatenate([po, poc], axis=0)
            ntok = T_ALL
        xa = merge(xa, mod, at, cv, ff, po, p, wo4, w_out[l].astype(BF16), ntiles=ntok // 256)

        wr_t = w_router[l].T
        wr_hi = wr_t.astype(BF16)
        wr_lo = (wr_t - wr_hi.astype(F32)).astype(BF16)
        tokens, eidx, wk = norm_router(xa, g_norm2[l], mod, wr_hi, wr_lo, router_bias[l], ntiles=ntok // 256)
        dest, zends, blk_e, nsub, next_e, nused = _routing_plan(eidx)
        xs, sh = dispatch_shared(tokens, _tile_major(dest, 256), zends, w_sh_gate[l].astype(BF16),
                                 w_sh_up[l].astype(BF16), w_sh_down[l].astype(BF16), ntiles=ntok // 256)
        ys = experts(xs, blk_e, nsub, next_e, nused, w_exp_gate, w_exp_up, w_exp_down, layer=l)
        xa = combine(xa, sh, mod, wk.T, g_final, _tile_major(dest, 128), ys, ntiles=ntok // 128, final=last)
    return xa.reshape(BATCH, SEQ, D_MODEL)
```

```python
import functools

import jax
import jax.numpy as jnp
import numpy as np
from jax import lax
from jax.experimental import pallas as pl
from jax.experimental.pallas import tpu as pltpu

F32 = jnp.float32
BF16 = jnp.bfloat16
I32 = jnp.int32

D_MODEL = 2048
BATCH = 4
SEQ = 4096
DEPTH = 2
GRID_W = 64
CTX_LEN = 256
MLA_HEADS = 8
QK_NOPE = 64
QK_ROPE = 32
V_HEAD = 64
Q_LORA = 512
KV_LORA = 256
ROPE_BASE = 10000.0
CONV_CH = 512
CONV_WIDTH = 31
FFT_CH = 512
FFT_GROUPS = 4
POOL_CH = 512
POOL_WINDOWS = (2, 4, 8, 16)
POOL_GROUP = POOL_CH // len(POOL_WINDOWS)
N_BRANCHES = 4
N_EXPERTS = 64
N_EXPERT_GROUPS = 8
TOPK_GROUPS = 4
TOP_K = 8
EXPERT_FF = 512
SHARED_FF = 512
ROUTED_SCALE = 2.5
EPS = 1e-6

MLA_IN = Q_LORA + KV_LORA + QK_ROPE
T_LAT = BATCH * SEQ
T_CTX = BATCH * CTX_LEN
T_ALL = T_LAT + T_CTX

LANES = 128
SUBLANES = 8
VMEM_LIMIT_BYTES = 56 * 1024 * 1024

P_GATE = 0
P_MLA = N_BRANCHES * D_MODEL
P_MLA_W = 1024
P_CONV = P_MLA + P_MLA_W
P_FFT = P_CONV + 2 * CONV_CH
P_POOL = P_FFT + FFT_CH
P_COLS = P_POOL + POOL_CH
HEAD_PAD = 128
ATT_SCALE = (QK_NOPE + QK_ROPE) ** -0.5
Q_SCALE = ATT_SCALE * float(np.log2(np.e))

ROW_TILE = 1024
ROUTER_TILE = 256
EXPERT_ROWS = 256
EXPERT_SUBS = 4
STEP_ROWS = EXPERT_ROWS * EXPERT_SUBS
N_BLOCKS = -(-(T_ALL * TOP_K) // STEP_ROWS) + N_EXPERTS


def _cparams(sem, vmem=VMEM_LIMIT_BYTES):
    return pltpu.CompilerParams(dimension_semantics=sem, vmem_limit_bytes=vmem)


def _sigmoid(x):
    return 1.0 / (1.0 + jnp.exp(-x))


def _silu(x):
    return x * _sigmoid(x)


def _mod_row(tile, tiles_per_seq):
    return jnp.minimum(tile // tiles_per_seq, BATCH)


def _modulation_kernel(c_ref, w_ref, b_ref, o_ref):
    c = c_ref[...]
    s = _silu(c).astype(BF16)
    o_ref[0] = jnp.dot(s, w_ref[0].astype(BF16), preferred_element_type=F32) + b_ref[0]


def modulation_all(cvec, w_mod, b_mod):
    tn = 1024
    n = 6 * D_MODEL
    return pl.pallas_call(
        _modulation_kernel,
        out_shape=jax.ShapeDtypeStruct((DEPTH, SUBLANES, n), F32),
        grid=(DEPTH, n // tn),
        in_specs=[
            pl.BlockSpec((SUBLANES, D_MODEL), lambda l, j: (0, 0)),
            pl.BlockSpec((1, D_MODEL, tn), lambda l, j: (l, 0, j)),
            pl.BlockSpec((1, 1, tn), lambda l, j: (l, 0, j)),
        ],
        out_specs=pl.BlockSpec((1, SUBLANES, tn), lambda l, j: (l, 0, j)),
        compiler_params=_cparams(("arbitrary", "arbitrary")),
        name="modulation",
    )(cvec, w_mod, b_mod.reshape(DEPTH, 1, n))


def _adaln(x, g, shift, scale):
    y = x * lax.rsqrt(jnp.mean(x * x, axis=-1, keepdims=True) + EPS)
    return (y * g) * (1.0 + scale) + shift


def _normproj_kernel(x_ref, g_ref, mod_ref, w_ref, o_ref, h_ref, *, shift_row, chunk):
    @pl.when(pl.program_id(1) == 0)
    def _():
        shift = mod_ref[0, shift_row:shift_row + 1, :]
        scale = mod_ref[0, shift_row + 1:shift_row + 2, :]
        for r in range(0, x_ref.shape[0], chunk):
            h = _adaln(x_ref[r:r + chunk, :], g_ref[...], shift, scale)
            h_ref[r:r + chunk, :] = h.astype(BF16)

    o_ref[...] = jnp.dot(h_ref[...], w_ref[...], preferred_element_type=F32).astype(o_ref.dtype)


def normproj(x, g, mod, w, *, tile0, ntiles, col0, ncols, shift_row=0):
    tm, tn = ROW_TILE, 1024
    tps = SEQ // tm
    cb0 = col0 // tn
    return pl.pallas_call(
        functools.partial(_normproj_kernel, shift_row=shift_row, chunk=256),
        out_shape=jax.ShapeDtypeStruct((ntiles * tm, ncols), BF16),
        grid=(ntiles, ncols // tn),
        in_specs=[
            pl.BlockSpec((tm, D_MODEL), lambda i, j: (i + tile0, 0)),
            pl.BlockSpec((1, D_MODEL), lambda i, j: (0, 0)),
            pl.BlockSpec((1, 6, D_MODEL), lambda i, j: (_mod_row(i + tile0, tps), 0, 0)),
            pl.BlockSpec((D_MODEL, tn), lambda i, j: (0, j + cb0)),
        ],
        out_specs=pl.BlockSpec((tm, tn), lambda i, j: (i, j)),
        scratch_shapes=[pltpu.VMEM((tm, D_MODEL), BF16)],
        compiler_params=_cparams(("arbitrary", "arbitrary")),
        name="normproj",
    )(x, g.reshape(1, D_MODEL), mod, w)


def _rms_gain(x, g):
    y = x * lax.rsqrt(jnp.mean(x * x, axis=-1, keepdims=True) + EPS)
    return (y * g).astype(BF16)


_NT = (((1,), (1,)), ((), ()))


def _mla_prep_kernel(p_ref, gq_ref, gkv_ref, wqt_ref, wk_ref, wvt_ref, pk_ref, cos_ref, sin_ref,
                     cost_ref, sint_ref, qt_ref, k_ref, vt_ref):
    hw = MLA_HEADS * HEAD_PAD
    cqn = _rms_gain(p_ref[:, :Q_LORA].astype(F32), gq_ref[...])
    ckvn = _rms_gain(p_ref[:, Q_LORA:Q_LORA + KV_LORA].astype(F32), gkv_ref[...])
    kr = p_ref[:, Q_LORA + KV_LORA:Q_LORA + KV_LORA + LANES]
    q2t = lax.dot_general(wqt_ref[...], cqn, _NT, preferred_element_type=F32)
    vt_ref[0] = lax.dot_general(wvt_ref[...], ckvn, _NT, preferred_element_type=F32).astype(BF16)
    kk = jnp.dot(ckvn, wk_ref[...], preferred_element_type=F32)
    kr2 = jnp.dot(kr, pk_ref[...], preferred_element_type=F32)
    cos, sin = cos_ref[...], sin_ref[...]
    cost, sint = cost_ref[...], sint_ref[...]
    for h in range(MLA_HEADS):
        a = slice(h * HEAD_PAD, (h + 1) * HEAD_PAD)
        b = slice(hw + h * HEAD_PAD, hw + (h + 1) * HEAD_PAD)
        qt_ref[0, h] = ((q2t[a, :] * cost + q2t[b, :] * sint) * Q_SCALE).astype(BF16)
        k_ref[0, h] = (kk[:, a] + kr2[:, a] * cos + kr2[:, b] * sin).astype(BF16)


def mla_prep(p, wts, tables, *, row0, col0, nseq, n):
    tm = 256
    tps = n // tm
    hw = MLA_HEADS * HEAD_PAD
    vw = MLA_HEADS * V_HEAD
    blk0 = row0 // tm
    cblk = col0 // P_MLA_W
    cos_t, sin_t = tables
    const = lambda i: (0, 0)
    return pl.pallas_call(
        _mla_prep_kernel,
        out_shape=(jax.ShapeDtypeStruct((nseq, MLA_HEADS, HEAD_PAD, n), BF16),
                   jax.ShapeDtypeStruct((nseq, MLA_HEADS, n, HEAD_PAD), BF16),
                   jax.ShapeDtypeStruct((nseq, vw, n), BF16)),
        grid=(nseq * tps,),
        in_specs=[
            pl.BlockSpec((tm, P_MLA_W), lambda i: (i + blk0, cblk)),
            pl.BlockSpec((1, Q_LORA), const),
            pl.BlockSpec((1, KV_LORA), const),
            pl.BlockSpec((2 * hw, Q_LORA), const),
            pl.BlockSpec((KV_LORA, hw), const),
            pl.BlockSpec((vw, KV_LORA), const),
            pl.BlockSpec((LANES, 2 * hw), const),
            pl.BlockSpec((tm, HEAD_PAD), lambda i: (i % tps, 0)),
            pl.BlockSpec((tm, HEAD_PAD), lambda i: (i % tps, 0)),
            pl.BlockSpec((HEAD_PAD, tm), lambda i: (0, i % tps)),
            pl.BlockSpec((HEAD_PAD, tm), lambda i: (0, i % tps)),
        ],
        out_specs=(pl.BlockSpec((1, MLA_HEADS, HEAD_PAD, tm), lambda i: (i // tps, 0, 0, i % tps)),
                   pl.BlockSpec((1, MLA_HEADS, tm, HEAD_PAD), lambda i: (i // tps, 0, i % tps, 0)),
                   pl.BlockSpec((1, vw, tm), lambda i: (i // tps, 0, i % tps))),
        compiler_params=_cparams(("arbitrary",)),
        name="mla_prep",
    )(p, wts["g_cq"], wts["g_ckv"], wts["wq2t"], wts["wk"], wts["wvt"], wts["pk"],
      cos_t, sin_t, cos_t.T, sin_t.T)


ATT_HEADS = 4


def _attn_kernel(qt_ref, k_ref, vt_ref, o_ref, s_ref, p_ref, *, nk, ck):
    tq = qt_ref.shape[3]
    chunks = [slice(c, c + ck) for c in range(0, nk, ck)]
    nh = ATT_HEADS
    m = [None] * nh
    l = [jnp.zeros((1, tq), F32) for _ in range(nh)]
    ot = [jnp.zeros((V_HEAD, tq), F32) for _ in range(nh)]
    for step in range(nh + 2):
        ha, hb, hc = step, step - 1, step - 2
        for c in chunks:
            if 0 <= ha < nh:
                s = jnp.dot(k_ref[0, ha, c, :], qt_ref[0, ha], preferred_element_type=F32)
                s_ref[ha % 2, c, :] = s
                mc = jnp.max(s, axis=0, keepdims=True)
                m[ha] = mc if m[ha] is None else jnp.maximum(m[ha], mc)
            if 0 <= hb < nh:
                p = jnp.exp2(s_ref[hb % 2, c, :] - m[hb])
                l[hb] = l[hb] + jnp.sum(p, axis=0, keepdims=True)
                p_ref[hb % 2, c, :] = p.astype(BF16)
            if 0 <= hc < nh:
                ot[hc] = ot[hc] + jnp.dot(vt_ref[0, hc * V_HEAD:(hc + 1) * V_HEAD, c], p_ref[hc % 2, c, :],
                                          preferred_element_type=F32)
    outs = [ot[h] / l[h] for h in range(nh)]
    o_ref[...] = jnp.concatenate(outs, axis=0).T.astype(o_ref.dtype)


def attention(qt4, k4, vt3, *, nq, nk):
    tq = 256
    ck = 256
    nqt = nq // tq
    hs = ATT_HEADS
    return pl.pallas_call(
        functools.partial(_attn_kernel, nk=nk, ck=ck),
        out_shape=jax.ShapeDtypeStruct((BATCH * nq, MLA_HEADS * V_HEAD), BF16),
        grid=(BATCH, MLA_HEADS // hs, nqt),
        in_specs=[
            pl.BlockSpec((1, hs, HEAD_PAD, tq), lambda b, hg, i: (b, hg, 0, i)),
            pl.BlockSpec((1, hs, nk, HEAD_PAD), lambda b, hg, i: (b, hg, 0, 0)),
            pl.BlockSpec((1, hs * V_HEAD, nk), lambda b, hg, i: (b, hg, 0)),
        ],
        out_specs=pl.BlockSpec((tq, hs * V_HEAD), lambda b, hg, i: (b * nqt + i, hg)),
        scratch_shapes=[pltpu.VMEM((2, nk, tq), F32), pltpu.VMEM((2, nk, tq), BF16)],
        compiler_params=_cparams(("arbitrary", "arbitrary", "arbitrary")),
        name="attention",
    )(qt4, k4, vt3)


CONV_PAD = 16


def _conv_kernel(p_ref, w_ref, cb_ref, lg_ref, lb_ref, o_ref, u_ref, *, n, tr):
    zeros = jnp.zeros((CONV_PAD, CONV_CH), F32)
    u_ref[0:CONV_PAD, :] = zeros
    u_ref[CONV_PAD + n:2 * CONV_PAD + n, :] = zeros

    def glu(i, carry):
        r = pl.multiple_of(i * tr, tr)
        a = p_ref[pl.ds(r, tr), 0:CONV_CH].astype(F32)
        b = p_ref[pl.ds(r, tr), CONV_CH:2 * CONV_CH].astype(F32)
        u_ref[pl.ds(CONV_PAD + r, tr), :] = a * _sigmoid(b)
        return carry

    lax.fori_loop(0, n // tr, glu, 0)
    off = CONV_PAD - CONV_WIDTH // 2
    win_rows = tr + 2 * CONV_PAD

    def conv(i, carry):
        r = pl.multiple_of(i * tr, tr)
        strips = []
        for c0 in range(0, CONV_CH, LANES):
            win = u_ref[pl.ds(r, win_rows), c0:c0 + LANES]
            acc = jnp.zeros((tr, LANES), F32)
            for b in range(SUBLANES):
                wb = pltpu.roll(win, win_rows - (off + b), axis=0)
                for k in range(b, CONV_WIDTH, SUBLANES):
                    acc = acc + wb[k - b:k - b + tr, :] * w_ref[k:k + 1, c0:c0 + LANES]
            strips.append(acc)
        acc = jnp.concatenate(strips, axis=-1) + cb_ref[...]
        mu = jnp.mean(acc, axis=-1, keepdims=True)
        d = acc - mu
        var = jnp.mean(d * d, axis=-1, keepdims=True)
        y = (d * lax.rsqrt(var + EPS)) * lg_ref[...] + lb_ref[...]
        o_ref[pl.ds(r, tr), :] = _silu(y).astype(o_ref.dtype)
        return carry

    lax.fori_loop(0, n // tr, conv, 0)


def conv_branch(p, conv_w, conv_b, ln_g, ln_b, *, row0, nseq, n):
    tr = 128
    blk0 = row0 // n
    const = lambda s: (0, 0)
    return pl.pallas_call(
        functools.partial(_conv_kernel, n=n, tr=tr),
        out_shape=jax.ShapeDtypeStruct((nseq * n, CONV_CH), BF16),
        grid=(nseq,),
        in_specs=[
            pl.BlockSpec((n, 2 * CONV_CH), lambda s: (s + blk0, P_CONV // (2 * CONV_CH))),
            pl.BlockSpec((CONV_WIDTH, CONV_CH), const),
            pl.BlockSpec((1, CONV_CH), const),
            pl.BlockSpec((1, CONV_CH), const),
            pl.BlockSpec((1, CONV_CH), const),
        ],
        out_specs=pl.BlockSpec((n, CONV_CH), lambda s: (s, 0)),
        scratch_shapes=[pltpu.VMEM((n + 2 * CONV_PAD, CONV_CH), F32)],
        compiler_params=_cparams(("arbitrary",)),
        name="conv_branch",
    )(p, conv_w, conv_b.reshape(1, -1), ln_g.reshape(1, -1), ln_b.reshape(1, -1))


def _fft_kernel(z_ref, cs_ref, m_ref, o_ref, ab_ref, *, n, tc):
    gw = FFT_CH // FFT_GROUPS

    @pl.when(pl.program_id(1) == 0)
    def _():
        def chan(i, carry):
            r = pl.multiple_of(i * tc, tc)
            for g in range(FFT_GROUPS):
                zg = z_ref[pl.ds(r, tc), g * gw:(g + 1) * gw]
                ab = jnp.dot(zg, cs_ref[...], preferred_element_type=F32)
                ab_ref[pl.ds(r, tc), g * gw:(g + 1) * gw] = ab[:, :gw].astype(BF16)
                ab_ref[pl.ds(n + r, tc), g * gw:(g + 1) * gw] = ab[:, gw:].astype(BF16)
            return carry

        lax.fori_loop(0, n // tc, chan, 0)

    norm = 1.0 / float(np.sqrt(n * gw))
    o_ref[...] = (jnp.dot(m_ref[...], ab_ref[...], preferred_element_type=F32) * norm).astype(o_ref.dtype)


def fft_branch(p, cs, mseq, *, row0, nseq, n):
    tm = min(n, 512)
    tc = min(n, 512)
    blk0 = row0 // n
    nt = n // tm
    return pl.pallas_call(
        functools.partial(_fft_kernel, n=n, tc=tc),
        out_shape=jax.ShapeDtypeStruct((nseq * n, FFT_CH), BF16),
        grid=(nseq, nt),
        in_specs=[
            pl.BlockSpec((n, FFT_CH), lambda s, i: (s + blk0, P_FFT // FFT_CH)),
            pl.BlockSpec((FFT_CH // FFT_GROUPS, 2 * FFT_CH // FFT_GROUPS), lambda s, i: (0, 0)),
            pl.BlockSpec((tm, 2 * n), lambda s, i: (i, 0)),
        ],
        out_specs=pl.BlockSpec((tm, FFT_CH), lambda s, i: (s * nt + i, 0)),
        scratch_shapes=[pltpu.VMEM((2 * n, FFT_CH), BF16)],
        compiler_params=_cparams(("arbitrary", "arbitrary")),
        name="fft_branch",
    )(p, cs, mseq)


POOL_PAD = 16


def _pool_kernel(z_ref, pw_ref, ps_ref, o_ref, zp_ref, d_ref, *, n, tr, tc):
    zeros = jnp.zeros((POOL_PAD, POOL_CH), F32)
    zp_ref[0:POOL_PAD, :] = zeros
    zp_ref[POOL_PAD + n:2 * POOL_PAD + n, :] = zeros

    def fill(i, carry):
        r = pl.multiple_of(i * tc, tc)
        zp_ref[pl.ds(POOL_PAD + r, tc), :] = z_ref[pl.ds(r, tc), :].astype(F32)
        return carry

    lax.fori_loop(0, n // tc, fill, 0)

    win_rows = tr + 2 * POOL_PAD

    def pool(i, carry):
        r = pl.multiple_of(i * tr, tr)
        t = (r + lax.broadcasted_iota(I32, (tr, POOL_GROUP), 0)).astype(F32)
        for gi, w in enumerate(POOL_WINDOWS):
            cols = slice(gi * POOL_GROUP, (gi + 1) * POOL_GROUP)
            win = zp_ref[pl.ds(r, win_rows), cols]
            s = jnp.zeros((tr, POOL_GROUP), F32)
            for j in range(w):
                start = POOL_PAD - w // 2 + j
                if start % SUBLANES == 0:
                    s = s + win[start:start + tr, :]
                else:
                    s = s + pltpu.roll(win, win_rows - start, axis=0)[0:tr, :]
            lo = jnp.maximum(t - (w // 2), 0.0)
            hi = jnp.minimum(t - (w // 2) + w, float(n))
            z = win[POOL_PAD:POOL_PAD + tr, :]
            d_ref[pl.ds(r, tr), cols] = (s / (hi - lo) - z).astype(BF16)
        return carry

    lax.fori_loop(0, n // tr, pool, 0)

    def proj(i, carry):
        r = pl.multiple_of(i * tc, tc)
        for gi in range(len(POOL_WINDOWS)):
            cols = slice(gi * POOL_GROUP, (gi + 1) * POOL_GROUP)
            y = jnp.dot(d_ref[pl.ds(r, tc), cols], pw_ref[gi], preferred_element_type=F32)
            o_ref[pl.ds(r, tc), cols] = (y * ps_ref[:, cols]).astype(o_ref.dtype)
        return carry

    lax.fori_loop(0, n // tc, proj, 0)


def pool_branch(p, pool_w, pool_scale, *, row0, nseq, n):
    tr = 128
    tc = min(n, 512)
    blk0 = row0 // n
    return pl.pallas_call(
        functools.partial(_pool_kernel, n=n, tr=tr, tc=tc),
        out_shape=jax.ShapeDtypeStruct((nseq * n, POOL_CH), BF16),
        grid=(nseq,),
        in_specs=[
            pl.BlockSpec((n, POOL_CH), lambda s: (s + blk0, P_POOL // POOL_CH)),
            pl.BlockSpec((len(POOL_WINDOWS), POOL_GROUP, POOL_GROUP), lambda s: (0, 0, 0)),
            pl.BlockSpec((1, POOL_CH), lambda s: (0, 0)),
        ],
        out_specs=pl.BlockSpec((n, POOL_CH), lambda s: (s, 0)),
        scratch_shapes=[pltpu.VMEM((n + 2 * POOL_PAD, POOL_CH), F32),
                        pltpu.VMEM((n, POOL_CH), BF16)],
        compiler_params=_cparams(("arbitrary",)),
        name="pool_branch",
    )(p, pool_w, pool_scale.reshape(1, -1))


def _merge_kernel(x_ref, mod_ref, at_ref, cv_ref, ff_ref, po_ref, g_ref, wo_ref, wout_ref, o_ref):
    mix = None
    for b, br in enumerate((at_ref, cv_ref, ff_ref, po_ref)):
        y = jnp.dot(br[...], wo_ref[b], preferred_element_type=F32)
        gate = _sigmoid(g_ref[:, b * D_MODEL:(b + 1) * D_MODEL].astype(F32))
        mix = gate * y if mix is None else mix + gate * y
    out = jnp.dot(mix.astype(BF16), wout_ref[...], preferred_element_type=F32)
    o_ref[...] = x_ref[...] + mod_ref[0, 2:3, :] * out


def merge(x, mod, attn, conv, fft, pool, p, wo4, w_out, *, ntiles):
    tm = 256
    tps = SEQ // tm
    br_spec = pl.BlockSpec((tm, 512), lambda i: (i, 0))
    return pl.pallas_call(
        _merge_kernel,
        out_shape=jax.ShapeDtypeStruct((ntiles * tm, D_MODEL), F32),
        grid=(ntiles,),
        in_specs=[
            pl.BlockSpec((tm, D_MODEL), lambda i: (i, 0)),
            pl.BlockSpec((1, 6, D_MODEL), lambda i: (_mod_row(i, tps), 0, 0)),
            br_spec, br_spec, br_spec, br_spec,
            pl.BlockSpec((tm, N_BRANCHES * D_MODEL), lambda i: (i, P_GATE // (N_BRANCHES * D_MODEL))),
            pl.BlockSpec((N_BRANCHES, 512, D_MODEL), lambda i: (0, 0, 0), pipeline_mode=pl.Buffered(1)),
            pl.BlockSpec((D_MODEL, D_MODEL), lambda i: (0, 0), pipeline_mode=pl.Buffered(1)),
        ],
        out_specs=pl.BlockSpec((tm, D_MODEL), lambda i: (i, 0)),
        compiler_params=_cparams(("arbitrary",)),
        name="merge",
    )(x, mod, attn, conv, fft, pool, p, wo4, w_out)


def _first_index(hit_value, cand, ids, big):
    return jnp.min(jnp.where(cand == hit_value, ids, big), axis=0, keepdims=True)


HALF = D_MODEL // 2
ROW_WORDS = HALF // LANES


def _round_bf16_bits(x):
    u = lax.bitcast_convert_type(x, I32)
    odd = lax.shift_right_logical(u, 16) & 1
    return (u + 0x7FFF + odd) & jnp.int32(-65536)


def _pack_words(lo, hi):
    return lax.shift_right_logical(_round_bf16_bits(lo), 16) | _round_bf16_bits(hi)


def _unpack_words(w):
    lo = lax.bitcast_convert_type(lax.shift_left(w, 16), F32)
    hi = lax.bitcast_convert_type(w & jnp.int32(-65536), F32)
    return lo, hi


def _store_packed(ref, x, tok0=0):
    tm = x.shape[0]
    for s in range(ROW_WORDS):
        w = _pack_words(x[:, s * LANES:(s + 1) * LANES], x[:, HALF + s * LANES:HALF + (s + 1) * LANES])
        ref[pl.ds(tok0 * ROW_WORDS + s, tm, stride=ROW_WORDS), :] = w


def _load_packed(ref, tm, dtype, tok0=0):
    los, his = [], []
    for s in range(ROW_WORDS):
        lo, hi = _unpack_words(ref[pl.ds(tok0 * ROW_WORDS + s, tm, stride=ROW_WORDS), :])
        los.append(lo.astype(dtype))
        his.append(hi.astype(dtype))
    return jnp.concatenate(los + his, axis=-1)


def _router_kernel(x_ref, g_ref, mod_ref, wrh_ref, wrl_ref, rb_ref, tri_ref,
                   tok_ref, eidx_ref, wk_ref, rank_ref, cnt_ref):
    tm = x_ref.shape[0]
    h = _adaln(x_ref[...], g_ref[...], mod_ref[0, 3:4, :], mod_ref[0, 4:5, :])
    _store_packed(tok_ref, h)
    hh = h.astype(BF16)
    hl = (h - hh.astype(F32)).astype(BF16)
    nt = (((1,), (1,)), ((), ()))
    logits = (lax.dot_general(wrh_ref[...], hh, nt, preferred_element_type=F32)
              + lax.dot_general(wrh_ref[...], hl, nt, preferred_element_type=F32)
              + lax.dot_general(wrl_ref[...], hh, nt, preferred_element_type=F32))
    scores = _sigmoid(logits)
    sel = scores + rb_ref[...]
    per = N_EXPERTS // N_EXPERT_GROUPS
    assert per == SUBLANES and N_EXPERT_GROUPS == SUBLANES and TOP_K == SUBLANES
    neg = -jnp.inf
    sub = lax.broadcasted_iota(I32, (SUBLANES, tm), 0).astype(F32)
    sg = [sel[g * per:(g + 1) * per, :] for g in range(N_EXPERT_GROUPS)]
    sc = [scores[g * per:(g + 1) * per, :] for g in range(N_EXPERT_GROUPS)]
    gsc = jnp.zeros((SUBLANES, tm), F32)
    for g in range(N_EXPERT_GROUPS):
        m1 = jnp.max(sg[g], axis=0, keepdims=True)
        i1 = _first_index(m1, sg[g], sub, float(per))
        m2 = jnp.max(jnp.where(sub == i1, neg, sg[g]), axis=0, keepdims=True)
        gsc = jnp.where(sub == float(g), m1 + m2, gsc)
    gsel = jnp.zeros((SUBLANES, tm), F32)
    for _ in range(TOPK_GROUPS):
        m = jnp.max(gsc, axis=0, keepdims=True)
        hit = sub == _first_index(m, gsc, sub, float(N_EXPERT_GROUPS))
        gsel = jnp.where(hit, 1.0, gsel)
        gsc = jnp.where(hit, neg, gsc)
    cand = []
    for g in range(N_EXPERT_GROUPS):
        allowed = jnp.max(jnp.where(sub == float(g), gsel, 0.0), axis=0, keepdims=True)
        cand.append(jnp.where(allowed > 0.0, sg[g], neg))
    eid = [sub + float(g * per) for g in range(N_EXPERT_GROUPS)]
    idxs = jnp.zeros((SUBLANES, tm), F32)
    vals = jnp.zeros((SUBLANES, tm), F32)
    picked = [jnp.zeros((per, tm), F32) for _ in range(N_EXPERT_GROUPS)]
    idx_k = []
    for k in range(TOP_K):
        m = functools.reduce(jnp.maximum, [jnp.max(c, axis=0, keepdims=True) for c in cand])
        idx = functools.reduce(
            jnp.minimum, [_first_index(m, cand[g], eid[g], float(N_EXPERTS)) for g in range(N_EXPERT_GROUPS)])
        val = jnp.zeros((1, tm), F32)
        for g in range(N_EXPERT_GROUPS):
            hit = eid[g] == idx
            val = val + jnp.sum(jnp.where(hit, sc[g], 0.0), axis=0, keepdims=True)
            cand[g] = jnp.where(hit, neg, cand[g])
            picked[g] = jnp.where(hit, 1.0, picked[g])
        idxs = jnp.where(sub == float(k), idx, idxs)
        vals = jnp.where(sub == float(k), val, vals)
        idx_k.append(idx)
    eidx_ref[...] = idxs.astype(I32)
    wk_ref[...] = vals / jnp.sum(vals, axis=0, keepdims=True) * ROUTED_SCALE
    mask = jnp.concatenate(picked, axis=0)
    before = jnp.dot(mask.astype(BF16), tri_ref[...], preferred_element_type=F32)
    ranks = jnp.zeros((SUBLANES, tm), F32)
    for k in range(TOP_K):
        r = jnp.zeros((1, tm), F32)
        for g in range(N_EXPERT_GROUPS):
            r = r + jnp.sum(jnp.where(eid[g] == idx_k[k], before[g * per:(g + 1) * per, :], 0.0),
                            axis=0, keepdims=True)
        ranks = jnp.where(sub == float(k), r, ranks)
    rank_ref[...] = ranks.astype(I32)
    cnt_ref[...] = jnp.broadcast_to(jnp.sum(mask, axis=1, keepdims=True), (N_EXPERTS, LANES)).astype(I32)


def norm_router(x, g, mod, wr_hi, wr_lo, rbias, *, ntiles):
    tm = ROUTER_TILE
    tps = SEQ // tm
    t = ntiles * tm
    tri = jnp.triu(jnp.ones((tm, tm), BF16), k=1)
    kt_spec = pl.BlockSpec((TOP_K, tm), lambda i: (0, i))
    return pl.pallas_call(
        _router_kernel,
        out_shape=(jax.ShapeDtypeStruct((t * ROW_WORDS, LANES), I32),
                   jax.ShapeDtypeStruct((TOP_K, t), I32),
                   jax.ShapeDtypeStruct((TOP_K, t), F32),
                   jax.ShapeDtypeStruct((TOP_K, t), I32),
                   jax.ShapeDtypeStruct((N_EXPERTS, ntiles * LANES), I32)),
        grid=(ntiles,),
        in_specs=[
            pl.BlockSpec((tm, D_MODEL), lambda i: (i, 0)),
            pl.BlockSpec((1, D_MODEL), lambda i: (0, 0)),
            pl.BlockSpec((1, 6, D_MODEL), lambda i: (_mod_row(i, tps), 0, 0)),
            pl.BlockSpec((N_EXPERTS, D_MODEL), lambda i: (0, 0)),
            pl.BlockSpec((N_EXPERTS, D_MODEL), lambda i: (0, 0)),
            pl.BlockSpec((N_EXPERTS, 1), lambda i: (0, 0)),
            pl.BlockSpec((tm, tm), lambda i: (0, 0)),
        ],
        out_specs=(pl.BlockSpec((tm * ROW_WORDS, LANES), lambda i: (i, 0)),
                   kt_spec, kt_spec, kt_spec,
                   pl.BlockSpec((N_EXPERTS, LANES), lambda i: (0, i))),
        compiler_params=_cparams(("arbitrary",)),
        name="norm_router",
    )(x, g.reshape(1, D_MODEL), mod, wr_hi, wr_lo, rbias.reshape(N_EXPERTS, 1), tri)


def _row_copy(src, src_tok, dst, dst_tok, sem):
    s = pl.multiple_of(src_tok * ROW_WORDS, ROW_WORDS)
    d = pl.multiple_of(dst_tok * ROW_WORDS, ROW_WORDS)
    return pltpu.make_async_copy(src.at[pl.ds(s, ROW_WORDS)], dst.at[pl.ds(d, ROW_WORDS)], sem)


def _dispatch_kernel(zends_ref, tok_ref, dest_hbm, wg_ref, wu_ref, wd_ref, xs_hbm, sh_ref,
                     dest_smem, zero_ref, sem_idx, sem_z, sem_s):
    i = pl.program_id(0)
    tm = tok_ref.shape[0] // ROW_WORDS
    idx_copy = pltpu.make_async_copy(dest_hbm.at[i], dest_smem, sem_idx)
    idx_copy.start()
    blk = EXPERT_ROWS * ROW_WORDS

    def pad_copy(e):
        row = pl.multiple_of((zends_ref[e] - EXPERT_ROWS) * ROW_WORDS, blk)
        return pltpu.make_async_copy(zero_ref, xs_hbm.at[pl.ds(row, blk)], sem_z)

    @pl.when(i == 0)
    def _():
        zero_ref[...] = jnp.zeros_like(zero_ref)

        def start(e, carry):
            @pl.when(zends_ref[e] > 0)
            def _():
                pad_copy(e).start()
            return carry

        lax.fori_loop(0, N_EXPERTS, start, 0)

        def wait(e, carry):
            @pl.when(zends_ref[e] > 0)
            def _():
                pad_copy(e).wait()
            return carry

        lax.fori_loop(0, N_EXPERTS, wait, 0)

    idx_copy.wait()

    x = _load_packed(tok_ref, tm, BF16)
    pieces = SHARED_FF // LANES
    per = tm // pieces
    acc = None
    for p in range(pieces):
        for r in range(p * per, (p + 1) * per):
            for k in range(TOP_K):
                _row_copy(tok_ref, r, xs_hbm, dest_smem[k * tm + r], sem_s).start(priority=k % 2)
        c = slice(p * LANES, (p + 1) * LANES)
        g = jnp.dot(x, wg_ref[:, c], preferred_element_type=F32)
        u = jnp.dot(x, wu_ref[:, c], preferred_element_type=F32)
        part = jnp.dot((_silu(g) * u).astype(BF16), wd_ref[c, :], preferred_element_type=F32)
        acc = part if acc is None else acc + part
    sh_ref[...] = acc

    def drain(r, carry):
        for k in range(TOP_K):
            _row_copy(tok_ref, r, xs_hbm, dest_smem[k * tm + r], sem_s).wait()
        return carry

    lax.fori_loop(0, tm, drain, 0)


def dispatch_shared(tokens, dest_tiles, zends, wg, wu, wd, *, ntiles):
    tm = tokens.shape[0] // ROW_WORDS // ntiles
    const = lambda i, zends: (0, 0)
    return pl.pallas_call(
        _dispatch_kernel,
        out_shape=(jax.ShapeDtypeStruct((N_BLOCKS * STEP_ROWS * ROW_WORDS, LANES), I32),
                   jax.ShapeDtypeStruct((ntiles * tm, D_MODEL), F32)),
        grid_spec=pltpu.PrefetchScalarGridSpec(
            num_scalar_prefetch=1,
            grid=(ntiles,),
            in_specs=[
                pl.BlockSpec((tm * ROW_WORDS, LANES), lambda i, zends: (i, 0)),
                pl.BlockSpec(memory_space=pl.ANY),
                pl.BlockSpec((D_MODEL, SHARED_FF), const),
                pl.BlockSpec((D_MODEL, SHARED_FF), const),
                pl.BlockSpec((SHARED_FF, D_MODEL), const),
            ],
            out_specs=(pl.BlockSpec(memory_space=pl.ANY),
                       pl.BlockSpec((tm, D_MODEL), lambda i, zends: (i, 0))),
            scratch_shapes=[
                pltpu.SMEM((TOP_K * tm,), I32),
                pltpu.VMEM((EXPERT_ROWS * ROW_WORDS, LANES), I32),
                pltpu.SemaphoreType.DMA,
                pltpu.SemaphoreType.DMA,
                pltpu.SemaphoreType.DMA,
            ],
        ),
        compiler_params=_cparams(("arbitrary",)),
        name="moe_dispatch",
    )(zends, tokens, dest_tiles, wg, wu, wd)


def _experts_kernel(blk_e_ref, nsub_ref, next_e_ref, nused_ref, xs_ref, wg_hbm, wu_hbm, wd_hbm, y_ref,
                    wgs, wus, wds, wgb, wub, wdb, sem_w, *, layer):
    i = pl.program_id(0)
    e = blk_e_ref[i]
    prev = blk_e_ref[jnp.maximum(i - 1, 0)]

    def weight_copies(expert):
        row = layer * N_EXPERTS + expert
        return (pltpu.make_async_copy(wg_hbm.at[row], wgs, sem_w.at[0]),
                pltpu.make_async_copy(wu_hbm.at[row], wus, sem_w.at[1]),
                pltpu.make_async_copy(wd_hbm.at[row], wds, sem_w.at[2]))

    @pl.when(i < nused_ref[0])
    def _():
        @pl.when(i == 0)
        def _():
            for cp in weight_copies(e):
                cp.start()

        @pl.when((i == 0) | (e != prev))
        def _():
            for cp in weight_copies(e):
                cp.wait()
            wgb[...] = wgs[...].astype(BF16)
            wub[...] = wus[...].astype(BF16)
            wdb[...] = wds[...].astype(BF16)

            @pl.when(next_e_ref[i] >= 0)
            def _():
                for cp in weight_copies(next_e_ref[i]):
                    cp.start()

        for sb in range(EXPERT_SUBS):
            tok0 = sb * EXPERT_ROWS

            @pl.when(sb < nsub_ref[i])
            def _():
                x = _load_packed(xs_ref, EXPERT_ROWS, BF16, tok0)
                g = jnp.dot(x, wgb[...], preferred_element_type=F32)
                u = jnp.dot(x, wub[...], preferred_element_type=F32)
                hb = (_silu(g) * u).astype(BF16)
                _store_packed(y_ref, jnp.dot(hb, wdb[...], preferred_element_type=F32), tok0)

            @pl.when(sb >= nsub_ref[i])
            def _():
                rows = EXPERT_ROWS * ROW_WORDS
                y_ref[tok0 * ROW_WORDS:tok0 * ROW_WORDS + rows, :] = jnp.zeros((rows, LANES), I32)


def experts(xs, blk_e, nsub, next_e, nused, w_gate, w_up, w_down, *, layer):
    w_gate = w_gate.reshape(DEPTH * N_EXPERTS, D_MODEL, EXPERT_FF)
    w_up = w_up.reshape(DEPTH * N_EXPERTS, D_MODEL, EXPERT_FF)
    w_down = w_down.reshape(DEPTH * N_EXPERTS, EXPERT_FF, D_MODEL)

    def row_map(i, blk_e, nsub, next_e, nused):
        return (jnp.minimum(i, nused[0] - 1), 0)

    return pl.pallas_call(
        functools.partial(_experts_kernel, layer=layer),
        out_shape=jax.ShapeDtypeStruct((N_BLOCKS * STEP_ROWS * ROW_WORDS, LANES), I32),
        grid_spec=pltpu.PrefetchScalarGridSpec(
            num_scalar_prefetch=4,
            grid=(N_BLOCKS,),
            in_specs=[
                pl.BlockSpec((STEP_ROWS * ROW_WORDS, LANES), row_map),
                pl.BlockSpec(memory_space=pl.ANY),
                pl.BlockSpec(memory_space=pl.ANY),
                pl.BlockSpec(memory_space=pl.ANY),
            ],
            out_specs=pl.BlockSpec((STEP_ROWS * ROW_WORDS, LANES), row_map),
            scratch_shapes=[
                pltpu.VMEM((D_MODEL, EXPERT_FF), F32),
                pltpu.VMEM((D_MODEL, EXPERT_FF), F32),
                pltpu.VMEM((EXPERT_FF, D_MODEL), F32),
                pltpu.VMEM((D_MODEL, EXPERT_FF), BF16),
                pltpu.VMEM((D_MODEL, EXPERT_FF), BF16),
                pltpu.VMEM((EXPERT_FF, D_MODEL), BF16),
                pltpu.SemaphoreType.DMA((3,)),
            ],
        ),
        compiler_params=_cparams(("arbitrary",)),
        name="moe_experts",
    )(blk_e, nsub, next_e, nused, xs, w_gate, w_up, w_down)


def _combine_kernel(x_ref, sh_ref, mod_ref, wk_ref, gf_ref, dest_hbm, ys_hbm, o_ref,
                    idx0, idx1, buf0, buf1, sem_idx, sem_g, *, final):
    i = pl.program_id(0)
    n = pl.num_programs(0)
    tm = x_ref.shape[0]
    idx = (idx0, idx1)
    buf = (buf0, buf1)

    def fetch_idx(tile, slot):
        idx_copy = pltpu.make_async_copy(dest_hbm.at[tile], idx[slot], sem_idx)
        idx_copy.start()
        idx_copy.wait()

    def start_rows(slot, r):
        for k in range(TOP_K):
            _row_copy(ys_hbm, idx[slot][k * tm + r], buf[slot].at[k], r, sem_g.at[slot]).start(priority=k % 2)

    def drain(slot):
        def wait_rows(r, carry):
            for k in range(TOP_K):
                _row_copy(ys_hbm, 0, buf[slot].at[k], r, sem_g.at[slot]).wait()
            return carry

        lax.fori_loop(0, tm, wait_rows, 0)

    def finish(slot):
        fetch_idx(jnp.minimum(i + 1, n - 1), 1 - slot)
        drain(slot)
        per = tm // ROW_WORDS
        los, his = [], []
        for s in range(ROW_WORDS):
            for r in range(s * per, (s + 1) * per):
                start_rows(1 - slot, r)
            acc_lo = sh_ref[:, s * LANES:(s + 1) * LANES]
            acc_hi = sh_ref[:, HALF + s * LANES:HALF + (s + 1) * LANES]
            for k in range(TOP_K):
                lo, hi = _unpack_words(buf[slot][k, pl.ds(s, tm, stride=ROW_WORDS), :])
                w = wk_ref[:, k:k + 1]
                acc_lo = acc_lo + lo * w
                acc_hi = acc_hi + hi * w
            los.append(acc_lo)
            his.append(acc_hi)
        f = jnp.concatenate(los + his, axis=-1)
        out = x_ref[...] + mod_ref[0, 5:6, :] * f
        if final:
            y = out * lax.rsqrt(jnp.mean(out * out, axis=-1, keepdims=True) + EPS)
            out = y * gf_ref[...]
        o_ref[...] = out

        @pl.when(i == n - 1)
        def _():
            drain(1 - slot)

    @pl.when(i == 0)
    def _():
        fetch_idx(0, 0)

        def first(r, carry):
            start_rows(0, r)
            return carry

        lax.fori_loop(0, tm, first, 0)

    for slot in range(2):
        @pl.when(i % 2 == slot)
        def _(slot=slot):
            finish(slot)


def combine(x, shared, mod, wk_t, g_final, dest_tiles, ys, *, ntiles, final):
    tm = 128
    tps = SEQ // tm
    return pl.pallas_call(
        functools.partial(_combine_kernel, final=final),
        out_shape=jax.ShapeDtypeStruct((ntiles * tm, D_MODEL), F32),
        grid=(ntiles,),
        in_specs=[
            pl.BlockSpec((tm, D_MODEL), lambda i: (i, 0)),
            pl.BlockSpec((tm, D_MODEL), lambda i: (i, 0)),
            pl.BlockSpec((1, 6, D_MODEL), lambda i: (_mod_row(i, tps), 0, 0)),
            pl.BlockSpec((tm, TOP_K), lambda i: (i, 0)),
            pl.BlockSpec((1, D_MODEL), lambda i: (0, 0)),
            pl.BlockSpec(memory_space=pl.ANY),
            pl.BlockSpec(memory_space=pl.ANY),
        ],
        out_specs=pl.BlockSpec((tm, D_MODEL), lambda i: (i, 0)),
        scratch_shapes=[
            pltpu.SMEM((TOP_K * tm,), I32),
            pltpu.SMEM((TOP_K * tm,), I32),
            pltpu.VMEM((TOP_K, tm * ROW_WORDS, LANES), I32),
            pltpu.VMEM((TOP_K, tm * ROW_WORDS, LANES), I32),
            pltpu.SemaphoreType.DMA,
            pltpu.SemaphoreType.DMA((2,)),
        ],
        compiler_params=_cparams(("arbitrary",)),
        name="moe_combine",
    )(x, shared, mod, wk_t, g_final.reshape(1, D_MODEL), dest_tiles, ys)


def _routing_plan(eidx, rank, tile_counts):
    t = eidx.shape[1]
    tiles = tile_counts.shape[1]
    counts = jnp.sum(tile_counts, axis=1)
    pcounts = (counts + STEP_ROWS - 1) // STEP_ROWS * STEP_ROWS
    pends = jnp.cumsum(pcounts)
    pstarts = pends - pcounts
    base = pstarts[:, None] + jnp.cumsum(tile_counts, axis=1) - tile_counts
    onehot = (eidx.reshape(TOP_K, tiles, 1, t // tiles)
              == jnp.arange(N_EXPERTS, dtype=I32)[None, None, :, None])
    dest = rank + jnp.sum(jnp.where(onehot, base.T[None, :, :, None], 0), axis=2).reshape(TOP_K, t)
    nused = (pends[-1] // STEP_ROWS).astype(I32).reshape(1)
    blocks = jnp.arange(N_BLOCKS, dtype=I32)
    blk_e = jnp.minimum(jnp.sum((pends[None, :] <= (blocks * STEP_ROWS)[:, None]).astype(I32), axis=1),
                        N_EXPERTS - 1).astype(I32)
    nsub_e = (counts + EXPERT_ROWS - 1) // EXPERT_ROWS
    zends = jnp.where(counts > 0, pstarts + nsub_e * EXPERT_ROWS, 0)
    done = (blocks - pstarts[blk_e] // STEP_ROWS) * EXPERT_SUBS
    nsub = jnp.clip(nsub_e[blk_e] - done, 0, EXPERT_SUBS)
    ids = jnp.where(counts > 0, jnp.arange(N_EXPERTS, dtype=I32), N_EXPERTS)
    later = jnp.concatenate([lax.cummin(ids[::-1])[::-1][1:], jnp.full((1,), N_EXPERTS, I32)])
    next_e = jnp.where(later < N_EXPERTS, later, -1)[blk_e]
    return dest.astype(I32), zends.astype(I32), blk_e, nsub.astype(I32), next_e.astype(I32), nused


def _tile_major(dest, tm):
    k, t = dest.shape
    return dest.reshape(k, t // tm, tm).transpose(1, 0, 2).reshape(t // tm, k * tm)


def _rope_tables(n_rows):
    row = jnp.repeat(jnp.arange(n_rows, dtype=F32), GRID_W)
    col = jnp.tile(jnp.arange(GRID_W, dtype=F32), n_rows)
    half = QK_ROPE // 2
    inv = ROPE_BASE ** (-jnp.arange(0, half, 2, dtype=F32) / half)
    ang_r = row[:, None] * inv
    ang_c = col[:, None] * inv
    ang = jnp.concatenate([ang_r, ang_r, ang_c, ang_c], axis=-1)
    n = ang.shape[0]
    ones = jnp.ones((n, QK_NOPE), F32)
    zeros = jnp.zeros((n, QK_NOPE), F32)
    tail = jnp.zeros((n, HEAD_PAD - QK_NOPE - QK_ROPE), F32)
    cos = jnp.concatenate([ones, jnp.cos(ang), tail], axis=-1)
    sin = jnp.concatenate([zeros, jnp.sin(ang), tail], axis=-1)
    return cos, sin


def _identity_tables(n):
    cos = jnp.concatenate([jnp.ones((n, QK_NOPE + QK_ROPE), F32),
                           jnp.zeros((n, HEAD_PAD - QK_NOPE - QK_ROPE), F32)], axis=-1)
    return cos, jnp.zeros((n, HEAD_PAD), F32)


def _rotate_cols(w):
    i = np.arange(QK_ROPE)
    first = (i % (QK_ROPE // 2)) < (QK_ROPE // 4)
    perm = np.where(first, i + QK_ROPE // 4, i - QK_ROPE // 4)
    sign = np.where(first, -1.0, 1.0).astype(np.float32)
    return w[:, perm] * sign


def _mla_weights(g_cq, w_uq, g_ckv, w_ukv):
    dk = QK_NOPE + QK_ROPE
    wq = w_uq.reshape(Q_LORA, MLA_HEADS, dk)
    pad = jnp.zeros((Q_LORA, MLA_HEADS, HEAD_PAD - dk), F32)
    zero_nope = jnp.zeros((Q_LORA, MLA_HEADS, QK_NOPE), F32)
    wq_rot = _rotate_cols(wq[..., QK_NOPE:].reshape(Q_LORA * MLA_HEADS, QK_ROPE)).reshape(
        Q_LORA, MLA_HEADS, QK_ROPE)
    wq_a = jnp.concatenate([wq, pad], axis=-1).reshape(Q_LORA, -1)
    wq_b = jnp.concatenate([zero_nope, wq_rot, pad], axis=-1).reshape(Q_LORA, -1)
    wkv = w_ukv.reshape(KV_LORA, MLA_HEADS, QK_NOPE + V_HEAD)
    wk = jnp.concatenate([wkv[..., :QK_NOPE],
                          jnp.zeros((KV_LORA, MLA_HEADS, HEAD_PAD - QK_NOPE), F32)], axis=-1)
    wv = wkv[..., QK_NOPE:]
    eye = jnp.eye(QK_ROPE, dtype=F32)
    place = jnp.zeros((LANES, MLA_HEADS, HEAD_PAD), F32)
    place_a = place.at[:QK_ROPE, :, QK_NOPE:dk].set(jnp.broadcast_to(eye[:, None, :], (QK_ROPE, MLA_HEADS, QK_ROPE)))
    rot = _rotate_cols(eye)
    place_b = place.at[:QK_ROPE, :, QK_NOPE:dk].set(jnp.broadcast_to(rot[:, None, :], (QK_ROPE, MLA_HEADS, QK_ROPE)))
    return {
        "g_cq": g_cq.reshape(1, -1), "g_ckv": g_ckv.reshape(1, -1),
        "wq2t": jnp.concatenate([wq_a, wq_b], axis=-1).T.astype(BF16),
        "wk": wk.reshape(KV_LORA, -1).astype(BF16),
        "wvt": wv.reshape(KV_LORA, -1).T.astype(BF16),
        "pk": jnp.concatenate([place_a.reshape(LANES, -1), place_b.reshape(LANES, -1)], axis=-1).astype(BF16),
    }


def _pack_w_in(w):
    gates0 = MLA_IN + 2 * CONV_CH + FFT_CH + POOL_CH
    pad = jnp.zeros((D_MODEL, P_MLA_W - MLA_IN), w.dtype)
    return jnp.concatenate([w[:, gates0:], w[:, :MLA_IN], pad, w[:, MLA_IN:gates0]], axis=-1).astype(BF16)


def _dft_tables(n):
    gw = FFT_CH // FFT_GROUPS

    def angles(rows, m):
        k = jnp.arange(m, dtype=I32)
        return (rows[:, None] * k[None, :] % m).astype(F32) * (2.0 * np.pi / m)

    ac = angles(jnp.arange(gw, dtype=I32), gw)
    cs = jnp.concatenate([jnp.cos(ac), jnp.sin(ac)], axis=-1).astype(BF16)
    ns = int(np.sqrt(n))
    assert ns * ns == n
    a_hi = angles(jnp.arange(ns, dtype=I32) * ns, n)
    a_lo = angles(jnp.arange(ns, dtype=I32), n)
    ch, sh, cl, sl = jnp.cos(a_hi)[:, None, :], jnp.sin(a_hi)[:, None, :], jnp.cos(a_lo)[None], jnp.sin(a_lo)[None]
    cos_n = (ch * cl - sh * sl).reshape(n, n)
    sin_n = (sh * cl + ch * sl).reshape(n, n)
    mseq = jnp.concatenate([cos_n, -sin_n], axis=-1).astype(BF16)
    return cs, mseq


def _mixers(p, row0, nseq, n, wts, tables, conv_args, pool_args, dft):
    q4, k4, v = mla_prep(p, wts, tables, row0=row0, col0=P_MLA, nseq=nseq, n=n)
    cv = conv_branch(p, *conv_args, row0=row0, nseq=nseq, n=n)
    ff = fft_branch(p, dft[0], dft[1], row0=row0, nseq=nseq, n=n)
    po = pool_branch(p, *pool_args, row0=row0, nseq=nseq, n=n)
    return q4, k4, v, cv, ff, po


def kernel(x, c, ctx, c_ctx, w_mod, b_mod, g_norm1, g_norm2, w_in, g_cq, w_uq, g_ckv, w_ukv, w_o_mla, conv_w, conv_b, conv_ln_g, conv_ln_b, w_o_conv, w_o_fft, pool_w, pool_scale, w_o_pool, w_out, w_router, router_bias, w_exp_gate, w_exp_up, w_exp_down, w_sh_gate, w_sh_up, w_sh_down, g_final):
    xa = jnp.concatenate([x.reshape(T_LAT, D_MODEL), ctx.reshape(T_CTX, D_MODEL)], axis=0)
    cvec = jnp.concatenate([c, c_ctx[None, :], jnp.zeros((SUBLANES - BATCH - 1, D_MODEL), F32)], axis=0)
    mod_all = modulation_all(cvec, w_mod, b_mod).reshape(DEPTH, SUBLANES, 6, D_MODEL)
    rope_lat = _rope_tables(SEQ // GRID_W)
    rope_ctx = _identity_tables(CTX_LEN)
    dft_lat = _dft_tables(SEQ)
    dft_ctx = _dft_tables(CTX_LEN)
    lat_tiles = T_LAT // ROW_TILE
    all_tiles = T_ALL // ROW_TILE

    for l in range(DEPTH):
        last = l == DEPTH - 1
        mod = mod_all[l]
        w_in_p = _pack_w_in(w_in[l])
        wts = _mla_weights(g_cq[l], w_uq[l], g_ckv[l], w_ukv[l])
        conv_args = (conv_w[l], conv_b[l], conv_ln_g[l], conv_ln_b[l])
        pool_args = (pool_w[l].astype(BF16), pool_scale[l])
        wo4 = jnp.stack([w_o_mla[l], w_o_conv[l], w_o_fft[l], w_o_pool[l]], axis=0).astype(BF16)

        if last:
            p = normproj(xa, g_norm1[l], mod, w_in_p, tile0=0, ntiles=lat_tiles, col0=0, ncols=P_COLS)
            p_ctx = normproj(xa, g_norm1[l], mod, w_in_p, tile0=lat_tiles, ntiles=all_tiles - lat_tiles,
                             col0=P_MLA, ncols=P_MLA_W)
        else:
            p = normproj(xa, g_norm1[l], mod, w_in_p, tile0=0, ntiles=all_tiles, col0=0, ncols=P_COLS)
        q4, k4, v, cv, ff, po = _mixers(p, 0, BATCH, SEQ, wts, rope_lat, conv_args, pool_args, dft_lat)
        if last:
            qc, kc, vc = mla_prep(p_ctx, wts, rope_ctx, row0=0, col0=0, nseq=BATCH, n=CTX_LEN)
        else:
            qc, kc, vc, cvc, ffc, poc = _mixers(p, T_LAT, BATCH, CTX_LEN, wts, rope_ctx,
                                                conv_args, pool_args, dft_ctx)
        k_all = jnp.concatenate([k4, kc], axis=2)
        v_all = jnp.concatenate([v, vc], axis=2)
        at = attention(q4, k_all, v_all, nq=SEQ, nk=SEQ + CTX_LEN)
        if last:
            ntok = T_LAT
        else:
            atc = attention(qc, kc, vc, nq=CTX_LEN, nk=CTX_LEN)
            at = jnp.concatenate([at, atc], axis=0)
            cv = jnp.concatenate([cv, cvc], axis=0)
            ff = jnp.concatenate([ff, ffc], axis=0)
            po = jnp.concatenate([po, poc], axis=0)
            ntok = T_ALL
        xa = merge(xa, mod, at, cv, ff, po, p, wo4, w_out[l].astype(BF16), ntiles=ntok // 256)

        wr_t = w_router[l].T
        wr_hi = wr_t.astype(BF16)
        wr_lo = (wr_t - wr_hi.astype(F32)).astype(BF16)
        tokens, eidx, wk, rank, cnt = norm_router(xa, g_norm2[l], mod, wr_hi, wr_lo, router_bias[l],
                                                  ntiles=ntok // ROUTER_TILE)
        dest, zends, blk_e, nsub, next_e, nused = _routing_plan(eidx, rank, cnt[:, ::LANES])
        xs, sh = dispatch_shared(tokens, _tile_major(dest, 256), zends, w_sh_gate[l].astype(BF16),
                                 w_sh_up[l].astype(BF16), w_sh_down[l].astype(BF16), ntiles=ntok // 256)
        ys = experts(xs, blk_e, nsub, next_e, nused, w_exp_gate, w_exp_up, w_exp_down, layer=l)
        xa = combine(xa, sh, mod, wk.T, g_final, _tile_major(dest, 128), ys, ntiles=ntok // 128, final=last)
    return xa.reshape(BATCH, SEQ, D_MODEL)
```

```python
import functools

import jax
import jax.numpy as jnp
import numpy as np
from jax import lax
from jax.experimental import pallas as pl
from jax.experimental.pallas import tpu as pltpu

F32 = jnp.float32
BF16 = jnp.bfloat16
I32 = jnp.int32

D_MODEL = 2048
BATCH = 4
SEQ = 4096
DEPTH = 2
GRID_W = 64
CTX_LEN = 256
MLA_HEADS = 8
QK_NOPE = 64
QK_ROPE = 32
V_HEAD = 64
Q_LORA = 512
KV_LORA = 256
ROPE_BASE = 10000.0
CONV_CH = 512
CONV_WIDTH = 31
FFT_CH = 512
FFT_GROUPS = 4
POOL_CH = 512
POOL_WINDOWS = (2, 4, 8, 16)
POOL_GROUP = POOL_CH // len(POOL_WINDOWS)
N_BRANCHES = 4
N_EXPERTS = 64
N_EXPERT_GROUPS = 8
TOPK_GROUPS = 4
TOP_K = 8
EXPERT_FF = 512
SHARED_FF = 512
ROUTED_SCALE = 2.5
EPS = 1e-6

MLA_IN = Q_LORA + KV_LORA + QK_ROPE
T_LAT = BATCH * SEQ
T_CTX = BATCH * CTX_LEN
T_ALL = T_LAT + T_CTX

LANES = 128
SUBLANES = 8
VMEM_LIMIT_BYTES = 56 * 1024 * 1024

P_GATE = 0
P_MLA = N_BRANCHES * D_MODEL
P_MLA_W = 1024
P_CONV = P_MLA + P_MLA_W
P_FFT = P_CONV + 2 * CONV_CH
P_POOL = P_FFT + FFT_CH
P_COLS = P_POOL + POOL_CH
HEAD_PAD = 128
ATT_SCALE = (QK_NOPE + QK_ROPE) ** -0.5
Q_SCALE = ATT_SCALE * float(np.log2(np.e))

ROW_TILE = 1024
ROUTER_TILE = 256
EXPERT_ROWS = 256
EXPERT_SUBS = 4
STEP_ROWS = EXPERT_ROWS * EXPERT_SUBS
N_BLOCKS = -(-(T_ALL * TOP_K) // STEP_ROWS) + N_EXPERTS


def _cparams(sem, vmem=VMEM_LIMIT_BYTES):
    return pltpu.CompilerParams(dimension_semantics=sem, vmem_limit_bytes=vmem)


def _sigmoid(x):
    return 1.0 / (1.0 + jnp.exp(-x))


def _silu(x):
    return x * _sigmoid(x)


def _mod_row(tile, tiles_per_seq):
    return jnp.minimum(tile // tiles_per_seq, BATCH)


def _modulation_kernel(c_ref, w_ref, b_ref, o_ref):
    c = c_ref[...]
    s = _silu(c).astype(BF16)
    o_ref[0] = jnp.dot(s, w_ref[0].astype(BF16), preferred_element_type=F32) + b_ref[0]


def modulation_all(cvec, w_mod, b_mod):
    tn = 1024
    n = 6 * D_MODEL
    return pl.pallas_call(
        _modulation_kernel,
        out_shape=jax.ShapeDtypeStruct((DEPTH, SUBLANES, n), F32),
        grid=(DEPTH, n // tn),
        in_specs=[
            pl.BlockSpec((SUBLANES, D_MODEL), lambda l, j: (0, 0)),
            pl.BlockSpec((1, D_MODEL, tn), lambda l, j: (l, 0, j)),
            pl.BlockSpec((1, 1, tn), lambda l, j: (l, 0, j)),
        ],
        out_specs=pl.BlockSpec((1, SUBLANES, tn), lambda l, j: (l, 0, j)),
        compiler_params=_cparams(("arbitrary", "arbitrary")),
        name="modulation",
    )(cvec, w_mod, b_mod.reshape(DEPTH, 1, n))


def _adaln(x, g, shift, scale):
    y = x * lax.rsqrt(jnp.mean(x * x, axis=-1, keepdims=True) + EPS)
    return (y * g) * (1.0 + scale) + shift


def _normproj_kernel(x_ref, g_ref, mod_ref, wg_ref, wr_ref, o_ref, h_ref, *, shift_row, chunk, cb0, ngate):
    j = pl.program_id(1)

    @pl.when(j == 0)
    def _():
        shift = mod_ref[0, shift_row:shift_row + 1, :]
        scale = mod_ref[0, shift_row + 1:shift_row + 2, :]
        for r in range(0, x_ref.shape[0], chunk):
            h = _adaln(x_ref[r:r + chunk, :], g_ref[...], shift, scale)
            h_ref[r:r + chunk, :] = h.astype(BF16)

    @pl.when(j + cb0 < ngate)
    def _():
        o_ref[...] = jnp.dot(h_ref[...], wg_ref[...], preferred_element_type=F32).astype(o_ref.dtype)

    @pl.when(j + cb0 >= ngate)
    def _():
        o_ref[...] = jnp.dot(h_ref[...], wr_ref[...], preferred_element_type=F32).astype(o_ref.dtype)


def normproj(x, g, mod, w_gate, w_rest, *, tile0, ntiles, col0, ncols, mod_tile0=None, shift_row=0):
    tm, tn = ROW_TILE, 1024
    tps = SEQ // tm
    cb0 = col0 // tn
    ngate = w_gate.shape[1] // tn
    mt0 = tile0 if mod_tile0 is None else mod_tile0
    return pl.pallas_call(
        functools.partial(_normproj_kernel, shift_row=shift_row, chunk=256, cb0=cb0, ngate=ngate),
        out_shape=jax.ShapeDtypeStruct((ntiles * tm, ncols), BF16),
        grid=(ntiles, ncols // tn),
        in_specs=[
            pl.BlockSpec((tm, D_MODEL), lambda i, j: (i + tile0, 0)),
            pl.BlockSpec((1, D_MODEL), lambda i, j: (0, 0)),
            pl.BlockSpec((1, 6, D_MODEL), lambda i, j: (_mod_row(i + mt0, tps), 0, 0)),
            pl.BlockSpec((D_MODEL, tn), lambda i, j: (0, jnp.minimum(j + cb0, ngate - 1))),
            pl.BlockSpec((D_MODEL, tn), lambda i, j: (0, jnp.maximum(j + cb0 - ngate, 0))),
        ],
        out_specs=pl.BlockSpec((tm, tn), lambda i, j: (i, j)),
        scratch_shapes=[pltpu.VMEM((tm, D_MODEL), BF16)],
        compiler_params=_cparams(("arbitrary", "arbitrary")),
        name="normproj",
    )(x, g.reshape(1, D_MODEL), mod, w_gate, w_rest)


def _rms_gain(x, g):
    y = x * lax.rsqrt(jnp.mean(x * x, axis=-1, keepdims=True) + EPS)
    return (y * g).astype(BF16)


_NT = (((1,), (1,)), ((), ()))


def _mla_prep_kernel(p_ref, gq_ref, gkv_ref, wqt_ref, wk_ref, wvt_ref, pk_ref, cos_ref, sin_ref,
                     cost_ref, sint_ref, qt_ref, k_ref, vt_ref):
    hw = MLA_HEADS * HEAD_PAD
    cqn = _rms_gain(p_ref[:, :Q_LORA].astype(F32), gq_ref[...])
    ckvn = _rms_gain(p_ref[:, Q_LORA:Q_LORA + KV_LORA].astype(F32), gkv_ref[...])
    kr = p_ref[:, Q_LORA + KV_LORA:Q_LORA + KV_LORA + LANES]
    q2t = lax.dot_general(wqt_ref[...], cqn, _NT, preferred_element_type=F32)
    vt_ref[0] = lax.dot_general(wvt_ref[...], ckvn, _NT, preferred_element_type=F32).astype(BF16)
    kk = jnp.dot(ckvn, wk_ref[...], preferred_element_type=F32)
    kr2 = jnp.dot(kr, pk_ref[...], preferred_element_type=F32)
    cos, sin = cos_ref[...], sin_ref[...]
    cost, sint = cost_ref[...], sint_ref[...]
    for h in range(MLA_HEADS):
        a = slice(h * HEAD_PAD, (h + 1) * HEAD_PAD)
        b = slice(hw + h * HEAD_PAD, hw + (h + 1) * HEAD_PAD)
        qt_ref[0, h] = ((q2t[a, :] * cost + q2t[b, :] * sint) * Q_SCALE).astype(BF16)
        k_ref[0, h] = (kk[:, a] + kr2[:, a] * cos + kr2[:, b] * sin).astype(BF16)


def mla_prep(p, wts, tables, *, row0, col0, nseq, n):
    tm = 256
    tps = n // tm
    hw = MLA_HEADS * HEAD_PAD
    vw = MLA_HEADS * V_HEAD
    blk0 = row0 // tm
    cblk = col0 // P_MLA_W
    cos_t, sin_t = tables
    const = lambda i: (0, 0)
    return pl.pallas_call(
        _mla_prep_kernel,
        out_shape=(jax.ShapeDtypeStruct((nseq, MLA_HEADS, HEAD_PAD, n), BF16),
                   jax.ShapeDtypeStruct((nseq, MLA_HEADS, n, HEAD_PAD), BF16),
                   jax.ShapeDtypeStruct((nseq, vw, n), BF16)),
        grid=(nseq * tps,),
        in_specs=[
            pl.BlockSpec((tm, P_MLA_W), lambda i: (i + blk0, cblk)),
            pl.BlockSpec((1, Q_LORA), const),
            pl.BlockSpec((1, KV_LORA), const),
            pl.BlockSpec((2 * hw, Q_LORA), const),
            pl.BlockSpec((KV_LORA, hw), const),
            pl.BlockSpec((vw, KV_LORA), const),
            pl.BlockSpec((LANES, 2 * hw), const),
            pl.BlockSpec((tm, HEAD_PAD), lambda i: (i % tps, 0)),
            pl.BlockSpec((tm, HEAD_PAD), lambda i: (i % tps, 0)),
            pl.BlockSpec((HEAD_PAD, tm), lambda i: (0, i % tps)),
            pl.BlockSpec((HEAD_PAD, tm), lambda i: (0, i % tps)),
        ],
        out_specs=(pl.BlockSpec((1, MLA_HEADS, HEAD_PAD, tm), lambda i: (i // tps, 0, 0, i % tps)),
                   pl.BlockSpec((1, MLA_HEADS, tm, HEAD_PAD), lambda i: (i // tps, 0, i % tps, 0)),
                   pl.BlockSpec((1, vw, tm), lambda i: (i // tps, 0, i % tps))),
        compiler_params=_cparams(("arbitrary",)),
        name="mla_prep",
    )(p, wts["g_cq"], wts["g_ckv"], wts["wq2t"], wts["wk"], wts["wvt"], wts["pk"],
      cos_t, sin_t, cos_t.T, sin_t.T)


ATT_HEADS = 4


def _attn_kernel(qt_ref, k_ref, vt_ref, o_ref, s_ref, p_ref, *, nk, ck):
    tq = qt_ref.shape[3]
    chunks = [slice(c, c + ck) for c in range(0, nk, ck)]
    nh = ATT_HEADS
    m = [None] * nh
    l = [jnp.zeros((1, tq), F32) for _ in range(nh)]
    ot = [jnp.zeros((V_HEAD, tq), F32) for _ in range(nh)]
    for step in range(nh + 2):
        ha, hb, hc = step, step - 1, step - 2
        for c in chunks:
            if 0 <= ha < nh:
                s = jnp.dot(k_ref[0, ha, c, :], qt_ref[0, ha], preferred_element_type=F32)
                s_ref[ha % 2, c, :] = s
                mc = jnp.max(s, axis=0, keepdims=True)
                m[ha] = mc if m[ha] is None else jnp.maximum(m[ha], mc)
            if 0 <= hb < nh:
                p = jnp.exp2(s_ref[hb % 2, c, :] - m[hb])
                l[hb] = l[hb] + jnp.sum(p, axis=0, keepdims=True)
                p_ref[hb % 2, c, :] = p.astype(BF16)
            if 0 <= hc < nh:
                ot[hc] = ot[hc] + jnp.dot(vt_ref[0, hc * V_HEAD:(hc + 1) * V_HEAD, c], p_ref[hc % 2, c, :],
                                          preferred_element_type=F32)
    outs = [ot[h] / l[h] for h in range(nh)]
    o_ref[...] = jnp.concatenate(outs, axis=0).T.astype(o_ref.dtype)


def attention(qt4, k4, vt3, *, nq, nk):
    tq = 256
    ck = 256
    nqt = nq // tq
    hs = ATT_HEADS
    return pl.pallas_call(
        functools.partial(_attn_kernel, nk=nk, ck=ck),
        out_shape=jax.ShapeDtypeStruct((BATCH * nq, MLA_HEADS * V_HEAD), BF16),
        grid=(BATCH, MLA_HEADS // hs, nqt),
        in_specs=[
            pl.BlockSpec((1, hs, HEAD_PAD, tq), lambda b, hg, i: (b, hg, 0, i)),
            pl.BlockSpec((1, hs, nk, HEAD_PAD), lambda b, hg, i: (b, hg, 0, 0)),
            pl.BlockSpec((1, hs * V_HEAD, nk), lambda b, hg, i: (b, hg, 0)),
        ],
        out_specs=pl.BlockSpec((tq, hs * V_HEAD), lambda b, hg, i: (b * nqt + i, hg)),
        scratch_shapes=[pltpu.VMEM((2, nk, tq), F32), pltpu.VMEM((2, nk, tq), BF16)],
        compiler_params=_cparams(("arbitrary", "arbitrary", "arbitrary")),
        name="attention",
    )(qt4, k4, vt3)


CONV_PAD = 16


def _conv_kernel(p_ref, w_ref, cb_ref, lg_ref, lb_ref, o_ref, u_ref, *, n, tr):
    zeros = jnp.zeros((CONV_PAD, CONV_CH), F32)
    u_ref[0:CONV_PAD, :] = zeros
    u_ref[CONV_PAD + n:2 * CONV_PAD + n, :] = zeros

    def glu(i, carry):
        r = pl.multiple_of(i * tr, tr)
        a = p_ref[pl.ds(r, tr), 0:CONV_CH].astype(F32)
        b = p_ref[pl.ds(r, tr), CONV_CH:2 * CONV_CH].astype(F32)
        u_ref[pl.ds(CONV_PAD + r, tr), :] = a * _sigmoid(b)
        return carry

    lax.fori_loop(0, n // tr, glu, 0)
    off = CONV_PAD - CONV_WIDTH // 2
    win_rows = tr + 2 * CONV_PAD

    def conv(i, carry):
        r = pl.multiple_of(i * tr, tr)
        strips = []
        for c0 in range(0, CONV_CH, LANES):
            win = u_ref[pl.ds(r, win_rows), c0:c0 + LANES]
            acc = jnp.zeros((tr, LANES), F32)
            for b in range(SUBLANES):
                wb = pltpu.roll(win, win_rows - (off + b), axis=0)
                for k in range(b, CONV_WIDTH, SUBLANES):
                    acc = acc + wb[k - b:k - b + tr, :] * w_ref[k:k + 1, c0:c0 + LANES]
            strips.append(acc)
        acc = jnp.concatenate(strips, axis=-1) + cb_ref[...]
        mu = jnp.mean(acc, axis=-1, keepdims=True)
        d = acc - mu
        var = jnp.mean(d * d, axis=-1, keepdims=True)
        y = (d * lax.rsqrt(var + EPS)) * lg_ref[...] + lb_ref[...]
        o_ref[pl.ds(r, tr), :] = _silu(y).astype(o_ref.dtype)
        return carry

    lax.fori_loop(0, n // tr, conv, 0)


def conv_branch(p, conv_w, conv_b, ln_g, ln_b, *, row0, nseq, n):
    tr = 128
    blk0 = row0 // n
    const = lambda s: (0, 0)
    return pl.pallas_call(
        functools.partial(_conv_kernel, n=n, tr=tr),
        out_shape=jax.ShapeDtypeStruct((nseq * n, CONV_CH), BF16),
        grid=(nseq,),
        in_specs=[
            pl.BlockSpec((n, 2 * CONV_CH), lambda s: (s + blk0, P_CONV // (2 * CONV_CH))),
            pl.BlockSpec((CONV_WIDTH, CONV_CH), const),
            pl.BlockSpec((1, CONV_CH), const),
            pl.BlockSpec((1, CONV_CH), const),
            pl.BlockSpec((1, CONV_CH), const),
        ],
        out_specs=pl.BlockSpec((n, CONV_CH), lambda s: (s, 0)),
        scratch_shapes=[pltpu.VMEM((n + 2 * CONV_PAD, CONV_CH), F32)],
        compiler_params=_cparams(("arbitrary",)),
        name="conv_branch",
    )(p, conv_w, conv_b.reshape(1, -1), ln_g.reshape(1, -1), ln_b.reshape(1, -1))


def _fft_kernel(z_ref, cs_ref, mc_ref, ms_ref, o_ref, ab_ref, *, n, tc):
    gw = FFT_CH // FFT_GROUPS

    @pl.when(pl.program_id(1) == 0)
    def _():
        def chan(i, carry):
            r = pl.multiple_of(i * tc, tc)
            for g in range(FFT_GROUPS):
                zg = z_ref[pl.ds(r, tc), g * gw:(g + 1) * gw]
                ab = jnp.dot(zg, cs_ref[...], preferred_element_type=F32)
                ab_ref[pl.ds(r, tc), g * gw:(g + 1) * gw] = ab[:, :gw].astype(BF16)
                ab_ref[pl.ds(n + r, tc), g * gw:(g + 1) * gw] = ab[:, gw:].astype(BF16)
            return carry

        lax.fori_loop(0, n // tc, chan, 0)

    norm = 1.0 / float(np.sqrt(n * gw))
    re = (jnp.dot(mc_ref[...], ab_ref[0:n, :], preferred_element_type=F32)
          + jnp.dot(ms_ref[...], ab_ref[n:2 * n, :], preferred_element_type=F32))
    o_ref[...] = (re * norm).astype(o_ref.dtype)


def fft_branch(p, cs, mcos, mnsin, *, row0, nseq, n):
    tm = min(n, 512)
    tc = min(n, 512)
    blk0 = row0 // n
    nt = n // tm
    return pl.pallas_call(
        functools.partial(_fft_kernel, n=n, tc=tc),
        out_shape=jax.ShapeDtypeStruct((nseq * n, FFT_CH), BF16),
        grid=(nseq, nt),
        in_specs=[
            pl.BlockSpec((n, FFT_CH), lambda s, i: (s + blk0, P_FFT // FFT_CH)),
            pl.BlockSpec((FFT_CH // FFT_GROUPS, 2 * FFT_CH // FFT_GROUPS), lambda s, i: (0, 0)),
            pl.BlockSpec((tm, n), lambda s, i: (i, 0)),
            pl.BlockSpec((tm, n), lambda s, i: (i, 0)),
        ],
        out_specs=pl.BlockSpec((tm, FFT_CH), lambda s, i: (s * nt + i, 0)),
        scratch_shapes=[pltpu.VMEM((2 * n, FFT_CH), BF16)],
        compiler_params=_cparams(("arbitrary", "arbitrary")),
        name="fft_branch",
    )(p, cs, mcos, mnsin)


POOL_PAD = 16


def _pool_kernel(z_ref, pw_ref, ps_ref, o_ref, zp_ref, d_ref, *, n, tr, tc):
    zeros = jnp.zeros((POOL_PAD, POOL_CH), F32)
    zp_ref[0:POOL_PAD, :] = zeros
    zp_ref[POOL_PAD + n:2 * POOL_PAD + n, :] = zeros

    def fill(i, carry):
        r = pl.multiple_of(i * tc, tc)
        zp_ref[pl.ds(POOL_PAD + r, tc), :] = z_ref[pl.ds(r, tc), :].astype(F32)
        return carry

    lax.fori_loop(0, n // tc, fill, 0)

    win_rows = tr + 2 * POOL_PAD

    def pool(i, carry):
        r = pl.multiple_of(i * tr, tr)
        t = (r + lax.broadcasted_iota(I32, (tr, POOL_GROUP), 0)).astype(F32)
        for gi, w in enumerate(POOL_WINDOWS):
            cols = slice(gi * POOL_GROUP, (gi + 1) * POOL_GROUP)
            win = zp_ref[pl.ds(r, win_rows), cols]
            s = jnp.zeros((tr, POOL_GROUP), F32)
            for j in range(w):
                start = POOL_PAD - w // 2 + j
                if start % SUBLANES == 0:
                    s = s + win[start:start + tr, :]
                else:
                    s = s + pltpu.roll(win, win_rows - start, axis=0)[0:tr, :]
            lo = jnp.maximum(t - (w // 2), 0.0)
            hi = jnp.minimum(t - (w // 2) + w, float(n))
            z = win[POOL_PAD:POOL_PAD + tr, :]
            d_ref[pl.ds(r, tr), cols] = (s / (hi - lo) - z).astype(BF16)
        return carry

    lax.fori_loop(0, n // tr, pool, 0)

    def proj(i, carry):
        r = pl.multiple_of(i * tc, tc)
        for gi in range(len(POOL_WINDOWS)):
            cols = slice(gi * POOL_GROUP, (gi + 1) * POOL_GROUP)
            y = jnp.dot(d_ref[pl.ds(r, tc), cols], pw_ref[gi], preferred_element_type=F32)
            o_ref[pl.ds(r, tc), cols] = (y * ps_ref[:, cols]).astype(o_ref.dtype)
        return carry

    lax.fori_loop(0, n // tc, proj, 0)


def pool_branch(p, pool_w, pool_scale, *, row0, nseq, n):
    tr = 128
    tc = min(n, 512)
    blk0 = row0 // n
    return pl.pallas_call(
        functools.partial(_pool_kernel, n=n, tr=tr, tc=tc),
        out_shape=jax.ShapeDtypeStruct((nseq * n, POOL_CH), BF16),
        grid=(nseq,),
        in_specs=[
            pl.BlockSpec((n, POOL_CH), lambda s: (s + blk0, P_POOL // POOL_CH)),
            pl.BlockSpec((len(POOL_WINDOWS), POOL_GROUP, POOL_GROUP), lambda s: (0, 0, 0)),
            pl.BlockSpec((1, POOL_CH), lambda s: (0, 0)),
        ],
        out_specs=pl.BlockSpec((n, POOL_CH), lambda s: (s, 0)),
        scratch_shapes=[pltpu.VMEM((n + 2 * POOL_PAD, POOL_CH), F32),
                        pltpu.VMEM((n, POOL_CH), BF16)],
        compiler_params=_cparams(("arbitrary",)),
        name="pool_branch",
    )(p, pool_w, pool_scale.reshape(1, -1))


def _merge_kernel(x_ref, mod_ref, at_ref, cv_ref, ff_ref, po_ref, g_ref, wo_ref, wout_ref, o_ref):
    mix = None
    for b, br in enumerate((at_ref, cv_ref, ff_ref, po_ref)):
        y = jnp.dot(br[...], wo_ref[b], preferred_element_type=F32)
        gate = _sigmoid(g_ref[:, b * D_MODEL:(b + 1) * D_MODEL].astype(F32))
        mix = gate * y if mix is None else mix + gate * y
    out = jnp.dot(mix.astype(BF16), wout_ref[...], preferred_element_type=F32)
    o_ref[...] = x_ref[...] + mod_ref[0, 2:3, :] * out


def _merge_into_kernel(x_ref, mod_ref, at_ref, cv_ref, ff_ref, po_ref, g_ref, wo_ref, wout_ref, prev_ref, o_ref):
    del prev_ref
    _merge_kernel(x_ref, mod_ref, at_ref, cv_ref, ff_ref, po_ref, g_ref, wo_ref, wout_ref, o_ref)


def merge(x, mod, attn, conv, fft, pool, p, wo4, w_out, *, ntiles, out_tile0=0, out_tiles=None, into=None):
    tm = 256
    tps = SEQ // tm
    out_tiles = ntiles if out_tiles is None else out_tiles
    br_spec = pl.BlockSpec((tm, 512), lambda i: (i, 0))
    in_specs = [
        pl.BlockSpec((tm, D_MODEL), lambda i: (i, 0)),
        pl.BlockSpec((1, 6, D_MODEL), lambda i: (_mod_row(i + out_tile0, tps), 0, 0)),
        br_spec, br_spec, br_spec, br_spec,
        pl.BlockSpec((tm, N_BRANCHES * D_MODEL), lambda i: (i, P_GATE // (N_BRANCHES * D_MODEL))),
        pl.BlockSpec((N_BRANCHES, 512, D_MODEL), lambda i: (0, 0, 0), pipeline_mode=pl.Buffered(1)),
        pl.BlockSpec((D_MODEL, D_MODEL), lambda i: (0, 0), pipeline_mode=pl.Buffered(1)),
    ]
    operands = [x, mod, attn, conv, fft, pool, p, wo4, w_out]
    kernel_fn, aliases = _merge_kernel, {}
    if into is not None:
        in_specs.append(pl.BlockSpec(memory_space=pl.ANY))
        operands.append(into)
        kernel_fn, aliases = _merge_into_kernel, {len(operands) - 1: 0}
    return pl.pallas_call(
        kernel_fn,
        out_shape=jax.ShapeDtypeStruct((out_tiles * tm, D_MODEL), F32),
        grid=(ntiles,),
        in_specs=in_specs,
        out_specs=pl.BlockSpec((tm, D_MODEL), lambda i: (i + out_tile0, 0)),
        input_output_aliases=aliases,
        compiler_params=_cparams(("arbitrary",)),
        name="merge",
    )(*operands)


def _first_index(hit_value, cand, ids, big):
    return jnp.min(jnp.where(cand == hit_value, ids, big), axis=0, keepdims=True)


HALF = D_MODEL // 2
ROW_WORDS = HALF // LANES


def _round_bf16_bits(x):
    u = lax.bitcast_convert_type(x, I32)
    odd = lax.shift_right_logical(u, 16) & 1
    return (u + 0x7FFF + odd) & jnp.int32(-65536)


def _pack_words(lo, hi):
    return lax.shift_right_logical(_round_bf16_bits(lo), 16) | _round_bf16_bits(hi)


def _unpack_words(w):
    lo = lax.bitcast_convert_type(lax.shift_left(w, 16), F32)
    hi = lax.bitcast_convert_type(w & jnp.int32(-65536), F32)
    return lo, hi


def _store_packed(ref, x, tok0=0):
    tm = x.shape[0]
    for s in range(ROW_WORDS):
        w = _pack_words(x[:, s * LANES:(s + 1) * LANES], x[:, HALF + s * LANES:HALF + (s + 1) * LANES])
        ref[pl.ds(tok0 * ROW_WORDS + s, tm, stride=ROW_WORDS), :] = w


def _load_packed(ref, tm, dtype, tok0=0):
    los, his = [], []
    for s in range(ROW_WORDS):
        lo, hi = _unpack_words(ref[pl.ds(tok0 * ROW_WORDS + s, tm, stride=ROW_WORDS), :])
        los.append(lo.astype(dtype))
        his.append(hi.astype(dtype))
    return jnp.concatenate(los + his, axis=-1)


def _router_kernel(x_ref, g_ref, mod_ref, wrh_ref, wrl_ref, rb_ref, tri_ref,
                   tok_ref, eidx_ref, wk_ref, rank_ref, cnt_ref):
    tm = x_ref.shape[0]
    h = _adaln(x_ref[...], g_ref[...], mod_ref[0, 3:4, :], mod_ref[0, 4:5, :])
    _store_packed(tok_ref, h)
    hh = h.astype(BF16)
    hl = (h - hh.astype(F32)).astype(BF16)
    nt = (((1,), (1,)), ((), ()))
    logits = (lax.dot_general(wrh_ref[...], hh, nt, preferred_element_type=F32)
              + lax.dot_general(wrh_ref[...], hl, nt, preferred_element_type=F32)
              + lax.dot_general(wrl_ref[...], hh, nt, preferred_element_type=F32))
    scores = _sigmoid(logits)
    sel = scores + rb_ref[...]
    per = N_EXPERTS // N_EXPERT_GROUPS
    assert per == SUBLANES and N_EXPERT_GROUPS == SUBLANES and TOP_K == SUBLANES
    neg = -jnp.inf
    sub = lax.broadcasted_iota(I32, (SUBLANES, tm), 0).astype(F32)
    sg = [sel[g * per:(g + 1) * per, :] for g in range(N_EXPERT_GROUPS)]
    sc = [scores[g * per:(g + 1) * per, :] for g in range(N_EXPERT_GROUPS)]
    gsc = jnp.zeros((SUBLANES, tm), F32)
    for g in range(N_EXPERT_GROUPS):
        m1 = jnp.max(sg[g], axis=0, keepdims=True)
        i1 = _first_index(m1, sg[g], sub, float(per))
        m2 = jnp.max(jnp.where(sub == i1, neg, sg[g]), axis=0, keepdims=True)
        gsc = jnp.where(sub == float(g), m1 + m2, gsc)
    gsel = jnp.zeros((SUBLANES, tm), F32)
    for _ in range(TOPK_GROUPS):
        m = jnp.max(gsc, axis=0, keepdims=True)
        hit = sub == _first_index(m, gsc, sub, float(N_EXPERT_GROUPS))
        gsel = jnp.where(hit, 1.0, gsel)
        gsc = jnp.where(hit, neg, gsc)
    cand = []
    for g in range(N_EXPERT_GROUPS):
        allowed = jnp.max(jnp.where(sub == float(g), gsel, 0.0), axis=0, keepdims=True)
        cand.append(jnp.where(allowed > 0.0, sg[g], neg))
    eid = [sub + float(g * per) for g in range(N_EXPERT_GROUPS)]
    idxs = jnp.zeros((SUBLANES, tm), F32)
    vals = jnp.zeros((SUBLANES, tm), F32)
    picked = [jnp.zeros((per, tm), F32) for _ in range(N_EXPERT_GROUPS)]
    idx_k = []
    for k in range(TOP_K):
        m = functools.reduce(jnp.maximum, [jnp.max(c, axis=0, keepdims=True) for c in cand])
        idx = functools.reduce(
            jnp.minimum, [_first_index(m, cand[g], eid[g], float(N_EXPERTS)) for g in range(N_EXPERT_GROUPS)])
        val = jnp.zeros((1, tm), F32)
        for g in range(N_EXPERT_GROUPS):
            hit = eid[g] == idx
            val = val + jnp.sum(jnp.where(hit, sc[g], 0.0), axis=0, keepdims=True)
            cand[g] = jnp.where(hit, neg, cand[g])
            picked[g] = jnp.where(hit, 1.0, picked[g])
        idxs = jnp.where(sub == float(k), idx, idxs)
        vals = jnp.where(sub == float(k), val, vals)
        idx_k.append(idx)
    eidx_ref[...] = idxs.astype(I32)
    wk_ref[...] = vals / jnp.sum(vals, axis=0, keepdims=True) * ROUTED_SCALE
    mask = jnp.concatenate(picked, axis=0)
    before = jnp.dot(mask.astype(BF16), tri_ref[...], preferred_element_type=F32)
    ranks = jnp.zeros((SUBLANES, tm), F32)
    for k in range(TOP_K):
        r = jnp.zeros((1, tm), F32)
        for g in range(N_EXPERT_GROUPS):
            r = r + jnp.sum(jnp.where(eid[g] == idx_k[k], before[g * per:(g + 1) * per, :], 0.0),
                            axis=0, keepdims=True)
        ranks = jnp.where(sub == float(k), r, ranks)
    rank_ref[...] = ranks.astype(I32)
    cnt_ref[...] = jnp.broadcast_to(jnp.sum(mask, axis=1, keepdims=True), (N_EXPERTS, LANES)).astype(I32)


def norm_router(x, g, mod, wr_hi, wr_lo, rbias, *, ntiles):
    tm = ROUTER_TILE
    tps = SEQ // tm
    t = ntiles * tm
    tri = jnp.triu(jnp.ones((tm, tm), BF16), k=1)
    kt_spec = pl.BlockSpec((TOP_K, tm), lambda i: (0, i))
    return pl.pallas_call(
        _router_kernel,
        out_shape=(jax.ShapeDtypeStruct((t * ROW_WORDS, LANES), I32),
                   jax.ShapeDtypeStruct((TOP_K, t), I32),
                   jax.ShapeDtypeStruct((TOP_K, t), F32),
                   jax.ShapeDtypeStruct((TOP_K, t), I32),
                   jax.ShapeDtypeStruct((N_EXPERTS, ntiles * LANES), I32)),
        grid=(ntiles,),
        in_specs=[
            pl.BlockSpec((tm, D_MODEL), lambda i: (i, 0)),
            pl.BlockSpec((1, D_MODEL), lambda i: (0, 0)),
            pl.BlockSpec((1, 6, D_MODEL), lambda i: (_mod_row(i, tps), 0, 0)),
            pl.BlockSpec((N_EXPERTS, D_MODEL), lambda i: (0, 0)),
            pl.BlockSpec((N_EXPERTS, D_MODEL), lambda i: (0, 0)),
            pl.BlockSpec((N_EXPERTS, 1), lambda i: (0, 0)),
            pl.BlockSpec((tm, tm), lambda i: (0, 0)),
        ],
        out_specs=(pl.BlockSpec((tm * ROW_WORDS, LANES), lambda i: (i, 0)),
                   kt_spec, kt_spec, kt_spec,
                   pl.BlockSpec((N_EXPERTS, LANES), lambda i: (0, i))),
        compiler_params=_cparams(("arbitrary",)),
        name="norm_router",
    )(x, g.reshape(1, D_MODEL), mod, wr_hi, wr_lo, rbias.reshape(N_EXPERTS, 1), tri)


def _row_copy(src, src_tok, dst, dst_tok, sem):
    s = pl.multiple_of(src_tok * ROW_WORDS, ROW_WORDS)
    d = pl.multiple_of(dst_tok * ROW_WORDS, ROW_WORDS)
    return pltpu.make_async_copy(src.at[pl.ds(s, ROW_WORDS)], dst.at[pl.ds(d, ROW_WORDS)], sem)


def _dispatch_kernel(zends_ref, tok_ref, dest_hbm, wg_ref, wu_ref, wd_ref, xs_hbm, sh_ref,
                     dest_smem, zero_ref, sem_idx, sem_z, sem_s):
    i = pl.program_id(0)
    tm = tok_ref.shape[0] // ROW_WORDS
    idx_copy = pltpu.make_async_copy(dest_hbm.at[i], dest_smem, sem_idx)
    idx_copy.start()
    blk = EXPERT_ROWS * ROW_WORDS

    def pad_copy(e):
        row = pl.multiple_of((zends_ref[e] - EXPERT_ROWS) * ROW_WORDS, blk)
        return pltpu.make_async_copy(zero_ref, xs_hbm.at[pl.ds(row, blk)], sem_z)

    @pl.when(i == 0)
    def _():
        zero_ref[...] = jnp.zeros_like(zero_ref)

        def start(e, carry):
            @pl.when(zends_ref[e] > 0)
            def _():
                pad_copy(e).start()
            return carry

        lax.fori_loop(0, N_EXPERTS, start, 0)

        def wait(e, carry):
            @pl.when(zends_ref[e] > 0)
            def _():
                pad_copy(e).wait()
            return carry

        lax.fori_loop(0, N_EXPERTS, wait, 0)

    idx_copy.wait()

    x = _load_packed(tok_ref, tm, BF16)
    pieces = SHARED_FF // LANES
    per = tm // pieces
    acc = None
    for p in range(pieces):
        for r in range(p * per, (p + 1) * per):
            for k in range(TOP_K):
                _row_copy(tok_ref, r, xs_hbm, dest_smem[k * tm + r], sem_s).start(priority=k % 2)
        c = slice(p * LANES, (p + 1) * LANES)
        g = jnp.dot(x, wg_ref[:, c], preferred_element_type=F32)
        u = jnp.dot(x, wu_ref[:, c], preferred_element_type=F32)
        part = jnp.dot((_silu(g) * u).astype(BF16), wd_ref[c, :], preferred_element_type=F32)
        acc = part if acc is None else acc + part
    sh_ref[...] = acc

    def drain(r, carry):
        for k in range(TOP_K):
            _row_copy(tok_ref, r, xs_hbm, dest_smem[k * tm + r], sem_s).wait()
        return carry

    lax.fori_loop(0, tm, drain, 0)


def dispatch_shared(tokens, dest_tiles, zends, wg, wu, wd, *, ntiles):
    tm = tokens.shape[0] // ROW_WORDS // ntiles
    const = lambda i, zends: (0, 0)
    return pl.pallas_call(
        _dispatch_kernel,
        out_shape=(jax.ShapeDtypeStruct((N_BLOCKS * STEP_ROWS * ROW_WORDS, LANES), I32),
                   jax.ShapeDtypeStruct((ntiles * tm, D_MODEL), F32)),
        grid_spec=pltpu.PrefetchScalarGridSpec(
            num_scalar_prefetch=1,
            grid=(ntiles,),
            in_specs=[
                pl.BlockSpec((tm * ROW_WORDS, LANES), lambda i, zends: (i, 0)),
                pl.BlockSpec(memory_space=pl.ANY),
                pl.BlockSpec((D_MODEL, SHARED_FF), const),
                pl.BlockSpec((D_MODEL, SHARED_FF), const),
                pl.BlockSpec((SHARED_FF, D_MODEL), const),
            ],
            out_specs=(pl.BlockSpec(memory_space=pl.ANY),
                       pl.BlockSpec((tm, D_MODEL), lambda i, zends: (i, 0))),
            scratch_shapes=[
                pltpu.SMEM((TOP_K * tm,), I32),
                pltpu.VMEM((EXPERT_ROWS * ROW_WORDS, LANES), I32),
                pltpu.SemaphoreType.DMA,
                pltpu.SemaphoreType.DMA,
                pltpu.SemaphoreType.DMA,
            ],
        ),
        compiler_params=_cparams(("arbitrary",)),
        name="moe_dispatch",
    )(zends, tokens, dest_tiles, wg, wu, wd)


def _experts_kernel(blk_e_ref, nsub_ref, next_e_ref, nused_ref, xs_ref, wg_hbm, wu_hbm, wd_hbm, y_ref,
                    wgs, wus, wds, wgb, wub, wdb, sem_w, *, layer):
    i = pl.program_id(0)
    e = blk_e_ref[i]
    prev = blk_e_ref[jnp.maximum(i - 1, 0)]

    def weight_copies(expert):
        row = layer * N_EXPERTS + expert
        return (pltpu.make_async_copy(wg_hbm.at[row], wgs, sem_w.at[0]),
                pltpu.make_async_copy(wu_hbm.at[row], wus, sem_w.at[1]),
                pltpu.make_async_copy(wd_hbm.at[row], wds, sem_w.at[2]))

    @pl.when(i < nused_ref[0])
    def _():
        @pl.when(i == 0)
        def _():
            for cp in weight_copies(e):
                cp.start()

        @pl.when((i == 0) | (e != prev))
        def _():
            for cp in weight_copies(e):
                cp.wait()
            wgb[...] = wgs[...].astype(BF16)
            wub[...] = wus[...].astype(BF16)
            wdb[...] = wds[...].astype(BF16)

            @pl.when(next_e_ref[i] >= 0)
            def _():
                for cp in weight_copies(next_e_ref[i]):
                    cp.start()

        for sb in range(EXPERT_SUBS):
            tok0 = sb * EXPERT_ROWS

            @pl.when(sb < nsub_ref[i])
            def _():
                x = _load_packed(xs_ref, EXPERT_ROWS, BF16, tok0)
                g = jnp.dot(x, wgb[...], preferred_element_type=F32)
                u = jnp.dot(x, wub[...], preferred_element_type=F32)
                hb = (_silu(g) * u).astype(BF16)
                _store_packed(y_ref, jnp.dot(hb, wdb[...], preferred_element_type=F32), tok0)

            @pl.when(sb >= nsub_ref[i])
            def _():
                rows = EXPERT_ROWS * ROW_WORDS
                y_ref[tok0 * ROW_WORDS:tok0 * ROW_WORDS + rows, :] = jnp.zeros((rows, LANES), I32)


def experts(xs, blk_e, nsub, next_e, nused, w_gate, w_up, w_down, *, layer):
    w_gate = w_gate.reshape(DEPTH * N_EXPERTS, D_MODEL, EXPERT_FF)
    w_up = w_up.reshape(DEPTH * N_EXPERTS, D_MODEL, EXPERT_FF)
    w_down = w_down.reshape(DEPTH * N_EXPERTS, EXPERT_FF, D_MODEL)

    def row_map(i, blk_e, nsub, next_e, nused):
        return (jnp.minimum(i, nused[0] - 1), 0)

    return pl.pallas_call(
        functools.partial(_experts_kernel, layer=layer),
        out_shape=jax.ShapeDtypeStruct((N_BLOCKS * STEP_ROWS * ROW_WORDS, LANES), I32),
        grid_spec=pltpu.PrefetchScalarGridSpec(
            num_scalar_prefetch=4,
            grid=(N_BLOCKS,),
            in_specs=[
                pl.BlockSpec((STEP_ROWS * ROW_WORDS, LANES), row_map),
                pl.BlockSpec(memory_space=pl.ANY),
                pl.BlockSpec(memory_space=pl.ANY),
                pl.BlockSpec(memory_space=pl.ANY),
            ],
            out_specs=pl.BlockSpec((STEP_ROWS * ROW_WORDS, LANES), row_map),
            scratch_shapes=[
                pltpu.VMEM((D_MODEL, EXPERT_FF), F32),
                pltpu.VMEM((D_MODEL, EXPERT_FF), F32),
                pltpu.VMEM((EXPERT_FF, D_MODEL), F32),
                pltpu.VMEM((D_MODEL, EXPERT_FF), BF16),
                pltpu.VMEM((D_MODEL, EXPERT_FF), BF16),
                pltpu.VMEM((EXPERT_FF, D_MODEL), BF16),
                pltpu.SemaphoreType.DMA((3,)),
            ],
        ),
        compiler_params=_cparams(("arbitrary",)),
        name="moe_experts",
    )(blk_e, nsub, next_e, nused, xs, w_gate, w_up, w_down)


def _combine_kernel(x_ref, sh_ref, mod_ref, wk_ref, gf_ref, dest_hbm, ys_hbm, o_ref,
                    idx0, idx1, buf0, buf1, sem_idx, sem_g, *, final):
    i = pl.program_id(0)
    n = pl.num_programs(0)
    tm = x_ref.shape[0]
    idx = (idx0, idx1)
    buf = (buf0, buf1)

    def fetch_idx(tile, slot):
        idx_copy = pltpu.make_async_copy(dest_hbm.at[tile], idx[slot], sem_idx)
        idx_copy.start()
        idx_copy.wait()

    def start_rows(slot, r):
        for k in range(TOP_K):
            _row_copy(ys_hbm, idx[slot][k * tm + r], buf[slot].at[k], r, sem_g.at[slot]).start(priority=k % 2)

    def drain(slot):
        def wait_rows(r, carry):
            for k in range(TOP_K):
                _row_copy(ys_hbm, 0, buf[slot].at[k], r, sem_g.at[slot]).wait()
            return carry

        lax.fori_loop(0, tm, wait_rows, 0)

    def finish(slot):
        fetch_idx(jnp.minimum(i + 1, n - 1), 1 - slot)
        drain(slot)
        per = tm // ROW_WORDS
        los, his = [], []
        for s in range(ROW_WORDS):
            for r in range(s * per, (s + 1) * per):
                start_rows(1 - slot, r)
            acc_lo = sh_ref[:, s * LANES:(s + 1) * LANES]
            acc_hi = sh_ref[:, HALF + s * LANES:HALF + (s + 1) * LANES]
            for k in range(TOP_K):
                lo, hi = _unpack_words(buf[slot][k, pl.ds(s, tm, stride=ROW_WORDS), :])
                w = wk_ref[:, k:k + 1]
                acc_lo = acc_lo + lo * w
                acc_hi = acc_hi + hi * w
            los.append(acc_lo)
            his.append(acc_hi)
        f = jnp.concatenate(los + his, axis=-1)
        out = x_ref[...] + mod_ref[0, 5:6, :] * f
        if final:
            y = out * lax.rsqrt(jnp.mean(out * out, axis=-1, keepdims=True) + EPS)
            out = y * gf_ref[...]
        o_ref[...] = out

        @pl.when(i == n - 1)
        def _():
            drain(1 - slot)

    @pl.when(i == 0)
    def _():
        fetch_idx(0, 0)

        def first(r, carry):
            start_rows(0, r)
            return carry

        lax.fori_loop(0, tm, first, 0)

    for slot in range(2):
        @pl.when(i % 2 == slot)
        def _(slot=slot):
            finish(slot)


def combine(x, shared, mod, wk_t, g_final, dest_tiles, ys, *, ntiles, final):
    tm = 128
    tps = SEQ // tm
    return pl.pallas_call(
        functools.partial(_combine_kernel, final=final),
        out_shape=jax.ShapeDtypeStruct((ntiles * tm, D_MODEL), F32),
        grid=(ntiles,),
        in_specs=[
            pl.BlockSpec((tm, D_MODEL), lambda i: (i, 0)),
            pl.BlockSpec((tm, D_MODEL), lambda i: (i, 0)),
            pl.BlockSpec((1, 6, D_MODEL), lambda i: (_mod_row(i, tps), 0, 0)),
            pl.BlockSpec((tm, TOP_K), lambda i: (i, 0)),
            pl.BlockSpec((1, D_MODEL), lambda i: (0, 0)),
            pl.BlockSpec(memory_space=pl.ANY),
            pl.BlockSpec(memory_space=pl.ANY),
        ],
        out_specs=pl.BlockSpec((tm, D_MODEL), lambda i: (i, 0)),
        scratch_shapes=[
            pltpu.SMEM((TOP_K * tm,), I32),
            pltpu.SMEM((TOP_K * tm,), I32),
            pltpu.VMEM((TOP_K, tm * ROW_WORDS, LANES), I32),
            pltpu.VMEM((TOP_K, tm * ROW_WORDS, LANES), I32),
            pltpu.SemaphoreType.DMA,
            pltpu.SemaphoreType.DMA((2,)),
        ],
        compiler_params=_cparams(("arbitrary",)),
        name="moe_combine",
    )(x, shared, mod, wk_t, g_final.reshape(1, D_MODEL), dest_tiles, ys)


def _routing_plan(eidx, rank, tile_counts):
    t = eidx.shape[1]
    tiles = tile_counts.shape[1]
    counts = jnp.sum(tile_counts, axis=1)
    pcounts = (counts + STEP_ROWS - 1) // STEP_ROWS * STEP_ROWS
    pends = jnp.cumsum(pcounts)
    pstarts = pends - pcounts
    base = pstarts[:, None] + jnp.cumsum(tile_counts, axis=1) - tile_counts
    onehot = (eidx.reshape(TOP_K, tiles, 1, t // tiles)
              == jnp.arange(N_EXPERTS, dtype=I32)[None, None, :, None])
    dest = rank + jnp.sum(jnp.where(onehot, base.T[None, :, :, None], 0), axis=2).reshape(TOP_K, t)
    nused = (pends[-1] // STEP_ROWS).astype(I32).reshape(1)
    blocks = jnp.arange(N_BLOCKS, dtype=I32)
    blk_e = jnp.minimum(jnp.sum((pends[None, :] <= (blocks * STEP_ROWS)[:, None]).astype(I32), axis=1),
                        N_EXPERTS - 1).astype(I32)
    nsub_e = (counts + EXPERT_ROWS - 1) // EXPERT_ROWS
    zends = jnp.where(counts > 0, pstarts + nsub_e * EXPERT_ROWS, 0)
    done = (blocks - pstarts[blk_e] // STEP_ROWS) * EXPERT_SUBS
    nsub = jnp.clip(nsub_e[blk_e] - done, 0, EXPERT_SUBS)
    ids = jnp.where(counts > 0, jnp.arange(N_EXPERTS, dtype=I32), N_EXPERTS)
    later = jnp.concatenate([lax.cummin(ids[::-1])[::-1][1:], jnp.full((1,), N_EXPERTS, I32)])
    next_e = jnp.where(later < N_EXPERTS, later, -1)[blk_e]
    return dest.astype(I32), zends.astype(I32), blk_e, nsub.astype(I32), next_e.astype(I32), nused


def _tile_major(dest, tm):
    k, t = dest.shape
    return dest.reshape(k, t // tm, tm).transpose(1, 0, 2).reshape(t // tm, k * tm)


def _rope_tables(n_rows):
    row = jnp.repeat(jnp.arange(n_rows, dtype=F32), GRID_W)
    col = jnp.tile(jnp.arange(GRID_W, dtype=F32), n_rows)
    half = QK_ROPE // 2
    inv = ROPE_BASE ** (-jnp.arange(0, half, 2, dtype=F32) / half)
    ang_r = row[:, None] * inv
    ang_c = col[:, None] * inv
    ang = jnp.concatenate([ang_r, ang_r, ang_c, ang_c], axis=-1)
    n = ang.shape[0]
    ones = jnp.ones((n, QK_NOPE), F32)
    zeros = jnp.zeros((n, QK_NOPE), F32)
    tail = jnp.zeros((n, HEAD_PAD - QK_NOPE - QK_ROPE), F32)
    cos = jnp.concatenate([ones, jnp.cos(ang), tail], axis=-1)
    sin = jnp.concatenate([zeros, jnp.sin(ang), tail], axis=-1)
    return cos, sin


def _identity_tables(n):
    cos = jnp.concatenate([jnp.ones((n, QK_NOPE + QK_ROPE), F32),
                           jnp.zeros((n, HEAD_PAD - QK_NOPE - QK_ROPE), F32)], axis=-1)
    return cos, jnp.zeros((n, HEAD_PAD), F32)


def _rotate_cols(w):
    i = np.arange(QK_ROPE)
    first = (i % (QK_ROPE // 2)) < (QK_ROPE // 4)
    perm = np.where(first, i + QK_ROPE // 4, i - QK_ROPE // 4)
    sign = np.where(first, -1.0, 1.0).astype(np.float32)
    return w[:, perm] * sign


def _mla_weights(g_cq, w_uq, g_ckv, w_ukv):
    dk = QK_NOPE + QK_ROPE
    wq = w_uq.reshape(Q_LORA, MLA_HEADS, dk)
    pad = jnp.zeros((Q_LORA, MLA_HEADS, HEAD_PAD - dk), F32)
    zero_nope = jnp.zeros((Q_LORA, MLA_HEADS, QK_NOPE), F32)
    wq_rot = _rotate_cols(wq[..., QK_NOPE:].reshape(Q_LORA * MLA_HEADS, QK_ROPE)).reshape(
        Q_LORA, MLA_HEADS, QK_ROPE)
    wq_a = jnp.concatenate([wq, pad], axis=-1).reshape(Q_LORA, -1)
    wq_b = jnp.concatenate([zero_nope, wq_rot, pad], axis=-1).reshape(Q_LORA, -1)
    wkv = w_ukv.reshape(KV_LORA, MLA_HEADS, QK_NOPE + V_HEAD)
    wk = jnp.concatenate([wkv[..., :QK_NOPE],
                          jnp.zeros((KV_LORA, MLA_HEADS, HEAD_PAD - QK_NOPE), F32)], axis=-1)
    wv = wkv[..., QK_NOPE:]
    eye = jnp.eye(QK_ROPE, dtype=F32)
    place = jnp.zeros((LANES, MLA_HEADS, HEAD_PAD), F32)
    place_a = place.at[:QK_ROPE, :, QK_NOPE:dk].set(jnp.broadcast_to(eye[:, None, :], (QK_ROPE, MLA_HEADS, QK_ROPE)))
    rot = _rotate_cols(eye)
    place_b = place.at[:QK_ROPE, :, QK_NOPE:dk].set(jnp.broadcast_to(rot[:, None, :], (QK_ROPE, MLA_HEADS, QK_ROPE)))
    return {
        "g_cq": g_cq.reshape(1, -1), "g_ckv": g_ckv.reshape(1, -1),
        "wq2t": jnp.concatenate([wq_a, wq_b], axis=-1).T.astype(BF16),
        "wk": wk.reshape(KV_LORA, -1).astype(BF16),
        "wvt": wv.reshape(KV_LORA, -1).T.astype(BF16),
        "pk": jnp.concatenate([place_a.reshape(LANES, -1), place_b.reshape(LANES, -1)], axis=-1).astype(BF16),
    }


def _pack_w_in(w):
    gates0 = MLA_IN + 2 * CONV_CH + FFT_CH + POOL_CH
    pad = jnp.zeros((D_MODEL, P_MLA_W - MLA_IN), BF16)
    rest = jnp.concatenate([w[:, :MLA_IN].astype(BF16), pad, w[:, MLA_IN:gates0].astype(BF16)], axis=-1)
    return w[:, gates0:].astype(BF16), rest


def _dft_tables(n):
    gw = FFT_CH // FFT_GROUPS

    def angles(rows, m):
        k = jnp.arange(m, dtype=I32)
        return (rows[:, None] * k[None, :] % m).astype(F32) * (2.0 * np.pi / m)

    ac = angles(jnp.arange(gw, dtype=I32), gw)
    cs = jnp.concatenate([jnp.cos(ac), jnp.sin(ac)], axis=-1).astype(BF16)
    ns = int(np.sqrt(n))
    assert ns * ns == n
    a_hi = angles(jnp.arange(ns, dtype=I32) * ns, n)
    a_lo = angles(jnp.arange(ns, dtype=I32), n)
    ch, sh, cl, sl = jnp.cos(a_hi)[:, None, :], jnp.sin(a_hi)[:, None, :], jnp.cos(a_lo)[None], jnp.sin(a_lo)[None]
    cos_n = (ch * cl - sh * sl).reshape(n, n).astype(BF16)
    nsin_n = (-(sh * cl + ch * sl)).reshape(n, n).astype(BF16)
    return cs, cos_n, nsin_n


def _mixers(p, nseq, n, wts, tables, conv_args, pool_args, dft):
    q4, k4, v = mla_prep(p, wts, tables, row0=0, col0=P_MLA, nseq=nseq, n=n)
    cv = conv_branch(p, *conv_args, row0=0, nseq=nseq, n=n)
    ff = fft_branch(p, *dft, row0=0, nseq=nseq, n=n)
    po = pool_branch(p, *pool_args, row0=0, nseq=nseq, n=n)
    return q4, k4, v, cv, ff, po


def kernel(x, c, ctx, c_ctx, w_mod, b_mod, g_norm1, g_norm2, w_in, g_cq, w_uq, g_ckv, w_ukv, w_o_mla, conv_w, conv_b, conv_ln_g, conv_ln_b, w_o_conv, w_o_fft, pool_w, pool_scale, w_o_pool, w_out, w_router, router_bias, w_exp_gate, w_exp_up, w_exp_down, w_sh_gate, w_sh_up, w_sh_down, g_final):
    x_lat = x.reshape(T_LAT, D_MODEL)
    x_ctx = ctx.reshape(T_CTX, D_MODEL)
    ctx_tile0 = 0
    cvec = jnp.concatenate([c, c_ctx[None, :], jnp.zeros((SUBLANES - BATCH - 1, D_MODEL), F32)], axis=0)
    mod_all = modulation_all(cvec, w_mod, b_mod).reshape(DEPTH, SUBLANES, 6, D_MODEL)
    rope_lat = _rope_tables(SEQ // GRID_W)
    rope_ctx = _identity_tables(CTX_LEN)
    dft_lat = _dft_tables(SEQ)
    dft_ctx = _dft_tables(CTX_LEN)
    lat_tiles = T_LAT // ROW_TILE
    all_tiles = T_ALL // ROW_TILE

    for l in range(DEPTH):
        last = l == DEPTH - 1
        mod = mod_all[l]
        w_gate, w_rest = _pack_w_in(w_in[l])
        wts = _mla_weights(g_cq[l], w_uq[l], g_ckv[l], w_ukv[l])
        conv_args = (conv_w[l], conv_b[l], conv_ln_g[l], conv_ln_b[l])
        pool_args = (pool_w[l].astype(BF16), pool_scale[l])
        wo4 = jnp.stack([w_o_mla[l], w_o_conv[l], w_o_fft[l], w_o_pool[l]], axis=0).astype(BF16)
        w_out_b = w_out[l].astype(BF16)

        ctx_col0, ctx_cols = (P_MLA, P_MLA_W) if last else (0, P_COLS)
        p = normproj(x_lat, g_norm1[l], mod, w_gate, w_rest, tile0=0, ntiles=lat_tiles, col0=0, ncols=P_COLS)
        p_ctx = normproj(x_ctx, g_norm1[l], mod, w_gate, w_rest, tile0=ctx_tile0, ntiles=all_tiles - lat_tiles,
                         mod_tile0=lat_tiles, col0=ctx_col0, ncols=ctx_cols)
        q4, k4, v, cv, ff, po = _mixers(p, BATCH, SEQ, wts, rope_lat, conv_args, pool_args, dft_lat)
        if last:
            qc, kc, vc = mla_prep(p_ctx, wts, rope_ctx, row0=0, col0=0, nseq=BATCH, n=CTX_LEN)
        else:
            qc, kc, vc, cvc, ffc, poc = _mixers(p_ctx, BATCH, CTX_LEN, wts, rope_ctx, conv_args, pool_args,
                                                dft_ctx)
        k_all = jnp.concatenate([k4, kc], axis=2)
        v_all = jnp.concatenate([v, vc], axis=2)
        at = attention(q4, k_all, v_all, nq=SEQ, nk=SEQ + CTX_LEN)
        lat_t, ctx_t = T_LAT // 256, T_CTX // 256
        if last:
            ntok = T_LAT
            xa = merge(x_lat, mod, at, cv, ff, po, p, wo4, w_out_b, ntiles=lat_t)
        else:
            ntok = T_ALL
            atc = attention(qc, kc, vc, nq=CTX_LEN, nk=CTX_LEN)
            xa = merge(x_lat, mod, at, cv, ff, po, p, wo4, w_out_b, ntiles=lat_t, out_tiles=lat_t + ctx_t)
            xa = merge(x_ctx, mod, atc, cvc, ffc, poc, p_ctx, wo4, w_out_b, ntiles=ctx_t, out_tile0=lat_t,
                       out_tiles=lat_t + ctx_t, into=xa)

        wr_t = w_router[l].T
        wr_hi = wr_t.astype(BF16)
        wr_lo = (wr_t - wr_hi.astype(F32)).astype(BF16)
        tokens, eidx, wk, rank, cnt = norm_router(xa, g_norm2[l], mod, wr_hi, wr_lo, router_bias[l],
                                                  ntiles=ntok // ROUTER_TILE)
        dest, zends, blk_e, nsub, next_e, nused = _routing_plan(eidx, rank, cnt[:, ::LANES])
        xs, sh = dispatch_shared(tokens, _tile_major(dest, 256), zends, w_sh_gate[l].astype(BF16),
                                 w_sh_up[l].astype(BF16), w_sh_down[l].astype(BF16), ntiles=ntok // 256)
        ys = experts(xs, blk_e, nsub, next_e, nused, w_exp_gate, w_exp_up, w_exp_down, layer=l)
        xa = combine(xa, sh, mod, wk.T, g_final, _tile_major(dest, 128), ys, ntiles=ntok // 128, final=last)
        x_lat, x_ctx, ctx_tile0 = xa, xa, lat_tiles
    return xa.reshape(BATCH, SEQ, D_MODEL)
```

```python
import functools

import jax
import jax.numpy as jnp
import numpy as np
from jax import lax
from jax.experimental import pallas as pl
from jax.experimental.pallas import tpu as pltpu

F32 = jnp.float32
BF16 = jnp.bfloat16
I32 = jnp.int32

D_MODEL = 2048
BATCH = 4
SEQ = 4096
DEPTH = 2
GRID_W = 64
CTX_LEN = 256
MLA_HEADS = 8
QK_NOPE = 64
QK_ROPE = 32
V_HEAD = 64
Q_LORA = 512
KV_LORA = 256
ROPE_BASE = 10000.0
CONV_CH = 512
CONV_WIDTH = 31
FFT_CH = 512
FFT_GROUPS = 4
POOL_CH = 512
POOL_WINDOWS = (2, 4, 8, 16)
POOL_GROUP = POOL_CH // len(POOL_WINDOWS)
N_BRANCHES = 4
N_EXPERTS = 64
N_EXPERT_GROUPS = 8
TOPK_GROUPS = 4
TOP_K = 8
EXPERT_FF = 512
SHARED_FF = 512
ROUTED_SCALE = 2.5
EPS = 1e-6

MLA_IN = Q_LORA + KV_LORA + QK_ROPE
T_LAT = BATCH * SEQ
T_CTX = BATCH * CTX_LEN
T_ALL = T_LAT + T_CTX

LANES = 128
SUBLANES = 8
VMEM_LIMIT_BYTES = 56 * 1024 * 1024

P_GATE = 0
P_MLA = N_BRANCHES * D_MODEL
P_MLA_W = 1024
P_CONV = P_MLA + P_MLA_W
P_FFT = P_CONV + 2 * CONV_CH
P_POOL = P_FFT + FFT_CH
P_COLS = P_POOL + POOL_CH
HEAD_PAD = 128
ATT_SCALE = (QK_NOPE + QK_ROPE) ** -0.5
Q_SCALE = ATT_SCALE * float(np.log2(np.e))

ROW_TILE = 1024
ROUTER_TILE = 256
EXPERT_ROWS = 256
EXPERT_SUBS = 4
STEP_ROWS = EXPERT_ROWS * EXPERT_SUBS
N_BLOCKS = -(-(T_ALL * TOP_K) // STEP_ROWS) + N_EXPERTS


def _cparams(sem, vmem=VMEM_LIMIT_BYTES):
    return pltpu.CompilerParams(dimension_semantics=sem, vmem_limit_bytes=vmem)


def _sigmoid(x):
    return 1.0 / (1.0 + jnp.exp(-x))


def _silu(x):
    return x * _sigmoid(x)


def _mod_row(tile, tiles_per_seq):
    return jnp.minimum(tile // tiles_per_seq, BATCH)


def _modulation_kernel(c_ref, w_ref, b_ref, o_ref):
    c = c_ref[...]
    s = _silu(c).astype(BF16)
    o_ref[0] = jnp.dot(s, w_ref[0].astype(BF16), preferred_element_type=F32) + b_ref[0]


def modulation_all(cvec, w_mod, b_mod):
    tn = 1024
    n = 6 * D_MODEL
    return pl.pallas_call(
        _modulation_kernel,
        out_shape=jax.ShapeDtypeStruct((DEPTH, SUBLANES, n), F32),
        grid=(DEPTH, n // tn),
        in_specs=[
            pl.BlockSpec((SUBLANES, D_MODEL), lambda l, j: (0, 0)),
            pl.BlockSpec((1, D_MODEL, tn), lambda l, j: (l, 0, j)),
            pl.BlockSpec((1, 1, tn), lambda l, j: (l, 0, j)),
        ],
        out_specs=pl.BlockSpec((1, SUBLANES, tn), lambda l, j: (l, 0, j)),
        compiler_params=_cparams(("arbitrary", "arbitrary")),
        name="modulation",
    )(cvec, w_mod, b_mod.reshape(DEPTH, 1, n))


def _adaln(x, g, shift, scale):
    y = x * lax.rsqrt(jnp.mean(x * x, axis=-1, keepdims=True) + EPS)
    return (y * g) * (1.0 + scale) + shift


def _normproj_kernel(x_ref, g_ref, mod_ref, wg_ref, wr_ref, o_ref, h_ref, *, shift_row, chunk, cb0, ngate):
    j = pl.program_id(1)

    @pl.when(j == 0)
    def _():
        shift = mod_ref[0, shift_row:shift_row + 1, :]
        scale = mod_ref[0, shift_row + 1:shift_row + 2, :]
        for r in range(0, x_ref.shape[0], chunk):
            h = _adaln(x_ref[r:r + chunk, :], g_ref[...], shift, scale)
            h_ref[r:r + chunk, :] = h.astype(BF16)

    @pl.when(j + cb0 < ngate)
    def _():
        o_ref[...] = jnp.dot(h_ref[...], wg_ref[...], preferred_element_type=F32).astype(o_ref.dtype)

    @pl.when(j + cb0 >= ngate)
    def _():
        o_ref[...] = jnp.dot(h_ref[...], wr_ref[...], preferred_element_type=F32).astype(o_ref.dtype)


def normproj(x, g, mod, w_gate, w_rest, *, tile0, ntiles, col0, ncols, mod_tile0=None, shift_row=0):
    tm, tn = ROW_TILE, 1024
    tps = SEQ // tm
    cb0 = col0 // tn
    ngate = w_gate.shape[1] // tn
    mt0 = tile0 if mod_tile0 is None else mod_tile0
    return pl.pallas_call(
        functools.partial(_normproj_kernel, shift_row=shift_row, chunk=256, cb0=cb0, ngate=ngate),
        out_shape=jax.ShapeDtypeStruct((ntiles * tm, ncols), BF16),
        grid=(ntiles, ncols // tn),
        in_specs=[
            pl.BlockSpec((tm, D_MODEL), lambda i, j: (i + tile0, 0)),
            pl.BlockSpec((1, D_MODEL), lambda i, j: (0, 0)),
            pl.BlockSpec((1, 6, D_MODEL), lambda i, j: (_mod_row(i + mt0, tps), 0, 0)),
            pl.BlockSpec((D_MODEL, tn), lambda i, j: (0, jnp.minimum(j + cb0, ngate - 1))),
            pl.BlockSpec((D_MODEL, tn), lambda i, j: (0, jnp.maximum(j + cb0 - ngate, 0))),
        ],
        out_specs=pl.BlockSpec((tm, tn), lambda i, j: (i, j)),
        scratch_shapes=[pltpu.VMEM((tm, D_MODEL), BF16)],
        compiler_params=_cparams(("arbitrary", "arbitrary")),
        name="normproj",
    )(x, g.reshape(1, D_MODEL), mod, w_gate, w_rest)


def _rms_gain(x, g):
    y = x * lax.rsqrt(jnp.mean(x * x, axis=-1, keepdims=True) + EPS)
    return (y * g).astype(BF16)


_NT = (((1,), (1,)), ((), ()))


def _mla_prep_kernel(p_ref, gq_ref, gkv_ref, wqt_ref, wk_ref, wvt_ref, pk_ref, cos_ref, sin_ref,
                     cost_ref, sint_ref, qt_ref, k_ref, vt_ref):
    hw = MLA_HEADS * HEAD_PAD
    cqn = _rms_gain(p_ref[:, :Q_LORA].astype(F32), gq_ref[...])
    ckvn = _rms_gain(p_ref[:, Q_LORA:Q_LORA + KV_LORA].astype(F32), gkv_ref[...])
    kr = p_ref[:, Q_LORA + KV_LORA:Q_LORA + KV_LORA + LANES]
    q2t = lax.dot_general(wqt_ref[...], cqn, _NT, preferred_element_type=F32)
    vt_ref[0] = lax.dot_general(wvt_ref[...], ckvn, _NT, preferred_element_type=F32).astype(BF16)
    kk = jnp.dot(ckvn, wk_ref[...], preferred_element_type=F32)
    kr2 = jnp.dot(kr, pk_ref[...], preferred_element_type=F32)
    cos, sin = cos_ref[...], sin_ref[...]
    cost, sint = cost_ref[...], sint_ref[...]
    for h in range(MLA_HEADS):
        a = slice(h * HEAD_PAD, (h + 1) * HEAD_PAD)
        b = slice(hw + h * HEAD_PAD, hw + (h + 1) * HEAD_PAD)
        qt_ref[0, h] = ((q2t[a, :] * cost + q2t[b, :] * sint) * Q_SCALE).astype(BF16)
        k_ref[0, h] = (kk[:, a] + kr2[:, a] * cos + kr2[:, b] * sin).astype(BF16)


def mla_prep(p, wts, tables, *, row0, col0, nseq, n):
    tm = 256
    tps = n // tm
    hw = MLA_HEADS * HEAD_PAD
    vw = MLA_HEADS * V_HEAD
    blk0 = row0 // tm
    cblk = col0 // P_MLA_W
    cos_t, sin_t = tables
    const = lambda i: (0, 0)
    return pl.pallas_call(
        _mla_prep_kernel,
        out_shape=(jax.ShapeDtypeStruct((nseq, MLA_HEADS, HEAD_PAD, n), BF16),
                   jax.ShapeDtypeStruct((nseq, MLA_HEADS, n, HEAD_PAD), BF16),
                   jax.ShapeDtypeStruct((nseq, vw, n), BF16)),
        grid=(nseq * tps,),
        in_specs=[
            pl.BlockSpec((tm, P_MLA_W), lambda i: (i + blk0, cblk)),
            pl.BlockSpec((1, Q_LORA), const),
            pl.BlockSpec((1, KV_LORA), const),
            pl.BlockSpec((2 * hw, Q_LORA), const),
            pl.BlockSpec((KV_LORA, hw), const),
            pl.BlockSpec((vw, KV_LORA), const),
            pl.BlockSpec((LANES, 2 * hw), const),
            pl.BlockSpec((tm, HEAD_PAD), lambda i: (i % tps, 0)),
            pl.BlockSpec((tm, HEAD_PAD), lambda i: (i % tps, 0)),
            pl.BlockSpec((HEAD_PAD, tm), lambda i: (0, i % tps)),
            pl.BlockSpec((HEAD_PAD, tm), lambda i: (0, i % tps)),
        ],
        out_specs=(pl.BlockSpec((1, MLA_HEADS, HEAD_PAD, tm), lambda i: (i // tps, 0, 0, i % tps)),
                   pl.BlockSpec((1, MLA_HEADS, tm, HEAD_PAD), lambda i: (i // tps, 0, i % tps, 0)),
                   pl.BlockSpec((1, vw, tm), lambda i: (i // tps, 0, i % tps))),
        compiler_params=_cparams(("arbitrary",)),
        name="mla_prep",
    )(p, wts["g_cq"], wts["g_ckv"], wts["wq2t"], wts["wk"], wts["wvt"], wts["pk"],
      cos_t, sin_t, cos_t.T, sin_t.T)


ATT_HEADS = 8


def _attn_kernel(qt_ref, k_ref, vt_ref, o_ref, s_ref, p_ref, *, nk, ck):
    tq = qt_ref.shape[3]
    chunks = [slice(c, c + ck) for c in range(0, nk, ck)]
    nh = ATT_HEADS
    m = [None] * nh
    l = [jnp.zeros((1, tq), F32) for _ in range(nh)]
    ot = [jnp.zeros((V_HEAD, tq), F32) for _ in range(nh)]
    for step in range(nh + 2):
        ha, hb, hc = step, step - 1, step - 2
        for c in chunks:
            if 0 <= ha < nh:
                s = jnp.dot(k_ref[0, ha, c, :], qt_ref[0, ha], preferred_element_type=F32)
                s_ref[ha % 2, c, :] = s
                mc = jnp.max(s, axis=0, keepdims=True)
                m[ha] = mc if m[ha] is None else jnp.maximum(m[ha], mc)
            if 0 <= hb < nh:
                p = jnp.exp2(s_ref[hb % 2, c, :] - m[hb])
                l[hb] = l[hb] + jnp.sum(p, axis=0, keepdims=True)
                p_ref[hb % 2, c, :] = p.astype(BF16)
            if 0 <= hc < nh:
                ot[hc] = ot[hc] + jnp.dot(vt_ref[0, hc * V_HEAD:(hc + 1) * V_HEAD, c], p_ref[hc % 2, c, :],
                                          preferred_element_type=F32)
    outs = [ot[h] / l[h] for h in range(nh)]
    o_ref[...] = jnp.concatenate(outs, axis=0).T.astype(o_ref.dtype)


def attention(qt4, k4, vt3, *, nq, nk):
    tq = 256
    ck = 256
    nqt = nq // tq
    hs = ATT_HEADS
    return pl.pallas_call(
        functools.partial(_attn_kernel, nk=nk, ck=ck),
        out_shape=jax.ShapeDtypeStruct((BATCH * nq, MLA_HEADS * V_HEAD), BF16),
        grid=(BATCH, MLA_HEADS // hs, nqt),
        in_specs=[
            pl.BlockSpec((1, hs, HEAD_PAD, tq), lambda b, hg, i: (b, hg, 0, i)),
            pl.BlockSpec((1, hs, nk, HEAD_PAD), lambda b, hg, i: (b, hg, 0, 0)),
            pl.BlockSpec((1, hs * V_HEAD, nk), lambda b, hg, i: (b, hg, 0)),
        ],
        out_specs=pl.BlockSpec((tq, hs * V_HEAD), lambda b, hg, i: (b * nqt + i, hg)),
        scratch_shapes=[pltpu.VMEM((2, nk, tq), F32), pltpu.VMEM((2, nk, tq), BF16)],
        compiler_params=_cparams(("arbitrary", "arbitrary", "arbitrary")),
        name="attention",
    )(qt4, k4, vt3)


CONV_PAD = 16


def _conv_kernel(p_ref, w_ref, cb_ref, lg_ref, lb_ref, o_ref, u_ref, *, n, tr):
    zeros = jnp.zeros((CONV_PAD, CONV_CH), F32)
    u_ref[0:CONV_PAD, :] = zeros
    u_ref[CONV_PAD + n:2 * CONV_PAD + n, :] = zeros

    def glu(i, carry):
        r = pl.multiple_of(i * tr, tr)
        a = p_ref[pl.ds(r, tr), 0:CONV_CH].astype(F32)
        b = p_ref[pl.ds(r, tr), CONV_CH:2 * CONV_CH].astype(F32)
        u_ref[pl.ds(CONV_PAD + r, tr), :] = a * _sigmoid(b)
        return carry

    lax.fori_loop(0, n // tr, glu, 0)
    off = CONV_PAD - CONV_WIDTH // 2
    win_rows = tr + 2 * CONV_PAD

    def conv(i, carry):
        r = pl.multiple_of(i * tr, tr)
        strips = []
        for c0 in range(0, CONV_CH, LANES):
            win = u_ref[pl.ds(r, win_rows), c0:c0 + LANES]
            acc = jnp.zeros((tr, LANES), F32)
            for b in range(SUBLANES):
                wb = pltpu.roll(win, win_rows - (off + b), axis=0)
                for k in range(b, CONV_WIDTH, SUBLANES):
                    acc = acc + wb[k - b:k - b + tr, :] * w_ref[k:k + 1, c0:c0 + LANES]
            strips.append(acc)
        acc = jnp.concatenate(strips, axis=-1) + cb_ref[...]
        mu = jnp.mean(acc, axis=-1, keepdims=True)
        d = acc - mu
        var = jnp.mean(d * d, axis=-1, keepdims=True)
        y = (d * lax.rsqrt(var + EPS)) * lg_ref[...] + lb_ref[...]
        o_ref[pl.ds(r, tr), :] = _silu(y).astype(o_ref.dtype)
        return carry

    lax.fori_loop(0, n // tr, conv, 0)


def conv_branch(p, conv_w, conv_b, ln_g, ln_b, *, row0, nseq, n):
    tr = 128
    blk0 = row0 // n
    const = lambda s: (0, 0)
    return pl.pallas_call(
        functools.partial(_conv_kernel, n=n, tr=tr),
        out_shape=jax.ShapeDtypeStruct((nseq * n, CONV_CH), BF16),
        grid=(nseq,),
        in_specs=[
            pl.BlockSpec((n, 2 * CONV_CH), lambda s: (s + blk0, P_CONV // (2 * CONV_CH))),
            pl.BlockSpec((CONV_WIDTH, CONV_CH), const),
            pl.BlockSpec((1, CONV_CH), const),
            pl.BlockSpec((1, CONV_CH), const),
            pl.BlockSpec((1, CONV_CH), const),
        ],
        out_specs=pl.BlockSpec((n, CONV_CH), lambda s: (s, 0)),
        scratch_shapes=[pltpu.VMEM((n + 2 * CONV_PAD, CONV_CH), F32)],
        compiler_params=_cparams(("arbitrary",)),
        name="conv_branch",
    )(p, conv_w, conv_b.reshape(1, -1), ln_g.reshape(1, -1), ln_b.reshape(1, -1))


def _fft_kernel(z_ref, cs_ref, mc_ref, ms_ref, o_ref, ab_ref, *, n, tc):
    gw = FFT_CH // FFT_GROUPS

    @pl.when(pl.program_id(1) == 0)
    def _():
        def chan(i, carry):
            r = pl.multiple_of(i * tc, tc)
            for g in range(FFT_GROUPS):
                zg = z_ref[pl.ds(r, tc), g * gw:(g + 1) * gw]
                ab = jnp.dot(zg, cs_ref[...], preferred_element_type=F32)
                ab_ref[pl.ds(r, tc), g * gw:(g + 1) * gw] = ab[:, :gw].astype(BF16)
                ab_ref[pl.ds(n + r, tc), g * gw:(g + 1) * gw] = ab[:, gw:].astype(BF16)
            return carry

        lax.fori_loop(0, n // tc, chan, 0)

    norm = 1.0 / float(np.sqrt(n * gw))
    re = (jnp.dot(mc_ref[...], ab_ref[0:n, :], preferred_element_type=F32)
          + jnp.dot(ms_ref[...], ab_ref[n:2 * n, :], preferred_element_type=F32))
    o_ref[...] = (re * norm).astype(o_ref.dtype)


def fft_branch(p, cs, mcos, mnsin, *, row0, nseq, n):
    tm = min(n, 512)
    tc = min(n, 512)
    blk0 = row0 // n
    nt = n // tm
    return pl.pallas_call(
        functools.partial(_fft_kernel, n=n, tc=tc),
        out_shape=jax.ShapeDtypeStruct((nseq * n, FFT_CH), BF16),
        grid=(nseq, nt),
        in_specs=[
            pl.BlockSpec((n, FFT_CH), lambda s, i: (s + blk0, P_FFT // FFT_CH)),
            pl.BlockSpec((FFT_CH // FFT_GROUPS, 2 * FFT_CH // FFT_GROUPS), lambda s, i: (0, 0)),
            pl.BlockSpec((tm, n), lambda s, i: (i, 0)),
            pl.BlockSpec((tm, n), lambda s, i: (i, 0)),
        ],
        out_specs=pl.BlockSpec((tm, FFT_CH), lambda s, i: (s * nt + i, 0)),
        scratch_shapes=[pltpu.VMEM((2 * n, FFT_CH), BF16)],
        compiler_params=_cparams(("arbitrary", "arbitrary")),
        name="fft_branch",
    )(p, cs, mcos, mnsin)


POOL_PAD = 16


def _pool_kernel(z_ref, pw_ref, ps_ref, o_ref, zp_ref, d_ref, *, n, tr, tc):
    zeros = jnp.zeros((POOL_PAD, POOL_CH), F32)
    zp_ref[0:POOL_PAD, :] = zeros
    zp_ref[POOL_PAD + n:2 * POOL_PAD + n, :] = zeros

    def fill(i, carry):
        r = pl.multiple_of(i * tc, tc)
        zp_ref[pl.ds(POOL_PAD + r, tc), :] = z_ref[pl.ds(r, tc), :].astype(F32)
        return carry

    lax.fori_loop(0, n // tc, fill, 0)

    win_rows = tr + 2 * POOL_PAD

    def pool(i, carry):
        r = pl.multiple_of(i * tr, tr)
        t = (r + lax.broadcasted_iota(I32, (tr, POOL_GROUP), 0)).astype(F32)
        for gi, w in enumerate(POOL_WINDOWS):
            cols = slice(gi * POOL_GROUP, (gi + 1) * POOL_GROUP)
            win = zp_ref[pl.ds(r, win_rows), cols]
            s = jnp.zeros((tr, POOL_GROUP), F32)
            for j in range(w):
                start = POOL_PAD - w // 2 + j
                if start % SUBLANES == 0:
                    s = s + win[start:start + tr, :]
                else:
                    s = s + pltpu.roll(win, win_rows - start, axis=0)[0:tr, :]
            lo = jnp.maximum(t - (w // 2), 0.0)
            hi = jnp.minimum(t - (w // 2) + w, float(n))
            z = win[POOL_PAD:POOL_PAD + tr, :]
            d_ref[pl.ds(r, tr), cols] = (s / (hi - lo) - z).astype(BF16)
        return carry

    lax.fori_loop(0, n // tr, pool, 0)

    def proj(i, carry):
        r = pl.multiple_of(i * tc, tc)
        for gi in range(len(POOL_WINDOWS)):
            cols = slice(gi * POOL_GROUP, (gi + 1) * POOL_GROUP)
            y = jnp.dot(d_ref[pl.ds(r, tc), cols], pw_ref[gi], preferred_element_type=F32)
            o_ref[pl.ds(r, tc), cols] = (y * ps_ref[:, cols]).astype(o_ref.dtype)
        return carry

    lax.fori_loop(0, n // tc, proj, 0)


def pool_branch(p, pool_w, pool_scale, *, row0, nseq, n):
    tr = 128
    tc = min(n, 512)
    blk0 = row0 // n
    return pl.pallas_call(
        functools.partial(_pool_kernel, n=n, tr=tr, tc=tc),
        out_shape=jax.ShapeDtypeStruct((nseq * n, POOL_CH), BF16),
        grid=(nseq,),
        in_specs=[
            pl.BlockSpec((n, POOL_CH), lambda s: (s + blk0, P_POOL // POOL_CH)),
            pl.BlockSpec((len(POOL_WINDOWS), POOL_GROUP, POOL_GROUP), lambda s: (0, 0, 0)),
            pl.BlockSpec((1, POOL_CH), lambda s: (0, 0)),
        ],
        out_specs=pl.BlockSpec((n, POOL_CH), lambda s: (s, 0)),
        scratch_shapes=[pltpu.VMEM((n + 2 * POOL_PAD, POOL_CH), F32),
                        pltpu.VMEM((n, POOL_CH), BF16)],
        compiler_params=_cparams(("arbitrary",)),
        name="pool_branch",
    )(p, pool_w, pool_scale.reshape(1, -1))


def _merge_kernel(x_ref, mod_ref, at_ref, cv_ref, ff_ref, po_ref, g_ref, wo_ref, wout_ref, o_ref):
    mix = None
    for b, br in enumerate((at_ref, cv_ref, ff_ref, po_ref)):
        y = jnp.dot(br[...], wo_ref[b], preferred_element_type=F32)
        gate = _sigmoid(g_ref[:, b * D_MODEL:(b + 1) * D_MODEL].astype(F32))
        mix = gate * y if mix is None else mix + gate * y
    out = jnp.dot(mix.astype(BF16), wout_ref[...], preferred_element_type=F32)
    o_ref[...] = x_ref[...] + mod_ref[0, 2:3, :] * out


def _merge_into_kernel(x_ref, mod_ref, at_ref, cv_ref, ff_ref, po_ref, g_ref, wo_ref, wout_ref, prev_ref, o_ref):
    del prev_ref
    _merge_kernel(x_ref, mod_ref, at_ref, cv_ref, ff_ref, po_ref, g_ref, wo_ref, wout_ref, o_ref)


def merge(x, mod, attn, conv, fft, pool, p, wo4, w_out, *, ntiles, out_tile0=0, out_tiles=None, into=None):
    tm = 256
    tps = SEQ // tm
    out_tiles = ntiles if out_tiles is None else out_tiles
    br_spec = pl.BlockSpec((tm, 512), lambda i: (i, 0))
    in_specs = [
        pl.BlockSpec((tm, D_MODEL), lambda i: (i, 0)),
        pl.BlockSpec((1, 6, D_MODEL), lambda i: (_mod_row(i + out_tile0, tps), 0, 0)),
        br_spec, br_spec, br_spec, br_spec,
        pl.BlockSpec((tm, N_BRANCHES * D_MODEL), lambda i: (i, P_GATE // (N_BRANCHES * D_MODEL))),
        pl.BlockSpec((N_BRANCHES, 512, D_MODEL), lambda i: (0, 0, 0), pipeline_mode=pl.Buffered(1)),
        pl.BlockSpec((D_MODEL, D_MODEL), lambda i: (0, 0), pipeline_mode=pl.Buffered(1)),
    ]
    operands = [x, mod, attn, conv, fft, pool, p, wo4, w_out]
    kernel_fn, aliases = _merge_kernel, {}
    if into is not None:
        in_specs.append(pl.BlockSpec(memory_space=pl.ANY))
        operands.append(into)
        kernel_fn, aliases = _merge_into_kernel, {len(operands) - 1: 0}
    return pl.pallas_call(
        kernel_fn,
        out_shape=jax.ShapeDtypeStruct((out_tiles * tm, D_MODEL), F32),
        grid=(ntiles,),
        in_specs=in_specs,
        out_specs=pl.BlockSpec((tm, D_MODEL), lambda i: (i + out_tile0, 0)),
        input_output_aliases=aliases,
        compiler_params=_cparams(("arbitrary",)),
        name="merge",
    )(*operands)


def _first_index(hit_value, cand, ids, big):
    return jnp.min(jnp.where(cand == hit_value, ids, big), axis=0, keepdims=True)


HALF = D_MODEL // 2
ROW_WORDS = HALF // LANES


def _round_bf16_bits(x):
    u = lax.bitcast_convert_type(x, I32)
    odd = lax.shift_right_logical(u, 16) & 1
    return (u + 0x7FFF + odd) & jnp.int32(-65536)


def _pack_words(lo, hi):
    return lax.shift_right_logical(_round_bf16_bits(lo), 16) | _round_bf16_bits(hi)


def _unpack_words(w):
    lo = lax.bitcast_convert_type(lax.shift_left(w, 16), F32)
    hi = lax.bitcast_convert_type(w & jnp.int32(-65536), F32)
    return lo, hi


def _store_packed(ref, x, tok0=0):
    tm = x.shape[0]
    for s in range(ROW_WORDS):
        w = _pack_words(x[:, s * LANES:(s + 1) * LANES], x[:, HALF + s * LANES:HALF + (s + 1) * LANES])
        ref[pl.ds(tok0 * ROW_WORDS + s, tm, stride=ROW_WORDS), :] = w


def _load_packed(ref, tm, dtype, tok0=0):
    los, his = [], []
    for s in range(ROW_WORDS):
        lo, hi = _unpack_words(ref[pl.ds(tok0 * ROW_WORDS + s, tm, stride=ROW_WORDS), :])
        los.append(lo.astype(dtype))
        his.append(hi.astype(dtype))
    return jnp.concatenate(los + his, axis=-1)


def _router_kernel(x_ref, g_ref, mod_ref, wrh_ref, wrl_ref, rb_ref, tri_ref,
                   tok_ref, eidx_ref, wk_ref, rank_ref, cnt_ref):
    tm = x_ref.shape[0]
    h = _adaln(x_ref[...], g_ref[...], mod_ref[0, 3:4, :], mod_ref[0, 4:5, :])
    _store_packed(tok_ref, h)
    hh = h.astype(BF16)
    hl = (h - hh.astype(F32)).astype(BF16)
    nt = (((1,), (1,)), ((), ()))
    logits = (lax.dot_general(wrh_ref[...], hh, nt, preferred_element_type=F32)
              + lax.dot_general(wrh_ref[...], hl, nt, preferred_element_type=F32)
              + lax.dot_general(wrl_ref[...], hh, nt, preferred_element_type=F32))
    scores = _sigmoid(logits)
    sel = scores + rb_ref[...]
    per = N_EXPERTS // N_EXPERT_GROUPS
    assert per == SUBLANES and N_EXPERT_GROUPS == SUBLANES and TOP_K == SUBLANES
    neg = -jnp.inf
    sub = lax.broadcasted_iota(I32, (SUBLANES, tm), 0).astype(F32)
    sg = [sel[g * per:(g + 1) * per, :] for g in range(N_EXPERT_GROUPS)]
    sc = [scores[g * per:(g + 1) * per, :] for g in range(N_EXPERT_GROUPS)]
    gsc = jnp.zeros((SUBLANES, tm), F32)
    for g in range(N_EXPERT_GROUPS):
        m1 = jnp.max(sg[g], axis=0, keepdims=True)
        i1 = _first_index(m1, sg[g], sub, float(per))
        m2 = jnp.max(jnp.where(sub == i1, neg, sg[g]), axis=0, keepdims=True)
        gsc = jnp.where(sub == float(g), m1 + m2, gsc)
    gsel = jnp.zeros((SUBLANES, tm), F32)
    for _ in range(TOPK_GROUPS):
        m = jnp.max(gsc, axis=0, keepdims=True)
        hit = sub == _first_index(m, gsc, sub, float(N_EXPERT_GROUPS))
        gsel = jnp.where(hit, 1.0, gsel)
        gsc = jnp.where(hit, neg, gsc)
    cand = []
    for g in range(N_EXPERT_GROUPS):
        allowed = jnp.max(jnp.where(sub == float(g), gsel, 0.0), axis=0, keepdims=True)
        cand.append(jnp.where(allowed > 0.0, sg[g], neg))
    eid = [sub + float(g * per) for g in range(N_EXPERT_GROUPS)]
    idxs = jnp.zeros((SUBLANES, tm), F32)
    vals = jnp.zeros((SUBLANES, tm), F32)
    picked = [jnp.zeros((per, tm), F32) for _ in range(N_EXPERT_GROUPS)]
    idx_k = []
    for k in range(TOP_K):
        m = functools.reduce(jnp.maximum, [jnp.max(c, axis=0, keepdims=True) for c in cand])
        idx = functools.reduce(
            jnp.minimum, [_first_index(m, cand[g], eid[g], float(N_EXPERTS)) for g in range(N_EXPERT_GROUPS)])
        val = jnp.zeros((1, tm), F32)
        for g in range(N_EXPERT_GROUPS):
            hit = eid[g] == idx
            val = val + jnp.sum(jnp.where(hit, sc[g], 0.0), axis=0, keepdims=True)
            cand[g] = jnp.where(hit, neg, cand[g])
            picked[g] = jnp.where(hit, 1.0, picked[g])
        idxs = jnp.where(sub == float(k), idx, idxs)
        vals = jnp.where(sub == float(k), val, vals)
        idx_k.append(idx)
    eidx_ref[...] = idxs.astype(I32)
    wk_ref[...] = vals / jnp.sum(vals, axis=0, keepdims=True) * ROUTED_SCALE
    mask = jnp.concatenate(picked, axis=0)
    before = jnp.dot(mask.astype(BF16), tri_ref[...], preferred_element_type=F32)
    ranks = jnp.zeros((SUBLANES, tm), F32)
    for k in range(TOP_K):
        r = jnp.zeros((1, tm), F32)
        for g in range(N_EXPERT_GROUPS):
            r = r + jnp.sum(jnp.where(eid[g] == idx_k[k], before[g * per:(g + 1) * per, :], 0.0),
                            axis=0, keepdims=True)
        ranks = jnp.where(sub == float(k), r, ranks)
    rank_ref[...] = ranks.astype(I32)
    cnt_ref[...] = jnp.broadcast_to(jnp.sum(mask, axis=1, keepdims=True), (N_EXPERTS, LANES)).astype(I32)


def norm_router(x, g, mod, wr_hi, wr_lo, rbias, *, ntiles):
    tm = ROUTER_TILE
    tps = SEQ // tm
    t = ntiles * tm
    tri = jnp.triu(jnp.ones((tm, tm), BF16), k=1)
    kt_spec = pl.BlockSpec((TOP_K, tm), lambda i: (0, i))
    return pl.pallas_call(
        _router_kernel,
        out_shape=(jax.ShapeDtypeStruct((t * ROW_WORDS, LANES), I32),
                   jax.ShapeDtypeStruct((TOP_K, t), I32),
                   jax.ShapeDtypeStruct((TOP_K, t), F32),
                   jax.ShapeDtypeStruct((TOP_K, t), I32),
                   jax.ShapeDtypeStruct((N_EXPERTS, ntiles * LANES), I32)),
        grid=(ntiles,),
        in_specs=[
            pl.BlockSpec((tm, D_MODEL), lambda i: (i, 0)),
            pl.BlockSpec((1, D_MODEL), lambda i: (0, 0)),
            pl.BlockSpec((1, 6, D_MODEL), lambda i: (_mod_row(i, tps), 0, 0)),
            pl.BlockSpec((N_EXPERTS, D_MODEL), lambda i: (0, 0)),
            pl.BlockSpec((N_EXPERTS, D_MODEL), lambda i: (0, 0)),
            pl.BlockSpec((N_EXPERTS, 1), lambda i: (0, 0)),
            pl.BlockSpec((tm, tm), lambda i: (0, 0)),
        ],
        out_specs=(pl.BlockSpec((tm * ROW_WORDS, LANES), lambda i: (i, 0)),
                   kt_spec, kt_spec, kt_spec,
                   pl.BlockSpec((N_EXPERTS, LANES), lambda i: (0, i))),
        compiler_params=_cparams(("arbitrary",)),
        name="norm_router",
    )(x, g.reshape(1, D_MODEL), mod, wr_hi, wr_lo, rbias.reshape(N_EXPERTS, 1), tri)


def _row_copy(src, src_tok, dst, dst_tok, sem):
    s = pl.multiple_of(src_tok * ROW_WORDS, ROW_WORDS)
    d = pl.multiple_of(dst_tok * ROW_WORDS, ROW_WORDS)
    return pltpu.make_async_copy(src.at[pl.ds(s, ROW_WORDS)], dst.at[pl.ds(d, ROW_WORDS)], sem)


def _dispatch_kernel(zends_ref, tok_ref, dest_hbm, wg_ref, wu_ref, wd_ref, xs_hbm, sh_ref,
                     dest_smem, zero_ref, sem_idx, sem_z, sem_s):
    i = pl.program_id(0)
    tm = tok_ref.shape[0] // ROW_WORDS
    idx_copy = pltpu.make_async_copy(dest_hbm.at[i], dest_smem, sem_idx)
    idx_copy.start()
    blk = EXPERT_ROWS * ROW_WORDS

    def pad_copy(e):
        row = pl.multiple_of((zends_ref[e] - EXPERT_ROWS) * ROW_WORDS, blk)
        return pltpu.make_async_copy(zero_ref, xs_hbm.at[pl.ds(row, blk)], sem_z)

    @pl.when(i == 0)
    def _():
        zero_ref[...] = jnp.zeros_like(zero_ref)

        def start(e, carry):
            @pl.when(zends_ref[e] > 0)
            def _():
                pad_copy(e).start()
            return carry

        lax.fori_loop(0, N_EXPERTS, start, 0)

        def wait(e, carry):
            @pl.when(zends_ref[e] > 0)
            def _():
                pad_copy(e).wait()
            return carry

        lax.fori_loop(0, N_EXPERTS, wait, 0)

    idx_copy.wait()

    x = _load_packed(tok_ref, tm, BF16)
    pieces = SHARED_FF // LANES
    per = tm // pieces
    acc = None
    for p in range(pieces):
        for r in range(p * per, (p + 1) * per):
            for k in range(TOP_K):
                _row_copy(tok_ref, r, xs_hbm, dest_smem[k * tm + r], sem_s).start(priority=k % 2)
        c = slice(p * LANES, (p + 1) * LANES)
        g = jnp.dot(x, wg_ref[:, c], preferred_element_type=F32)
        u = jnp.dot(x, wu_ref[:, c], preferred_element_type=F32)
        part = jnp.dot((_silu(g) * u).astype(BF16), wd_ref[c, :], preferred_element_type=F32)
        acc = part if acc is None else acc + part
    sh_ref[...] = acc

    def drain(r, carry):
        for k in range(TOP_K):
            _row_copy(tok_ref, r, xs_hbm, dest_smem[k * tm + r], sem_s).wait()
        return carry

    lax.fori_loop(0, tm, drain, 0)


def dispatch_shared(tokens, dest_tiles, zends, wg, wu, wd, *, ntiles):
    tm = tokens.shape[0] // ROW_WORDS // ntiles
    const = lambda i, zends: (0, 0)
    return pl.pallas_call(
        _dispatch_kernel,
        out_shape=(jax.ShapeDtypeStruct((N_BLOCKS * STEP_ROWS * ROW_WORDS, LANES), I32),
                   jax.ShapeDtypeStruct((ntiles * tm, D_MODEL), F32)),
        grid_spec=pltpu.PrefetchScalarGridSpec(
            num_scalar_prefetch=1,
            grid=(ntiles,),
            in_specs=[
                pl.BlockSpec((tm * ROW_WORDS, LANES), lambda i, zends: (i, 0)),
                pl.BlockSpec(memory_space=pl.ANY),
                pl.BlockSpec((D_MODEL, SHARED_FF), const),
                pl.BlockSpec((D_MODEL, SHARED_FF), const),
                pl.BlockSpec((SHARED_FF, D_MODEL), const),
            ],
            out_specs=(pl.BlockSpec(memory_space=pl.ANY),
                       pl.BlockSpec((tm, D_MODEL), lambda i, zends: (i, 0))),
            scratch_shapes=[
                pltpu.SMEM((TOP_K * tm,), I32),
                pltpu.VMEM((EXPERT_ROWS * ROW_WORDS, LANES), I32),
                pltpu.SemaphoreType.DMA,
                pltpu.SemaphoreType.DMA,
                pltpu.SemaphoreType.DMA,
            ],
        ),
        compiler_params=_cparams(("arbitrary",)),
        name="moe_dispatch",
    )(zends, tokens, dest_tiles, wg, wu, wd)


def _experts_kernel(blk_e_ref, nsub_ref, next_e_ref, nused_ref, xs_ref, wg_hbm, wu_hbm, wd_hbm, y_ref,
                    wgs, wus, wds, wgb, wub, wdb, sem_w, *, layer):
    i = pl.program_id(0)
    e = blk_e_ref[i]
    prev = blk_e_ref[jnp.maximum(i - 1, 0)]

    def weight_copies(expert):
        row = layer * N_EXPERTS + expert
        return (pltpu.make_async_copy(wg_hbm.at[row], wgs, sem_w.at[0]),
                pltpu.make_async_copy(wu_hbm.at[row], wus, sem_w.at[1]),
                pltpu.make_async_copy(wd_hbm.at[row], wds, sem_w.at[2]))

    @pl.when(i < nused_ref[0])
    def _():
        @pl.when(i == 0)
        def _():
            for cp in weight_copies(e):
                cp.start()

        @pl.when((i == 0) | (e != prev))
        def _():
            for cp in weight_copies(e):
                cp.wait()
            wgb[...] = wgs[...].astype(BF16)
            wub[...] = wus[...].astype(BF16)
            wdb[...] = wds[...].astype(BF16)

            @pl.when(next_e_ref[i] >= 0)
            def _():
                for cp in weight_copies(next_e_ref[i]):
                    cp.start()

        def mlp(sb):
            tok0 = sb * EXPERT_ROWS
            x = _load_packed(xs_ref, EXPERT_ROWS, BF16, tok0)
            g = jnp.dot(x, wgb[...], preferred_element_type=F32)
            u = jnp.dot(x, wub[...], preferred_element_type=F32)
            hb = (_silu(g) * u).astype(BF16)
            _store_packed(y_ref, jnp.dot(hb, wdb[...], preferred_element_type=F32), tok0)

        def clear(sb):
            rows = EXPERT_ROWS * ROW_WORDS
            y_ref[sb * rows:(sb + 1) * rows, :] = jnp.zeros((rows, LANES), I32)

        nsub = nsub_ref[i]
        for a in range(0, EXPERT_SUBS, 2):
            @pl.when(nsub >= a + 2)
            def _():
                mlp(a)
                mlp(a + 1)

            @pl.when(nsub == a + 1)
            def _():
                mlp(a)
                clear(a + 1)

            @pl.when(nsub <= a)
            def _():
                clear(a)
                clear(a + 1)


def experts(xs, blk_e, nsub, next_e, nused, w_gate, w_up, w_down, *, layer):
    w_gate = w_gate.reshape(DEPTH * N_EXPERTS, D_MODEL, EXPERT_FF)
    w_up = w_up.reshape(DEPTH * N_EXPERTS, D_MODEL, EXPERT_FF)
    w_down = w_down.reshape(DEPTH * N_EXPERTS, EXPERT_FF, D_MODEL)

    def row_map(i, blk_e, nsub, next_e, nused):
        return (jnp.minimum(i, nused[0] - 1), 0)

    return pl.pallas_call(
        functools.partial(_experts_kernel, layer=layer),
        out_shape=jax.ShapeDtypeStruct((N_BLOCKS * STEP_ROWS * ROW_WORDS, LANES), I32),
        grid_spec=pltpu.PrefetchScalarGridSpec(
            num_scalar_prefetch=4,
            grid=(N_BLOCKS,),
            in_specs=[
                pl.BlockSpec((STEP_ROWS * ROW_WORDS, LANES), row_map),
                pl.BlockSpec(memory_space=pl.ANY),
                pl.BlockSpec(memory_space=pl.ANY),
                pl.BlockSpec(memory_space=pl.ANY),
            ],
            out_specs=pl.BlockSpec((STEP_ROWS * ROW_WORDS, LANES), row_map),
            scratch_shapes=[
                pltpu.VMEM((D_MODEL, EXPERT_FF), F32),
                pltpu.VMEM((D_MODEL, EXPERT_FF), F32),
                pltpu.VMEM((EXPERT_FF, D_MODEL), F32),
                pltpu.VMEM((D_MODEL, EXPERT_FF), BF16),
                pltpu.VMEM((D_MODEL, EXPERT_FF), BF16),
                pltpu.VMEM((EXPERT_FF, D_MODEL), BF16),
                pltpu.SemaphoreType.DMA((3,)),
            ],
        ),
        compiler_params=_cparams(("arbitrary",)),
        name="moe_experts",
    )(blk_e, nsub, next_e, nused, xs, w_gate, w_up, w_down)


def _combine_kernel(x_ref, sh_ref, mod_ref, wk_ref, gf_ref, dest_hbm, ys_hbm, o_ref,
                    idx0, idx1, buf0, buf1, sem_idx, sem_g, *, final):
    i = pl.program_id(0)
    n = pl.num_programs(0)
    tm = x_ref.shape[0]
    idx = (idx0, idx1)
    buf = (buf0, buf1)

    def fetch_idx(tile, slot):
        idx_copy = pltpu.make_async_copy(dest_hbm.at[tile], idx[slot], sem_idx)
        idx_copy.start()
        idx_copy.wait()

    def start_rows(slot, r):
        for k in range(TOP_K):
            _row_copy(ys_hbm, idx[slot][k * tm + r], buf[slot].at[k], r, sem_g.at[slot]).start(priority=k % 2)

    def drain(slot):
        def wait_rows(r, carry):
            for k in range(TOP_K):
                _row_copy(ys_hbm, 0, buf[slot].at[k], r, sem_g.at[slot]).wait()
            return carry

        lax.fori_loop(0, tm, wait_rows, 0)

    def finish(slot):
        fetch_idx(jnp.minimum(i + 1, n - 1), 1 - slot)
        drain(slot)
        per = tm // ROW_WORDS
        los, his = [], []
        for s in range(ROW_WORDS):
            for r in range(s * per, (s + 1) * per):
                start_rows(1 - slot, r)
            acc_lo = sh_ref[:, s * LANES:(s + 1) * LANES]
            acc_hi = sh_ref[:, HALF + s * LANES:HALF + (s + 1) * LANES]
            for k in range(TOP_K):
                lo, hi = _unpack_words(buf[slot][k, pl.ds(s, tm, stride=ROW_WORDS), :])
                w = wk_ref[:, k:k + 1]
                acc_lo = acc_lo + lo * w
                acc_hi = acc_hi + hi * w
            los.append(acc_lo)
            his.append(acc_hi)
        f = jnp.concatenate(los + his, axis=-1)
        out = x_ref[...] + mod_ref[0, 5:6, :] * f
        if final:
            y = out * lax.rsqrt(jnp.mean(out * out, axis=-1, keepdims=True) + EPS)
            out = y * gf_ref[...]
        o_ref[...] = out

        @pl.when(i == n - 1)
        def _():
            drain(1 - slot)

    @pl.when(i == 0)
    def _():
        fetch_idx(0, 0)

        def first(r, carry):
            start_rows(0, r)
            return carry

        lax.fori_loop(0, tm, first, 0)

    for slot in range(2):
        @pl.when(i % 2 == slot)
        def _(slot=slot):
            finish(slot)


def combine(x, shared, mod, wk_t, g_final, dest_tiles, ys, *, ntiles, final):
    tm = 128
    tps = SEQ // tm
    return pl.pallas_call(
        functools.partial(_combine_kernel, final=final),
        out_shape=jax.ShapeDtypeStruct((ntiles * tm, D_MODEL), F32),
        grid=(ntiles,),
        in_specs=[
            pl.BlockSpec((tm, D_MODEL), lambda i: (i, 0)),
            pl.BlockSpec((tm, D_MODEL), lambda i: (i, 0)),
            pl.BlockSpec((1, 6, D_MODEL), lambda i: (_mod_row(i, tps), 0, 0)),
            pl.BlockSpec((tm, TOP_K), lambda i: (i, 0)),
            pl.BlockSpec((1, D_MODEL), lambda i: (0, 0)),
            pl.BlockSpec(memory_space=pl.ANY),
            pl.BlockSpec(memory_space=pl.ANY),
        ],
        out_specs=pl.BlockSpec((tm, D_MODEL), lambda i: (i, 0)),
        scratch_shapes=[
            pltpu.SMEM((TOP_K * tm,), I32),
            pltpu.SMEM((TOP_K * tm,), I32),
            pltpu.VMEM((TOP_K, tm * ROW_WORDS, LANES), I32),
            pltpu.VMEM((TOP_K, tm * ROW_WORDS, LANES), I32),
            pltpu.SemaphoreType.DMA,
            pltpu.SemaphoreType.DMA((2,)),
        ],
        compiler_params=_cparams(("arbitrary",)),
        name="moe_combine",
    )(x, shared, mod, wk_t, g_final.reshape(1, D_MODEL), dest_tiles, ys)


def _routing_plan(eidx, rank, tile_counts):
    t = eidx.shape[1]
    tiles = tile_counts.shape[1]
    counts = jnp.sum(tile_counts, axis=1)
    pcounts = (counts + STEP_ROWS - 1) // STEP_ROWS * STEP_ROWS
    pends = jnp.cumsum(pcounts)
    pstarts = pends - pcounts
    base = pstarts[:, None] + jnp.cumsum(tile_counts, axis=1) - tile_counts
    onehot = (eidx.reshape(TOP_K, tiles, 1, t // tiles)
              == jnp.arange(N_EXPERTS, dtype=I32)[None, None, :, None])
    dest = rank + jnp.sum(jnp.where(onehot, base.T[None, :, :, None], 0), axis=2).reshape(TOP_K, t)
    nused = (pends[-1] // STEP_ROWS).astype(I32).reshape(1)
    blocks = jnp.arange(N_BLOCKS, dtype=I32)
    blk_e = jnp.minimum(jnp.sum((pends[None, :] <= (blocks * STEP_ROWS)[:, None]).astype(I32), axis=1),
                        N_EXPERTS - 1).astype(I32)
    nsub_e = (counts + EXPERT_ROWS - 1) // EXPERT_ROWS
    zends = jnp.where(counts > 0, pstarts + nsub_e * EXPERT_ROWS, 0)
    done = (blocks - pstarts[blk_e] // STEP_ROWS) * EXPERT_SUBS
    nsub = jnp.clip(nsub_e[blk_e] - done, 0, EXPERT_SUBS)
    ids = jnp.where(counts > 0, jnp.arange(N_EXPERTS, dtype=I32), N_EXPERTS)
    later = jnp.concatenate([lax.cummin(ids[::-1])[::-1][1:], jnp.full((1,), N_EXPERTS, I32)])
    next_e = jnp.where(later < N_EXPERTS, later, -1)[blk_e]
    return dest.astype(I32), zends.astype(I32), blk_e, nsub.astype(I32), next_e.astype(I32), nused


def _tile_major(dest, tm):
    k, t = dest.shape
    return dest.reshape(k, t // tm, tm).transpose(1, 0, 2).reshape(t // tm, k * tm)


def _rope_tables(n_rows):
    row = jnp.repeat(jnp.arange(n_rows, dtype=F32), GRID_W)
    col = jnp.tile(jnp.arange(GRID_W, dtype=F32), n_rows)
    half = QK_ROPE // 2
    inv = ROPE_BASE ** (-jnp.arange(0, half, 2, dtype=F32) / half)
    ang_r = row[:, None] * inv
    ang_c = col[:, None] * inv
    ang = jnp.concatenate([ang_r, ang_r, ang_c, ang_c], axis=-1)
    n = ang.shape[0]
    ones = jnp.ones((n, QK_NOPE), F32)
    zeros = jnp.zeros((n, QK_NOPE), F32)
    tail = jnp.zeros((n, HEAD_PAD - QK_NOPE - QK_ROPE), F32)
    cos = jnp.concatenate([ones, jnp.cos(ang), tail], axis=-1)
    sin = jnp.concatenate([zeros, jnp.sin(ang), tail], axis=-1)
    return cos, sin


def _identity_tables(n):
    cos = jnp.concatenate([jnp.ones((n, QK_NOPE + QK_ROPE), F32),
                           jnp.zeros((n, HEAD_PAD - QK_NOPE - QK_ROPE), F32)], axis=-1)
    return cos, jnp.zeros((n, HEAD_PAD), F32)


def _rotate_cols(w):
    i = np.arange(QK_ROPE)
    first = (i % (QK_ROPE // 2)) < (QK_ROPE // 4)
    perm = np.where(first, i + QK_ROPE // 4, i - QK_ROPE // 4)
    sign = np.where(first, -1.0, 1.0).astype(np.float32)
    return w[:, perm] * sign


def _mla_weights(g_cq, w_uq, g_ckv, w_ukv):
    dk = QK_NOPE + QK_ROPE
    wq = w_uq.reshape(Q_LORA, MLA_HEADS, dk)
    pad = jnp.zeros((Q_LORA, MLA_HEADS, HEAD_PAD - dk), F32)
    zero_nope = jnp.zeros((Q_LORA, MLA_HEADS, QK_NOPE), F32)
    wq_rot = _rotate_cols(wq[..., QK_NOPE:].reshape(Q_LORA * MLA_HEADS, QK_ROPE)).reshape(
        Q_LORA, MLA_HEADS, QK_ROPE)
    wq_a = jnp.concatenate([wq, pad], axis=-1).reshape(Q_LORA, -1)
    wq_b = jnp.concatenate([zero_nope, wq_rot, pad], axis=-1).reshape(Q_LORA, -1)
    wkv = w_ukv.reshape(KV_LORA, MLA_HEADS, QK_NOPE + V_HEAD)
    wk = jnp.concatenate([wkv[..., :QK_NOPE],
                          jnp.zeros((KV_LORA, MLA_HEADS, HEAD_PAD - QK_NOPE), F32)], axis=-1)
    wv = wkv[..., QK_NOPE:]
    eye = jnp.eye(QK_ROPE, dtype=F32)
    place = jnp.zeros((LANES, MLA_HEADS, HEAD_PAD), F32)
    place_a = place.at[:QK_ROPE, :, QK_NOPE:dk].set(jnp.broadcast_to(eye[:, None, :], (QK_ROPE, MLA_HEADS, QK_ROPE)))
    rot = _rotate_cols(eye)
    place_b = place.at[:QK_ROPE, :, QK_NOPE:dk].set(jnp.broadcast_to(rot[:, None, :], (QK_ROPE, MLA_HEADS, QK_ROPE)))
    return {
        "g_cq": g_cq.reshape(1, -1), "g_ckv": g_ckv.reshape(1, -1),
        "wq2t": jnp.concatenate([wq_a, wq_b], axis=-1).T.astype(BF16),
        "wk": wk.reshape(KV_LORA, -1).astype(BF16),
        "wvt": wv.reshape(KV_LORA, -1).T.astype(BF16),
        "pk": jnp.concatenate([place_a.reshape(LANES, -1), place_b.reshape(LANES, -1)], axis=-1).astype(BF16),
    }


def _pack_w_in(w):
    gates0 = MLA_IN + 2 * CONV_CH + FFT_CH + POOL_CH
    pad = jnp.zeros((D_MODEL, P_MLA_W - MLA_IN), BF16)
    rest = jnp.concatenate([w[:, :MLA_IN].astype(BF16), pad, w[:, MLA_IN:gates0].astype(BF16)], axis=-1)
    return w[:, gates0:].astype(BF16), rest


def _dft_tables(n):
    gw = FFT_CH // FFT_GROUPS

    def angles(rows, m):
        k = jnp.arange(m, dtype=I32)
        return (rows[:, None] * k[None, :] % m).astype(F32) * (2.0 * np.pi / m)

    ac = angles(jnp.arange(gw, dtype=I32), gw)
    cs = jnp.concatenate([jnp.cos(ac), jnp.sin(ac)], axis=-1).astype(BF16)
    ns = int(np.sqrt(n))
    assert ns * ns == n
    a_hi = angles(jnp.arange(ns, dtype=I32) * ns, n)
    a_lo = angles(jnp.arange(ns, dtype=I32), n)
    ch, sh, cl, sl = jnp.cos(a_hi)[:, None, :], jnp.sin(a_hi)[:, None, :], jnp.cos(a_lo)[None], jnp.sin(a_lo)[None]
    cos_n = (ch * cl - sh * sl).reshape(n, n).astype(BF16)
    nsin_n = (-(sh * cl + ch * sl)).reshape(n, n).astype(BF16)
    return cs, cos_n, nsin_n


def _mixers(p, nseq, n, wts, tables, conv_args, pool_args, dft):
    q4, k4, v = mla_prep(p, wts, tables, row0=0, col0=P_MLA, nseq=nseq, n=n)
    cv = conv_branch(p, *conv_args, row0=0, nseq=nseq, n=n)
    ff = fft_branch(p, *dft, row0=0, nseq=nseq, n=n)
    po = pool_branch(p, *pool_args, row0=0, nseq=nseq, n=n)
    return q4, k4, v, cv, ff, po


def kernel(x, c, ctx, c_ctx, w_mod, b_mod, g_norm1, g_norm2, w_in, g_cq, w_uq, g_ckv, w_ukv, w_o_mla, conv_w, conv_b, conv_ln_g, conv_ln_b, w_o_conv, w_o_fft, pool_w, pool_scale, w_o_pool, w_out, w_router, router_bias, w_exp_gate, w_exp_up, w_exp_down, w_sh_gate, w_sh_up, w_sh_down, g_final):
    x_lat = x.reshape(T_LAT, D_MODEL)
    x_ctx = ctx.reshape(T_CTX, D_MODEL)
    ctx_tile0 = 0
    cvec = jnp.concatenate([c, c_ctx[None, :], jnp.zeros((SUBLANES - BATCH - 1, D_MODEL), F32)], axis=0)
    mod_all = modulation_all(cvec, w_mod, b_mod).reshape(DEPTH, SUBLANES, 6, D_MODEL)
    rope_lat = _rope_tables(SEQ // GRID_W)
    rope_ctx = _identity_tables(CTX_LEN)
    dft_lat = _dft_tables(SEQ)
    dft_ctx = _dft_tables(CTX_LEN)
    lat_tiles = T_LAT // ROW_TILE
    all_tiles = T_ALL // ROW_TILE

    for l in range(DEPTH):
        last = l == DEPTH - 1
        mod = mod_all[l]
        w_gate, w_rest = _pack_w_in(w_in[l])
        wts = _mla_weights(g_cq[l], w_uq[l], g_ckv[l], w_ukv[l])
        conv_args = (conv_w[l], conv_b[l], conv_ln_g[l], conv_ln_b[l])
        pool_args = (pool_w[l].astype(BF16), pool_scale[l])
        wo4 = jnp.stack([w_o_mla[l], w_o_conv[l], w_o_fft[l], w_o_pool[l]], axis=0).astype(BF16)
        w_out_b = w_out[l].astype(BF16)

        ctx_col0, ctx_cols = (P_MLA, P_MLA_W) if last else (0, P_COLS)
        p = normproj(x_lat, g_norm1[l], mod, w_gate, w_rest, tile0=0, ntiles=lat_tiles, col0=0, ncols=P_COLS)
        p_ctx = normproj(x_ctx, g_norm1[l], mod, w_gate, w_rest, tile0=ctx_tile0, ntiles=all_tiles - lat_tiles,
                         mod_tile0=lat_tiles, col0=ctx_col0, ncols=ctx_cols)
        q4, k4, v, cv, ff, po = _mixers(p, BATCH, SEQ, wts, rope_lat, conv_args, pool_args, dft_lat)
        if last:
            qc, kc, vc = mla_prep(p_ctx, wts, rope_ctx, row0=0, col0=0, nseq=BATCH, n=CTX_LEN)
        else:
            qc, kc, vc, cvc, ffc, poc = _mixers(p_ctx, BATCH, CTX_LEN, wts, rope_ctx, conv_args, pool_args,
                                                dft_ctx)
        k_all = jnp.concatenate([k4, kc], axis=2)
        v_all = jnp.concatenate([v, vc], axis=2)
        at = attention(q4, k_all, v_all, nq=SEQ, nk=SEQ + CTX_LEN)
        lat_t, ctx_t = T_LAT // 256, T_CTX // 256
        if last:
            ntok = T_LAT
            xa = merge(x_lat, mod, at, cv, ff, po, p, wo4, w_out_b, ntiles=lat_t)
        else:
            ntok = T_ALL
            atc = attention(qc, kc, vc, nq=CTX_LEN, nk=CTX_LEN)
            xa = merge(x_lat, mod, at, cv, ff, po, p, wo4, w_out_b, ntiles=lat_t, out_tiles=lat_t + ctx_t)
            xa = merge(x_ctx, mod, atc, cvc, ffc, poc, p_ctx, wo4, w_out_b, ntiles=ctx_t, out_tile0=lat_t,
                       out_tiles=lat_t + ctx_t, into=xa)

        wr_t = w_router[l].T
        wr_hi = wr_t.astype(BF16)
        wr_lo = (wr_t - wr_hi.astype(F32)).astype(BF16)
        tokens, eidx, wk, rank, cnt = norm_router(xa, g_norm2[l], mod, wr_hi, wr_lo, router_bias[l],
                                                  ntiles=ntok // ROUTER_TILE)
        dest, zends, blk_e, nsub, next_e, nused = _routing_plan(eidx, rank, cnt[:, ::LANES])
        xs, sh = dispatch_shared(tokens, _tile_major(dest, 256), zends, w_sh_gate[l].astype(BF16),
                                 w_sh_up[l].astype(BF16), w_sh_down[l].astype(BF16), ntiles=ntok // 256)
        ys = experts(xs, blk_e, nsub, next_e, nused, w_exp_gate, w_exp_up, w_exp_down, layer=l)
        xa = combine(xa, sh, mod, wk.T, g_final, _tile_major(dest, 128), ys, ntiles=ntok // 128, final=last)
        x_lat, x_ctx, ctx_tile0 = xa, xa, lat_tiles
    return xa.reshape(BATCH, SEQ, D_MODEL)
```

```python
import functools

import jax
import jax.numpy as jnp
import numpy as np
from jax import lax
from jax.experimental import pallas as pl
from jax.experimental.pallas import tpu as pltpu

F32 = jnp.float32
BF16 = jnp.bfloat16
I32 = jnp.int32

D_MODEL = 2048
BATCH = 4
SEQ = 4096
DEPTH = 2
GRID_W = 64
CTX_LEN = 256
MLA_HEADS = 8
QK_NOPE = 64
QK_ROPE = 32
V_HEAD = 64
Q_LORA = 512
KV_LORA = 256
ROPE_BASE = 10000.0
CONV_CH = 512
CONV_WIDTH = 31
FFT_CH = 512
FFT_GROUPS = 4
POOL_CH = 512
POOL_WINDOWS = (2, 4, 8, 16)
POOL_GROUP = POOL_CH // len(POOL_WINDOWS)
N_BRANCHES = 4
N_EXPERTS = 64
N_EXPERT_GROUPS = 8
TOPK_GROUPS = 4
TOP_K = 8
EXPERT_FF = 512
SHARED_FF = 512
ROUTED_SCALE = 2.5
EPS = 1e-6

MLA_IN = Q_LORA + KV_LORA + QK_ROPE
T_LAT = BATCH * SEQ
T_CTX = BATCH * CTX_LEN
T_ALL = T_LAT + T_CTX

LANES = 128
SUBLANES = 8
VMEM_LIMIT_BYTES = 56 * 1024 * 1024

P_GATE = 0
P_MLA = N_BRANCHES * D_MODEL
P_MLA_W = 1024
P_CONV = P_MLA + P_MLA_W
P_FFT = P_CONV + 2 * CONV_CH
P_POOL = P_FFT + FFT_CH
P_COLS = P_POOL + POOL_CH
HEAD_PAD = 128
ATT_SCALE = (QK_NOPE + QK_ROPE) ** -0.5
Q_SCALE = ATT_SCALE * float(np.log2(np.e))

ROW_TILE = 1024
ROUTER_TILE = 256
EXPERT_ROWS = 256
EXPERT_SUBS = 4
STEP_ROWS = EXPERT_ROWS * EXPERT_SUBS
N_BLOCKS = -(-(T_ALL * TOP_K) // STEP_ROWS) + N_EXPERTS


def _cparams(sem, vmem=VMEM_LIMIT_BYTES):
    return pltpu.CompilerParams(dimension_semantics=sem, vmem_limit_bytes=vmem)


def _sigmoid(x):
    return 1.0 / (1.0 + jnp.exp(-x))


def _silu(x):
    return x * _sigmoid(x)


def _mod_row(tile, tiles_per_seq):
    return jnp.minimum(tile // tiles_per_seq, BATCH)


def _modulation_kernel(c_ref, w_ref, b_ref, o_ref):
    c = c_ref[...]
    s = _silu(c).astype(BF16)
    o_ref[0] = jnp.dot(s, w_ref[0].astype(BF16), preferred_element_type=F32) + b_ref[0]


def modulation_all(cvec, w_mod, b_mod):
    n = 6 * D_MODEL
    tn = 2048 if n % 2048 == 0 else 1024
    return pl.pallas_call(
        _modulation_kernel,
        out_shape=jax.ShapeDtypeStruct((DEPTH, SUBLANES, n), F32),
        grid=(DEPTH, n // tn),
        in_specs=[
            pl.BlockSpec((SUBLANES, D_MODEL), lambda l, j: (0, 0)),
            pl.BlockSpec((1, D_MODEL, tn), lambda l, j: (l, 0, j)),
            pl.BlockSpec((1, 1, tn), lambda l, j: (l, 0, j)),
        ],
        out_specs=pl.BlockSpec((1, SUBLANES, tn), lambda l, j: (l, 0, j)),
        compiler_params=_cparams(("arbitrary", "arbitrary")),
        name="modulation",
    )(cvec, w_mod, b_mod.reshape(DEPTH, 1, n))


def _adaln(x, g, shift, scale):
    y = x * lax.rsqrt(jnp.mean(x * x, axis=-1, keepdims=True) + EPS)
    return (y * g) * (1.0 + scale) + shift


def _normproj_kernel(x_ref, g_ref, mod_ref, wg_ref, wr_ref, o_ref, h_ref, *, shift_row, chunk, cb0, ngate):
    j = pl.program_id(1)

    @pl.when(j == 0)
    def _():
        shift = mod_ref[0, shift_row:shift_row + 1, :]
        scale = mod_ref[0, shift_row + 1:shift_row + 2, :]
        for r in range(0, x_ref.shape[0], chunk):
            h = _adaln(x_ref[r:r + chunk, :], g_ref[...], shift, scale)
            h_ref[r:r + chunk, :] = h.astype(BF16)

    @pl.when(j + cb0 < ngate)
    def _():
        o_ref[...] = jnp.dot(h_ref[...], wg_ref[...], preferred_element_type=F32).astype(o_ref.dtype)

    @pl.when(j + cb0 >= ngate)
    def _():
        o_ref[...] = jnp.dot(h_ref[...], wr_ref[...], preferred_element_type=F32).astype(o_ref.dtype)


def normproj(x, g, mod, w_gate, w_rest, *, tile0, ntiles, col0, ncols, mod_tile0=None, shift_row=0):
    tm, tn = ROW_TILE, 1024
    tps = SEQ // tm
    cb0 = col0 // tn
    ngate = w_gate.shape[1] // tn
    mt0 = tile0 if mod_tile0 is None else mod_tile0
    return pl.pallas_call(
        functools.partial(_normproj_kernel, shift_row=shift_row, chunk=256, cb0=cb0, ngate=ngate),
        out_shape=jax.ShapeDtypeStruct((ntiles * tm, ncols), BF16),
        grid=(ntiles, ncols // tn),
        in_specs=[
            pl.BlockSpec((tm, D_MODEL), lambda i, j: (i + tile0, 0)),
            pl.BlockSpec((1, D_MODEL), lambda i, j: (0, 0)),
            pl.BlockSpec((1, 6, D_MODEL), lambda i, j: (_mod_row(i + mt0, tps), 0, 0)),
            pl.BlockSpec((D_MODEL, tn), lambda i, j: (0, jnp.minimum(j + cb0, ngate - 1))),
            pl.BlockSpec((D_MODEL, tn), lambda i, j: (0, jnp.maximum(j + cb0 - ngate, 0))),
        ],
        out_specs=pl.BlockSpec((tm, tn), lambda i, j: (i, j)),
        scratch_shapes=[pltpu.VMEM((tm, D_MODEL), BF16)],
        compiler_params=_cparams(("arbitrary", "arbitrary")),
        name="normproj",
    )(x, g.reshape(1, D_MODEL), mod, w_gate, w_rest)


def _rms_gain(x, g):
    y = x * lax.rsqrt(jnp.mean(x * x, axis=-1, keepdims=True) + EPS)
    return (y * g).astype(BF16)


_NT = (((1,), (1,)), ((), ()))


def _mla_prep_kernel(p_ref, gq_ref, gkv_ref, wqt_ref, wk_ref, wvt_ref, pk_ref, cos_ref, sin_ref,
                     cost_ref, sint_ref, qt_ref, k_ref, vt_ref):
    hw = MLA_HEADS * HEAD_PAD
    cqn = _rms_gain(p_ref[:, :Q_LORA].astype(F32), gq_ref[...])
    ckvn = _rms_gain(p_ref[:, Q_LORA:Q_LORA + KV_LORA].astype(F32), gkv_ref[...])
    kr = p_ref[:, Q_LORA + KV_LORA:Q_LORA + KV_LORA + LANES]
    q2t = lax.dot_general(wqt_ref[...], cqn, _NT, preferred_element_type=F32)
    vt_ref[0] = lax.dot_general(wvt_ref[...], ckvn, _NT, preferred_element_type=F32).astype(BF16)
    kk = jnp.dot(ckvn, wk_ref[...], preferred_element_type=F32)
    kr2 = jnp.dot(kr, pk_ref[...], preferred_element_type=F32)
    cos, sin = cos_ref[...], sin_ref[...]
    cost, sint = cost_ref[...], sint_ref[...]
    for h in range(MLA_HEADS):
        a = slice(h * HEAD_PAD, (h + 1) * HEAD_PAD)
        b = slice(hw + h * HEAD_PAD, hw + (h + 1) * HEAD_PAD)
        qt_ref[0, h] = ((q2t[a, :] * cost + q2t[b, :] * sint) * Q_SCALE).astype(BF16)
        k_ref[0, h] = (kk[:, a] + kr2[:, a] * cos + kr2[:, b] * sin).astype(BF16)


def _mla_prep_into_kernel(*refs):
    _mla_prep_kernel(*refs[:11], *refs[13:])


def mla_prep(p, wts, tables, *, row0, col0, nseq, n, n_keys=None, key0=0, into=None):
    tm = 256
    tps = n // tm
    hw = MLA_HEADS * HEAD_PAD
    vw = MLA_HEADS * V_HEAD
    blk0 = row0 // tm
    cblk = col0 // P_MLA_W
    n_keys = n if n_keys is None else n_keys
    kb0 = key0 // tm
    cos_t, sin_t = tables
    const = lambda i: (0, 0)
    in_specs = [
        pl.BlockSpec((tm, P_MLA_W), lambda i: (i + blk0, cblk)),
        pl.BlockSpec((1, Q_LORA), const),
        pl.BlockSpec((1, KV_LORA), const),
        pl.BlockSpec((2 * hw, Q_LORA), const),
        pl.BlockSpec((KV_LORA, hw), const),
        pl.BlockSpec((vw, KV_LORA), const),
        pl.BlockSpec((LANES, 2 * hw), const),
        pl.BlockSpec((tm, HEAD_PAD), lambda i: (i % tps, 0)),
        pl.BlockSpec((tm, HEAD_PAD), lambda i: (i % tps, 0)),
        pl.BlockSpec((HEAD_PAD, tm), lambda i: (0, i % tps)),
        pl.BlockSpec((HEAD_PAD, tm), lambda i: (0, i % tps)),
    ]
    operands = [p, wts["g_cq"], wts["g_ckv"], wts["wq2t"], wts["wk"], wts["wvt"], wts["pk"],
                cos_t, sin_t, cos_t.T, sin_t.T]
    kernel_fn, aliases = _mla_prep_kernel, {}
    if into is not None:
        in_specs += [pl.BlockSpec(memory_space=pl.ANY), pl.BlockSpec(memory_space=pl.ANY)]
        operands += list(into)
        kernel_fn, aliases = _mla_prep_into_kernel, {11: 1, 12: 2}
    return pl.pallas_call(
        kernel_fn,
        out_shape=(jax.ShapeDtypeStruct((nseq, MLA_HEADS, HEAD_PAD, n), BF16),
                   jax.ShapeDtypeStruct((nseq, MLA_HEADS, n_keys, HEAD_PAD), BF16),
                   jax.ShapeDtypeStruct((nseq, vw, n_keys), BF16)),
        grid=(nseq * tps,),
        in_specs=in_specs,
        out_specs=(pl.BlockSpec((1, MLA_HEADS, HEAD_PAD, tm), lambda i: (i // tps, 0, 0, i % tps)),
                   pl.BlockSpec((1, MLA_HEADS, tm, HEAD_PAD), lambda i: (i // tps, 0, i % tps + kb0, 0)),
                   pl.BlockSpec((1, vw, tm), lambda i: (i // tps, 0, i % tps + kb0))),
        input_output_aliases=aliases,
        compiler_params=_cparams(("arbitrary",)),
        name="mla_prep",
    )(*operands)


ATT_HEADS = 8


def _attn_kernel(qt_ref, k_ref, vt_ref, o_ref, s_ref, p_ref, *, nk, ck):
    tq = qt_ref.shape[3]
    chunks = [slice(c, c + ck) for c in range(0, nk, ck)]
    nh = ATT_HEADS
    m = [None] * nh
    l = [jnp.zeros((1, tq), F32) for _ in range(nh)]
    ot = [jnp.zeros((V_HEAD, tq), F32) for _ in range(nh)]
    for step in range(nh + 2):
        ha, hb, hc = step, step - 1, step - 2
        for c in chunks:
            if 0 <= ha < nh:
                s = jnp.dot(k_ref[0, ha, c, :], qt_ref[0, ha], preferred_element_type=F32)
                s_ref[ha % 2, c, :] = s
                mc = jnp.max(s, axis=0, keepdims=True)
                m[ha] = mc if m[ha] is None else jnp.maximum(m[ha], mc)
            if 0 <= hb < nh:
                p = jnp.exp2(s_ref[hb % 2, c, :] - m[hb])
                l[hb] = l[hb] + jnp.sum(p, axis=0, keepdims=True)
                p_ref[hb % 2, c, :] = p.astype(BF16)
            if 0 <= hc < nh:
                ot[hc] = ot[hc] + jnp.dot(vt_ref[0, hc * V_HEAD:(hc + 1) * V_HEAD, c], p_ref[hc % 2, c, :],
                                          preferred_element_type=F32)
    outs = [ot[h] / l[h] for h in range(nh)]
    o_ref[...] = jnp.concatenate(outs, axis=0).T.astype(o_ref.dtype)


def attention(qt4, k4, vt3, *, nq, nk, key0=0):
    tq = 256
    ck = 256
    nqt = nq // tq
    hs = ATT_HEADS
    kb = key0 // nk
    return pl.pallas_call(
        functools.partial(_attn_kernel, nk=nk, ck=ck),
        out_shape=jax.ShapeDtypeStruct((BATCH * nq, MLA_HEADS * V_HEAD), BF16),
        grid=(BATCH, MLA_HEADS // hs, nqt),
        in_specs=[
            pl.BlockSpec((1, hs, HEAD_PAD, tq), lambda b, hg, i: (b, hg, 0, i)),
            pl.BlockSpec((1, hs, nk, HEAD_PAD), lambda b, hg, i: (b, hg, kb, 0)),
            pl.BlockSpec((1, hs * V_HEAD, nk), lambda b, hg, i: (b, hg, kb)),
        ],
        out_specs=pl.BlockSpec((tq, hs * V_HEAD), lambda b, hg, i: (b * nqt + i, hg)),
        scratch_shapes=[pltpu.VMEM((2, nk, tq), F32), pltpu.VMEM((2, nk, tq), BF16)],
        compiler_params=_cparams(("arbitrary", "arbitrary", "arbitrary")),
        name="attention",
    )(qt4, k4, vt3)


CONV_PAD = 16


def _conv_kernel(p_ref, w_ref, cb_ref, lg_ref, lb_ref, o_ref, u_ref, *, n, tr):
    zeros = jnp.zeros((CONV_PAD, CONV_CH), F32)
    u_ref[0:CONV_PAD, :] = zeros
    u_ref[CONV_PAD + n:2 * CONV_PAD + n, :] = zeros

    def glu(i, carry):
        r = pl.multiple_of(i * tr, tr)
        a = p_ref[pl.ds(r, tr), 0:CONV_CH].astype(F32)
        b = p_ref[pl.ds(r, tr), CONV_CH:2 * CONV_CH].astype(F32)
        u_ref[pl.ds(CONV_PAD + r, tr), :] = a * _sigmoid(b)
        return carry

    lax.fori_loop(0, n // tr, glu, 0)
    off = CONV_PAD - CONV_WIDTH // 2
    win_rows = tr + 2 * CONV_PAD

    def conv(i, carry):
        r = pl.multiple_of(i * tr, tr)
        strips = []
        for c0 in range(0, CONV_CH, LANES):
            win = u_ref[pl.ds(r, win_rows), c0:c0 + LANES]
            acc = jnp.zeros((tr, LANES), F32)
            for b in range(SUBLANES):
                wb = pltpu.roll(win, win_rows - (off + b), axis=0)
                for k in range(b, CONV_WIDTH, SUBLANES):
                    acc = acc + wb[k - b:k - b + tr, :] * w_ref[k:k + 1, c0:c0 + LANES]
            strips.append(acc)
        acc = jnp.concatenate(strips, axis=-1) + cb_ref[...]
        mu = jnp.mean(acc, axis=-1, keepdims=True)
        d = acc - mu
        var = jnp.mean(d * d, axis=-1, keepdims=True)
        y = (d * lax.rsqrt(var + EPS)) * lg_ref[...] + lb_ref[...]
        o_ref[pl.ds(r, tr), :] = _silu(y).astype(o_ref.dtype)
        return carry

    lax.fori_loop(0, n // tr, conv, 0)


def conv_branch(p, conv_w, conv_b, ln_g, ln_b, *, row0, nseq, n):
    tr = 128
    blk0 = row0 // n
    const = lambda s: (0, 0)
    return pl.pallas_call(
        functools.partial(_conv_kernel, n=n, tr=tr),
        out_shape=jax.ShapeDtypeStruct((nseq * n, CONV_CH), BF16),
        grid=(nseq,),
        in_specs=[
            pl.BlockSpec((n, 2 * CONV_CH), lambda s: (s + blk0, P_CONV // (2 * CONV_CH))),
            pl.BlockSpec((CONV_WIDTH, CONV_CH), const),
            pl.BlockSpec((1, CONV_CH), const),
            pl.BlockSpec((1, CONV_CH), const),
            pl.BlockSpec((1, CONV_CH), const),
        ],
        out_specs=pl.BlockSpec((n, CONV_CH), lambda s: (s, 0)),
        scratch_shapes=[pltpu.VMEM((n + 2 * CONV_PAD, CONV_CH), F32)],
        compiler_params=_cparams(("arbitrary",)),
        name="conv_branch",
    )(p, conv_w, conv_b.reshape(1, -1), ln_g.reshape(1, -1), ln_b.reshape(1, -1))


def _fft_kernel(z_ref, cs_ref, mc_ref, ms_ref, o_ref, ab_ref, *, n, tc):
    gw = FFT_CH // FFT_GROUPS

    @pl.when(pl.program_id(1) == 0)
    def _():
        def chan(i, carry):
            r = pl.multiple_of(i * tc, tc)
            for g in range(FFT_GROUPS):
                zg = z_ref[pl.ds(r, tc), g * gw:(g + 1) * gw]
                ab = jnp.dot(zg, cs_ref[...], preferred_element_type=F32)
                ab_ref[pl.ds(r, tc), g * gw:(g + 1) * gw] = ab[:, :gw].astype(BF16)
                ab_ref[pl.ds(n + r, tc), g * gw:(g + 1) * gw] = ab[:, gw:].astype(BF16)
            return carry

        lax.fori_loop(0, n // tc, chan, 0)

    norm = 1.0 / float(np.sqrt(n * gw))
    re = (jnp.dot(mc_ref[...], ab_ref[0:n, :], preferred_element_type=F32)
          + jnp.dot(ms_ref[...], ab_ref[n:2 * n, :], preferred_element_type=F32))
    o_ref[...] = (re * norm).astype(o_ref.dtype)


def fft_branch(p, cs, mcos, mnsin, *, row0, nseq, n):
    tm = min(n, 512)
    tc = min(n, 512)
    blk0 = row0 // n
    nt = n // tm
    return pl.pallas_call(
        functools.partial(_fft_kernel, n=n, tc=tc),
        out_shape=jax.ShapeDtypeStruct((nseq * n, FFT_CH), BF16),
        grid=(nseq, nt),
        in_specs=[
            pl.BlockSpec((n, FFT_CH), lambda s, i: (s + blk0, P_FFT // FFT_CH)),
            pl.BlockSpec((FFT_CH // FFT_GROUPS, 2 * FFT_CH // FFT_GROUPS), lambda s, i: (0, 0)),
            pl.BlockSpec((tm, n), lambda s, i: (i, 0)),
            pl.BlockSpec((tm, n), lambda s, i: (i, 0)),
        ],
        out_specs=pl.BlockSpec((tm, FFT_CH), lambda s, i: (s * nt + i, 0)),
        scratch_shapes=[pltpu.VMEM((2 * n, FFT_CH), BF16)],
        compiler_params=_cparams(("arbitrary", "arbitrary")),
        name="fft_branch",
    )(p, cs, mcos, mnsin)


POOL_PAD = 16


def _pool_kernel(z_ref, pw_ref, ps_ref, o_ref, zp_ref, d_ref, *, n, tr, tc):
    zeros = jnp.zeros((POOL_PAD, POOL_CH), F32)
    zp_ref[0:POOL_PAD, :] = zeros
    zp_ref[POOL_PAD + n:2 * POOL_PAD + n, :] = zeros

    def fill(i, carry):
        r = pl.multiple_of(i * tc, tc)
        zp_ref[pl.ds(POOL_PAD + r, tc), :] = z_ref[pl.ds(r, tc), :].astype(F32)
        return carry

    lax.fori_loop(0, n // tc, fill, 0)

    win_rows = tr + 2 * POOL_PAD

    def pool(i, carry):
        r = pl.multiple_of(i * tr, tr)
        t = (r + lax.broadcasted_iota(I32, (tr, POOL_GROUP), 0)).astype(F32)
        for gi, w in enumerate(POOL_WINDOWS):
            cols = slice(gi * POOL_GROUP, (gi + 1) * POOL_GROUP)
            win = zp_ref[pl.ds(r, win_rows), cols]
            s = jnp.zeros((tr, POOL_GROUP), F32)
            for j in range(w):
                start = POOL_PAD - w // 2 + j
                if start % SUBLANES == 0:
                    s = s + win[start:start + tr, :]
                else:
                    s = s + pltpu.roll(win, win_rows - start, axis=0)[0:tr, :]
            lo = jnp.maximum(t - (w // 2), 0.0)
            hi = jnp.minimum(t - (w // 2) + w, float(n))
            z = win[POOL_PAD:POOL_PAD + tr, :]
            d_ref[pl.ds(r, tr), cols] = (s / (hi - lo) - z).astype(BF16)
        return carry

    lax.fori_loop(0, n // tr, pool, 0)

    def proj(i, carry):
        r = pl.multiple_of(i * tc, tc)
        for gi in range(len(POOL_WINDOWS)):
            cols = slice(gi * POOL_GROUP, (gi + 1) * POOL_GROUP)
            y = jnp.dot(d_ref[pl.ds(r, tc), cols], pw_ref[gi], preferred_element_type=F32)
            o_ref[pl.ds(r, tc), cols] = (y * ps_ref[:, cols]).astype(o_ref.dtype)
        return carry

    lax.fori_loop(0, n // tc, proj, 0)


def pool_branch(p, pool_w, pool_scale, *, row0, nseq, n):
    tr = 128
    tc = min(n, 512)
    blk0 = row0 // n
    return pl.pallas_call(
        functools.partial(_pool_kernel, n=n, tr=tr, tc=tc),
        out_shape=jax.ShapeDtypeStruct((nseq * n, POOL_CH), BF16),
        grid=(nseq,),
        in_specs=[
            pl.BlockSpec((n, POOL_CH), lambda s: (s + blk0, P_POOL // POOL_CH)),
            pl.BlockSpec((len(POOL_WINDOWS), POOL_GROUP, POOL_GROUP), lambda s: (0, 0, 0)),
            pl.BlockSpec((1, POOL_CH), lambda s: (0, 0)),
        ],
        out_specs=pl.BlockSpec((n, POOL_CH), lambda s: (s, 0)),
        scratch_shapes=[pltpu.VMEM((n + 2 * POOL_PAD, POOL_CH), F32),
                        pltpu.VMEM((n, POOL_CH), BF16)],
        compiler_params=_cparams(("arbitrary",)),
        name="pool_branch",
    )(p, pool_w, pool_scale.reshape(1, -1))


def _merge_kernel(x_ref, mod_ref, at_ref, cv_ref, ff_ref, po_ref, g_ref, wo_ref, wout_ref, o_ref):
    mix = None
    for b, br in enumerate((at_ref, cv_ref, ff_ref, po_ref)):
        y = jnp.dot(br[...], wo_ref[b], preferred_element_type=F32)
        gate = _sigmoid(g_ref[:, b * D_MODEL:(b + 1) * D_MODEL].astype(F32))
        mix = gate * y if mix is None else mix + gate * y
    out = jnp.dot(mix.astype(BF16), wout_ref[...], preferred_element_type=F32)
    o_ref[...] = x_ref[...] + mod_ref[0, 2:3, :] * out


def _merge_into_kernel(x_ref, mod_ref, at_ref, cv_ref, ff_ref, po_ref, g_ref, wo_ref, wout_ref, prev_ref, o_ref):
    del prev_ref
    _merge_kernel(x_ref, mod_ref, at_ref, cv_ref, ff_ref, po_ref, g_ref, wo_ref, wout_ref, o_ref)


def merge(x, mod, attn, conv, fft, pool, p, wo4, w_out, *, ntiles, out_tile0=0, out_tiles=None, into=None):
    tm = 256
    tps = SEQ // tm
    out_tiles = ntiles if out_tiles is None else out_tiles
    br_spec = pl.BlockSpec((tm, 512), lambda i: (i, 0))
    in_specs = [
        pl.BlockSpec((tm, D_MODEL), lambda i: (i, 0)),
        pl.BlockSpec((1, 6, D_MODEL), lambda i: (_mod_row(i + out_tile0, tps), 0, 0)),
        br_spec, br_spec, br_spec, br_spec,
        pl.BlockSpec((tm, N_BRANCHES * D_MODEL), lambda i: (i, P_GATE // (N_BRANCHES * D_MODEL))),
        pl.BlockSpec((N_BRANCHES, 512, D_MODEL), lambda i: (0, 0, 0), pipeline_mode=pl.Buffered(1)),
        pl.BlockSpec((D_MODEL, D_MODEL), lambda i: (0, 0), pipeline_mode=pl.Buffered(1)),
    ]
    operands = [x, mod, attn, conv, fft, pool, p, wo4, w_out]
    kernel_fn, aliases = _merge_kernel, {}
    if into is not None:
        in_specs.append(pl.BlockSpec(memory_space=pl.ANY))
        operands.append(into)
        kernel_fn, aliases = _merge_into_kernel, {len(operands) - 1: 0}
    return pl.pallas_call(
        kernel_fn,
        out_shape=jax.ShapeDtypeStruct((out_tiles * tm, D_MODEL), F32),
        grid=(ntiles,),
        in_specs=in_specs,
        out_specs=pl.BlockSpec((tm, D_MODEL), lambda i: (i + out_tile0, 0)),
        input_output_aliases=aliases,
        compiler_params=_cparams(("arbitrary",)),
        name="merge",
    )(*operands)


def _first_index(hit_value, cand, ids, big):
    return jnp.min(jnp.where(cand == hit_value, ids, big), axis=0, keepdims=True)


HALF = D_MODEL // 2
ROW_WORDS = HALF // LANES


def _round_bf16_bits(x):
    u = lax.bitcast_convert_type(x, I32)
    odd = lax.shift_right_logical(u, 16) & 1
    return (u + 0x7FFF + odd) & jnp.int32(-65536)


def _pack_words(lo, hi):
    return lax.shift_right_logical(_round_bf16_bits(lo), 16) | _round_bf16_bits(hi)


def _unpack_words(w):
    lo = lax.bitcast_convert_type(lax.shift_left(w, 16), F32)
    hi = lax.bitcast_convert_type(w & jnp.int32(-65536), F32)
    return lo, hi


def _store_packed(ref, x, tok0=0):
    tm = x.shape[0]
    for s in range(ROW_WORDS):
        w = _pack_words(x[:, s * LANES:(s + 1) * LANES], x[:, HALF + s * LANES:HALF + (s + 1) * LANES])
        ref[pl.ds(tok0 * ROW_WORDS + s, tm, stride=ROW_WORDS), :] = w


def _load_packed(ref, tm, dtype, tok0=0):
    los, his = [], []
    for s in range(ROW_WORDS):
        lo, hi = _unpack_words(ref[pl.ds(tok0 * ROW_WORDS + s, tm, stride=ROW_WORDS), :])
        los.append(lo.astype(dtype))
        his.append(hi.astype(dtype))
    return jnp.concatenate(los + his, axis=-1)


def _router_kernel(x_ref, g_ref, mod_ref, wrh_ref, wrl_ref, rb_ref, tri_ref,
                   tok_ref, eidx_ref, wk_ref, rank_ref, cnt_ref):
    tm = x_ref.shape[0]
    h = _adaln(x_ref[...], g_ref[...], mod_ref[0, 3:4, :], mod_ref[0, 4:5, :])
    _store_packed(tok_ref, h)
    hh = h.astype(BF16)
    hl = (h - hh.astype(F32)).astype(BF16)
    nt = (((1,), (1,)), ((), ()))
    logits = (lax.dot_general(wrh_ref[...], hh, nt, preferred_element_type=F32)
              + lax.dot_general(wrh_ref[...], hl, nt, preferred_element_type=F32)
              + lax.dot_general(wrl_ref[...], hh, nt, preferred_element_type=F32))
    scores = _sigmoid(logits)
    sel = scores + rb_ref[...]
    per = N_EXPERTS // N_EXPERT_GROUPS
    assert per == SUBLANES and N_EXPERT_GROUPS == SUBLANES and TOP_K == SUBLANES
    neg = -jnp.inf
    sub = lax.broadcasted_iota(I32, (SUBLANES, tm), 0).astype(F32)
    sg = [sel[g * per:(g + 1) * per, :] for g in range(N_EXPERT_GROUPS)]
    sc = [scores[g * per:(g + 1) * per, :] for g in range(N_EXPERT_GROUPS)]
    gsc = jnp.zeros((SUBLANES, tm), F32)
    for g in range(N_EXPERT_GROUPS):
        m1 = jnp.max(sg[g], axis=0, keepdims=True)
        i1 = _first_index(m1, sg[g], sub, float(per))
        m2 = jnp.max(jnp.where(sub == i1, neg, sg[g]), axis=0, keepdims=True)
        gsc = jnp.where(sub == float(g), m1 + m2, gsc)
    gsel = jnp.zeros((SUBLANES, tm), F32)
    for _ in range(TOPK_GROUPS):
        m = jnp.max(gsc, axis=0, keepdims=True)
        hit = sub == _first_index(m, gsc, sub, float(N_EXPERT_GROUPS))
        gsel = jnp.where(hit, 1.0, gsel)
        gsc = jnp.where(hit, neg, gsc)
    cand = []
    for g in range(N_EXPERT_GROUPS):
        allowed = jnp.max(jnp.where(sub == float(g), gsel, 0.0), axis=0, keepdims=True)
        cand.append(jnp.where(allowed > 0.0, sg[g], neg))
    eid = [sub + float(g * per) for g in range(N_EXPERT_GROUPS)]
    idxs = jnp.zeros((SUBLANES, tm), F32)
    vals = jnp.zeros((SUBLANES, tm), F32)
    picked = [jnp.zeros((per, tm), F32) for _ in range(N_EXPERT_GROUPS)]
    idx_k = []
    for k in range(TOP_K):
        m = functools.reduce(jnp.maximum, [jnp.max(c, axis=0, keepdims=True) for c in cand])
        idx = functools.reduce(
            jnp.minimum, [_first_index(m, cand[g], eid[g], float(N_EXPERTS)) for g in range(N_EXPERT_GROUPS)])
        val = jnp.zeros((1, tm), F32)
        for g in range(N_EXPERT_GROUPS):
            hit = eid[g] == idx
            val = val + jnp.sum(jnp.where(hit, sc[g], 0.0), axis=0, keepdims=True)
            cand[g] = jnp.where(hit, neg, cand[g])
            picked[g] = jnp.where(hit, 1.0, picked[g])
        idxs = jnp.where(sub == float(k), idx, idxs)
        vals = jnp.where(sub == float(k), val, vals)
        idx_k.append(idx)
    eidx_ref[...] = idxs.astype(I32)
    wk_ref[...] = vals / jnp.sum(vals, axis=0, keepdims=True) * ROUTED_SCALE
    mask = jnp.concatenate(picked, axis=0)
    before = jnp.dot(mask.astype(BF16), tri_ref[...], preferred_element_type=F32)
    ranks = jnp.zeros((SUBLANES, tm), F32)
    for k in range(TOP_K):
        r = jnp.zeros((1, tm), F32)
        for g in range(N_EXPERT_GROUPS):
            r = r + jnp.sum(jnp.where(eid[g] == idx_k[k], before[g * per:(g + 1) * per, :], 0.0),
                            axis=0, keepdims=True)
        ranks = jnp.where(sub == float(k), r, ranks)
    rank_ref[...] = ranks.astype(I32)
    cnt_ref[...] = jnp.broadcast_to(jnp.sum(mask, axis=1, keepdims=True), (N_EXPERTS, LANES)).astype(I32)


def norm_router(x, g, mod, wr_hi, wr_lo, rbias, *, ntiles):
    tm = ROUTER_TILE
    tps = SEQ // tm
    t = ntiles * tm
    tri = jnp.triu(jnp.ones((tm, tm), BF16), k=1)
    kt_spec = pl.BlockSpec((TOP_K, tm), lambda i: (0, i))
    return pl.pallas_call(
        _router_kernel,
        out_shape=(jax.ShapeDtypeStruct((t * ROW_WORDS, LANES), I32),
                   jax.ShapeDtypeStruct((TOP_K, t), I32),
                   jax.ShapeDtypeStruct((TOP_K, t), F32),
                   jax.ShapeDtypeStruct((TOP_K, t), I32),
                   jax.ShapeDtypeStruct((N_EXPERTS, ntiles * LANES), I32)),
        grid=(ntiles,),
        in_specs=[
            pl.BlockSpec((tm, D_MODEL), lambda i: (i, 0)),
            pl.BlockSpec((1, D_MODEL), lambda i: (0, 0)),
            pl.BlockSpec((1, 6, D_MODEL), lambda i: (_mod_row(i, tps), 0, 0)),
            pl.BlockSpec((N_EXPERTS, D_MODEL), lambda i: (0, 0)),
            pl.BlockSpec((N_EXPERTS, D_MODEL), lambda i: (0, 0)),
            pl.BlockSpec((N_EXPERTS, 1), lambda i: (0, 0)),
            pl.BlockSpec((tm, tm), lambda i: (0, 0)),
        ],
        out_specs=(pl.BlockSpec((tm * ROW_WORDS, LANES), lambda i: (i, 0)),
                   kt_spec, kt_spec, kt_spec,
                   pl.BlockSpec((N_EXPERTS, LANES), lambda i: (0, i))),
        compiler_params=_cparams(("arbitrary",)),
        name="norm_router",
    )(x, g.reshape(1, D_MODEL), mod, wr_hi, wr_lo, rbias.reshape(N_EXPERTS, 1), tri)


def _row_copy(src, src_tok, dst, dst_tok, sem):
    s = pl.multiple_of(src_tok * ROW_WORDS, ROW_WORDS)
    d = pl.multiple_of(dst_tok * ROW_WORDS, ROW_WORDS)
    return pltpu.make_async_copy(src.at[pl.ds(s, ROW_WORDS)], dst.at[pl.ds(d, ROW_WORDS)], sem)


def _dispatch_kernel(zends_ref, tok_ref, dest_hbm, wg_ref, wu_ref, wd_ref, xs_hbm, sh_ref,
                     dest_smem, zero_ref, sem_idx, sem_z, sem_s):
    i = pl.program_id(0)
    tm = tok_ref.shape[0] // ROW_WORDS
    idx_copy = pltpu.make_async_copy(dest_hbm.at[i], dest_smem, sem_idx)
    idx_copy.start()
    blk = EXPERT_ROWS * ROW_WORDS

    def pad_copy(e):
        row = pl.multiple_of((zends_ref[e] - EXPERT_ROWS) * ROW_WORDS, blk)
        return pltpu.make_async_copy(zero_ref, xs_hbm.at[pl.ds(row, blk)], sem_z)

    @pl.when(i == 0)
    def _():
        zero_ref[...] = jnp.zeros_like(zero_ref)

        def start(e, carry):
            @pl.when(zends_ref[e] > 0)
            def _():
                pad_copy(e).start()
            return carry

        lax.fori_loop(0, N_EXPERTS, start, 0)

        def wait(e, carry):
            @pl.when(zends_ref[e] > 0)
            def _():
                pad_copy(e).wait()
            return carry

        lax.fori_loop(0, N_EXPERTS, wait, 0)

    idx_copy.wait()

    x = _load_packed(tok_ref, tm, BF16)
    pieces = SHARED_FF // LANES
    per = tm // pieces
    acc = None
    for p in range(pieces):
        for r in range(p * per, (p + 1) * per):
            for k in range(TOP_K):
                _row_copy(tok_ref, r, xs_hbm, dest_smem[k * tm + r], sem_s).start(priority=k % 2)
        c = slice(p * LANES, (p + 1) * LANES)
        g = jnp.dot(x, wg_ref[:, c], preferred_element_type=F32)
        u = jnp.dot(x, wu_ref[:, c], preferred_element_type=F32)
        part = jnp.dot((_silu(g) * u).astype(BF16), wd_ref[c, :], preferred_element_type=F32)
        acc = part if acc is None else acc + part
    sh_ref[...] = acc

    def drain(r, carry):
        for k in range(TOP_K):
            _row_copy(tok_ref, r, xs_hbm, dest_smem[k * tm + r], sem_s).wait()
        return carry

    lax.fori_loop(0, tm, drain, 0)


def dispatch_shared(tokens, dest_tiles, zends, wg, wu, wd, *, ntiles):
    tm = tokens.shape[0] // ROW_WORDS // ntiles
    const = lambda i, zends: (0, 0)
    return pl.pallas_call(
        _dispatch_kernel,
        out_shape=(jax.ShapeDtypeStruct((N_BLOCKS * STEP_ROWS * ROW_WORDS, LANES), I32),
                   jax.ShapeDtypeStruct((ntiles * tm, D_MODEL), F32)),
        grid_spec=pltpu.PrefetchScalarGridSpec(
            num_scalar_prefetch=1,
            grid=(ntiles,),
            in_specs=[
                pl.BlockSpec((tm * ROW_WORDS, LANES), lambda i, zends: (i, 0)),
                pl.BlockSpec(memory_space=pl.ANY),
                pl.BlockSpec((D_MODEL, SHARED_FF), const),
                pl.BlockSpec((D_MODEL, SHARED_FF), const),
                pl.BlockSpec((SHARED_FF, D_MODEL), const),
            ],
            out_specs=(pl.BlockSpec(memory_space=pl.ANY),
                       pl.BlockSpec((tm, D_MODEL), lambda i, zends: (i, 0))),
            scratch_shapes=[
                pltpu.SMEM((TOP_K * tm,), I32),
                pltpu.VMEM((EXPERT_ROWS * ROW_WORDS, LANES), I32),
                pltpu.SemaphoreType.DMA,
                pltpu.SemaphoreType.DMA,
                pltpu.SemaphoreType.DMA,
            ],
        ),
        compiler_params=_cparams(("arbitrary",)),
        name="moe_dispatch",
    )(zends, tokens, dest_tiles, wg, wu, wd)


def _experts_kernel(blk_e_ref, nsub_ref, next_e_ref, nused_ref, xs_ref, wg_hbm, wu_hbm, wd_hbm, y_ref,
                    wgs, wus, wds, wgb, wub, wdb, sem_w, *, layer):
    i = pl.program_id(0)
    e = blk_e_ref[i]
    prev = blk_e_ref[jnp.maximum(i - 1, 0)]

    def weight_copies(expert):
        row = layer * N_EXPERTS + expert
        return (pltpu.make_async_copy(wg_hbm.at[row], wgs, sem_w.at[0]),
                pltpu.make_async_copy(wu_hbm.at[row], wus, sem_w.at[1]),
                pltpu.make_async_copy(wd_hbm.at[row], wds, sem_w.at[2]))

    @pl.when(i < nused_ref[0])
    def _():
        @pl.when(i == 0)
        def _():
            for cp in weight_copies(e):
                cp.start()

        @pl.when((i == 0) | (e != prev))
        def _():
            for cp in weight_copies(e):
                cp.wait()
            wgb[...] = wgs[...].astype(BF16)
            wub[...] = wus[...].astype(BF16)
            wdb[...] = wds[...].astype(BF16)

            @pl.when(next_e_ref[i] >= 0)
            def _():
                for cp in weight_copies(next_e_ref[i]):
                    cp.start()

        def mlp(sb):
            tok0 = sb * EXPERT_ROWS
            x = _load_packed(xs_ref, EXPERT_ROWS, BF16, tok0)
            g = jnp.dot(x, wgb[...], preferred_element_type=F32)
            u = jnp.dot(x, wub[...], preferred_element_type=F32)
            hb = (_silu(g) * u).astype(BF16)
            _store_packed(y_ref, jnp.dot(hb, wdb[...], preferred_element_type=F32), tok0)

        def clear(sb):
            rows = EXPERT_ROWS * ROW_WORDS
            y_ref[sb * rows:(sb + 1) * rows, :] = jnp.zeros((rows, LANES), I32)

        nsub = nsub_ref[i]
        for a in range(0, EXPERT_SUBS, 2):
            @pl.when(nsub >= a + 2)
            def _():
                mlp(a)
                mlp(a + 1)

            @pl.when(nsub == a + 1)
            def _():
                mlp(a)
                clear(a + 1)

            @pl.when(nsub <= a)
            def _():
                clear(a)
                clear(a + 1)


def experts(xs, blk_e, nsub, next_e, nused, w_gate, w_up, w_down, *, layer):
    w_gate = w_gate.reshape(DEPTH * N_EXPERTS, D_MODEL, EXPERT_FF)
    w_up = w_up.reshape(DEPTH * N_EXPERTS, D_MODEL, EXPERT_FF)
    w_down = w_down.reshape(DEPTH * N_EXPERTS, EXPERT_FF, D_MODEL)

    def row_map(i, blk_e, nsub, next_e, nused):
        return (jnp.minimum(i, nused[0] - 1), 0)

    return pl.pallas_call(
        functools.partial(_experts_kernel, layer=layer),
        out_shape=jax.ShapeDtypeStruct((N_BLOCKS * STEP_ROWS * ROW_WORDS, LANES), I32),
        grid_spec=pltpu.PrefetchScalarGridSpec(
            num_scalar_prefetch=4,
            grid=(N_BLOCKS,),
            in_specs=[
                pl.BlockSpec((STEP_ROWS * ROW_WORDS, LANES), row_map),
                pl.BlockSpec(memory_space=pl.ANY),
                pl.BlockSpec(memory_space=pl.ANY),
                pl.BlockSpec(memory_space=pl.ANY),
            ],
            out_specs=pl.BlockSpec((STEP_ROWS * ROW_WORDS, LANES), row_map),
            scratch_shapes=[
                pltpu.VMEM((D_MODEL, EXPERT_FF), F32),
                pltpu.VMEM((D_MODEL, EXPERT_FF), F32),
                pltpu.VMEM((EXPERT_FF, D_MODEL), F32),
                pltpu.VMEM((D_MODEL, EXPERT_FF), BF16),
                pltpu.VMEM((D_MODEL, EXPERT_FF), BF16),
                pltpu.VMEM((EXPERT_FF, D_MODEL), BF16),
                pltpu.SemaphoreType.DMA((3,)),
            ],
        ),
        compiler_params=_cparams(("arbitrary",)),
        name="moe_experts",
    )(blk_e, nsub, next_e, nused, xs, w_gate, w_up, w_down)


def _combine_kernel(x_ref, sh_ref, mod_ref, wk_ref, gf_ref, dest_hbm, ys_hbm, o_ref,
                    idx0, idx1, buf0, buf1, sem_idx, sem_g, *, final):
    i = pl.program_id(0)
    n = pl.num_programs(0)
    tm = x_ref.shape[0]
    idx = (idx0, idx1)
    buf = (buf0, buf1)

    def fetch_idx(tile, slot):
        idx_copy = pltpu.make_async_copy(dest_hbm.at[tile], idx[slot], sem_idx)
        idx_copy.start()
        idx_copy.wait()

    def start_rows(slot, r):
        for k in range(TOP_K):
            _row_copy(ys_hbm, idx[slot][k * tm + r], buf[slot].at[k], r, sem_g.at[slot]).start(priority=k % 2)

    def drain(slot):
        def wait_rows(r, carry):
            for k in range(TOP_K):
                _row_copy(ys_hbm, 0, buf[slot].at[k], r, sem_g.at[slot]).wait()
            return carry

        lax.fori_loop(0, tm, wait_rows, 0)

    def finish(slot):
        fetch_idx(jnp.minimum(i + 1, n - 1), 1 - slot)
        drain(slot)
        per = tm // ROW_WORDS
        los, his = [], []
        for s in range(ROW_WORDS):
            for r in range(s * per, (s + 1) * per):
                start_rows(1 - slot, r)
            acc_lo = sh_ref[:, s * LANES:(s + 1) * LANES]
            acc_hi = sh_ref[:, HALF + s * LANES:HALF + (s + 1) * LANES]
            for k in range(TOP_K):
                lo, hi = _unpack_words(buf[slot][k, pl.ds(s, tm, stride=ROW_WORDS), :])
                w = wk_ref[:, k:k + 1]
                acc_lo = acc_lo + lo * w
                acc_hi = acc_hi + hi * w
            los.append(acc_lo)
            his.append(acc_hi)
        f = jnp.concatenate(los + his, axis=-1)
        out = x_ref[...] + mod_ref[0, 5:6, :] * f
        if final:
            y = out * lax.rsqrt(jnp.mean(out * out, axis=-1, keepdims=True) + EPS)
            out = y * gf_ref[...]
        o_ref[...] = out

        @pl.when(i == n - 1)
        def _():
            drain(1 - slot)

    @pl.when(i == 0)
    def _():
        fetch_idx(0, 0)

        def first(r, carry):
            start_rows(0, r)
            return carry

        lax.fori_loop(0, tm, first, 0)

    for slot in range(2):
        @pl.when(i % 2 == slot)
        def _(slot=slot):
            finish(slot)


def combine(x, shared, mod, wk_t, g_final, dest_tiles, ys, *, ntiles, final):
    tm = 128
    tps = SEQ // tm
    return pl.pallas_call(
        functools.partial(_combine_kernel, final=final),
        out_shape=jax.ShapeDtypeStruct((ntiles * tm, D_MODEL), F32),
        grid=(ntiles,),
        in_specs=[
            pl.BlockSpec((tm, D_MODEL), lambda i: (i, 0)),
            pl.BlockSpec((tm, D_MODEL), lambda i: (i, 0)),
            pl.BlockSpec((1, 6, D_MODEL), lambda i: (_mod_row(i, tps), 0, 0)),
            pl.BlockSpec((tm, TOP_K), lambda i: (i, 0)),
            pl.BlockSpec((1, D_MODEL), lambda i: (0, 0)),
            pl.BlockSpec(memory_space=pl.ANY),
            pl.BlockSpec(memory_space=pl.ANY),
        ],
        out_specs=pl.BlockSpec((tm, D_MODEL), lambda i: (i, 0)),
        scratch_shapes=[
            pltpu.SMEM((TOP_K * tm,), I32),
            pltpu.SMEM((TOP_K * tm,), I32),
            pltpu.VMEM((TOP_K, tm * ROW_WORDS, LANES), I32),
            pltpu.VMEM((TOP_K, tm * ROW_WORDS, LANES), I32),
            pltpu.SemaphoreType.DMA,
            pltpu.SemaphoreType.DMA((2,)),
        ],
        compiler_params=_cparams(("arbitrary",)),
        name="moe_combine",
    )(x, shared, mod, wk_t, g_final.reshape(1, D_MODEL), dest_tiles, ys)


def _routing_plan(eidx, rank, tile_counts):
    t = eidx.shape[1]
    tiles = tile_counts.shape[1]
    counts = jnp.sum(tile_counts, axis=1)
    pcounts = (counts + STEP_ROWS - 1) // STEP_ROWS * STEP_ROWS
    pends = jnp.cumsum(pcounts)
    pstarts = pends - pcounts
    base = pstarts[:, None] + jnp.cumsum(tile_counts, axis=1) - tile_counts
    onehot = (eidx.reshape(TOP_K, tiles, 1, t // tiles)
              == jnp.arange(N_EXPERTS, dtype=I32)[None, None, :, None])
    dest = rank + jnp.sum(jnp.where(onehot, base.T[None, :, :, None], 0), axis=2).reshape(TOP_K, t)
    nused = (pends[-1] // STEP_ROWS).astype(I32).reshape(1)
    blocks = jnp.arange(N_BLOCKS, dtype=I32)
    blk_e = jnp.minimum(jnp.sum((pends[None, :] <= (blocks * STEP_ROWS)[:, None]).astype(I32), axis=1),
                        N_EXPERTS - 1).astype(I32)
    nsub_e = (counts + EXPERT_ROWS - 1) // EXPERT_ROWS
    zends = jnp.where(counts > 0, pstarts + nsub_e * EXPERT_ROWS, 0)
    done = (blocks - pstarts[blk_e] // STEP_ROWS) * EXPERT_SUBS
    nsub = jnp.clip(nsub_e[blk_e] - done, 0, EXPERT_SUBS)
    ids = jnp.where(counts > 0, jnp.arange(N_EXPERTS, dtype=I32), N_EXPERTS)
    later = jnp.concatenate([lax.cummin(ids[::-1])[::-1][1:], jnp.full((1,), N_EXPERTS, I32)])
    next_e = jnp.where(later < N_EXPERTS, later, -1)[blk_e]
    return dest.astype(I32), zends.astype(I32), blk_e, nsub.astype(I32), next_e.astype(I32), nused


def _tile_major(dest, tm):
    k, t = dest.shape
    return dest.reshape(k, t // tm, tm).transpose(1, 0, 2).reshape(t // tm, k * tm)


def _rope_tables(n_rows):
    row = jnp.repeat(jnp.arange(n_rows, dtype=F32), GRID_W)
    col = jnp.tile(jnp.arange(GRID_W, dtype=F32), n_rows)
    half = QK_ROPE // 2
    inv = ROPE_BASE ** (-jnp.arange(0, half, 2, dtype=F32) / half)
    ang_r = row[:, None] * inv
    ang_c = col[:, None] * inv
    ang = jnp.concatenate([ang_r, ang_r, ang_c, ang_c], axis=-1)
    n = ang.shape[0]
    ones = jnp.ones((n, QK_NOPE), F32)
    zeros = jnp.zeros((n, QK_NOPE), F32)
    tail = jnp.zeros((n, HEAD_PAD - QK_NOPE - QK_ROPE), F32)
    cos = jnp.concatenate([ones, jnp.cos(ang), tail], axis=-1)
    sin = jnp.concatenate([zeros, jnp.sin(ang), tail], axis=-1)
    return cos, sin


def _identity_tables(n):
    cos = jnp.concatenate([jnp.ones((n, QK_NOPE + QK_ROPE), F32),
                           jnp.zeros((n, HEAD_PAD - QK_NOPE - QK_ROPE), F32)], axis=-1)
    return cos, jnp.zeros((n, HEAD_PAD), F32)


def _rotate_cols(w):
    i = np.arange(QK_ROPE)
    first = (i % (QK_ROPE // 2)) < (QK_ROPE // 4)
    perm = np.where(first, i + QK_ROPE // 4, i - QK_ROPE // 4)
    sign = np.where(first, -1.0, 1.0).astype(np.float32)
    return w[:, perm] * sign


def _mla_weights(g_cq, w_uq, g_ckv, w_ukv):
    dk = QK_NOPE + QK_ROPE
    wq = w_uq.reshape(Q_LORA, MLA_HEADS, dk)
    pad = jnp.zeros((Q_LORA, MLA_HEADS, HEAD_PAD - dk), F32)
    zero_nope = jnp.zeros((Q_LORA, MLA_HEADS, QK_NOPE), F32)
    wq_rot = _rotate_cols(wq[..., QK_NOPE:].reshape(Q_LORA * MLA_HEADS, QK_ROPE)).reshape(
        Q_LORA, MLA_HEADS, QK_ROPE)
    wq_a = jnp.concatenate([wq, pad], axis=-1).reshape(Q_LORA, -1)
    wq_b = jnp.concatenate([zero_nope, wq_rot, pad], axis=-1).reshape(Q_LORA, -1)
    wkv = w_ukv.reshape(KV_LORA, MLA_HEADS, QK_NOPE + V_HEAD)
    wk = jnp.concatenate([wkv[..., :QK_NOPE],
                          jnp.zeros((KV_LORA, MLA_HEADS, HEAD_PAD - QK_NOPE), F32)], axis=-1)
    wv = wkv[..., QK_NOPE:]
    eye = jnp.eye(QK_ROPE, dtype=F32)
    place = jnp.zeros((LANES, MLA_HEADS, HEAD_PAD), F32)
    place_a = place.at[:QK_ROPE, :, QK_NOPE:dk].set(jnp.broadcast_to(eye[:, None, :], (QK_ROPE, MLA_HEADS, QK_ROPE)))
    rot = _rotate_cols(eye)
    place_b = place.at[:QK_ROPE, :, QK_NOPE:dk].set(jnp.broadcast_to(rot[:, None, :], (QK_ROPE, MLA_HEADS, QK_ROPE)))
    return {
        "g_cq": g_cq.reshape(1, -1), "g_ckv": g_ckv.reshape(1, -1),
        "wq2t": jnp.concatenate([wq_a, wq_b], axis=-1).T.astype(BF16),
        "wk": wk.reshape(KV_LORA, -1).astype(BF16),
        "wvt": wv.reshape(KV_LORA, -1).T.astype(BF16),
        "pk": jnp.concatenate([place_a.reshape(LANES, -1), place_b.reshape(LANES, -1)], axis=-1).astype(BF16),
    }


def _pack_w_in(w):
    gates0 = MLA_IN + 2 * CONV_CH + FFT_CH + POOL_CH
    pad = jnp.zeros((D_MODEL, P_MLA_W - MLA_IN), BF16)
    rest = jnp.concatenate([w[:, :MLA_IN].astype(BF16), pad, w[:, MLA_IN:gates0].astype(BF16)], axis=-1)
    return w[:, gates0:].astype(BF16), rest


def _dft_tables(n):
    gw = FFT_CH // FFT_GROUPS

    def angles(rows, m):
        k = jnp.arange(m, dtype=I32)
        return (rows[:, None] * k[None, :] % m).astype(F32) * (2.0 * np.pi / m)

    ac = angles(jnp.arange(gw, dtype=I32), gw)
    cs = jnp.concatenate([jnp.cos(ac), jnp.sin(ac)], axis=-1).astype(BF16)
    ns = int(np.sqrt(n))
    assert ns * ns == n
    a_hi = angles(jnp.arange(ns, dtype=I32) * ns, n)
    a_lo = angles(jnp.arange(ns, dtype=I32), n)
    ch, sh, cl, sl = jnp.cos(a_hi)[:, None, :], jnp.sin(a_hi)[:, None, :], jnp.cos(a_lo)[None], jnp.sin(a_lo)[None]
    cos_n = (ch * cl - sh * sl).reshape(n, n).astype(BF16)
    nsin_n = (-(sh * cl + ch * sl)).reshape(n, n).astype(BF16)
    return cs, cos_n, nsin_n


def _mixers(p, nseq, n, wts, tables, conv_args, pool_args, dft, **kv):
    q4, k4, v = mla_prep(p, wts, tables, row0=0, col0=P_MLA, nseq=nseq, n=n, **kv)
    cv = conv_branch(p, *conv_args, row0=0, nseq=nseq, n=n)
    ff = fft_branch(p, *dft, row0=0, nseq=nseq, n=n)
    po = pool_branch(p, *pool_args, row0=0, nseq=nseq, n=n)
    return q4, k4, v, cv, ff, po


def kernel(x, c, ctx, c_ctx, w_mod, b_mod, g_norm1, g_norm2, w_in, g_cq, w_uq, g_ckv, w_ukv, w_o_mla, conv_w, conv_b, conv_ln_g, conv_ln_b, w_o_conv, w_o_fft, pool_w, pool_scale, w_o_pool, w_out, w_router, router_bias, w_exp_gate, w_exp_up, w_exp_down, w_sh_gate, w_sh_up, w_sh_down, g_final):
    x_lat = x.reshape(T_LAT, D_MODEL)
    x_ctx = ctx.reshape(T_CTX, D_MODEL)
    ctx_tile0 = 0
    cvec = jnp.concatenate([c, c_ctx[None, :], jnp.zeros((SUBLANES - BATCH - 1, D_MODEL), F32)], axis=0)
    mod_all = modulation_all(cvec, w_mod, b_mod).reshape(DEPTH, SUBLANES, 6, D_MODEL)
    rope_lat = _rope_tables(SEQ // GRID_W)
    rope_ctx = _identity_tables(CTX_LEN)
    dft_lat = _dft_tables(SEQ)
    dft_ctx = _dft_tables(CTX_LEN)
    lat_tiles = T_LAT // ROW_TILE
    all_tiles = T_ALL // ROW_TILE

    for l in range(DEPTH):
        last = l == DEPTH - 1
        mod = mod_all[l]
        w_gate, w_rest = _pack_w_in(w_in[l])
        wts = _mla_weights(g_cq[l], w_uq[l], g_ckv[l], w_ukv[l])
        conv_args = (conv_w[l], conv_b[l], conv_ln_g[l], conv_ln_b[l])
        pool_args = (pool_w[l].astype(BF16), pool_scale[l])
        wo4 = jnp.stack([w_o_mla[l], w_o_conv[l], w_o_fft[l], w_o_pool[l]], axis=0).astype(BF16)
        w_out_b = w_out[l].astype(BF16)

        ctx_col0, ctx_cols = (P_MLA, P_MLA_W) if last else (0, P_COLS)
        p = normproj(x_lat, g_norm1[l], mod, w_gate, w_rest, tile0=0, ntiles=lat_tiles, col0=0, ncols=P_COLS)
        p_ctx = normproj(x_ctx, g_norm1[l], mod, w_gate, w_rest, tile0=ctx_tile0, ntiles=all_tiles - lat_tiles,
                         mod_tile0=lat_tiles, col0=ctx_col0, ncols=ctx_cols)
        nkeys = SEQ + CTX_LEN
        q4, k_all, v_all, cv, ff, po = _mixers(p, BATCH, SEQ, wts, rope_lat, conv_args, pool_args, dft_lat,
                                               n_keys=nkeys)
        if last:
            qc, k_all, v_all = mla_prep(p_ctx, wts, rope_ctx, row0=0, col0=0, nseq=BATCH, n=CTX_LEN,
                                        n_keys=nkeys, key0=SEQ, into=(k_all, v_all))
        else:
            qc, k_all, v_all, cvc, ffc, poc = _mixers(p_ctx, BATCH, CTX_LEN, wts, rope_ctx, conv_args,
                                                      pool_args, dft_ctx, n_keys=nkeys, key0=SEQ,
                                                      into=(k_all, v_all))
        at = attention(q4, k_all, v_all, nq=SEQ, nk=nkeys)
        lat_t, ctx_t = T_LAT // 256, T_CTX // 256
        if last:
            ntok = T_LAT
            xa = merge(x_lat, mod, at, cv, ff, po, p, wo4, w_out_b, ntiles=lat_t)
        else:
            ntok = T_ALL
            atc = attention(qc, k_all, v_all, nq=CTX_LEN, nk=CTX_LEN, key0=SEQ)
            xa = merge(x_lat, mod, at, cv, ff, po, p, wo4, w_out_b, ntiles=lat_t, out_tiles=lat_t + ctx_t)
            xa = merge(x_ctx, mod, atc, cvc, ffc, poc, p_ctx, wo4, w_out_b, ntiles=ctx_t, out_tile0=lat_t,
                       out_tiles=lat_t + ctx_t, into=xa)

        wr_t = w_router[l].T
        wr_hi = wr_t.astype(BF16)
        wr_lo = (wr_t - wr_hi.astype(F32)).astype(BF16)
        tokens, eidx, wk, rank, cnt = norm_router(xa, g_norm2[l], mod, wr_hi, wr_lo, router_bias[l],
                                                  ntiles=ntok // ROUTER_TILE)
        dest, zends, blk_e, nsub, next_e, nused = _routing_plan(eidx, rank, cnt[:, ::LANES])
        xs, sh = dispatch_shared(tokens, _tile_major(dest, 256), zends, w_sh_gate[l].astype(BF16),
                                 w_sh_up[l].astype(BF16), w_sh_down[l].astype(BF16), ntiles=ntok // 256)
        ys = experts(xs, blk_e, nsub, next_e, nused, w_exp_gate, w_exp_up, w_exp_down, layer=l)
        xa = combine(xa, sh, mod, wk.T, g_final, _tile_major(dest, 128), ys, ntiles=ntok // 128, final=last)
        x_lat, x_ctx, ctx_tile0 = xa, xa, lat_tiles
    return xa.reshape(BATCH, SEQ, D_MODEL)
```

```python
import functools

import jax
import jax.numpy as jnp
import numpy as np
from jax import lax
from jax.experimental import pallas as pl
from jax.experimental.pallas import tpu as pltpu

F32 = jnp.float32
BF16 = jnp.bfloat16
I32 = jnp.int32

D_MODEL = 2048
BATCH = 4
SEQ = 4096
DEPTH = 2
GRID_W = 64
CTX_LEN = 256
MLA_HEADS = 8
QK_NOPE = 64
QK_ROPE = 32
V_HEAD = 64
Q_LORA = 512
KV_LORA = 256
ROPE_BASE = 10000.0
CONV_CH = 512
CONV_WIDTH = 31
FFT_CH = 512
FFT_GROUPS = 4
POOL_CH = 512
POOL_WINDOWS = (2, 4, 8, 16)
POOL_GROUP = POOL_CH // len(POOL_WINDOWS)
N_BRANCHES = 4
N_EXPERTS = 64
N_EXPERT_GROUPS = 8
TOPK_GROUPS = 4
TOP_K = 8
EXPERT_FF = 512
SHARED_FF = 512
ROUTED_SCALE = 2.5
EPS = 1e-6

MLA_IN = Q_LORA + KV_LORA + QK_ROPE
T_LAT = BATCH * SEQ
T_CTX = BATCH * CTX_LEN
T_ALL = T_LAT + T_CTX

LANES = 128
SUBLANES = 8
VMEM_LIMIT_BYTES = 56 * 1024 * 1024

P_GATE = 0
P_MLA = N_BRANCHES * D_MODEL
P_MLA_W = 1024
P_CONV = P_MLA + P_MLA_W
P_FFT = P_CONV + 2 * CONV_CH
P_POOL = P_FFT + FFT_CH
P_COLS = P_POOL + POOL_CH
HEAD_PAD = 128
ATT_SCALE = (QK_NOPE + QK_ROPE) ** -0.5
Q_SCALE = ATT_SCALE * float(np.log2(np.e))

ROW_TILE = 1024
ROUTER_TILE = 256
EXPERT_ROWS = 256
EXPERT_SUBS = 4
STEP_ROWS = EXPERT_ROWS * EXPERT_SUBS
N_BLOCKS = -(-(T_ALL * TOP_K) // STEP_ROWS) + N_EXPERTS


def _cparams(sem, vmem=VMEM_LIMIT_BYTES):
    return pltpu.CompilerParams(dimension_semantics=sem, vmem_limit_bytes=vmem)


def _sigmoid(x):
    return 1.0 / (1.0 + jnp.exp(-x))


def _silu(x):
    return x * _sigmoid(x)


def _mod_row(tile, tiles_per_seq):
    return jnp.minimum(tile // tiles_per_seq, BATCH)


def _modulation_kernel(c_ref, w_ref, b_ref, o_ref):
    c = c_ref[...]
    s = _silu(c).astype(BF16)
    o_ref[0] = jnp.dot(s, w_ref[0].astype(BF16), preferred_element_type=F32) + b_ref[0]


def modulation_all(cvec, w_mod, b_mod):
    n = 6 * D_MODEL
    tn = 2048 if n % 2048 == 0 else 1024
    return pl.pallas_call(
        _modulation_kernel,
        out_shape=jax.ShapeDtypeStruct((DEPTH, SUBLANES, n), F32),
        grid=(DEPTH, n // tn),
        in_specs=[
            pl.BlockSpec((SUBLANES, D_MODEL), lambda l, j: (0, 0)),
            pl.BlockSpec((1, D_MODEL, tn), lambda l, j: (l, 0, j)),
            pl.BlockSpec((1, 1, tn), lambda l, j: (l, 0, j)),
        ],
        out_specs=pl.BlockSpec((1, SUBLANES, tn), lambda l, j: (l, 0, j)),
        compiler_params=_cparams(("arbitrary", "arbitrary")),
        name="modulation",
    )(cvec, w_mod, b_mod.reshape(DEPTH, 1, n))


def _adaln(x, g, shift, scale):
    y = x * lax.rsqrt(jnp.mean(x * x, axis=-1, keepdims=True) + EPS)
    return (y * g) * (1.0 + scale) + shift


def _normproj_kernel(x_ref, g_ref, mod_ref, wg_ref, wr_ref, o_ref, h_ref, *, shift_row, chunk, cb0, ngate):
    j = pl.program_id(1)

    @pl.when(j == 0)
    def _():
        shift = mod_ref[0, shift_row:shift_row + 1, :]
        scale = mod_ref[0, shift_row + 1:shift_row + 2, :]
        for r in range(0, x_ref.shape[0], chunk):
            h = _adaln(x_ref[r:r + chunk, :], g_ref[...], shift, scale)
            h_ref[r:r + chunk, :] = h.astype(BF16)

    @pl.when(j + cb0 < ngate)
    def _():
        o_ref[...] = jnp.dot(h_ref[...], wg_ref[...], preferred_element_type=F32).astype(o_ref.dtype)

    @pl.when(j + cb0 >= ngate)
    def _():
        o_ref[...] = jnp.dot(h_ref[...], wr_ref[...], preferred_element_type=F32).astype(o_ref.dtype)


def normproj(x, g, mod, w_gate, w_rest, *, tile0, ntiles, col0, ncols, mod_tile0=None, shift_row=0):
    tm, tn = ROW_TILE, 1024
    tps = SEQ // tm
    cb0 = col0 // tn
    ngate = w_gate.shape[1] // tn
    mt0 = tile0 if mod_tile0 is None else mod_tile0
    return pl.pallas_call(
        functools.partial(_normproj_kernel, shift_row=shift_row, chunk=256, cb0=cb0, ngate=ngate),
        out_shape=jax.ShapeDtypeStruct((ntiles * tm, ncols), BF16),
        grid=(ntiles, ncols // tn),
        in_specs=[
            pl.BlockSpec((tm, D_MODEL), lambda i, j: (i + tile0, 0)),
            pl.BlockSpec((1, D_MODEL), lambda i, j: (0, 0)),
            pl.BlockSpec((1, 6, D_MODEL), lambda i, j: (_mod_row(i + mt0, tps), 0, 0)),
            pl.BlockSpec((D_MODEL, tn), lambda i, j: (0, jnp.minimum(j + cb0, ngate - 1))),
            pl.BlockSpec((D_MODEL, tn), lambda i, j: (0, jnp.maximum(j + cb0 - ngate, 0))),
        ],
        out_specs=pl.BlockSpec((tm, tn), lambda i, j: (i, j)),
        scratch_shapes=[pltpu.VMEM((tm, D_MODEL), BF16)],
        compiler_params=_cparams(("arbitrary", "arbitrary")),
        name="normproj",
    )(x, g.reshape(1, D_MODEL), mod, w_gate, w_rest)


def _rms_gain(x, g):
    y = x * lax.rsqrt(jnp.mean(x * x, axis=-1, keepdims=True) + EPS)
    return (y * g).astype(BF16)


_NT = (((1,), (1,)), ((), ()))


def _mla_prep_kernel(p_ref, gq_ref, gkv_ref, wqt_ref, wk_ref, wvt_ref, pk_ref, cos_ref, sin_ref,
                     cost_ref, sint_ref, qt_ref, k_ref, vt_ref):
    hw = MLA_HEADS * HEAD_PAD
    cqn = _rms_gain(p_ref[:, :Q_LORA].astype(F32), gq_ref[...])
    ckvn = _rms_gain(p_ref[:, Q_LORA:Q_LORA + KV_LORA].astype(F32), gkv_ref[...])
    kr = p_ref[:, Q_LORA + KV_LORA:Q_LORA + KV_LORA + LANES]
    q2t = lax.dot_general(wqt_ref[...], cqn, _NT, preferred_element_type=F32)
    vt_ref[0] = lax.dot_general(wvt_ref[...], ckvn, _NT, preferred_element_type=F32).astype(BF16)
    kk = jnp.dot(ckvn, wk_ref[...], preferred_element_type=F32)
    kr2 = jnp.dot(kr, pk_ref[...], preferred_element_type=F32)
    cos, sin = cos_ref[...], sin_ref[...]
    cost, sint = cost_ref[...], sint_ref[...]
    for h in range(MLA_HEADS):
        a = slice(h * HEAD_PAD, (h + 1) * HEAD_PAD)
        b = slice(hw + h * HEAD_PAD, hw + (h + 1) * HEAD_PAD)
        qt_ref[0, h] = ((q2t[a, :] * cost + q2t[b, :] * sint) * Q_SCALE).astype(BF16)
        k_ref[0, h] = (kk[:, a] + kr2[:, a] * cos + kr2[:, b] * sin).astype(BF16)


def _mla_prep_into_kernel(*refs):
    _mla_prep_kernel(*refs[:11], *refs[13:])


def mla_prep(p, wts, tables, *, row0, col0, nseq, n, n_keys=None, key0=0, into=None):
    tm = 256
    tps = n // tm
    hw = MLA_HEADS * HEAD_PAD
    vw = MLA_HEADS * V_HEAD
    blk0 = row0 // tm
    cblk = col0 // P_MLA_W
    n_keys = n if n_keys is None else n_keys
    kb0 = key0 // tm
    cos_t, sin_t = tables
    const = lambda i: (0, 0)
    in_specs = [
        pl.BlockSpec((tm, P_MLA_W), lambda i: (i + blk0, cblk)),
        pl.BlockSpec((1, Q_LORA), const),
        pl.BlockSpec((1, KV_LORA), const),
        pl.BlockSpec((2 * hw, Q_LORA), const),
        pl.BlockSpec((KV_LORA, hw), const),
        pl.BlockSpec((vw, KV_LORA), const),
        pl.BlockSpec((LANES, 2 * hw), const),
        pl.BlockSpec((tm, HEAD_PAD), lambda i: (i % tps, 0)),
        pl.BlockSpec((tm, HEAD_PAD), lambda i: (i % tps, 0)),
        pl.BlockSpec((HEAD_PAD, tm), lambda i: (0, i % tps)),
        pl.BlockSpec((HEAD_PAD, tm), lambda i: (0, i % tps)),
    ]
    operands = [p, wts["g_cq"], wts["g_ckv"], wts["wq2t"], wts["wk"], wts["wvt"], wts["pk"],
                cos_t, sin_t, cos_t.T, sin_t.T]
    kernel_fn, aliases = _mla_prep_kernel, {}
    if into is not None:
        in_specs += [pl.BlockSpec(memory_space=pl.ANY), pl.BlockSpec(memory_space=pl.ANY)]
        operands += list(into)
        kernel_fn, aliases = _mla_prep_into_kernel, {11: 1, 12: 2}
    return pl.pallas_call(
        kernel_fn,
        out_shape=(jax.ShapeDtypeStruct((nseq, MLA_HEADS, HEAD_PAD, n), BF16),
                   jax.ShapeDtypeStruct((nseq, MLA_HEADS, n_keys, HEAD_PAD), BF16),
                   jax.ShapeDtypeStruct((nseq, vw, n_keys), BF16)),
        grid=(nseq * tps,),
        in_specs=in_specs,
        out_specs=(pl.BlockSpec((1, MLA_HEADS, HEAD_PAD, tm), lambda i: (i // tps, 0, 0, i % tps)),
                   pl.BlockSpec((1, MLA_HEADS, tm, HEAD_PAD), lambda i: (i // tps, 0, i % tps + kb0, 0)),
                   pl.BlockSpec((1, vw, tm), lambda i: (i // tps, 0, i % tps + kb0))),
        input_output_aliases=aliases,
        compiler_params=_cparams(("arbitrary",)),
        name="mla_prep",
    )(*operands)


ATT_HEADS = 8


def _attn_kernel(qt_ref, k_ref, vt_ref, o_ref, s_ref, p_ref, *, nk, ck):
    tq = qt_ref.shape[3]
    chunks = [slice(c, c + ck) for c in range(0, nk, ck)]
    nh = ATT_HEADS
    m = [None] * nh
    l = [jnp.zeros((1, tq), F32) for _ in range(nh)]
    ot = [jnp.zeros((V_HEAD, tq), F32) for _ in range(nh)]
    for step in range(nh + 2):
        ha, hb, hc = step, step - 1, step - 2
        for c in chunks:
            if 0 <= ha < nh:
                s = jnp.dot(k_ref[0, ha, c, :], qt_ref[0, ha], preferred_element_type=F32)
                s_ref[ha % 2, c, :] = s
                mc = jnp.max(s, axis=0, keepdims=True)
                m[ha] = mc if m[ha] is None else jnp.maximum(m[ha], mc)
            if 0 <= hb < nh:
                p = jnp.exp2(s_ref[hb % 2, c, :] - m[hb])
                l[hb] = l[hb] + jnp.sum(p, axis=0, keepdims=True)
                p_ref[hb % 2, c, :] = p.astype(BF16)
            if 0 <= hc < nh:
                ot[hc] = ot[hc] + jnp.dot(vt_ref[0, hc * V_HEAD:(hc + 1) * V_HEAD, c], p_ref[hc % 2, c, :],
                                          preferred_element_type=F32)
    outs = [ot[h] / l[h] for h in range(nh)]
    o_ref[...] = jnp.concatenate(outs, axis=0).T.astype(o_ref.dtype)


def attention(qt4, k4, vt3, *, nq, nk, key0=0):
    tq = 256
    ck = 256
    nqt = nq // tq
    hs = ATT_HEADS
    kb = key0 // nk
    return pl.pallas_call(
        functools.partial(_attn_kernel, nk=nk, ck=ck),
        out_shape=jax.ShapeDtypeStruct((BATCH * nq, MLA_HEADS * V_HEAD), BF16),
        grid=(BATCH, MLA_HEADS // hs, nqt),
        in_specs=[
            pl.BlockSpec((1, hs, HEAD_PAD, tq), lambda b, hg, i: (b, hg, 0, i)),
            pl.BlockSpec((1, hs, nk, HEAD_PAD), lambda b, hg, i: (b, hg, kb, 0)),
            pl.BlockSpec((1, hs * V_HEAD, nk), lambda b, hg, i: (b, hg, kb)),
        ],
        out_specs=pl.BlockSpec((tq, hs * V_HEAD), lambda b, hg, i: (b * nqt + i, hg)),
        scratch_shapes=[pltpu.VMEM((2, nk, tq), F32), pltpu.VMEM((2, nk, tq), BF16)],
        compiler_params=_cparams(("arbitrary", "arbitrary", "arbitrary")),
        name="attention",
    )(qt4, k4, vt3)


CONV_PAD = 16


def _conv_kernel(p_ref, w_ref, cb_ref, lg_ref, lb_ref, o_ref, u_ref, *, n, tr):
    zeros = jnp.zeros((CONV_PAD, CONV_CH), F32)
    u_ref[0:CONV_PAD, :] = zeros
    u_ref[CONV_PAD + n:2 * CONV_PAD + n, :] = zeros

    def glu(i, carry):
        r = pl.multiple_of(i * tr, tr)
        a = p_ref[pl.ds(r, tr), 0:CONV_CH].astype(F32)
        b = p_ref[pl.ds(r, tr), CONV_CH:2 * CONV_CH].astype(F32)
        u_ref[pl.ds(CONV_PAD + r, tr), :] = a * _sigmoid(b)
        return carry

    lax.fori_loop(0, n // tr, glu, 0)
    off = CONV_PAD - CONV_WIDTH // 2
    win_rows = tr + 2 * CONV_PAD

    def conv(i, carry):
        r = pl.multiple_of(i * tr, tr)
        strips = []
        for c0 in range(0, CONV_CH, LANES):
            win = u_ref[pl.ds(r, win_rows), c0:c0 + LANES]
            acc = jnp.zeros((tr, LANES), F32)
            for b in range(SUBLANES):
                wb = pltpu.roll(win, win_rows - (off + b), axis=0)
                for k in range(b, CONV_WIDTH, SUBLANES):
                    acc = acc + wb[k - b:k - b + tr, :] * w_ref[k:k + 1, c0:c0 + LANES]
            strips.append(acc)
        acc = jnp.concatenate(strips, axis=-1) + cb_ref[...]
        mu = jnp.mean(acc, axis=-1, keepdims=True)
        d = acc - mu
        var = jnp.mean(d * d, axis=-1, keepdims=True)
        y = (d * lax.rsqrt(var + EPS)) * lg_ref[...] + lb_ref[...]
        o_ref[pl.ds(r, tr), :] = _silu(y).astype(o_ref.dtype)
        return carry

    lax.fori_loop(0, n // tr, conv, 0)


def conv_branch(p, conv_w, conv_b, ln_g, ln_b, *, row0, nseq, n):
    tr = 128
    blk0 = row0 // n
    const = lambda s: (0, 0)
    return pl.pallas_call(
        functools.partial(_conv_kernel, n=n, tr=tr),
        out_shape=jax.ShapeDtypeStruct((nseq * n, CONV_CH), BF16),
        grid=(nseq,),
        in_specs=[
            pl.BlockSpec((n, 2 * CONV_CH), lambda s: (s + blk0, P_CONV // (2 * CONV_CH))),
            pl.BlockSpec((CONV_WIDTH, CONV_CH), const),
            pl.BlockSpec((1, CONV_CH), const),
            pl.BlockSpec((1, CONV_CH), const),
            pl.BlockSpec((1, CONV_CH), const),
        ],
        out_specs=pl.BlockSpec((n, CONV_CH), lambda s: (s, 0)),
        scratch_shapes=[pltpu.VMEM((n + 2 * CONV_PAD, CONV_CH), F32)],
        compiler_params=_cparams(("arbitrary",)),
        name="conv_branch",
    )(p, conv_w, conv_b.reshape(1, -1), ln_g.reshape(1, -1), ln_b.reshape(1, -1))


def _fft_kernel(z_ref, cs_ref, mc_ref, ms_ref, o_ref, ab_ref, *, n, tc):
    gw = FFT_CH // FFT_GROUPS

    @pl.when(pl.program_id(1) == 0)
    def _():
        def chan(i, carry):
            r = pl.multiple_of(i * tc, tc)
            for g in range(FFT_GROUPS):
                zg = z_ref[pl.ds(r, tc), g * gw:(g + 1) * gw]
                ab = jnp.dot(zg, cs_ref[...], preferred_element_type=F32)
                ab_ref[pl.ds(r, tc), g * gw:(g + 1) * gw] = ab[:, :gw].astype(BF16)
                ab_ref[pl.ds(n + r, tc), g * gw:(g + 1) * gw] = ab[:, gw:].astype(BF16)
            return carry

        lax.fori_loop(0, n // tc, chan, 0)

    norm = 1.0 / float(np.sqrt(n * gw))
    re = (jnp.dot(mc_ref[...], ab_ref[0:n, :], preferred_element_type=F32)
          + jnp.dot(ms_ref[...], ab_ref[n:2 * n, :], preferred_element_type=F32))
    o_ref[...] = (re * norm).astype(o_ref.dtype)


def fft_branch(p, cs, mcos, mnsin, *, row0, nseq, n):
    tm = min(n, 512)
    tc = min(n, 512)
    blk0 = row0 // n
    nt = n // tm
    return pl.pallas_call(
        functools.partial(_fft_kernel, n=n, tc=tc),
        out_shape=jax.ShapeDtypeStruct((nseq * n, FFT_CH), BF16),
        grid=(nseq, nt),
        in_specs=[
            pl.BlockSpec((n, FFT_CH), lambda s, i: (s + blk0, P_FFT // FFT_CH)),
            pl.BlockSpec((FFT_CH // FFT_GROUPS, 2 * FFT_CH // FFT_GROUPS), lambda s, i: (0, 0)),
            pl.BlockSpec((tm, n), lambda s, i: (i, 0)),
            pl.BlockSpec((tm, n), lambda s, i: (i, 0)),
        ],
        out_specs=pl.BlockSpec((tm, FFT_CH), lambda s, i: (s * nt + i, 0)),
        scratch_shapes=[pltpu.VMEM((2 * n, FFT_CH), BF16)],
        compiler_params=_cparams(("arbitrary", "arbitrary")),
        name="fft_branch",
    )(p, cs, mcos, mnsin)


POOL_PAD = 16


def _pool_kernel(z_ref, pw_ref, ps_ref, o_ref, zp_ref, d_ref, *, n, tr, tc):
    zeros = jnp.zeros((POOL_PAD, POOL_CH), F32)
    zp_ref[0:POOL_PAD, :] = zeros
    zp_ref[POOL_PAD + n:2 * POOL_PAD + n, :] = zeros

    def fill(i, carry):
        r = pl.multiple_of(i * tc, tc)
        zp_ref[pl.ds(POOL_PAD + r, tc), :] = z_ref[pl.ds(r, tc), :].astype(F32)
        return carry

    lax.fori_loop(0, n // tc, fill, 0)

    win_rows = tr + 2 * POOL_PAD

    def pool(i, carry):
        r = pl.multiple_of(i * tr, tr)
        t = (r + lax.broadcasted_iota(I32, (tr, POOL_GROUP), 0)).astype(F32)
        for gi, w in enumerate(POOL_WINDOWS):
            cols = slice(gi * POOL_GROUP, (gi + 1) * POOL_GROUP)
            win = zp_ref[pl.ds(r, win_rows), cols]
            s = jnp.zeros((tr, POOL_GROUP), F32)
            for j in range(w):
                start = POOL_PAD - w // 2 + j
                if start % SUBLANES == 0:
                    s = s + win[start:start + tr, :]
                else:
                    s = s + pltpu.roll(win, win_rows - start, axis=0)[0:tr, :]
            lo = jnp.maximum(t - (w // 2), 0.0)
            hi = jnp.minimum(t - (w // 2) + w, float(n))
            z = win[POOL_PAD:POOL_PAD + tr, :]
            d_ref[pl.ds(r, tr), cols] = (s / (hi - lo) - z).astype(BF16)
        return carry

    lax.fori_loop(0, n // tr, pool, 0)

    def proj(i, carry):
        r = pl.multiple_of(i * tc, tc)
        for gi in range(len(POOL_WINDOWS)):
            cols = slice(gi * POOL_GROUP, (gi + 1) * POOL_GROUP)
            y = jnp.dot(d_ref[pl.ds(r, tc), cols], pw_ref[gi], preferred_element_type=F32)
            o_ref[pl.ds(r, tc), cols] = (y * ps_ref[:, cols]).astype(o_ref.dtype)
        return carry

    lax.fori_loop(0, n // tc, proj, 0)


def pool_branch(p, pool_w, pool_scale, *, row0, nseq, n):
    tr = 128
    tc = min(n, 512)
    blk0 = row0 // n
    return pl.pallas_call(
        functools.partial(_pool_kernel, n=n, tr=tr, tc=tc),
        out_shape=jax.ShapeDtypeStruct((nseq * n, POOL_CH), BF16),
        grid=(nseq,),
        in_specs=[
            pl.BlockSpec((n, POOL_CH), lambda s: (s + blk0, P_POOL // POOL_CH)),
            pl.BlockSpec((len(POOL_WINDOWS), POOL_GROUP, POOL_GROUP), lambda s: (0, 0, 0)),
            pl.BlockSpec((1, POOL_CH), lambda s: (0, 0)),
        ],
        out_specs=pl.BlockSpec((n, POOL_CH), lambda s: (s, 0)),
        scratch_shapes=[pltpu.VMEM((n + 2 * POOL_PAD, POOL_CH), F32),
                        pltpu.VMEM((n, POOL_CH), BF16)],
        compiler_params=_cparams(("arbitrary",)),
        name="pool_branch",
    )(p, pool_w, pool_scale.reshape(1, -1))


def _merge_kernel(x_ref, mod_ref, at_ref, cv_ref, ff_ref, po_ref, g_ref, wo_ref, wout_ref, o_ref):
    mix = None
    for b, br in enumerate((at_ref, cv_ref, ff_ref, po_ref)):
        y = jnp.dot(br[...], wo_ref[b], preferred_element_type=F32)
        gate = _sigmoid(g_ref[:, b * D_MODEL:(b + 1) * D_MODEL].astype(F32))
        mix = gate * y if mix is None else mix + gate * y
    out = jnp.dot(mix.astype(BF16), wout_ref[...], preferred_element_type=F32)
    o_ref[...] = x_ref[...] + mod_ref[0, 2:3, :] * out


def _merge_into_kernel(x_ref, mod_ref, at_ref, cv_ref, ff_ref, po_ref, g_ref, wo_ref, wout_ref, prev_ref, o_ref):
    del prev_ref
    _merge_kernel(x_ref, mod_ref, at_ref, cv_ref, ff_ref, po_ref, g_ref, wo_ref, wout_ref, o_ref)


def merge(x, mod, attn, conv, fft, pool, p, wo4, w_out, *, ntiles, out_tile0=0, out_tiles=None, into=None):
    tm = 256
    tps = SEQ // tm
    out_tiles = ntiles if out_tiles is None else out_tiles
    bw = MLA_HEADS * V_HEAD
    assert bw == CONV_CH == FFT_CH == POOL_CH
    br_spec = pl.BlockSpec((tm, bw), lambda i: (i, 0))
    in_specs = [
        pl.BlockSpec((tm, D_MODEL), lambda i: (i, 0)),
        pl.BlockSpec((1, 6, D_MODEL), lambda i: (_mod_row(i + out_tile0, tps), 0, 0)),
        br_spec, br_spec, br_spec, br_spec,
        pl.BlockSpec((tm, N_BRANCHES * D_MODEL), lambda i: (i, P_GATE // (N_BRANCHES * D_MODEL))),
        pl.BlockSpec((N_BRANCHES, bw, D_MODEL), lambda i: (0, 0, 0), pipeline_mode=pl.Buffered(1)),
        pl.BlockSpec((D_MODEL, D_MODEL), lambda i: (0, 0), pipeline_mode=pl.Buffered(1)),
    ]
    operands = [x, mod, attn, conv, fft, pool, p, wo4, w_out]
    kernel_fn, aliases = _merge_kernel, {}
    if into is not None:
        in_specs.append(pl.BlockSpec(memory_space=pl.ANY))
        operands.append(into)
        kernel_fn, aliases = _merge_into_kernel, {len(operands) - 1: 0}
    return pl.pallas_call(
        kernel_fn,
        out_shape=jax.ShapeDtypeStruct((out_tiles * tm, D_MODEL), F32),
        grid=(ntiles,),
        in_specs=in_specs,
        out_specs=pl.BlockSpec((tm, D_MODEL), lambda i: (i + out_tile0, 0)),
        input_output_aliases=aliases,
        compiler_params=_cparams(("arbitrary",)),
        name="merge",
    )(*operands)


def _first_index(hit_value, cand, ids, big):
    return jnp.min(jnp.where(cand == hit_value, ids, big), axis=0, keepdims=True)


HALF = D_MODEL // 2
ROW_WORDS = HALF // LANES


def _round_bf16_bits(x):
    u = lax.bitcast_convert_type(x, I32)
    odd = lax.shift_right_logical(u, 16) & 1
    return (u + 0x7FFF + odd) & jnp.int32(-65536)


def _pack_words(lo, hi):
    return lax.shift_right_logical(_round_bf16_bits(lo), 16) | _round_bf16_bits(hi)


def _unpack_words(w):
    lo = lax.bitcast_convert_type(lax.shift_left(w, 16), F32)
    hi = lax.bitcast_convert_type(w & jnp.int32(-65536), F32)
    return lo, hi


def _store_packed(ref, x, tok0=0):
    tm = x.shape[0]
    for s in range(ROW_WORDS):
        w = _pack_words(x[:, s * LANES:(s + 1) * LANES], x[:, HALF + s * LANES:HALF + (s + 1) * LANES])
        ref[pl.ds(tok0 * ROW_WORDS + s, tm, stride=ROW_WORDS), :] = w


def _load_packed(ref, tm, dtype, tok0=0):
    los, his = [], []
    for s in range(ROW_WORDS):
        lo, hi = _unpack_words(ref[pl.ds(tok0 * ROW_WORDS + s, tm, stride=ROW_WORDS), :])
        los.append(lo.astype(dtype))
        his.append(hi.astype(dtype))
    return jnp.concatenate(los + his, axis=-1)


def _router_kernel(x_ref, g_ref, mod_ref, wrh_ref, wrl_ref, rb_ref, tri_ref,
                   tok_ref, eidx_ref, wk_ref, rank_ref, cnt_ref):
    tm = x_ref.shape[0]
    h = _adaln(x_ref[...], g_ref[...], mod_ref[0, 3:4, :], mod_ref[0, 4:5, :])
    _store_packed(tok_ref, h)
    hh = h.astype(BF16)
    hl = (h - hh.astype(F32)).astype(BF16)
    nt = (((1,), (1,)), ((), ()))
    logits = (lax.dot_general(wrh_ref[...], hh, nt, preferred_element_type=F32)
              + lax.dot_general(wrh_ref[...], hl, nt, preferred_element_type=F32)
              + lax.dot_general(wrl_ref[...], hh, nt, preferred_element_type=F32))
    scores = _sigmoid(logits)
    sel = scores + rb_ref[...]
    per = N_EXPERTS // N_EXPERT_GROUPS
    assert per == SUBLANES and N_EXPERT_GROUPS == SUBLANES and TOP_K == SUBLANES
    neg = -jnp.inf
    sub = lax.broadcasted_iota(I32, (SUBLANES, tm), 0).astype(F32)
    sg = [sel[g * per:(g + 1) * per, :] for g in range(N_EXPERT_GROUPS)]
    sc = [scores[g * per:(g + 1) * per, :] for g in range(N_EXPERT_GROUPS)]
    gsc = jnp.zeros((SUBLANES, tm), F32)
    for g in range(N_EXPERT_GROUPS):
        m1 = jnp.max(sg[g], axis=0, keepdims=True)
        i1 = _first_index(m1, sg[g], sub, float(per))
        m2 = jnp.max(jnp.where(sub == i1, neg, sg[g]), axis=0, keepdims=True)
        gsc = jnp.where(sub == float(g), m1 + m2, gsc)
    gsel = jnp.zeros((SUBLANES, tm), F32)
    for _ in range(TOPK_GROUPS):
        m = jnp.max(gsc, axis=0, keepdims=True)
        hit = sub == _first_index(m, gsc, sub, float(N_EXPERT_GROUPS))
        gsel = jnp.where(hit, 1.0, gsel)
        gsc = jnp.where(hit, neg, gsc)
    cand = []
    for g in range(N_EXPERT_GROUPS):
        allowed = jnp.max(jnp.where(sub == float(g), gsel, 0.0), axis=0, keepdims=True)
        cand.append(jnp.where(allowed > 0.0, sg[g], neg))
    eid = [sub + float(g * per) for g in range(N_EXPERT_GROUPS)]
    idxs = jnp.zeros((SUBLANES, tm), F32)
    vals = jnp.zeros((SUBLANES, tm), F32)
    picked = [jnp.zeros((per, tm), F32) for _ in range(N_EXPERT_GROUPS)]
    idx_k = []
    for k in range(TOP_K):
        m = functools.reduce(jnp.maximum, [jnp.max(c, axis=0, keepdims=True) for c in cand])
        idx = functools.reduce(
            jnp.minimum, [_first_index(m, cand[g], eid[g], float(N_EXPERTS)) for g in range(N_EXPERT_GROUPS)])
        val = jnp.zeros((1, tm), F32)
        for g in range(N_EXPERT_GROUPS):
            hit = eid[g] == idx
            val = val + jnp.sum(jnp.where(hit, sc[g], 0.0), axis=0, keepdims=True)
            cand[g] = jnp.where(hit, neg, cand[g])
            picked[g] = jnp.where(hit, 1.0, picked[g])
        idxs = jnp.where(sub == float(k), idx, idxs)
        vals = jnp.where(sub == float(k), val, vals)
        idx_k.append(idx)
    eidx_ref[...] = idxs.astype(I32)
    wk_ref[...] = vals / jnp.sum(vals, axis=0, keepdims=True) * ROUTED_SCALE
    mask = jnp.concatenate(picked, axis=0)
    before = jnp.dot(mask.astype(BF16), tri_ref[...], preferred_element_type=F32)
    ranks = jnp.zeros((SUBLANES, tm), F32)
    for k in range(TOP_K):
        r = jnp.zeros((1, tm), F32)
        for g in range(N_EXPERT_GROUPS):
            r = r + jnp.sum(jnp.where(eid[g] == idx_k[k], before[g * per:(g + 1) * per, :], 0.0),
                            axis=0, keepdims=True)
        ranks = jnp.where(sub == float(k), r, ranks)
    rank_ref[...] = ranks.astype(I32)
    cnt_ref[...] = jnp.broadcast_to(jnp.sum(mask, axis=1, keepdims=True), (N_EXPERTS, LANES)).astype(I32)


def norm_router(x, g, mod, wr_hi, wr_lo, rbias, *, ntiles):
    tm = ROUTER_TILE
    tps = SEQ // tm
    t = ntiles * tm
    tri = jnp.triu(jnp.ones((tm, tm), BF16), k=1)
    kt_spec = pl.BlockSpec((TOP_K, tm), lambda i: (0, i))
    return pl.pallas_call(
        _router_kernel,
        out_shape=(jax.ShapeDtypeStruct((t * ROW_WORDS, LANES), I32),
                   jax.ShapeDtypeStruct((TOP_K, t), I32),
                   jax.ShapeDtypeStruct((TOP_K, t), F32),
                   jax.ShapeDtypeStruct((TOP_K, t), I32),
                   jax.ShapeDtypeStruct((N_EXPERTS, ntiles * LANES), I32)),
        grid=(ntiles,),
        in_specs=[
            pl.BlockSpec((tm, D_MODEL), lambda i: (i, 0)),
            pl.BlockSpec((1, D_MODEL), lambda i: (0, 0)),
            pl.BlockSpec((1, 6, D_MODEL), lambda i: (_mod_row(i, tps), 0, 0)),
            pl.BlockSpec((N_EXPERTS, D_MODEL), lambda i: (0, 0)),
            pl.BlockSpec((N_EXPERTS, D_MODEL), lambda i: (0, 0)),
            pl.BlockSpec((N_EXPERTS, 1), lambda i: (0, 0)),
            pl.BlockSpec((tm, tm), lambda i: (0, 0)),
        ],
        out_specs=(pl.BlockSpec((tm * ROW_WORDS, LANES), lambda i: (i, 0)),
                   kt_spec, kt_spec, kt_spec,
                   pl.BlockSpec((N_EXPERTS, LANES), lambda i: (0, i))),
        compiler_params=_cparams(("arbitrary",)),
        name="norm_router",
    )(x, g.reshape(1, D_MODEL), mod, wr_hi, wr_lo, rbias.reshape(N_EXPERTS, 1), tri)


def _row_copy(src, src_tok, dst, dst_tok, sem):
    s = pl.multiple_of(src_tok * ROW_WORDS, ROW_WORDS)
    d = pl.multiple_of(dst_tok * ROW_WORDS, ROW_WORDS)
    return pltpu.make_async_copy(src.at[pl.ds(s, ROW_WORDS)], dst.at[pl.ds(d, ROW_WORDS)], sem)


def _dispatch_kernel(zends_ref, tok_ref, dest_hbm, wg_ref, wu_ref, wd_ref, xs_hbm, sh_ref,
                     dest_smem, zero_ref, sem_idx, sem_z, sem_s):
    i = pl.program_id(0)
    tm = tok_ref.shape[0] // ROW_WORDS
    idx_copy = pltpu.make_async_copy(dest_hbm.at[i], dest_smem, sem_idx)
    idx_copy.start()
    blk = EXPERT_ROWS * ROW_WORDS

    def pad_copy(e):
        row = pl.multiple_of((zends_ref[e] - EXPERT_ROWS) * ROW_WORDS, blk)
        return pltpu.make_async_copy(zero_ref, xs_hbm.at[pl.ds(row, blk)], sem_z)

    @pl.when(i == 0)
    def _():
        zero_ref[...] = jnp.zeros_like(zero_ref)

        def start(e, carry):
            @pl.when(zends_ref[e] > 0)
            def _():
                pad_copy(e).start()
            return carry

        lax.fori_loop(0, N_EXPERTS, start, 0)

        def wait(e, carry):
            @pl.when(zends_ref[e] > 0)
            def _():
                pad_copy(e).wait()
            return carry

        lax.fori_loop(0, N_EXPERTS, wait, 0)

    idx_copy.wait()

    x = _load_packed(tok_ref, tm, BF16)
    pieces = SHARED_FF // LANES
    per = tm // pieces
    acc = None
    for p in range(pieces):
        for r in range(p * per, (p + 1) * per):
            for k in range(TOP_K):
                _row_copy(tok_ref, r, xs_hbm, dest_smem[k * tm + r], sem_s).start(priority=k % 2)
        c = slice(p * LANES, (p + 1) * LANES)
        g = jnp.dot(x, wg_ref[:, c], preferred_element_type=F32)
        u = jnp.dot(x, wu_ref[:, c], preferred_element_type=F32)
        part = jnp.dot((_silu(g) * u).astype(BF16), wd_ref[c, :], preferred_element_type=F32)
        acc = part if acc is None else acc + part
    sh_ref[...] = acc

    def drain(r, carry):
        for k in range(TOP_K):
            _row_copy(tok_ref, r, xs_hbm, dest_smem[k * tm + r], sem_s).wait()
        return carry

    lax.fori_loop(0, tm, drain, 0)


def dispatch_shared(tokens, dest_tiles, zends, wg, wu, wd, *, ntiles):
    tm = tokens.shape[0] // ROW_WORDS // ntiles
    const = lambda i, zends: (0, 0)
    return pl.pallas_call(
        _dispatch_kernel,
        out_shape=(jax.ShapeDtypeStruct((N_BLOCKS * STEP_ROWS * ROW_WORDS, LANES), I32),
                   jax.ShapeDtypeStruct((ntiles * tm, D_MODEL), F32)),
        grid_spec=pltpu.PrefetchScalarGridSpec(
            num_scalar_prefetch=1,
            grid=(ntiles,),
            in_specs=[
                pl.BlockSpec((tm * ROW_WORDS, LANES), lambda i, zends: (i, 0)),
                pl.BlockSpec(memory_space=pl.ANY),
                pl.BlockSpec((D_MODEL, SHARED_FF), const),
                pl.BlockSpec((D_MODEL, SHARED_FF), const),
                pl.BlockSpec((SHARED_FF, D_MODEL), const),
            ],
            out_specs=(pl.BlockSpec(memory_space=pl.ANY),
                       pl.BlockSpec((tm, D_MODEL), lambda i, zends: (i, 0))),
            scratch_shapes=[
                pltpu.SMEM((TOP_K * tm,), I32),
                pltpu.VMEM((EXPERT_ROWS * ROW_WORDS, LANES), I32),
                pltpu.SemaphoreType.DMA,
                pltpu.SemaphoreType.DMA,
                pltpu.SemaphoreType.DMA,
            ],
        ),
        compiler_params=_cparams(("arbitrary",)),
        name="moe_dispatch",
    )(zends, tokens, dest_tiles, wg, wu, wd)


def _experts_kernel(blk_e_ref, nsub_ref, next_e_ref, nused_ref, xs_ref, wg_hbm, wu_hbm, wd_hbm, y_ref,
                    wgs, wus, wds, wgb, wub, wdb, sem_w, *, layer):
    i = pl.program_id(0)
    e = blk_e_ref[i]
    prev = blk_e_ref[jnp.maximum(i - 1, 0)]

    def weight_copies(expert):
        row = layer * N_EXPERTS + expert
        return (pltpu.make_async_copy(wg_hbm.at[row], wgs, sem_w.at[0]),
                pltpu.make_async_copy(wu_hbm.at[row], wus, sem_w.at[1]),
                pltpu.make_async_copy(wd_hbm.at[row], wds, sem_w.at[2]))

    @pl.when(i < nused_ref[0])
    def _():
        @pl.when(i == 0)
        def _():
            for cp in weight_copies(e):
                cp.start()

        @pl.when((i == 0) | (e != prev))
        def _():
            for cp in weight_copies(e):
                cp.wait()
            wgb[...] = wgs[...].astype(BF16)
            wub[...] = wus[...].astype(BF16)
            wdb[...] = wds[...].astype(BF16)

            @pl.when(next_e_ref[i] >= 0)
            def _():
                for cp in weight_copies(next_e_ref[i]):
                    cp.start()

        def mlp(sb):
            tok0 = sb * EXPERT_ROWS
            x = _load_packed(xs_ref, EXPERT_ROWS, BF16, tok0)
            g = jnp.dot(x, wgb[...], preferred_element_type=F32)
            u = jnp.dot(x, wub[...], preferred_element_type=F32)
            hb = (_silu(g) * u).astype(BF16)
            _store_packed(y_ref, jnp.dot(hb, wdb[...], preferred_element_type=F32), tok0)

        def clear(sb):
            rows = EXPERT_ROWS * ROW_WORDS
            y_ref[sb * rows:(sb + 1) * rows, :] = jnp.zeros((rows, LANES), I32)

        nsub = nsub_ref[i]
        for a in range(0, EXPERT_SUBS, 2):
            @pl.when(nsub >= a + 2)
            def _():
                mlp(a)
                mlp(a + 1)

            @pl.when(nsub == a + 1)
            def _():
                mlp(a)
                clear(a + 1)

            @pl.when(nsub <= a)
            def _():
                clear(a)
                clear(a + 1)


def experts(xs, blk_e, nsub, next_e, nused, w_gate, w_up, w_down, *, layer):
    w_gate = w_gate.reshape(DEPTH * N_EXPERTS, D_MODEL, EXPERT_FF)
    w_up = w_up.reshape(DEPTH * N_EXPERTS, D_MODEL, EXPERT_FF)
    w_down = w_down.reshape(DEPTH * N_EXPERTS, EXPERT_FF, D_MODEL)

    def row_map(i, blk_e, nsub, next_e, nused):
        return (jnp.minimum(i, nused[0] - 1), 0)

    return pl.pallas_call(
        functools.partial(_experts_kernel, layer=layer),
        out_shape=jax.ShapeDtypeStruct((N_BLOCKS * STEP_ROWS * ROW_WORDS, LANES), I32),
        grid_spec=pltpu.PrefetchScalarGridSpec(
            num_scalar_prefetch=4,
            grid=(N_BLOCKS,),
            in_specs=[
                pl.BlockSpec((STEP_ROWS * ROW_WORDS, LANES), row_map),
                pl.BlockSpec(memory_space=pl.ANY),
                pl.BlockSpec(memory_space=pl.ANY),
                pl.BlockSpec(memory_space=pl.ANY),
            ],
            out_specs=pl.BlockSpec((STEP_ROWS * ROW_WORDS, LANES), row_map),
            scratch_shapes=[
                pltpu.VMEM((D_MODEL, EXPERT_FF), F32),
                pltpu.VMEM((D_MODEL, EXPERT_FF), F32),
                pltpu.VMEM((EXPERT_FF, D_MODEL), F32),
                pltpu.VMEM((D_MODEL, EXPERT_FF), BF16),
                pltpu.VMEM((D_MODEL, EXPERT_FF), BF16),
                pltpu.VMEM((EXPERT_FF, D_MODEL), BF16),
                pltpu.SemaphoreType.DMA((3,)),
            ],
        ),
        compiler_params=_cparams(("arbitrary",)),
        name="moe_experts",
    )(blk_e, nsub, next_e, nused, xs, w_gate, w_up, w_down)


def _combine_kernel(x_ref, sh_ref, mod_ref, wk_ref, gf_ref, dest_hbm, ys_hbm, o_ref,
                    idx0, idx1, buf0, buf1, sem_idx, sem_g, *, final):
    i = pl.program_id(0)
    n = pl.num_programs(0)
    tm = x_ref.shape[0]
    idx = (idx0, idx1)
    buf = (buf0, buf1)

    def fetch_idx(tile, slot):
        idx_copy = pltpu.make_async_copy(dest_hbm.at[tile], idx[slot], sem_idx)
        idx_copy.start()
        idx_copy.wait()

    def start_rows(slot, r):
        for k in range(TOP_K):
            _row_copy(ys_hbm, idx[slot][k * tm + r], buf[slot].at[k], r, sem_g.at[slot]).start(priority=k % 2)

    def drain(slot):
        def wait_rows(r, carry):
            for k in range(TOP_K):
                _row_copy(ys_hbm, 0, buf[slot].at[k], r, sem_g.at[slot]).wait()
            return carry

        lax.fori_loop(0, tm, wait_rows, 0)

    def finish(slot):
        fetch_idx(jnp.minimum(i + 1, n - 1), 1 - slot)
        drain(slot)
        per = tm // ROW_WORDS
        los, his = [], []
        for s in range(ROW_WORDS):
            for r in range(s * per, (s + 1) * per):
                start_rows(1 - slot, r)
            acc_lo = sh_ref[:, s * LANES:(s + 1) * LANES]
            acc_hi = sh_ref[:, HALF + s * LANES:HALF + (s + 1) * LANES]
            for k in range(TOP_K):
                lo, hi = _unpack_words(buf[slot][k, pl.ds(s, tm, stride=ROW_WORDS), :])
                w = wk_ref[:, k:k + 1]
                acc_lo = acc_lo + lo * w
                acc_hi = acc_hi + hi * w
            los.append(acc_lo)
            his.append(acc_hi)
        f = jnp.concatenate(los + his, axis=-1)
        out = x_ref[...] + mod_ref[0, 5:6, :] * f
        if final:
            y = out * lax.rsqrt(jnp.mean(out * out, axis=-1, keepdims=True) + EPS)
            out = y * gf_ref[...]
        o_ref[...] = out

        @pl.when(i == n - 1)
        def _():
            drain(1 - slot)

    @pl.when(i == 0)
    def _():
        fetch_idx(0, 0)

        def first(r, carry):
            start_rows(0, r)
            return carry

        lax.fori_loop(0, tm, first, 0)

    for slot in range(2):
        @pl.when(i % 2 == slot)
        def _(slot=slot):
            finish(slot)


def combine(x, shared, mod, wk_t, g_final, dest_tiles, ys, *, ntiles, final):
    tm = 128
    tps = SEQ // tm
    return pl.pallas_call(
        functools.partial(_combine_kernel, final=final),
        out_shape=jax.ShapeDtypeStruct((ntiles * tm, D_MODEL), F32),
        grid=(ntiles,),
        in_specs=[
            pl.BlockSpec((tm, D_MODEL), lambda i: (i, 0)),
            pl.BlockSpec((tm, D_MODEL), lambda i: (i, 0)),
            pl.BlockSpec((1, 6, D_MODEL), lambda i: (_mod_row(i, tps), 0, 0)),
            pl.BlockSpec((tm, TOP_K), lambda i: (i, 0)),
            pl.BlockSpec((1, D_MODEL), lambda i: (0, 0)),
            pl.BlockSpec(memory_space=pl.ANY),
            pl.BlockSpec(memory_space=pl.ANY),
        ],
        out_specs=pl.BlockSpec((tm, D_MODEL), lambda i: (i, 0)),
        scratch_shapes=[
            pltpu.SMEM((TOP_K * tm,), I32),
            pltpu.SMEM((TOP_K * tm,), I32),
            pltpu.VMEM((TOP_K, tm * ROW_WORDS, LANES), I32),
            pltpu.VMEM((TOP_K, tm * ROW_WORDS, LANES), I32),
            pltpu.SemaphoreType.DMA,
            pltpu.SemaphoreType.DMA((2,)),
        ],
        compiler_params=_cparams(("arbitrary",)),
        name="moe_combine",
    )(x, shared, mod, wk_t, g_final.reshape(1, D_MODEL), dest_tiles, ys)


def _routing_plan(eidx, rank, tile_counts):
    t = eidx.shape[1]
    tiles = tile_counts.shape[1]
    counts = jnp.sum(tile_counts, axis=1)
    pcounts = (counts + STEP_ROWS - 1) // STEP_ROWS * STEP_ROWS
    pends = jnp.cumsum(pcounts)
    pstarts = pends - pcounts
    base = pstarts[:, None] + jnp.cumsum(tile_counts, axis=1) - tile_counts
    onehot = (eidx.reshape(TOP_K, tiles, 1, t // tiles)
              == jnp.arange(N_EXPERTS, dtype=I32)[None, None, :, None])
    dest = rank + jnp.sum(jnp.where(onehot, base.T[None, :, :, None], 0), axis=2).reshape(TOP_K, t)
    nused = (pends[-1] // STEP_ROWS).astype(I32).reshape(1)
    blocks = jnp.arange(N_BLOCKS, dtype=I32)
    blk_e = jnp.minimum(jnp.sum((pends[None, :] <= (blocks * STEP_ROWS)[:, None]).astype(I32), axis=1),
                        N_EXPERTS - 1).astype(I32)
    nsub_e = (counts + EXPERT_ROWS - 1) // EXPERT_ROWS
    zends = jnp.where(counts > 0, pstarts + nsub_e * EXPERT_ROWS, 0)
    done = (blocks - pstarts[blk_e] // STEP_ROWS) * EXPERT_SUBS
    nsub = jnp.clip(nsub_e[blk_e] - done, 0, EXPERT_SUBS)
    ids = jnp.where(counts > 0, jnp.arange(N_EXPERTS, dtype=I32), N_EXPERTS)
    later = jnp.concatenate([lax.cummin(ids[::-1])[::-1][1:], jnp.full((1,), N_EXPERTS, I32)])
    next_e = jnp.where(later < N_EXPERTS, later, -1)[blk_e]
    return dest.astype(I32), zends.astype(I32), blk_e, nsub.astype(I32), next_e.astype(I32), nused


def _tile_major(dest, tm):
    k, t = dest.shape
    return dest.reshape(k, t // tm, tm).transpose(1, 0, 2).reshape(t // tm, k * tm)


def _rope_tables(n_rows):
    row = jnp.repeat(jnp.arange(n_rows, dtype=F32), GRID_W)
    col = jnp.tile(jnp.arange(GRID_W, dtype=F32), n_rows)
    half = QK_ROPE // 2
    inv = ROPE_BASE ** (-jnp.arange(0, half, 2, dtype=F32) / half)
    ang_r = row[:, None] * inv
    ang_c = col[:, None] * inv
    ang = jnp.concatenate([ang_r, ang_r, ang_c, ang_c], axis=-1)
    n = ang.shape[0]
    ones = jnp.ones((n, QK_NOPE), F32)
    zeros = jnp.zeros((n, QK_NOPE), F32)
    tail = jnp.zeros((n, HEAD_PAD - QK_NOPE - QK_ROPE), F32)
    cos = jnp.concatenate([ones, jnp.cos(ang), tail], axis=-1)
    sin = jnp.concatenate([zeros, jnp.sin(ang), tail], axis=-1)
    return cos, sin


def _identity_tables(n):
    cos = jnp.concatenate([jnp.ones((n, QK_NOPE + QK_ROPE), F32),
                           jnp.zeros((n, HEAD_PAD - QK_NOPE - QK_ROPE), F32)], axis=-1)
    return cos, jnp.zeros((n, HEAD_PAD), F32)


def _rotate_cols(w):
    i = np.arange(QK_ROPE)
    first = (i % (QK_ROPE // 2)) < (QK_ROPE // 4)
    perm = np.where(first, i + QK_ROPE // 4, i - QK_ROPE // 4)
    sign = np.where(first, -1.0, 1.0).astype(np.float32)
    return w[:, perm] * sign


def _mla_weights(g_cq, w_uq, g_ckv, w_ukv):
    dk = QK_NOPE + QK_ROPE
    wq = w_uq.reshape(Q_LORA, MLA_HEADS, dk)
    pad = jnp.zeros((Q_LORA, MLA_HEADS, HEAD_PAD - dk), F32)
    zero_nope = jnp.zeros((Q_LORA, MLA_HEADS, QK_NOPE), F32)
    wq_rot = _rotate_cols(wq[..., QK_NOPE:].reshape(Q_LORA * MLA_HEADS, QK_ROPE)).reshape(
        Q_LORA, MLA_HEADS, QK_ROPE)
    wq_a = jnp.concatenate([wq, pad], axis=-1).reshape(Q_LORA, -1)
    wq_b = jnp.concatenate([zero_nope, wq_rot, pad], axis=-1).reshape(Q_LORA, -1)
    wkv = w_ukv.reshape(KV_LORA, MLA_HEADS, QK_NOPE + V_HEAD)
    wk = jnp.concatenate([wkv[..., :QK_NOPE],
                          jnp.zeros((KV_LORA, MLA_HEADS, HEAD_PAD - QK_NOPE), F32)], axis=-1)
    wv = wkv[..., QK_NOPE:]
    eye = jnp.eye(QK_ROPE, dtype=F32)
    place = jnp.zeros((LANES, MLA_HEADS, HEAD_PAD), F32)
    place_a = place.at[:QK_ROPE, :, QK_NOPE:dk].set(jnp.broadcast_to(eye[:, None, :], (QK_ROPE, MLA_HEADS, QK_ROPE)))
    rot = _rotate_cols(eye)
    place_b = place.at[:QK_ROPE, :, QK_NOPE:dk].set(jnp.broadcast_to(rot[:, None, :], (QK_ROPE, MLA_HEADS, QK_ROPE)))
    return {
        "g_cq": g_cq.reshape(1, -1), "g_ckv": g_ckv.reshape(1, -1),
        "wq2t": jnp.concatenate([wq_a, wq_b], axis=-1).T.astype(BF16),
        "wk": wk.reshape(KV_LORA, -1).astype(BF16),
        "wvt": wv.reshape(KV_LORA, -1).T.astype(BF16),
        "pk": jnp.concatenate([place_a.reshape(LANES, -1), place_b.reshape(LANES, -1)], axis=-1).astype(BF16),
    }


def _pack_w_in(w):
    gates0 = MLA_IN + 2 * CONV_CH + FFT_CH + POOL_CH
    pad = jnp.zeros((D_MODEL, P_MLA_W - MLA_IN), BF16)
    rest = jnp.concatenate([w[:, :MLA_IN].astype(BF16), pad, w[:, MLA_IN:gates0].astype(BF16)], axis=-1)
    return w[:, gates0:].astype(BF16), rest


def _dft_tables(n):
    gw = FFT_CH // FFT_GROUPS

    def angles(rows, m):
        k = jnp.arange(m, dtype=I32)
        return (rows[:, None] * k[None, :] % m).astype(F32) * (2.0 * np.pi / m)

    ac = angles(jnp.arange(gw, dtype=I32), gw)
    cs = jnp.concatenate([jnp.cos(ac), jnp.sin(ac)], axis=-1).astype(BF16)
    ns = int(np.sqrt(n))
    assert ns * ns == n
    a_hi = angles(jnp.arange(ns, dtype=I32) * ns, n)
    a_lo = angles(jnp.arange(ns, dtype=I32), n)
    ch, sh, cl, sl = jnp.cos(a_hi)[:, None, :], jnp.sin(a_hi)[:, None, :], jnp.cos(a_lo)[None], jnp.sin(a_lo)[None]
    cos_n = (ch * cl - sh * sl).reshape(n, n).astype(BF16)
    nsin_n = (-(sh * cl + ch * sl)).reshape(n, n).astype(BF16)
    return cs, cos_n, nsin_n


def _mixers(p, nseq, n, wts, tables, conv_args, pool_args, dft, **kv):
    q4, k4, v = mla_prep(p, wts, tables, row0=0, col0=P_MLA, nseq=nseq, n=n, **kv)
    cv = conv_branch(p, *conv_args, row0=0, nseq=nseq, n=n)
    ff = fft_branch(p, *dft, row0=0, nseq=nseq, n=n)
    po = pool_branch(p, *pool_args, row0=0, nseq=nseq, n=n)
    return q4, k4, v, cv, ff, po


def kernel(x, c, ctx, c_ctx, w_mod, b_mod, g_norm1, g_norm2, w_in, g_cq, w_uq, g_ckv, w_ukv, w_o_mla, conv_w, conv_b, conv_ln_g, conv_ln_b, w_o_conv, w_o_fft, pool_w, pool_scale, w_o_pool, w_out, w_router, router_bias, w_exp_gate, w_exp_up, w_exp_down, w_sh_gate, w_sh_up, w_sh_down, g_final):
    x_lat = x.reshape(T_LAT, D_MODEL)
    x_ctx = ctx.reshape(T_CTX, D_MODEL)
    ctx_tile0 = 0
    cvec = jnp.concatenate([c, c_ctx[None, :], jnp.zeros((SUBLANES - BATCH - 1, D_MODEL), F32)], axis=0)
    mod_all = modulation_all(cvec, w_mod, b_mod).reshape(DEPTH, SUBLANES, 6, D_MODEL)
    rope_lat = _rope_tables(SEQ // GRID_W)
    rope_ctx = _identity_tables(CTX_LEN)
    dft_lat = _dft_tables(SEQ)
    dft_ctx = _dft_tables(CTX_LEN)
    lat_tiles = T_LAT // ROW_TILE
    all_tiles = T_ALL // ROW_TILE

    for l in range(DEPTH):
        last = l == DEPTH - 1
        mod = mod_all[l]
        w_gate, w_rest = _pack_w_in(w_in[l])
        wts = _mla_weights(g_cq[l], w_uq[l], g_ckv[l], w_ukv[l])
        conv_args = (conv_w[l], conv_b[l], conv_ln_g[l], conv_ln_b[l])
        pool_args = (pool_w[l].astype(BF16), pool_scale[l])
        wo4 = jnp.stack([w_o_mla[l], w_o_conv[l], w_o_fft[l], w_o_pool[l]], axis=0).astype(BF16)
        w_out_b = w_out[l].astype(BF16)

        ctx_col0, ctx_cols = (P_MLA, P_MLA_W) if last else (0, P_COLS)
        p = normproj(x_lat, g_norm1[l], mod, w_gate, w_rest, tile0=0, ntiles=lat_tiles, col0=0, ncols=P_COLS)
        p_ctx = normproj(x_ctx, g_norm1[l], mod, w_gate, w_rest, tile0=ctx_tile0, ntiles=all_tiles - lat_tiles,
                         mod_tile0=lat_tiles, col0=ctx_col0, ncols=ctx_cols)
        nkeys = SEQ + CTX_LEN
        q4, k_all, v_all, cv, ff, po = _mixers(p, BATCH, SEQ, wts, rope_lat, conv_args, pool_args, dft_lat,
                                               n_keys=nkeys)
        if last:
            qc, k_all, v_all = mla_prep(p_ctx, wts, rope_ctx, row0=0, col0=0, nseq=BATCH, n=CTX_LEN,
                                        n_keys=nkeys, key0=SEQ, into=(k_all, v_all))
        else:
            qc, k_all, v_all, cvc, ffc, poc = _mixers(p_ctx, BATCH, CTX_LEN, wts, rope_ctx, conv_args,
                                                      pool_args, dft_ctx, n_keys=nkeys, key0=SEQ,
                                                      into=(k_all, v_all))
        at = attention(q4, k_all, v_all, nq=SEQ, nk=nkeys)
        lat_t, ctx_t = T_LAT // 256, T_CTX // 256
        if last:
            ntok = T_LAT
            xa = merge(x_lat, mod, at, cv, ff, po, p, wo4, w_out_b, ntiles=lat_t)
        else:
            ntok = T_ALL
            atc = attention(qc, k_all, v_all, nq=CTX_LEN, nk=CTX_LEN, key0=SEQ)
            xa = merge(x_lat, mod, at, cv, ff, po, p, wo4, w_out_b, ntiles=lat_t, out_tiles=lat_t + ctx_t)
            xa = merge(x_ctx, mod, atc, cvc, ffc, poc, p_ctx, wo4, w_out_b, ntiles=ctx_t, out_tile0=lat_t,
                       out_tiles=lat_t + ctx_t, into=xa)

        wr_t = w_router[l].T
        wr_hi = wr_t.astype(BF16)
        wr_lo = (wr_t - wr_hi.astype(F32)).astype(BF16)
        tokens, eidx, wk, rank, cnt = norm_router(xa, g_norm2[l], mod, wr_hi, wr_lo, router_bias[l],
                                                  ntiles=ntok // ROUTER_TILE)
        dest, zends, blk_e, nsub, next_e, nused = _routing_plan(eidx, rank, cnt[:, ::LANES])
        xs, sh = dispatch_shared(tokens, _tile_major(dest, 256), zends, w_sh_gate[l].astype(BF16),
                                 w_sh_up[l].astype(BF16), w_sh_down[l].astype(BF16), ntiles=ntok // 256)
        ys = experts(xs, blk_e, nsub, next_e, nused, w_exp_gate, w_exp_up, w_exp_down, layer=l)
        xa = combine(xa, sh, mod, wk.T, g_final, _tile_major(dest, 128), ys, ntiles=ntok // 128, final=last)
        x_lat, x_ctx, ctx_tile0 = xa, xa, lat_tiles
    return xa.reshape(BATCH, SEQ, D_MODEL)
```

```python
import functools

import jax
import jax.numpy as jnp
import numpy as np
from jax import lax
from jax.experimental import pallas as pl
from jax.experimental.pallas import tpu as pltpu

F32 = jnp.float32
BF16 = jnp.bfloat16
I32 = jnp.int32

D_MODEL = 2048
BATCH = 4
SEQ = 4096
DEPTH = 2
GRID_W = 64
CTX_LEN = 256
MLA_HEADS = 8
QK_NOPE = 64
QK_ROPE = 32
V_HEAD = 64
Q_LORA = 512
KV_LORA = 256
ROPE_BASE = 10000.0
CONV_CH = 512
CONV_WIDTH = 31
FFT_CH = 512
FFT_GROUPS = 4
POOL_CH = 512
POOL_WINDOWS = (2, 4, 8, 16)
POOL_GROUP = POOL_CH // len(POOL_WINDOWS)
N_BRANCHES = 4
N_EXPERTS = 64
N_EXPERT_GROUPS = 8
TOPK_GROUPS = 4
TOP_K = 8
EXPERT_FF = 512
SHARED_FF = 512
ROUTED_SCALE = 2.5
EPS = 1e-6

MLA_IN = Q_LORA + KV_LORA + QK_ROPE
T_LAT = BATCH * SEQ
T_CTX = BATCH * CTX_LEN
T_ALL = T_LAT + T_CTX

LANES = 128
SUBLANES = 8
VMEM_LIMIT_BYTES = 56 * 1024 * 1024

P_GATE = 0
P_MLA = N_BRANCHES * D_MODEL
P_MLA_W = 1024
P_CONV = P_MLA + P_MLA_W
P_FFT = P_CONV + 2 * CONV_CH
P_POOL = P_FFT + FFT_CH
P_COLS = P_POOL + POOL_CH
HEAD_PAD = 128
ATT_SCALE = (QK_NOPE + QK_ROPE) ** -0.5
Q_SCALE = ATT_SCALE * float(np.log2(np.e))

ROW_TILE = 1024
ROUTER_TILE = 256
EXPERT_ROWS = 256
EXPERT_SUBS = 4
STEP_ROWS = EXPERT_ROWS * EXPERT_SUBS
N_BLOCKS = -(-(T_ALL * TOP_K) // STEP_ROWS) + N_EXPERTS


def _cparams(sem, vmem=VMEM_LIMIT_BYTES):
    return pltpu.CompilerParams(dimension_semantics=sem, vmem_limit_bytes=vmem)


def _sigmoid(x):
    return 1.0 / (1.0 + jnp.exp(-x))


def _silu(x):
    return x * _sigmoid(x)


def _mod_row(tile, tiles_per_seq):
    return jnp.minimum(tile // tiles_per_seq, BATCH)


def _modulation_kernel(c_ref, w_ref, b_ref, o_ref):
    c = c_ref[...]
    s = _silu(c).astype(BF16)
    o_ref[0] = jnp.dot(s, w_ref[0].astype(BF16), preferred_element_type=F32) + b_ref[0]


def modulation_all(cvec, w_mod, b_mod):
    n = 6 * D_MODEL
    tn = 2048 if n % 2048 == 0 else 1024
    return pl.pallas_call(
        _modulation_kernel,
        out_shape=jax.ShapeDtypeStruct((DEPTH, SUBLANES, n), F32),
        grid=(DEPTH, n // tn),
        in_specs=[
            pl.BlockSpec((SUBLANES, D_MODEL), lambda l, j: (0, 0)),
            pl.BlockSpec((1, D_MODEL, tn), lambda l, j: (l, 0, j)),
            pl.BlockSpec((1, 1, tn), lambda l, j: (l, 0, j)),
        ],
        out_specs=pl.BlockSpec((1, SUBLANES, tn), lambda l, j: (l, 0, j)),
        compiler_params=_cparams(("arbitrary", "arbitrary")),
        name="modulation",
    )(cvec, w_mod, b_mod.reshape(DEPTH, 1, n))


def _adaln(x, g, shift, scale):
    y = x * lax.rsqrt(jnp.mean(x * x, axis=-1, keepdims=True) + EPS)
    return (y * g) * (1.0 + scale) + shift


def _normproj_kernel(x_ref, g_ref, mod_ref, wg_ref, wr_ref, o_ref, h_ref, *, shift_row, chunk, cb0, ngate):
    j = pl.program_id(1)

    @pl.when(j == 0)
    def _():
        shift = mod_ref[0, shift_row:shift_row + 1, :]
        scale = mod_ref[0, shift_row + 1:shift_row + 2, :]
        for r in range(0, x_ref.shape[0], chunk):
            h = _adaln(x_ref[r:r + chunk, :], g_ref[...], shift, scale)
            h_ref[r:r + chunk, :] = h.astype(BF16)

    @pl.when(j + cb0 < ngate)
    def _():
        o_ref[...] = jnp.dot(h_ref[...], wg_ref[...], preferred_element_type=F32).astype(o_ref.dtype)

    @pl.when(j + cb0 >= ngate)
    def _():
        o_ref[...] = jnp.dot(h_ref[...], wr_ref[...], preferred_element_type=F32).astype(o_ref.dtype)


def normproj(x, g, mod, w_gate, w_rest, *, tile0, ntiles, col0, ncols, mod_tile0=None, shift_row=0):
    tm, tn = ROW_TILE, 1024
    tps = SEQ // tm
    cb0 = col0 // tn
    ngate = w_gate.shape[1] // tn
    mt0 = tile0 if mod_tile0 is None else mod_tile0
    return pl.pallas_call(
        functools.partial(_normproj_kernel, shift_row=shift_row, chunk=256, cb0=cb0, ngate=ngate),
        out_shape=jax.ShapeDtypeStruct((ntiles * tm, ncols), BF16),
        grid=(ntiles, ncols // tn),
        in_specs=[
            pl.BlockSpec((tm, D_MODEL), lambda i, j: (i + tile0, 0)),
            pl.BlockSpec((1, D_MODEL), lambda i, j: (0, 0)),
            pl.BlockSpec((1, 6, D_MODEL), lambda i, j: (_mod_row(i + mt0, tps), 0, 0)),
            pl.BlockSpec((D_MODEL, tn), lambda i, j: (0, jnp.minimum(j + cb0, ngate - 1))),
            pl.BlockSpec((D_MODEL, tn), lambda i, j: (0, jnp.maximum(j + cb0 - ngate, 0))),
        ],
        out_specs=pl.BlockSpec((tm, tn), lambda i, j: (i, j)),
        scratch_shapes=[pltpu.VMEM((tm, D_MODEL), BF16)],
        compiler_params=_cparams(("arbitrary", "arbitrary")),
        name="normproj",
    )(x, g.reshape(1, D_MODEL), mod, w_gate, w_rest)


def _rms_gain(x, g):
    y = x * lax.rsqrt(jnp.mean(x * x, axis=-1, keepdims=True) + EPS)
    return (y * g).astype(BF16)


_NT = (((1,), (1,)), ((), ()))


def _mla_prep_kernel(p_ref, gq_ref, gkv_ref, wqt_ref, wk_ref, wvt_ref, pk_ref, cos_ref, sin_ref,
                     cost_ref, sint_ref, qt_ref, k_ref, vt_ref):
    hw = MLA_HEADS * HEAD_PAD
    cqn = _rms_gain(p_ref[:, :Q_LORA].astype(F32), gq_ref[...])
    ckvn = _rms_gain(p_ref[:, Q_LORA:Q_LORA + KV_LORA].astype(F32), gkv_ref[...])
    kr = p_ref[:, Q_LORA + KV_LORA:Q_LORA + KV_LORA + LANES]
    q2t = lax.dot_general(wqt_ref[...], cqn, _NT, preferred_element_type=F32)
    vt_ref[0] = lax.dot_general(wvt_ref[...], ckvn, _NT, preferred_element_type=F32).astype(BF16)
    kk = jnp.dot(ckvn, wk_ref[...], preferred_element_type=F32)
    kr2 = jnp.dot(kr, pk_ref[...], preferred_element_type=F32)
    cos, sin = cos_ref[...], sin_ref[...]
    cost, sint = cost_ref[...], sint_ref[...]
    for h in range(MLA_HEADS):
        a = slice(h * HEAD_PAD, (h + 1) * HEAD_PAD)
        b = slice(hw + h * HEAD_PAD, hw + (h + 1) * HEAD_PAD)
        qt_ref[0, h] = ((q2t[a, :] * cost + q2t[b, :] * sint) * Q_SCALE).astype(BF16)
        k_ref[0, h] = (kk[:, a] + kr2[:, a] * cos + kr2[:, b] * sin).astype(BF16)


def _mla_prep_into_kernel(*refs):
    _mla_prep_kernel(*refs[:11], *refs[13:])


def mla_prep(p, wts, tables, *, row0, col0, nseq, n, n_keys=None, key0=0, into=None):
    tm = 256
    tps = n // tm
    hw = MLA_HEADS * HEAD_PAD
    vw = MLA_HEADS * V_HEAD
    blk0 = row0 // tm
    cblk = col0 // P_MLA_W
    n_keys = n if n_keys is None else n_keys
    kb0 = key0 // tm
    cos_t, sin_t = tables
    const = lambda i: (0, 0)
    in_specs = [
        pl.BlockSpec((tm, P_MLA_W), lambda i: (i + blk0, cblk)),
        pl.BlockSpec((1, Q_LORA), const),
        pl.BlockSpec((1, KV_LORA), const),
        pl.BlockSpec((2 * hw, Q_LORA), const),
        pl.BlockSpec((KV_LORA, hw), const),
        pl.BlockSpec((vw, KV_LORA), const),
        pl.BlockSpec((LANES, 2 * hw), const),
        pl.BlockSpec((tm, HEAD_PAD), lambda i: (i % tps, 0)),
        pl.BlockSpec((tm, HEAD_PAD), lambda i: (i % tps, 0)),
        pl.BlockSpec((HEAD_PAD, tm), lambda i: (0, i % tps)),
        pl.BlockSpec((HEAD_PAD, tm), lambda i: (0, i % tps)),
    ]
    operands = [p, wts["g_cq"], wts["g_ckv"], wts["wq2t"], wts["wk"], wts["wvt"], wts["pk"],
                cos_t, sin_t, cos_t.T, sin_t.T]
    kernel_fn, aliases = _mla_prep_kernel, {}
    if into is not None:
        in_specs += [pl.BlockSpec(memory_space=pl.ANY), pl.BlockSpec(memory_space=pl.ANY)]
        operands += list(into)
        kernel_fn, aliases = _mla_prep_into_kernel, {11: 1, 12: 2}
    return pl.pallas_call(
        kernel_fn,
        out_shape=(jax.ShapeDtypeStruct((nseq, MLA_HEADS, HEAD_PAD, n), BF16),
                   jax.ShapeDtypeStruct((nseq, MLA_HEADS, n_keys, HEAD_PAD), BF16),
                   jax.ShapeDtypeStruct((nseq, vw, n_keys), BF16)),
        grid=(nseq * tps,),
        in_specs=in_specs,
        out_specs=(pl.BlockSpec((1, MLA_HEADS, HEAD_PAD, tm), lambda i: (i // tps, 0, 0, i % tps)),
                   pl.BlockSpec((1, MLA_HEADS, tm, HEAD_PAD), lambda i: (i // tps, 0, i % tps + kb0, 0)),
                   pl.BlockSpec((1, vw, tm), lambda i: (i // tps, 0, i % tps + kb0))),
        input_output_aliases=aliases,
        compiler_params=_cparams(("arbitrary",)),
        name="mla_prep",
    )(*operands)


ATT_HEADS = 8


def _attn_kernel(qt_ref, k_ref, vt_ref, o_ref, s_ref, p_ref, *, nk, ck):
    tq = qt_ref.shape[3]
    chunks = [slice(c, c + ck) for c in range(0, nk, ck)]
    nh = ATT_HEADS
    m = [None] * nh
    l = [jnp.zeros((1, tq), F32) for _ in range(nh)]
    ot = [jnp.zeros((V_HEAD, tq), F32) for _ in range(nh)]
    for step in range(nh + 2):
        ha, hb, hc = step, step - 1, step - 2
        for c in chunks:
            if 0 <= ha < nh:
                s = jnp.dot(k_ref[0, ha, c, :], qt_ref[0, ha], preferred_element_type=F32)
                s_ref[ha % 2, c, :] = s
                mc = jnp.max(s, axis=0, keepdims=True)
                m[ha] = mc if m[ha] is None else jnp.maximum(m[ha], mc)
            if 0 <= hb < nh:
                p = jnp.exp2(s_ref[hb % 2, c, :] - m[hb])
                l[hb] = l[hb] + jnp.sum(p, axis=0, keepdims=True)
                p_ref[hb % 2, c, :] = p.astype(BF16)
            if 0 <= hc < nh:
                ot[hc] = ot[hc] + jnp.dot(vt_ref[0, hc * V_HEAD:(hc + 1) * V_HEAD, c], p_ref[hc % 2, c, :],
                                          preferred_element_type=F32)
    outs = [ot[h] / l[h] for h in range(nh)]
    o_ref[...] = jnp.concatenate(outs, axis=0).T.astype(o_ref.dtype)


def attention(qt4, k4, vt3, *, nq, nk, key0=0):
    tq = 256
    ck = 256
    nqt = nq // tq
    hs = ATT_HEADS
    kb = key0 // nk
    return pl.pallas_call(
        functools.partial(_attn_kernel, nk=nk, ck=ck),
        out_shape=jax.ShapeDtypeStruct((BATCH * nq, MLA_HEADS * V_HEAD), BF16),
        grid=(BATCH, MLA_HEADS // hs, nqt),
        in_specs=[
            pl.BlockSpec((1, hs, HEAD_PAD, tq), lambda b, hg, i: (b, hg, 0, i)),
            pl.BlockSpec((1, hs, nk, HEAD_PAD), lambda b, hg, i: (b, hg, kb, 0)),
            pl.BlockSpec((1, hs * V_HEAD, nk), lambda b, hg, i: (b, hg, kb)),
        ],
        out_specs=pl.BlockSpec((tq, hs * V_HEAD), lambda b, hg, i: (b * nqt + i, hg)),
        scratch_shapes=[pltpu.VMEM((2, nk, tq), F32), pltpu.VMEM((2, nk, tq), BF16)],
        compiler_params=_cparams(("arbitrary", "arbitrary", "arbitrary")),
        name="attention",
    )(qt4, k4, vt3)


CONV_PAD = 16


def _conv_kernel(p_ref, w_ref, cb_ref, lg_ref, lb_ref, o_ref, u_ref, *, n, tr):
    zeros = jnp.zeros((CONV_PAD, CONV_CH), F32)
    u_ref[0:CONV_PAD, :] = zeros
    u_ref[CONV_PAD + n:2 * CONV_PAD + n, :] = zeros

    def glu(i, carry):
        r = pl.multiple_of(i * tr, tr)
        a = p_ref[pl.ds(r, tr), 0:CONV_CH].astype(F32)
        b = p_ref[pl.ds(r, tr), CONV_CH:2 * CONV_CH].astype(F32)
        u_ref[pl.ds(CONV_PAD + r, tr), :] = a * _sigmoid(b)
        return carry

    lax.fori_loop(0, n // tr, glu, 0)
    off = CONV_PAD - CONV_WIDTH // 2
    win_rows = tr + 2 * CONV_PAD

    def conv(i, carry):
        r = pl.multiple_of(i * tr, tr)
        strips = []
        for c0 in range(0, CONV_CH, LANES):
            win = u_ref[pl.ds(r, win_rows), c0:c0 + LANES]
            acc = jnp.zeros((tr, LANES), F32)
            for b in range(SUBLANES):
                wb = pltpu.roll(win, win_rows - (off + b), axis=0)
                for k in range(b, CONV_WIDTH, SUBLANES):
                    acc = acc + wb[k - b:k - b + tr, :] * w_ref[k:k + 1, c0:c0 + LANES]
            strips.append(acc)
        acc = jnp.concatenate(strips, axis=-1) + cb_ref[...]
        mu = jnp.mean(acc, axis=-1, keepdims=True)
        d = acc - mu
        var = jnp.mean(d * d, axis=-1, keepdims=True)
        y = (d * lax.rsqrt(var + EPS)) * lg_ref[...] + lb_ref[...]
        o_ref[pl.ds(r, tr), :] = _silu(y).astype(o_ref.dtype)
        return carry

    lax.fori_loop(0, n // tr, conv, 0)


def conv_branch(p, conv_w, conv_b, ln_g, ln_b, *, row0, nseq, n):
    tr = 128
    blk0 = row0 // n
    const = lambda s: (0, 0)
    return pl.pallas_call(
        functools.partial(_conv_kernel, n=n, tr=tr),
        out_shape=jax.ShapeDtypeStruct((nseq * n, CONV_CH), BF16),
        grid=(nseq,),
        in_specs=[
            pl.BlockSpec((n, 2 * CONV_CH), lambda s: (s + blk0, P_CONV // (2 * CONV_CH))),
            pl.BlockSpec((CONV_WIDTH, CONV_CH), const),
            pl.BlockSpec((1, CONV_CH), const),
            pl.BlockSpec((1, CONV_CH), const),
            pl.BlockSpec((1, CONV_CH), const),
        ],
        out_specs=pl.BlockSpec((n, CONV_CH), lambda s: (s, 0)),
        scratch_shapes=[pltpu.VMEM((n + 2 * CONV_PAD, CONV_CH), F32)],
        compiler_params=_cparams(("arbitrary",)),
        name="conv_branch",
    )(p, conv_w, conv_b.reshape(1, -1), ln_g.reshape(1, -1), ln_b.reshape(1, -1))


def _fft_kernel(z_ref, cs_ref, mc_ref, ms_ref, o_ref, ab_ref, *, n, tc):
    gw = FFT_CH // FFT_GROUPS

    @pl.when(pl.program_id(1) == 0)
    def _():
        def chan(i, carry):
            r = pl.multiple_of(i * tc, tc)
            for g in range(FFT_GROUPS):
                zg = z_ref[pl.ds(r, tc), g * gw:(g + 1) * gw]
                ab = jnp.dot(zg, cs_ref[...], preferred_element_type=F32)
                ab_ref[pl.ds(r, tc), g * gw:(g + 1) * gw] = ab[:, :gw].astype(BF16)
                ab_ref[pl.ds(n + r, tc), g * gw:(g + 1) * gw] = ab[:, gw:].astype(BF16)
            return carry

        lax.fori_loop(0, n // tc, chan, 0)

    norm = 1.0 / float(np.sqrt(n * gw))
    re = (jnp.dot(mc_ref[...], ab_ref[0:n, :], preferred_element_type=F32)
          + jnp.dot(ms_ref[...], ab_ref[n:2 * n, :], preferred_element_type=F32))
    o_ref[...] = (re * norm).astype(o_ref.dtype)


def fft_branch(p, cs, mcos, mnsin, *, row0, nseq, n):
    tm = min(n, 512)
    tc = min(n, 512)
    blk0 = row0 // n
    nt = n // tm
    return pl.pallas_call(
        functools.partial(_fft_kernel, n=n, tc=tc),
        out_shape=jax.ShapeDtypeStruct((nseq * n, FFT_CH), BF16),
        grid=(nseq, nt),
        in_specs=[
            pl.BlockSpec((n, FFT_CH), lambda s, i: (s + blk0, P_FFT // FFT_CH)),
            pl.BlockSpec((FFT_CH // FFT_GROUPS, 2 * FFT_CH // FFT_GROUPS), lambda s, i: (0, 0)),
            pl.BlockSpec((tm, n), lambda s, i: (i, 0)),
            pl.BlockSpec((tm, n), lambda s, i: (i, 0)),
        ],
        out_specs=pl.BlockSpec((tm, FFT_CH), lambda s, i: (s * nt + i, 0)),
        scratch_shapes=[pltpu.VMEM((2 * n, FFT_CH), BF16)],
        compiler_params=_cparams(("arbitrary", "arbitrary")),
        name="fft_branch",
    )(p, cs, mcos, mnsin)


POOL_PAD = 16


def _pool_kernel(z_ref, pw_ref, ps_ref, o_ref, zp_ref, d_ref, *, n, tr, tc):
    zeros = jnp.zeros((POOL_PAD, POOL_CH), F32)
    zp_ref[0:POOL_PAD, :] = zeros
    zp_ref[POOL_PAD + n:2 * POOL_PAD + n, :] = zeros

    def fill(i, carry):
        r = pl.multiple_of(i * tc, tc)
        zp_ref[pl.ds(POOL_PAD + r, tc), :] = z_ref[pl.ds(r, tc), :].astype(F32)
        return carry

    lax.fori_loop(0, n // tc, fill, 0)

    win_rows = tr + 2 * POOL_PAD

    def pool(i, carry):
        r = pl.multiple_of(i * tr, tr)
        t = (r + lax.broadcasted_iota(I32, (tr, POOL_GROUP), 0)).astype(F32)
        for gi, w in enumerate(POOL_WINDOWS):
            cols = slice(gi * POOL_GROUP, (gi + 1) * POOL_GROUP)
            win = zp_ref[pl.ds(r, win_rows), cols]
            s = jnp.zeros((tr, POOL_GROUP), F32)
            for j in range(w):
                start = POOL_PAD - w // 2 + j
                if start % SUBLANES == 0:
                    s = s + win[start:start + tr, :]
                else:
                    s = s + pltpu.roll(win, win_rows - start, axis=0)[0:tr, :]
            lo = jnp.maximum(t - (w // 2), 0.0)
            hi = jnp.minimum(t - (w // 2) + w, float(n))
            z = win[POOL_PAD:POOL_PAD + tr, :]
            d_ref[pl.ds(r, tr), cols] = (s / (hi - lo) - z).astype(BF16)
        return carry

    lax.fori_loop(0, n // tr, pool, 0)

    def proj(i, carry):
        r = pl.multiple_of(i * tc, tc)
        for gi in range(len(POOL_WINDOWS)):
            cols = slice(gi * POOL_GROUP, (gi + 1) * POOL_GROUP)
            y = jnp.dot(d_ref[pl.ds(r, tc), cols], pw_ref[gi], preferred_element_type=F32)
            o_ref[pl.ds(r, tc), cols] = (y * ps_ref[:, cols]).astype(o_ref.dtype)
        return carry

    lax.fori_loop(0, n // tc, proj, 0)


def pool_branch(p, pool_w, pool_scale, *, row0, nseq, n):
    tr = 128
    tc = min(n, 512)
    blk0 = row0 // n
    return pl.pallas_call(
        functools.partial(_pool_kernel, n=n, tr=tr, tc=tc),
        out_shape=jax.ShapeDtypeStruct((nseq * n, POOL_CH), BF16),
        grid=(nseq,),
        in_specs=[
            pl.BlockSpec((n, POOL_CH), lambda s: (s + blk0, P_POOL // POOL_CH)),
            pl.BlockSpec((len(POOL_WINDOWS), POOL_GROUP, POOL_GROUP), lambda s: (0, 0, 0)),
            pl.BlockSpec((1, POOL_CH), lambda s: (0, 0)),
        ],
        out_specs=pl.BlockSpec((n, POOL_CH), lambda s: (s, 0)),
        scratch_shapes=[pltpu.VMEM((n + 2 * POOL_PAD, POOL_CH), F32),
                        pltpu.VMEM((n, POOL_CH), BF16)],
        compiler_params=_cparams(("arbitrary",)),
        name="pool_branch",
    )(p, pool_w, pool_scale.reshape(1, -1))


def _merge_kernel(x_ref, mod_ref, at_ref, cv_ref, ff_ref, po_ref, g_ref, wo_ref, wout_ref, o_ref):
    mix = None
    for b, br in enumerate((at_ref, cv_ref, ff_ref, po_ref)):
        y = jnp.dot(br[...], wo_ref[b], preferred_element_type=F32)
        gate = _sigmoid(g_ref[:, b * D_MODEL:(b + 1) * D_MODEL].astype(F32))
        mix = gate * y if mix is None else mix + gate * y
    out = jnp.dot(mix.astype(BF16), wout_ref[...], preferred_element_type=F32)
    o_ref[...] = x_ref[...] + mod_ref[0, 2:3, :] * out


def _merge_into_kernel(x_ref, mod_ref, at_ref, cv_ref, ff_ref, po_ref, g_ref, wo_ref, wout_ref, prev_ref, o_ref):
    del prev_ref
    _merge_kernel(x_ref, mod_ref, at_ref, cv_ref, ff_ref, po_ref, g_ref, wo_ref, wout_ref, o_ref)


def merge(x, mod, attn, conv, fft, pool, p, wo4, w_out, *, ntiles, out_tile0=0, out_tiles=None, into=None):
    tm = 256
    tps = SEQ // tm
    out_tiles = ntiles if out_tiles is None else out_tiles
    bw = MLA_HEADS * V_HEAD
    assert bw == CONV_CH == FFT_CH == POOL_CH
    br_spec = pl.BlockSpec((tm, bw), lambda i: (i, 0))
    in_specs = [
        pl.BlockSpec((tm, D_MODEL), lambda i: (i, 0)),
        pl.BlockSpec((1, 6, D_MODEL), lambda i: (_mod_row(i + out_tile0, tps), 0, 0)),
        br_spec, br_spec, br_spec, br_spec,
        pl.BlockSpec((tm, N_BRANCHES * D_MODEL), lambda i: (i, P_GATE // (N_BRANCHES * D_MODEL))),
        pl.BlockSpec((N_BRANCHES, bw, D_MODEL), lambda i: (0, 0, 0), pipeline_mode=pl.Buffered(1)),
        pl.BlockSpec((D_MODEL, D_MODEL), lambda i: (0, 0), pipeline_mode=pl.Buffered(1)),
    ]
    operands = [x, mod, attn, conv, fft, pool, p, wo4, w_out]
    kernel_fn, aliases = _merge_kernel, {}
    if into is not None:
        in_specs.append(pl.BlockSpec(memory_space=pl.ANY))
        operands.append(into)
        kernel_fn, aliases = _merge_into_kernel, {len(operands) - 1: 0}
    return pl.pallas_call(
        kernel_fn,
        out_shape=jax.ShapeDtypeStruct((out_tiles * tm, D_MODEL), F32),
        grid=(ntiles,),
        in_specs=in_specs,
        out_specs=pl.BlockSpec((tm, D_MODEL), lambda i: (i + out_tile0, 0)),
        input_output_aliases=aliases,
        compiler_params=_cparams(("arbitrary",)),
        name="merge",
    )(*operands)


def _first_index(hit_value, cand, ids, big):
    return jnp.min(jnp.where(cand == hit_value, ids, big), axis=0, keepdims=True)


HALF = D_MODEL // 2
ROW_WORDS = HALF // LANES


def _round_bf16_bits(x):
    u = lax.bitcast_convert_type(x, I32)
    odd = lax.shift_right_logical(u, 16) & 1
    return (u + 0x7FFF + odd) & jnp.int32(-65536)


def _pack_words(lo, hi):
    return lax.shift_right_logical(_round_bf16_bits(lo), 16) | _round_bf16_bits(hi)


def _unpack_words(w):
    lo = lax.bitcast_convert_type(lax.shift_left(w, 16), F32)
    hi = lax.bitcast_convert_type(w & jnp.int32(-65536), F32)
    return lo, hi


def _store_packed(ref, x, tok0=0):
    tm = x.shape[0]
    for s in range(ROW_WORDS):
        w = _pack_words(x[:, s * LANES:(s + 1) * LANES], x[:, HALF + s * LANES:HALF + (s + 1) * LANES])
        ref[pl.ds(tok0 * ROW_WORDS + s, tm, stride=ROW_WORDS), :] = w


def _load_packed(ref, tm, dtype, tok0=0):
    los, his = [], []
    for s in range(ROW_WORDS):
        lo, hi = _unpack_words(ref[pl.ds(tok0 * ROW_WORDS + s, tm, stride=ROW_WORDS), :])
        los.append(lo.astype(dtype))
        his.append(hi.astype(dtype))
    return jnp.concatenate(los + his, axis=-1)


def _router_kernel(x_ref, g_ref, mod_ref, wrh_ref, wrl_ref, rb_ref, tri_ref,
                   tok_ref, eidx_ref, wk_ref, rank_ref, cnt_ref):
    tm = x_ref.shape[0]
    h = _adaln(x_ref[...], g_ref[...], mod_ref[0, 3:4, :], mod_ref[0, 4:5, :])
    _store_packed(tok_ref, h)
    hh = h.astype(BF16)
    hl = (h - hh.astype(F32)).astype(BF16)
    nt = (((1,), (1,)), ((), ()))
    logits = (lax.dot_general(wrh_ref[...], hh, nt, preferred_element_type=F32)
              + lax.dot_general(wrh_ref[...], hl, nt, preferred_element_type=F32)
              + lax.dot_general(wrl_ref[...], hh, nt, preferred_element_type=F32))
    scores = _sigmoid(logits)
    sel = scores + rb_ref[...]
    per = N_EXPERTS // N_EXPERT_GROUPS
    assert per == SUBLANES and N_EXPERT_GROUPS == SUBLANES and TOP_K == SUBLANES
    neg = -jnp.inf
    sub = lax.broadcasted_iota(I32, (SUBLANES, tm), 0).astype(F32)
    sg = [sel[g * per:(g + 1) * per, :] for g in range(N_EXPERT_GROUPS)]
    sc = [scores[g * per:(g + 1) * per, :] for g in range(N_EXPERT_GROUPS)]
    gsc = jnp.zeros((SUBLANES, tm), F32)
    for g in range(N_EXPERT_GROUPS):
        m1 = jnp.max(sg[g], axis=0, keepdims=True)
        i1 = _first_index(m1, sg[g], sub, float(per))
        m2 = jnp.max(jnp.where(sub == i1, neg, sg[g]), axis=0, keepdims=True)
        gsc = jnp.where(sub == float(g), m1 + m2, gsc)
    gsel = jnp.zeros((SUBLANES, tm), F32)
    for _ in range(TOPK_GROUPS):
        m = jnp.max(gsc, axis=0, keepdims=True)
        hit = sub == _first_index(m, gsc, sub, float(N_EXPERT_GROUPS))
        gsel = jnp.where(hit, 1.0, gsel)
        gsc = jnp.where(hit, neg, gsc)
    cand = []
    for g in range(N_EXPERT_GROUPS):
        allowed = jnp.max(jnp.where(sub == float(g), gsel, 0.0), axis=0, keepdims=True)
        cand.append(jnp.where(allowed > 0.0, sg[g], neg))
    eid = [sub + float(g * per) for g in range(N_EXPERT_GROUPS)]
    idxs = jnp.zeros((SUBLANES, tm), F32)
    vals = jnp.zeros((SUBLANES, tm), F32)
    picked = [jnp.zeros((per, tm), F32) for _ in range(N_EXPERT_GROUPS)]
    idx_k = []
    for k in range(TOP_K):
        m = functools.reduce(jnp.maximum, [jnp.max(c, axis=0, keepdims=True) for c in cand])
        idx = functools.reduce(
            jnp.minimum, [_first_index(m, cand[g], eid[g], float(N_EXPERTS)) for g in range(N_EXPERT_GROUPS)])
        val = jnp.zeros((1, tm), F32)
        for g in range(N_EXPERT_GROUPS):
            hit = eid[g] == idx
            val = val + jnp.sum(jnp.where(hit, sc[g], 0.0), axis=0, keepdims=True)
            cand[g] = jnp.where(hit, neg, cand[g])
            picked[g] = jnp.where(hit, 1.0, picked[g])
        idxs = jnp.where(sub == float(k), idx, idxs)
        vals = jnp.where(sub == float(k), val, vals)
        idx_k.append(idx)
    eidx_ref[...] = idxs.astype(I32)
    wk_ref[...] = vals / jnp.sum(vals, axis=0, keepdims=True) * ROUTED_SCALE
    mask = jnp.concatenate(picked, axis=0)
    before = jnp.dot(mask.astype(BF16), tri_ref[...], preferred_element_type=F32)
    ranks = jnp.zeros((SUBLANES, tm), F32)
    for k in range(TOP_K):
        r = jnp.zeros((1, tm), F32)
        for g in range(N_EXPERT_GROUPS):
            r = r + jnp.sum(jnp.where(eid[g] == idx_k[k], before[g * per:(g + 1) * per, :], 0.0),
                            axis=0, keepdims=True)
        ranks = jnp.where(sub == float(k), r, ranks)
    rank_ref[...] = ranks.astype(I32)
    cnt_ref[...] = jnp.broadcast_to(jnp.sum(mask, axis=1, keepdims=True), (N_EXPERTS, LANES)).astype(I32)


def norm_router(x, g, mod, wr_hi, wr_lo, rbias, *, ntiles):
    tm = ROUTER_TILE
    tps = SEQ // tm
    t = ntiles * tm
    tri = jnp.triu(jnp.ones((tm, tm), BF16), k=1)
    kt_spec = pl.BlockSpec((TOP_K, tm), lambda i: (0, i))
    return pl.pallas_call(
        _router_kernel,
        out_shape=(jax.ShapeDtypeStruct((t * ROW_WORDS, LANES), I32),
                   jax.ShapeDtypeStruct((TOP_K, t), I32),
                   jax.ShapeDtypeStruct((TOP_K, t), F32),
                   jax.ShapeDtypeStruct((TOP_K, t), I32),
                   jax.ShapeDtypeStruct((N_EXPERTS, ntiles * LANES), I32)),
        grid=(ntiles,),
        in_specs=[
            pl.BlockSpec((tm, D_MODEL), lambda i: (i, 0)),
            pl.BlockSpec((1, D_MODEL), lambda i: (0, 0)),
            pl.BlockSpec((1, 6, D_MODEL), lambda i: (_mod_row(i, tps), 0, 0)),
            pl.BlockSpec((N_EXPERTS, D_MODEL), lambda i: (0, 0)),
            pl.BlockSpec((N_EXPERTS, D_MODEL), lambda i: (0, 0)),
            pl.BlockSpec((N_EXPERTS, 1), lambda i: (0, 0)),
            pl.BlockSpec((tm, tm), lambda i: (0, 0)),
        ],
        out_specs=(pl.BlockSpec((tm * ROW_WORDS, LANES), lambda i: (i, 0)),
                   kt_spec, kt_spec, kt_spec,
                   pl.BlockSpec((N_EXPERTS, LANES), lambda i: (0, i))),
        compiler_params=_cparams(("arbitrary",)),
        name="norm_router",
    )(x, g.reshape(1, D_MODEL), mod, wr_hi, wr_lo, rbias.reshape(N_EXPERTS, 1), tri)


def _row_copy(src, src_tok, dst, dst_tok, sem):
    s = pl.multiple_of(src_tok * ROW_WORDS, ROW_WORDS)
    d = pl.multiple_of(dst_tok * ROW_WORDS, ROW_WORDS)
    return pltpu.make_async_copy(src.at[pl.ds(s, ROW_WORDS)], dst.at[pl.ds(d, ROW_WORDS)], sem)


def _dispatch_kernel(zends_ref, tok_ref, dest_hbm, wg_ref, wu_ref, wd_ref, xs_hbm, sh_ref,
                     dest_smem, zero_ref, sem_idx, sem_z, sem_s):
    i = pl.program_id(0)
    tm = tok_ref.shape[0] // ROW_WORDS
    idx_copy = pltpu.make_async_copy(dest_hbm.at[i], dest_smem, sem_idx)
    idx_copy.start()
    blk = EXPERT_ROWS * ROW_WORDS

    def pad_copy(e):
        row = pl.multiple_of((zends_ref[e] - EXPERT_ROWS) * ROW_WORDS, blk)
        return pltpu.make_async_copy(zero_ref, xs_hbm.at[pl.ds(row, blk)], sem_z)

    @pl.when(i == 0)
    def _():
        zero_ref[...] = jnp.zeros_like(zero_ref)

        def start(e, carry):
            @pl.when(zends_ref[e] > 0)
            def _():
                pad_copy(e).start()
            return carry

        lax.fori_loop(0, N_EXPERTS, start, 0)

        def wait(e, carry):
            @pl.when(zends_ref[e] > 0)
            def _():
                pad_copy(e).wait()
            return carry

        lax.fori_loop(0, N_EXPERTS, wait, 0)

    idx_copy.wait()

    x = _load_packed(tok_ref, tm, BF16)
    pieces = SHARED_FF // LANES
    per = tm // pieces
    acc = None
    for p in range(pieces):
        for r in range(p * per, (p + 1) * per):
            for k in range(TOP_K):
                _row_copy(tok_ref, r, xs_hbm, dest_smem[k * tm + r], sem_s).start(priority=k % 2)
        c = slice(p * LANES, (p + 1) * LANES)
        g = jnp.dot(x, wg_ref[:, c], preferred_element_type=F32)
        u = jnp.dot(x, wu_ref[:, c], preferred_element_type=F32)
        part = jnp.dot((_silu(g) * u).astype(BF16), wd_ref[c, :], preferred_element_type=F32)
        acc = part if acc is None else acc + part
    sh_ref[...] = acc

    def drain(r, carry):
        for k in range(TOP_K):
            _row_copy(tok_ref, r, xs_hbm, dest_smem[k * tm + r], sem_s).wait()
        return carry

    lax.fori_loop(0, tm, drain, 0)


def dispatch_shared(tokens, dest_tiles, zends, wg, wu, wd, *, ntiles):
    tm = tokens.shape[0] // ROW_WORDS // ntiles
    const = lambda i, zends: (0, 0)
    return pl.pallas_call(
        _dispatch_kernel,
        out_shape=(jax.ShapeDtypeStruct((N_BLOCKS * STEP_ROWS * ROW_WORDS, LANES), I32),
                   jax.ShapeDtypeStruct((ntiles * tm, D_MODEL), F32)),
        grid_spec=pltpu.PrefetchScalarGridSpec(
            num_scalar_prefetch=1,
            grid=(ntiles,),
            in_specs=[
                pl.BlockSpec((tm * ROW_WORDS, LANES), lambda i, zends: (i, 0)),
                pl.BlockSpec(memory_space=pl.ANY),
                pl.BlockSpec((D_MODEL, SHARED_FF), const),
                pl.BlockSpec((D_MODEL, SHARED_FF), const),
                pl.BlockSpec((SHARED_FF, D_MODEL), const),
            ],
            out_specs=(pl.BlockSpec(memory_space=pl.ANY),
                       pl.BlockSpec((tm, D_MODEL), lambda i, zends: (i, 0))),
            scratch_shapes=[
                pltpu.SMEM((TOP_K * tm,), I32),
                pltpu.VMEM((EXPERT_ROWS * ROW_WORDS, LANES), I32),
                pltpu.SemaphoreType.DMA,
                pltpu.SemaphoreType.DMA,
                pltpu.SemaphoreType.DMA,
            ],
        ),
        compiler_params=_cparams(("arbitrary",)),
        name="moe_dispatch",
    )(zends, tokens, dest_tiles, wg, wu, wd)


def _experts_kernel(blk_e_ref, nsub_ref, next_e_ref, nused_ref, xs_ref, wg_hbm, wu_hbm, wd_hbm, y_ref,
                    wgs, wus, wds, wgb, wub, wdb, sem_w, *, layer):
    i = pl.program_id(0)
    e = blk_e_ref[i]
    prev = blk_e_ref[jnp.maximum(i - 1, 0)]

    def weight_copies(expert):
        row = layer * N_EXPERTS + expert
        return (pltpu.make_async_copy(wg_hbm.at[row], wgs, sem_w.at[0]),
                pltpu.make_async_copy(wu_hbm.at[row], wus, sem_w.at[1]),
                pltpu.make_async_copy(wd_hbm.at[row], wds, sem_w.at[2]))

    @pl.when(i < nused_ref[0])
    def _():
        @pl.when(i == 0)
        def _():
            for cp in weight_copies(e):
                cp.start()

        @pl.when((i == 0) | (e != prev))
        def _():
            for cp in weight_copies(e):
                cp.wait()
            wgb[...] = wgs[...].astype(BF16)
            wub[...] = wus[...].astype(BF16)
            wdb[...] = wds[...].astype(BF16)

            @pl.when(next_e_ref[i] >= 0)
            def _():
                for cp in weight_copies(next_e_ref[i]):
                    cp.start()

        def mlp(sb):
            tok0 = sb * EXPERT_ROWS
            x = _load_packed(xs_ref, EXPERT_ROWS, BF16, tok0)
            g = jnp.dot(x, wgb[...], preferred_element_type=F32)
            u = jnp.dot(x, wub[...], preferred_element_type=F32)
            hb = (_silu(g) * u).astype(BF16)
            _store_packed(y_ref, jnp.dot(hb, wdb[...], preferred_element_type=F32), tok0)

        def clear(sb):
            rows = EXPERT_ROWS * ROW_WORDS
            y_ref[sb * rows:(sb + 1) * rows, :] = jnp.zeros((rows, LANES), I32)

        nsub = nsub_ref[i]
        for a in range(0, EXPERT_SUBS, 2):
            @pl.when(nsub >= a + 2)
            def _():
                mlp(a)
                mlp(a + 1)

            @pl.when(nsub == a + 1)
            def _():
                mlp(a)
                clear(a + 1)

            @pl.when(nsub <= a)
            def _():
                clear(a)
                clear(a + 1)


def experts(xs, blk_e, nsub, next_e, nused, w_gate, w_up, w_down, *, layer):
    w_gate = w_gate.reshape(DEPTH * N_EXPERTS, D_MODEL, EXPERT_FF)
    w_up = w_up.reshape(DEPTH * N_EXPERTS, D_MODEL, EXPERT_FF)
    w_down = w_down.reshape(DEPTH * N_EXPERTS, EXPERT_FF, D_MODEL)

    def row_map(i, blk_e, nsub, next_e, nused):
        return (jnp.minimum(i, nused[0] - 1), 0)

    return pl.pallas_call(
        functools.partial(_experts_kernel, layer=layer),
        out_shape=jax.ShapeDtypeStruct((N_BLOCKS * STEP_ROWS * ROW_WORDS, LANES), I32),
        grid_spec=pltpu.PrefetchScalarGridSpec(
            num_scalar_prefetch=4,
            grid=(N_BLOCKS,),
            in_specs=[
                pl.BlockSpec((STEP_ROWS * ROW_WORDS, LANES), row_map),
                pl.BlockSpec(memory_space=pl.ANY),
                pl.BlockSpec(memory_space=pl.ANY),
                pl.BlockSpec(memory_space=pl.ANY),
            ],
            out_specs=pl.BlockSpec((STEP_ROWS * ROW_WORDS, LANES), row_map),
            scratch_shapes=[
                pltpu.VMEM((D_MODEL, EXPERT_FF), F32),
                pltpu.VMEM((D_MODEL, EXPERT_FF), F32),
                pltpu.VMEM((EXPERT_FF, D_MODEL), F32),
                pltpu.VMEM((D_MODEL, EXPERT_FF), BF16),
                pltpu.VMEM((D_MODEL, EXPERT_FF), BF16),
                pltpu.VMEM((EXPERT_FF, D_MODEL), BF16),
                pltpu.SemaphoreType.DMA((3,)),
            ],
        ),
        compiler_params=_cparams(("arbitrary",)),
        name="moe_experts",
    )(blk_e, nsub, next_e, nused, xs, w_gate, w_up, w_down)


def _combine_kernel(x_ref, sh_ref, mod_ref, wk_ref, gf_ref, dest_hbm, ys_hbm, o_ref,
                    idx0, idx1, buf0, buf1, sem_idx, sem_g, *, final):
    i = pl.program_id(0)
    n = pl.num_programs(0)
    tm = x_ref.shape[0]
    idx = (idx0, idx1)
    buf = (buf0, buf1)

    def idx_copy(tile, slot):
        return pltpu.make_async_copy(dest_hbm.at[tile], idx[slot], sem_idx)

    def start_rows(slot, r):
        for k in range(TOP_K):
            _row_copy(ys_hbm, idx[slot][k * tm + r], buf[slot].at[k], r, sem_g.at[slot]).start(priority=k % 2)

    def drain(slot):
        def wait_rows(r, carry):
            for k in range(TOP_K):
                _row_copy(ys_hbm, 0, buf[slot].at[k], r, sem_g.at[slot]).wait()
            return carry

        lax.fori_loop(0, tm, wait_rows, 0)

    def finish(slot):
        idx_copy(0, 1 - slot).wait()
        drain(slot)
        idx_copy(jnp.minimum(i + 2, n - 1), slot).start()
        per = tm // ROW_WORDS
        los, his = [], []
        for s in range(ROW_WORDS):
            for r in range(s * per, (s + 1) * per):
                start_rows(1 - slot, r)
            acc_lo = sh_ref[:, s * LANES:(s + 1) * LANES]
            acc_hi = sh_ref[:, HALF + s * LANES:HALF + (s + 1) * LANES]
            for k in range(TOP_K):
                lo, hi = _unpack_words(buf[slot][k, pl.ds(s, tm, stride=ROW_WORDS), :])
                w = wk_ref[:, k:k + 1]
                acc_lo = acc_lo + lo * w
                acc_hi = acc_hi + hi * w
            los.append(acc_lo)
            his.append(acc_hi)
        f = jnp.concatenate(los + his, axis=-1)
        out = x_ref[...] + mod_ref[0, 5:6, :] * f
        if final:
            y = out * lax.rsqrt(jnp.mean(out * out, axis=-1, keepdims=True) + EPS)
            out = y * gf_ref[...]
        o_ref[...] = out

        @pl.when(i == n - 1)
        def _():
            drain(1 - slot)
            idx_copy(0, slot).wait()

    @pl.when(i == 0)
    def _():
        idx_copy(0, 0).start()
        idx_copy(0, 0).wait()

        def first(r, carry):
            start_rows(0, r)
            return carry

        lax.fori_loop(0, tm, first, 0)
        idx_copy(jnp.minimum(1, n - 1), 1).start()

    for slot in range(2):
        @pl.when(i % 2 == slot)
        def _(slot=slot):
            finish(slot)


def combine(x, shared, mod, wk_t, g_final, dest_tiles, ys, *, ntiles, final):
    tm = 128
    tps = SEQ // tm
    return pl.pallas_call(
        functools.partial(_combine_kernel, final=final),
        out_shape=jax.ShapeDtypeStruct((ntiles * tm, D_MODEL), F32),
        grid=(ntiles,),
        in_specs=[
            pl.BlockSpec((tm, D_MODEL), lambda i: (i, 0)),
            pl.BlockSpec((tm, D_MODEL), lambda i: (i, 0)),
            pl.BlockSpec((1, 6, D_MODEL), lambda i: (_mod_row(i, tps), 0, 0)),
            pl.BlockSpec((tm, TOP_K), lambda i: (i, 0)),
            pl.BlockSpec((1, D_MODEL), lambda i: (0, 0)),
            pl.BlockSpec(memory_space=pl.ANY),
            pl.BlockSpec(memory_space=pl.ANY),
        ],
        out_specs=pl.BlockSpec((tm, D_MODEL), lambda i: (i, 0)),
        scratch_shapes=[
            pltpu.SMEM((TOP_K * tm,), I32),
            pltpu.SMEM((TOP_K * tm,), I32),
            pltpu.VMEM((TOP_K, tm * ROW_WORDS, LANES), I32),
            pltpu.VMEM((TOP_K, tm * ROW_WORDS, LANES), I32),
            pltpu.SemaphoreType.DMA,
            pltpu.SemaphoreType.DMA((2,)),
        ],
        compiler_params=_cparams(("arbitrary",)),
        name="moe_combine",
    )(x, shared, mod, wk_t, g_final.reshape(1, D_MODEL), dest_tiles, ys)


def _routing_plan(eidx, rank, tile_counts):
    t = eidx.shape[1]
    tiles = tile_counts.shape[1]
    counts = jnp.sum(tile_counts, axis=1)
    pcounts = (counts + STEP_ROWS - 1) // STEP_ROWS * STEP_ROWS
    pends = jnp.cumsum(pcounts)
    pstarts = pends - pcounts
    base = pstarts[:, None] + jnp.cumsum(tile_counts, axis=1) - tile_counts
    onehot = (eidx.reshape(TOP_K, tiles, 1, t // tiles)
              == jnp.arange(N_EXPERTS, dtype=I32)[None, None, :, None])
    dest = rank + jnp.sum(jnp.where(onehot, base.T[None, :, :, None], 0), axis=2).reshape(TOP_K, t)
    nused = (pends[-1] // STEP_ROWS).astype(I32).reshape(1)
    blocks = jnp.arange(N_BLOCKS, dtype=I32)
    blk_e = jnp.minimum(jnp.sum((pends[None, :] <= (blocks * STEP_ROWS)[:, None]).astype(I32), axis=1),
                        N_EXPERTS - 1).astype(I32)
    nsub_e = (counts + EXPERT_ROWS - 1) // EXPERT_ROWS
    zends = jnp.where(counts > 0, pstarts + nsub_e * EXPERT_ROWS, 0)
    done = (blocks - pstarts[blk_e] // STEP_ROWS) * EXPERT_SUBS
    nsub = jnp.clip(nsub_e[blk_e] - done, 0, EXPERT_SUBS)
    ids = jnp.where(counts > 0, jnp.arange(N_EXPERTS, dtype=I32), N_EXPERTS)
    later = jnp.concatenate([lax.cummin(ids[::-1])[::-1][1:], jnp.full((1,), N_EXPERTS, I32)])
    next_e = jnp.where(later < N_EXPERTS, later, -1)[blk_e]
    return dest.astype(I32), zends.astype(I32), blk_e, nsub.astype(I32), next_e.astype(I32), nused


def _tile_major(dest, tm):
    k, t = dest.shape
    return dest.reshape(k, t // tm, tm).transpose(1, 0, 2).reshape(t // tm, k * tm)


def _rope_tables(n_rows):
    row = jnp.repeat(jnp.arange(n_rows, dtype=F32), GRID_W)
    col = jnp.tile(jnp.arange(GRID_W, dtype=F32), n_rows)
    half = QK_ROPE // 2
    inv = ROPE_BASE ** (-jnp.arange(0, half, 2, dtype=F32) / half)
    ang_r = row[:, None] * inv
    ang_c = col[:, None] * inv
    ang = jnp.concatenate([ang_r, ang_r, ang_c, ang_c], axis=-1)
    n = ang.shape[0]
    ones = jnp.ones((n, QK_NOPE), F32)
    zeros = jnp.zeros((n, QK_NOPE), F32)
    tail = jnp.zeros((n, HEAD_PAD - QK_NOPE - QK_ROPE), F32)
    cos = jnp.concatenate([ones, jnp.cos(ang), tail], axis=-1)
    sin = jnp.concatenate([zeros, jnp.sin(ang), tail], axis=-1)
    return cos, sin


def _identity_tables(n):
    cos = jnp.concatenate([jnp.ones((n, QK_NOPE + QK_ROPE), F32),
                           jnp.zeros((n, HEAD_PAD - QK_NOPE - QK_ROPE), F32)], axis=-1)
    return cos, jnp.zeros((n, HEAD_PAD), F32)


def _rotate_cols(w):
    i = np.arange(QK_ROPE)
    first = (i % (QK_ROPE // 2)) < (QK_ROPE // 4)
    perm = np.where(first, i + QK_ROPE // 4, i - QK_ROPE // 4)
    sign = np.where(first, -1.0, 1.0).astype(np.float32)
    return w[:, perm] * sign


def _mla_weights(g_cq, w_uq, g_ckv, w_ukv):
    dk = QK_NOPE + QK_ROPE
    wq = w_uq.reshape(Q_LORA, MLA_HEADS, dk)
    pad = jnp.zeros((Q_LORA, MLA_HEADS, HEAD_PAD - dk), F32)
    zero_nope = jnp.zeros((Q_LORA, MLA_HEADS, QK_NOPE), F32)
    wq_rot = _rotate_cols(wq[..., QK_NOPE:].reshape(Q_LORA * MLA_HEADS, QK_ROPE)).reshape(
        Q_LORA, MLA_HEADS, QK_ROPE)
    wq_a = jnp.concatenate([wq, pad], axis=-1).reshape(Q_LORA, -1)
    wq_b = jnp.concatenate([zero_nope, wq_rot, pad], axis=-1).reshape(Q_LORA, -1)
    wkv = w_ukv.reshape(KV_LORA, MLA_HEADS, QK_NOPE + V_HEAD)
    wk = jnp.concatenate([wkv[..., :QK_NOPE],
                          jnp.zeros((KV_LORA, MLA_HEADS, HEAD_PAD - QK_NOPE), F32)], axis=-1)
    wv = wkv[..., QK_NOPE:]
    eye = jnp.eye(QK_ROPE, dtype=F32)
    place = jnp.zeros((LANES, MLA_HEADS, HEAD_PAD), F32)
    place_a = place.at[:QK_ROPE, :, QK_NOPE:dk].set(jnp.broadcast_to(eye[:, None, :], (QK_ROPE, MLA_HEADS, QK_ROPE)))
    rot = _rotate_cols(eye)
    place_b = place.at[:QK_ROPE, :, QK_NOPE:dk].set(jnp.broadcast_to(rot[:, None, :], (QK_ROPE, MLA_HEADS, QK_ROPE)))
    return {
        "g_cq": g_cq.reshape(1, -1), "g_ckv": g_ckv.reshape(1, -1),
        "wq2t": jnp.concatenate([wq_a, wq_b], axis=-1).T.astype(BF16),
        "wk": wk.reshape(KV_LORA, -1).astype(BF16),
        "wvt": wv.reshape(KV_LORA, -1).T.astype(BF16),
        "pk": jnp.concatenate([place_a.reshape(LANES, -1), place_b.reshape(LANES, -1)], axis=-1).astype(BF16),
    }


def _pack_w_in(w):
    gates0 = MLA_IN + 2 * CONV_CH + FFT_CH + POOL_CH
    pad = jnp.zeros((D_MODEL, P_MLA_W - MLA_IN), BF16)
    rest = jnp.concatenate([w[:, :MLA_IN].astype(BF16), pad, w[:, MLA_IN:gates0].astype(BF16)], axis=-1)
    return w[:, gates0:].astype(BF16), rest


def _dft_tables(n):
    gw = FFT_CH // FFT_GROUPS

    def angles(rows, m):
        k = jnp.arange(m, dtype=I32)
        return (rows[:, None] * k[None, :] % m).astype(F32) * (2.0 * np.pi / m)

    ac = angles(jnp.arange(gw, dtype=I32), gw)
    cs = jnp.concatenate([jnp.cos(ac), jnp.sin(ac)], axis=-1).astype(BF16)
    ns = int(np.sqrt(n))
    assert ns * ns == n
    a_hi = angles(jnp.arange(ns, dtype=I32) * ns, n)
    a_lo = angles(jnp.arange(ns, dtype=I32), n)
    ch, sh, cl, sl = jnp.cos(a_hi)[:, None, :], jnp.sin(a_hi)[:, None, :], jnp.cos(a_lo)[None], jnp.sin(a_lo)[None]
    cos_n = (ch * cl - sh * sl).reshape(n, n).astype(BF16)
    nsin_n = (-(sh * cl + ch * sl)).reshape(n, n).astype(BF16)
    return cs, cos_n, nsin_n


def _mixers(p, nseq, n, wts, tables, conv_args, pool_args, dft, **kv):
    q4, k4, v = mla_prep(p, wts, tables, row0=0, col0=P_MLA, nseq=nseq, n=n, **kv)
    cv = conv_branch(p, *conv_args, row0=0, nseq=nseq, n=n)
    ff = fft_branch(p, *dft, row0=0, nseq=nseq, n=n)
    po = pool_branch(p, *pool_args, row0=0, nseq=nseq, n=n)
    return q4, k4, v, cv, ff, po


def kernel(x, c, ctx, c_ctx, w_mod, b_mod, g_norm1, g_norm2, w_in, g_cq, w_uq, g_ckv, w_ukv, w_o_mla, conv_w, conv_b, conv_ln_g, conv_ln_b, w_o_conv, w_o_fft, pool_w, pool_scale, w_o_pool, w_out, w_router, router_bias, w_exp_gate, w_exp_up, w_exp_down, w_sh_gate, w_sh_up, w_sh_down, g_final):
    x_lat = x.reshape(T_LAT, D_MODEL)
    x_ctx = ctx.reshape(T_CTX, D_MODEL)
    ctx_tile0 = 0
    cvec = jnp.concatenate([c, c_ctx[None, :], jnp.zeros((SUBLANES - BATCH - 1, D_MODEL), F32)], axis=0)
    mod_all = modulation_all(cvec, w_mod, b_mod).reshape(DEPTH, SUBLANES, 6, D_MODEL)
    rope_lat = _rope_tables(SEQ // GRID_W)
    rope_ctx = _identity_tables(CTX_LEN)
    dft_lat = _dft_tables(SEQ)
    dft_ctx = _dft_tables(CTX_LEN)
    lat_tiles = T_LAT // ROW_TILE
    all_tiles = T_ALL // ROW_TILE

    for l in range(DEPTH):
        last = l == DEPTH - 1
        mod = mod_all[l]
        w_gate, w_rest = _pack_w_in(w_in[l])
        wts = _mla_weights(g_cq[l], w_uq[l], g_ckv[l], w_ukv[l])
        conv_args = (conv_w[l], conv_b[l], conv_ln_g[l], conv_ln_b[l])
        pool_args = (pool_w[l].astype(BF16), pool_scale[l])
        wo4 = jnp.stack([w_o_mla[l], w_o_conv[l], w_o_fft[l], w_o_pool[l]], axis=0).astype(BF16)
        w_out_b = w_out[l].astype(BF16)

        ctx_col0, ctx_cols = (P_MLA, P_MLA_W) if last else (0, P_COLS)
        p = normproj(x_lat, g_norm1[l], mod, w_gate, w_rest, tile0=0, ntiles=lat_tiles, col0=0, ncols=P_COLS)
        p_ctx = normproj(x_ctx, g_norm1[l], mod, w_gate, w_rest, tile0=ctx_tile0, ntiles=all_tiles - lat_tiles,
                         mod_tile0=lat_tiles, col0=ctx_col0, ncols=ctx_cols)
        nkeys = SEQ + CTX_LEN
        q4, k_all, v_all, cv, ff, po = _mixers(p, BATCH, SEQ, wts, rope_lat, conv_args, pool_args, dft_lat,
                                               n_keys=nkeys)
        if last:
            qc, k_all, v_all = mla_prep(p_ctx, wts, rope_ctx, row0=0, col0=0, nseq=BATCH, n=CTX_LEN,
                                        n_keys=nkeys, key0=SEQ, into=(k_all, v_all))
        else:
            qc, k_all, v_all, cvc, ffc, poc = _mixers(p_ctx, BATCH, CTX_LEN, wts, rope_ctx, conv_args,
                                                      pool_args, dft_ctx, n_keys=nkeys, key0=SEQ,
                                                      into=(k_all, v_all))
        at = attention(q4, k_all, v_all, nq=SEQ, nk=nkeys)
        lat_t, ctx_t = T_LAT // 256, T_CTX // 256
        if last:
            ntok = T_LAT
            xa = merge(x_lat, mod, at, cv, ff, po, p, wo4, w_out_b, ntiles=lat_t)
        else:
            ntok = T_ALL
            atc = attention(qc, k_all, v_all, nq=CTX_LEN, nk=CTX_LEN, key0=SEQ)
            xa = merge(x_lat, mod, at, cv, ff, po, p, wo4, w_out_b, ntiles=lat_t, out_tiles=lat_t + ctx_t)
            xa = merge(x_ctx, mod, atc, cvc, ffc, poc, p_ctx, wo4, w_out_b, ntiles=ctx_t, out_tile0=lat_t,
                       out_tiles=lat_t + ctx_t, into=xa)

        wr_t = w_router[l].T
        wr_hi = wr_t.astype(BF16)
        wr_lo = (wr_t - wr_hi.astype(F32)).astype(BF16)
        tokens, eidx, wk, rank, cnt = norm_router(xa, g_norm2[l], mod, wr_hi, wr_lo, router_bias[l],
                                                  ntiles=ntok // ROUTER_TILE)
        dest, zends, blk_e, nsub, next_e, nused = _routing_plan(eidx, rank, cnt[:, ::LANES])
        xs, sh = dispatch_shared(tokens, _tile_major(dest, 256), zends, w_sh_gate[l].astype(BF16),
                                 w_sh_up[l].astype(BF16), w_sh_down[l].astype(BF16), ntiles=ntok // 256)
        ys = experts(xs, blk_e, nsub, next_e, nused, w_exp_gate, w_exp_up, w_exp_down, layer=l)
        xa = combine(xa, sh, mod, wk.T, g_final, _tile_major(dest, 128), ys, ntiles=ntok // 128, final=last)
        x_lat, x_ctx, ctx_tile0 = xa, xa, lat_tiles
    return xa.reshape(BATCH, SEQ, D_MODEL)
```

```python
import functools

import jax
import jax.numpy as jnp
import numpy as np
from jax import lax
from jax.experimental import pallas as pl
from jax.experimental.pallas import tpu as pltpu

F32 = jnp.float32
BF16 = jnp.bfloat16
I32 = jnp.int32

D_MODEL = 2048
BATCH = 4
SEQ = 4096
DEPTH = 2
GRID_W = 64
CTX_LEN = 256
MLA_HEADS = 8
QK_NOPE = 64
QK_ROPE = 32
V_HEAD = 64
Q_LORA = 512
KV_LORA = 256
ROPE_BASE = 10000.0
CONV_CH = 512
CONV_WIDTH = 31
FFT_CH = 512
FFT_GROUPS = 4
POOL_CH = 512
POOL_WINDOWS = (2, 4, 8, 16)
POOL_GROUP = POOL_CH // len(POOL_WINDOWS)
N_BRANCHES = 4
N_EXPERTS = 64
N_EXPERT_GROUPS = 8
TOPK_GROUPS = 4
TOP_K = 8
EXPERT_FF = 512
SHARED_FF = 512
ROUTED_SCALE = 2.5
EPS = 1e-6

MLA_IN = Q_LORA + KV_LORA + QK_ROPE
T_LAT = BATCH * SEQ
T_CTX = BATCH * CTX_LEN
T_ALL = T_LAT + T_CTX

LANES = 128
SUBLANES = 8
VMEM_LIMIT_BYTES = 56 * 1024 * 1024

P_GATE = 0
P_MLA = N_BRANCHES * D_MODEL
P_MLA_W = 1024
P_CONV = P_MLA + P_MLA_W
P_FFT = P_CONV + 2 * CONV_CH
P_POOL = P_FFT + FFT_CH
P_COLS = P_POOL + POOL_CH
HEAD_PAD = 128
ATT_SCALE = (QK_NOPE + QK_ROPE) ** -0.5
Q_SCALE = ATT_SCALE * float(np.log2(np.e))

ROW_TILE = 1024
ROUTER_TILE = 256
EXPERT_ROWS = 256
EXPERT_SUBS = 4
STEP_ROWS = EXPERT_ROWS * EXPERT_SUBS
N_BLOCKS = -(-(T_ALL * TOP_K) // STEP_ROWS) + N_EXPERTS


def _cparams(sem, vmem=VMEM_LIMIT_BYTES):
    return pltpu.CompilerParams(dimension_semantics=sem, vmem_limit_bytes=vmem)


def _sigmoid(x):
    return 1.0 / (1.0 + jnp.exp(-x))


def _silu(x):
    return x * _sigmoid(x)


def _mod_row(tile, tiles_per_seq):
    return jnp.minimum(tile // tiles_per_seq, BATCH)


def _modulation_kernel(c_ref, w_ref, b_ref, o_ref):
    c = c_ref[...]
    s = _silu(c).astype(BF16)
    o_ref[0] = jnp.dot(s, w_ref[0].astype(BF16), preferred_element_type=F32) + b_ref[0]


def modulation_all(cvec, w_mod, b_mod):
    n = 6 * D_MODEL
    tn = 2048 if n % 2048 == 0 else 1024
    return pl.pallas_call(
        _modulation_kernel,
        out_shape=jax.ShapeDtypeStruct((DEPTH, SUBLANES, n), F32),
        grid=(DEPTH, n // tn),
        in_specs=[
            pl.BlockSpec((SUBLANES, D_MODEL), lambda l, j: (0, 0)),
            pl.BlockSpec((1, D_MODEL, tn), lambda l, j: (l, 0, j)),
            pl.BlockSpec((1, 1, tn), lambda l, j: (l, 0, j)),
        ],
        out_specs=pl.BlockSpec((1, SUBLANES, tn), lambda l, j: (l, 0, j)),
        compiler_params=_cparams(("arbitrary", "arbitrary")),
        name="modulation",
    )(cvec, w_mod, b_mod.reshape(DEPTH, 1, n))


def _adaln(x, g, shift, scale):
    y = x * lax.rsqrt(jnp.mean(x * x, axis=-1, keepdims=True) + EPS)
    return (y * g) * (1.0 + scale) + shift


def _normproj_kernel(x_ref, g_ref, mod_ref, wg_ref, wr_ref, o_ref, h_ref, *, shift_row, chunk, cb0, ngate):
    j = pl.program_id(1)

    @pl.when(j == 0)
    def _():
        shift = mod_ref[0, shift_row:shift_row + 1, :]
        scale = mod_ref[0, shift_row + 1:shift_row + 2, :]
        for r in range(0, x_ref.shape[0], chunk):
            h = _adaln(x_ref[r:r + chunk, :], g_ref[...], shift, scale)
            h_ref[r:r + chunk, :] = h.astype(BF16)

    @pl.when(j + cb0 < ngate)
    def _():
        o_ref[...] = jnp.dot(h_ref[...], wg_ref[...], preferred_element_type=F32).astype(o_ref.dtype)

    @pl.when(j + cb0 >= ngate)
    def _():
        o_ref[...] = jnp.dot(h_ref[...], wr_ref[...], preferred_element_type=F32).astype(o_ref.dtype)


def normproj(x, g, mod, w_gate, w_rest, *, tile0, ntiles, col0, ncols, mod_tile0=None, shift_row=0):
    tm, tn = ROW_TILE, 1024
    tps = SEQ // tm
    cb0 = col0 // tn
    ngate = w_gate.shape[1] // tn
    mt0 = tile0 if mod_tile0 is None else mod_tile0
    return pl.pallas_call(
        functools.partial(_normproj_kernel, shift_row=shift_row, chunk=256, cb0=cb0, ngate=ngate),
        out_shape=jax.ShapeDtypeStruct((ntiles * tm, ncols), BF16),
        grid=(ntiles, ncols // tn),
        in_specs=[
            pl.BlockSpec((tm, D_MODEL), lambda i, j: (i + tile0, 0)),
            pl.BlockSpec((1, D_MODEL), lambda i, j: (0, 0)),
            pl.BlockSpec((1, 6, D_MODEL), lambda i, j: (_mod_row(i + mt0, tps), 0, 0)),
            pl.BlockSpec((D_MODEL, tn), lambda i, j: (0, jnp.minimum(j + cb0, ngate - 1))),
            pl.BlockSpec((D_MODEL, tn), lambda i, j: (0, jnp.maximum(j + cb0 - ngate, 0))),
        ],
        out_specs=pl.BlockSpec((tm, tn), lambda i, j: (i, j)),
        scratch_shapes=[pltpu.VMEM((tm, D_MODEL), BF16)],
        compiler_params=_cparams(("arbitrary", "arbitrary")),
        name="normproj",
    )(x, g.reshape(1, D_MODEL), mod, w_gate, w_rest)


def _rms_gain(x, g):
    y = x * lax.rsqrt(jnp.mean(x * x, axis=-1, keepdims=True) + EPS)
    return (y * g).astype(BF16)


_NT = (((1,), (1,)), ((), ()))


def _mla_prep_kernel(p_ref, gq_ref, gkv_ref, wqt_ref, wk_ref, wvt_ref, pk_ref, cos_ref, sin_ref,
                     cost_ref, sint_ref, qt_ref, k_ref, vt_ref):
    hw = MLA_HEADS * HEAD_PAD
    cqn = _rms_gain(p_ref[:, :Q_LORA].astype(F32), gq_ref[...])
    ckvn = _rms_gain(p_ref[:, Q_LORA:Q_LORA + KV_LORA].astype(F32), gkv_ref[...])
    kr = p_ref[:, Q_LORA + KV_LORA:Q_LORA + KV_LORA + LANES]
    q2t = lax.dot_general(wqt_ref[...], cqn, _NT, preferred_element_type=F32)
    vt_ref[0] = lax.dot_general(wvt_ref[...], ckvn, _NT, preferred_element_type=F32).astype(BF16)
    kk = jnp.dot(ckvn, wk_ref[...], preferred_element_type=F32)
    kr2 = jnp.dot(kr, pk_ref[...], preferred_element_type=F32)
    cos, sin = cos_ref[...], sin_ref[...]
    cost, sint = cost_ref[...], sint_ref[...]
    for h in range(MLA_HEADS):
        a = slice(h * HEAD_PAD, (h + 1) * HEAD_PAD)
        b = slice(hw + h * HEAD_PAD, hw + (h + 1) * HEAD_PAD)
        qt_ref[0, h] = ((q2t[a, :] * cost + q2t[b, :] * sint) * Q_SCALE).astype(BF16)
        k_ref[0, h] = (kk[:, a] + kr2[:, a] * cos + kr2[:, b] * sin).astype(BF16)


def _mla_prep_into_kernel(*refs):
    _mla_prep_kernel(*refs[:11], *refs[13:])


def mla_prep(p, wts, tables, *, row0, col0, nseq, n, n_keys=None, key0=0, into=None):
    tm = 256
    tps = n // tm
    hw = MLA_HEADS * HEAD_PAD
    vw = MLA_HEADS * V_HEAD
    blk0 = row0 // tm
    cblk = col0 // P_MLA_W
    n_keys = n if n_keys is None else n_keys
    kb0 = key0 // tm
    cos_t, sin_t = tables
    const = lambda i: (0, 0)
    in_specs = [
        pl.BlockSpec((tm, P_MLA_W), lambda i: (i + blk0, cblk)),
        pl.BlockSpec((1, Q_LORA), const),
        pl.BlockSpec((1, KV_LORA), const),
        pl.BlockSpec((2 * hw, Q_LORA), const),
        pl.BlockSpec((KV_LORA, hw), const),
        pl.BlockSpec((vw, KV_LORA), const),
        pl.BlockSpec((LANES, 2 * hw), const),
        pl.BlockSpec((tm, HEAD_PAD), lambda i: (i % tps, 0)),
        pl.BlockSpec((tm, HEAD_PAD), lambda i: (i % tps, 0)),
        pl.BlockSpec((HEAD_PAD, tm), lambda i: (0, i % tps)),
        pl.BlockSpec((HEAD_PAD, tm), lambda i: (0, i % tps)),
    ]
    operands = [p, wts["g_cq"], wts["g_ckv"], wts["wq2t"], wts["wk"], wts["wvt"], wts["pk"],
                cos_t, sin_t, cos_t.T, sin_t.T]
    kernel_fn, aliases = _mla_prep_kernel, {}
    if into is not None:
        in_specs += [pl.BlockSpec(memory_space=pl.ANY), pl.BlockSpec(memory_space=pl.ANY)]
        operands += list(into)
        kernel_fn, aliases = _mla_prep_into_kernel, {11: 1, 12: 2}
    return pl.pallas_call(
        kernel_fn,
        out_shape=(jax.ShapeDtypeStruct((nseq, MLA_HEADS, HEAD_PAD, n), BF16),
                   jax.ShapeDtypeStruct((nseq, MLA_HEADS, n_keys, HEAD_PAD), BF16),
                   jax.ShapeDtypeStruct((nseq, vw, n_keys), BF16)),
        grid=(nseq * tps,),
        in_specs=in_specs,
        out_specs=(pl.BlockSpec((1, MLA_HEADS, HEAD_PAD, tm), lambda i: (i // tps, 0, 0, i % tps)),
                   pl.BlockSpec((1, MLA_HEADS, tm, HEAD_PAD), lambda i: (i // tps, 0, i % tps + kb0, 0)),
                   pl.BlockSpec((1, vw, tm), lambda i: (i // tps, 0, i % tps + kb0))),
        input_output_aliases=aliases,
        compiler_params=_cparams(("arbitrary",)),
        name="mla_prep",
    )(*operands)


ATT_HEADS = 8


def _attn_kernel(qt_ref, k_ref, vt_ref, o_ref, s_ref, p_ref, *, nk, ck):
    tq = qt_ref.shape[3]
    chunks = [slice(c, c + ck) for c in range(0, nk, ck)]
    nh = ATT_HEADS
    m = [None] * nh
    l = [jnp.zeros((1, tq), F32) for _ in range(nh)]
    ot = [jnp.zeros((V_HEAD, tq), F32) for _ in range(nh)]
    for step in range(nh + 2):
        ha, hb, hc = step, step - 1, step - 2
        for c in chunks:
            if 0 <= ha < nh:
                s = jnp.dot(k_ref[0, ha, c, :], qt_ref[0, ha], preferred_element_type=F32)
                s_ref[ha % 2, c, :] = s
                mc = jnp.max(s, axis=0, keepdims=True)
                m[ha] = mc if m[ha] is None else jnp.maximum(m[ha], mc)
            if 0 <= hb < nh:
                p = jnp.exp2(s_ref[hb % 2, c, :] - m[hb])
                l[hb] = l[hb] + jnp.sum(p, axis=0, keepdims=True)
                p_ref[hb % 2, c, :] = p.astype(BF16)
            if 0 <= hc < nh:
                ot[hc] = ot[hc] + jnp.dot(vt_ref[0, hc * V_HEAD:(hc + 1) * V_HEAD, c], p_ref[hc % 2, c, :],
                                          preferred_element_type=F32)
    outs = [ot[h] / l[h] for h in range(nh)]
    o_ref[...] = jnp.concatenate(outs, axis=0).T.astype(o_ref.dtype)


def attention(qt4, k4, vt3, *, nq, nk, key0=0):
    tq = 256
    ck = 256
    nqt = nq // tq
    hs = ATT_HEADS
    kb = key0 // nk
    return pl.pallas_call(
        functools.partial(_attn_kernel, nk=nk, ck=ck),
        out_shape=jax.ShapeDtypeStruct((BATCH * nq, MLA_HEADS * V_HEAD), BF16),
        grid=(BATCH, MLA_HEADS // hs, nqt),
        in_specs=[
            pl.BlockSpec((1, hs, HEAD_PAD, tq), lambda b, hg, i: (b, hg, 0, i)),
            pl.BlockSpec((1, hs, nk, HEAD_PAD), lambda b, hg, i: (b, hg, kb, 0)),
            pl.BlockSpec((1, hs * V_HEAD, nk), lambda b, hg, i: (b, hg, kb)),
        ],
        out_specs=pl.BlockSpec((tq, hs * V_HEAD), lambda b, hg, i: (b * nqt + i, hg)),
        scratch_shapes=[pltpu.VMEM((2, nk, tq), F32), pltpu.VMEM((2, nk, tq), BF16)],
        compiler_params=_cparams(("arbitrary", "arbitrary", "arbitrary")),
        name="attention",
    )(qt4, k4, vt3)


CONV_PAD = 16


def _conv_kernel(p_ref, w_ref, cb_ref, lg_ref, lb_ref, o_ref, u_ref, *, n, tr):
    zeros = jnp.zeros((CONV_PAD, CONV_CH), F32)
    u_ref[0:CONV_PAD, :] = zeros
    u_ref[CONV_PAD + n:2 * CONV_PAD + n, :] = zeros

    def glu(i, carry):
        r = pl.multiple_of(i * tr, tr)
        a = p_ref[pl.ds(r, tr), 0:CONV_CH].astype(F32)
        b = p_ref[pl.ds(r, tr), CONV_CH:2 * CONV_CH].astype(F32)
        u_ref[pl.ds(CONV_PAD + r, tr), :] = a * _sigmoid(b)
        return carry

    lax.fori_loop(0, n // tr, glu, 0)
    off = CONV_PAD - CONV_WIDTH // 2
    win_rows = tr + 2 * CONV_PAD

    def conv(i, carry):
        r = pl.multiple_of(i * tr, tr)
        strips = []
        for c0 in range(0, CONV_CH, LANES):
            win = u_ref[pl.ds(r, win_rows), c0:c0 + LANES]
            acc = jnp.zeros((tr, LANES), F32)
            for b in range(SUBLANES):
                wb = pltpu.roll(win, win_rows - (off + b), axis=0)
                for k in range(b, CONV_WIDTH, SUBLANES):
                    acc = acc + wb[k - b:k - b + tr, :] * w_ref[k:k + 1, c0:c0 + LANES]
            strips.append(acc)
        acc = jnp.concatenate(strips, axis=-1) + cb_ref[...]
        mu = jnp.mean(acc, axis=-1, keepdims=True)
        d = acc - mu
        var = jnp.mean(d * d, axis=-1, keepdims=True)
        y = (d * lax.rsqrt(var + EPS)) * lg_ref[...] + lb_ref[...]
        o_ref[pl.ds(r, tr), :] = _silu(y).astype(o_ref.dtype)
        return carry

    lax.fori_loop(0, n // tr, conv, 0)


def conv_branch(p, conv_w, conv_b, ln_g, ln_b, *, row0, nseq, n):
    tr = 128
    blk0 = row0 // n
    const = lambda s: (0, 0)
    return pl.pallas_call(
        functools.partial(_conv_kernel, n=n, tr=tr),
        out_shape=jax.ShapeDtypeStruct((nseq * n, CONV_CH), BF16),
        grid=(nseq,),
        in_specs=[
            pl.BlockSpec((n, 2 * CONV_CH), lambda s: (s + blk0, P_CONV // (2 * CONV_CH))),
            pl.BlockSpec((CONV_WIDTH, CONV_CH), const),
            pl.BlockSpec((1, CONV_CH), const),
            pl.BlockSpec((1, CONV_CH), const),
            pl.BlockSpec((1, CONV_CH), const),
        ],
        out_specs=pl.BlockSpec((n, CONV_CH), lambda s: (s, 0)),
        scratch_shapes=[pltpu.VMEM((n + 2 * CONV_PAD, CONV_CH), F32)],
        compiler_params=_cparams(("arbitrary",)),
        name="conv_branch",
    )(p, conv_w, conv_b.reshape(1, -1), ln_g.reshape(1, -1), ln_b.reshape(1, -1))


def _fft_kernel(z_ref, cs_ref, mc_ref, ms_ref, o_ref, ab_ref, *, n, tc):
    gw = FFT_CH // FFT_GROUPS

    @pl.when(pl.program_id(1) == 0)
    def _():
        def chan(i, carry):
            r = pl.multiple_of(i * tc, tc)
            for g in range(FFT_GROUPS):
                zg = z_ref[pl.ds(r, tc), g * gw:(g + 1) * gw]
                ab = jnp.dot(zg, cs_ref[...], preferred_element_type=F32)
                ab_ref[pl.ds(r, tc), g * gw:(g + 1) * gw] = ab[:, :gw].astype(BF16)
                ab_ref[pl.ds(n + r, tc), g * gw:(g + 1) * gw] = ab[:, gw:].astype(BF16)
            return carry

        lax.fori_loop(0, n // tc, chan, 0)

    norm = 1.0 / float(np.sqrt(n * gw))
    re = (jnp.dot(mc_ref[...], ab_ref[0:n, :], preferred_element_type=F32)
          + jnp.dot(ms_ref[...], ab_ref[n:2 * n, :], preferred_element_type=F32))
    o_ref[...] = (re * norm).astype(o_ref.dtype)


def fft_branch(p, cs, mcos, mnsin, *, row0, nseq, n):
    tm = min(n, 512)
    tc = min(n, 512)
    blk0 = row0 // n
    nt = n // tm
    return pl.pallas_call(
        functools.partial(_fft_kernel, n=n, tc=tc),
        out_shape=jax.ShapeDtypeStruct((nseq * n, FFT_CH), BF16),
        grid=(nseq, nt),
        in_specs=[
            pl.BlockSpec((n, FFT_CH), lambda s, i: (s + blk0, P_FFT // FFT_CH)),
            pl.BlockSpec((FFT_CH // FFT_GROUPS, 2 * FFT_CH // FFT_GROUPS), lambda s, i: (0, 0)),
            pl.BlockSpec((tm, n), lambda s, i: (i, 0)),
            pl.BlockSpec((tm, n), lambda s, i: (i, 0)),
        ],
        out_specs=pl.BlockSpec((tm, FFT_CH), lambda s, i: (s * nt + i, 0)),
        scratch_shapes=[pltpu.VMEM((2 * n, FFT_CH), BF16)],
        compiler_params=_cparams(("arbitrary", "arbitrary")),
        name="fft_branch",
    )(p, cs, mcos, mnsin)


POOL_PAD = 16


def _pool_kernel(z_ref, pw_ref, ps_ref, o_ref, zp_ref, d_ref, *, n, tr, tc):
    zeros = jnp.zeros((POOL_PAD, POOL_CH), F32)
    zp_ref[0:POOL_PAD, :] = zeros
    zp_ref[POOL_PAD + n:2 * POOL_PAD + n, :] = zeros

    def fill(i, carry):
        r = pl.multiple_of(i * tc, tc)
        zp_ref[pl.ds(POOL_PAD + r, tc), :] = z_ref[pl.ds(r, tc), :].astype(F32)
        return carry

    lax.fori_loop(0, n // tc, fill, 0)

    win_rows = tr + 2 * POOL_PAD

    def pool(i, carry):
        r = pl.multiple_of(i * tr, tr)
        t = (r + lax.broadcasted_iota(I32, (tr, POOL_GROUP), 0)).astype(F32)
        for gi, w in enumerate(POOL_WINDOWS):
            cols = slice(gi * POOL_GROUP, (gi + 1) * POOL_GROUP)
            win = zp_ref[pl.ds(r, win_rows), cols]
            s = jnp.zeros((tr, POOL_GROUP), F32)
            for j in range(w):
                start = POOL_PAD - w // 2 + j
                if start % SUBLANES == 0:
                    s = s + win[start:start + tr, :]
                else:
                    s = s + pltpu.roll(win, win_rows - start, axis=0)[0:tr, :]
            lo = jnp.maximum(t - (w // 2), 0.0)
            hi = jnp.minimum(t - (w // 2) + w, float(n))
            z = win[POOL_PAD:POOL_PAD + tr, :]
            d_ref[pl.ds(r, tr), cols] = (s / (hi - lo) - z).astype(BF16)
        return carry

    lax.fori_loop(0, n // tr, pool, 0)

    def proj(i, carry):
        r = pl.multiple_of(i * tc, tc)
        for gi in range(len(POOL_WINDOWS)):
            cols = slice(gi * POOL_GROUP, (gi + 1) * POOL_GROUP)
            y = jnp.dot(d_ref[pl.ds(r, tc), cols], pw_ref[gi], preferred_element_type=F32)
            o_ref[pl.ds(r, tc), cols] = (y * ps_ref[:, cols]).astype(o_ref.dtype)
        return carry

    lax.fori_loop(0, n // tc, proj, 0)


def pool_branch(p, pool_w, pool_scale, *, row0, nseq, n):
    tr = 128
    tc = min(n, 512)
    blk0 = row0 // n
    return pl.pallas_call(
        functools.partial(_pool_kernel, n=n, tr=tr, tc=tc),
        out_shape=jax.ShapeDtypeStruct((nseq * n, POOL_CH), BF16),
        grid=(nseq,),
        in_specs=[
            pl.BlockSpec((n, POOL_CH), lambda s: (s + blk0, P_POOL // POOL_CH)),
            pl.BlockSpec((len(POOL_WINDOWS), POOL_GROUP, POOL_GROUP), lambda s: (0, 0, 0)),
            pl.BlockSpec((1, POOL_CH), lambda s: (0, 0)),
        ],
        out_specs=pl.BlockSpec((n, POOL_CH), lambda s: (s, 0)),
        scratch_shapes=[pltpu.VMEM((n + 2 * POOL_PAD, POOL_CH), F32),
                        pltpu.VMEM((n, POOL_CH), BF16)],
        compiler_params=_cparams(("arbitrary",)),
        name="pool_branch",
    )(p, pool_w, pool_scale.reshape(1, -1))


def _merge_kernel(x_ref, mod_ref, at_ref, cv_ref, ff_ref, po_ref, g_ref, wo_ref, wout_ref, o_ref):
    mix = None
    for b, br in enumerate((at_ref, cv_ref, ff_ref, po_ref)):
        y = jnp.dot(br[...], wo_ref[b], preferred_element_type=F32)
        gate = _sigmoid(g_ref[:, b * D_MODEL:(b + 1) * D_MODEL].astype(F32))
        mix = gate * y if mix is None else mix + gate * y
    out = jnp.dot(mix.astype(BF16), wout_ref[...], preferred_element_type=F32)
    o_ref[...] = x_ref[...] + mod_ref[0, 2:3, :] * out


def _merge_into_kernel(x_ref, mod_ref, at_ref, cv_ref, ff_ref, po_ref, g_ref, wo_ref, wout_ref, prev_ref, o_ref):
    del prev_ref
    _merge_kernel(x_ref, mod_ref, at_ref, cv_ref, ff_ref, po_ref, g_ref, wo_ref, wout_ref, o_ref)


def merge(x, mod, attn, conv, fft, pool, p, wo4, w_out, *, ntiles, out_tile0=0, out_tiles=None, into=None):
    tm = 256
    tps = SEQ // tm
    out_tiles = ntiles if out_tiles is None else out_tiles
    bw = MLA_HEADS * V_HEAD
    assert bw == CONV_CH == FFT_CH == POOL_CH
    br_spec = pl.BlockSpec((tm, bw), lambda i: (i, 0))
    in_specs = [
        pl.BlockSpec((tm, D_MODEL), lambda i: (i, 0)),
        pl.BlockSpec((1, 6, D_MODEL), lambda i: (_mod_row(i + out_tile0, tps), 0, 0)),
        br_spec, br_spec, br_spec, br_spec,
        pl.BlockSpec((tm, N_BRANCHES * D_MODEL), lambda i: (i, P_GATE // (N_BRANCHES * D_MODEL))),
        pl.BlockSpec((N_BRANCHES, bw, D_MODEL), lambda i: (0, 0, 0), pipeline_mode=pl.Buffered(1)),
        pl.BlockSpec((D_MODEL, D_MODEL), lambda i: (0, 0), pipeline_mode=pl.Buffered(1)),
    ]
    operands = [x, mod, attn, conv, fft, pool, p, wo4, w_out]
    kernel_fn, aliases = _merge_kernel, {}
    if into is not None:
        in_specs.append(pl.BlockSpec(memory_space=pl.ANY))
        operands.append(into)
        kernel_fn, aliases = _merge_into_kernel, {len(operands) - 1: 0}
    return pl.pallas_call(
        kernel_fn,
        out_shape=jax.ShapeDtypeStruct((out_tiles * tm, D_MODEL), F32),
        grid=(ntiles,),
        in_specs=in_specs,
        out_specs=pl.BlockSpec((tm, D_MODEL), lambda i: (i + out_tile0, 0)),
        input_output_aliases=aliases,
        compiler_params=_cparams(("arbitrary",)),
        name="merge",
    )(*operands)


def _first_index(hit_value, cand, ids, big):
    return jnp.min(jnp.where(cand == hit_value, ids, big), axis=0, keepdims=True)


HALF = D_MODEL // 2
ROW_WORDS = HALF // LANES


def _round_bf16_bits(x):
    u = lax.bitcast_convert_type(x, I32)
    odd = lax.shift_right_logical(u, 16) & 1
    return (u + 0x7FFF + odd) & jnp.int32(-65536)


def _pack_words(lo, hi):
    return lax.shift_right_logical(_round_bf16_bits(lo), 16) | _round_bf16_bits(hi)


def _unpack_words(w):
    lo = lax.bitcast_convert_type(lax.shift_left(w, 16), F32)
    hi = lax.bitcast_convert_type(w & jnp.int32(-65536), F32)
    return lo, hi


def _store_packed(ref, x, tok0=0):
    tm = x.shape[0]
    for s in range(ROW_WORDS):
        w = _pack_words(x[:, s * LANES:(s + 1) * LANES], x[:, HALF + s * LANES:HALF + (s + 1) * LANES])
        ref[pl.ds(tok0 * ROW_WORDS + s, tm, stride=ROW_WORDS), :] = w


def _load_packed(ref, tm, dtype, tok0=0):
    los, his = [], []
    for s in range(ROW_WORDS):
        lo, hi = _unpack_words(ref[pl.ds(tok0 * ROW_WORDS + s, tm, stride=ROW_WORDS), :])
        los.append(lo.astype(dtype))
        his.append(hi.astype(dtype))
    return jnp.concatenate(los + his, axis=-1)


def _router_kernel(x_ref, g_ref, mod_ref, wrh_ref, wrl_ref, rb_ref, tri_ref,
                   tok_ref, eidx_ref, wk_ref, rank_ref, cnt_ref):
    tm = x_ref.shape[0]
    h = _adaln(x_ref[...], g_ref[...], mod_ref[0, 3:4, :], mod_ref[0, 4:5, :])
    _store_packed(tok_ref, h)
    hh = h.astype(BF16)
    hl = (h - hh.astype(F32)).astype(BF16)
    nt = (((1,), (1,)), ((), ()))
    logits = (lax.dot_general(wrh_ref[...], hh, nt, preferred_element_type=F32)
              + lax.dot_general(wrh_ref[...], hl, nt, preferred_element_type=F32)
              + lax.dot_general(wrl_ref[...], hh, nt, preferred_element_type=F32))
    scores = _sigmoid(logits)
    sel = scores + rb_ref[...]
    per = N_EXPERTS // N_EXPERT_GROUPS
    assert per == SUBLANES and N_EXPERT_GROUPS == SUBLANES and TOP_K == SUBLANES
    neg = -jnp.inf
    sub = lax.broadcasted_iota(I32, (SUBLANES, tm), 0).astype(F32)
    sg = [sel[g * per:(g + 1) * per, :] for g in range(N_EXPERT_GROUPS)]
    sc = [scores[g * per:(g + 1) * per, :] for g in range(N_EXPERT_GROUPS)]
    gsc = jnp.zeros((SUBLANES, tm), F32)
    for g in range(N_EXPERT_GROUPS):
        m1 = jnp.max(sg[g], axis=0, keepdims=True)
        i1 = _first_index(m1, sg[g], sub, float(per))
        m2 = jnp.max(jnp.where(sub == i1, neg, sg[g]), axis=0, keepdims=True)
        gsc = jnp.where(sub == float(g), m1 + m2, gsc)
    gsel = jnp.zeros((SUBLANES, tm), F32)
    for _ in range(TOPK_GROUPS):
        m = jnp.max(gsc, axis=0, keepdims=True)
        hit = sub == _first_index(m, gsc, sub, float(N_EXPERT_GROUPS))
        gsel = jnp.where(hit, 1.0, gsel)
        gsc = jnp.where(hit, neg, gsc)
    cand = []
    for g in range(N_EXPERT_GROUPS):
        allowed = jnp.max(jnp.where(sub == float(g), gsel, 0.0), axis=0, keepdims=True)
        cand.append(jnp.where(allowed > 0.0, sg[g], neg))
    eid = [sub + float(g * per) for g in range(N_EXPERT_GROUPS)]
    idxs = jnp.zeros((SUBLANES, tm), F32)
    vals = jnp.zeros((SUBLANES, tm), F32)
    picked = [jnp.zeros((per, tm), F32) for _ in range(N_EXPERT_GROUPS)]
    idx_k = []
    for k in range(TOP_K):
        m = functools.reduce(jnp.maximum, [jnp.max(c, axis=0, keepdims=True) for c in cand])
        idx = functools.reduce(
            jnp.minimum, [_first_index(m, cand[g], eid[g], float(N_EXPERTS)) for g in range(N_EXPERT_GROUPS)])
        val = jnp.zeros((1, tm), F32)
        for g in range(N_EXPERT_GROUPS):
            hit = eid[g] == idx
            val = val + jnp.sum(jnp.where(hit, sc[g], 0.0), axis=0, keepdims=True)
            cand[g] = jnp.where(hit, neg, cand[g])
            picked[g] = jnp.where(hit, 1.0, picked[g])
        idxs = jnp.where(sub == float(k), idx, idxs)
        vals = jnp.where(sub == float(k), val, vals)
        idx_k.append(idx)
    eidx_ref[...] = idxs.astype(I32)
    wk_ref[...] = vals / jnp.sum(vals, axis=0, keepdims=True) * ROUTED_SCALE
    mask = jnp.concatenate(picked, axis=0)
    before = jnp.dot(mask.astype(BF16), tri_ref[...], preferred_element_type=F32)
    ranks = jnp.zeros((SUBLANES, tm), F32)
    for k in range(TOP_K):
        r = jnp.zeros((1, tm), F32)
        for g in range(N_EXPERT_GROUPS):
            r = r + jnp.sum(jnp.where(eid[g] == idx_k[k], before[g * per:(g + 1) * per, :], 0.0),
                            axis=0, keepdims=True)
        ranks = jnp.where(sub == float(k), r, ranks)
    rank_ref[...] = ranks.astype(I32)
    cnt_ref[...] = jnp.broadcast_to(jnp.sum(mask, axis=1, keepdims=True), (N_EXPERTS, LANES)).astype(I32)


def norm_router(x, g, mod, wr_hi, wr_lo, rbias, *, ntiles):
    tm = ROUTER_TILE
    tps = SEQ // tm
    t = ntiles * tm
    tri = jnp.triu(jnp.ones((tm, tm), BF16), k=1)
    kt_spec = pl.BlockSpec((TOP_K, tm), lambda i: (0, i))
    return pl.pallas_call(
        _router_kernel,
        out_shape=(jax.ShapeDtypeStruct((t * ROW_WORDS, LANES), I32),
                   jax.ShapeDtypeStruct((TOP_K, t), I32),
                   jax.ShapeDtypeStruct((TOP_K, t), F32),
                   jax.ShapeDtypeStruct((TOP_K, t), I32),
                   jax.ShapeDtypeStruct((N_EXPERTS, ntiles * LANES), I32)),
        grid=(ntiles,),
        in_specs=[
            pl.BlockSpec((tm, D_MODEL), lambda i: (i, 0)),
            pl.BlockSpec((1, D_MODEL), lambda i: (0, 0)),
            pl.BlockSpec((1, 6, D_MODEL), lambda i: (_mod_row(i, tps), 0, 0)),
            pl.BlockSpec((N_EXPERTS, D_MODEL), lambda i: (0, 0)),
            pl.BlockSpec((N_EXPERTS, D_MODEL), lambda i: (0, 0)),
            pl.BlockSpec((N_EXPERTS, 1), lambda i: (0, 0)),
            pl.BlockSpec((tm, tm), lambda i: (0, 0)),
        ],
        out_specs=(pl.BlockSpec((tm * ROW_WORDS, LANES), lambda i: (i, 0)),
                   kt_spec, kt_spec, kt_spec,
                   pl.BlockSpec((N_EXPERTS, LANES), lambda i: (0, i))),
        compiler_params=_cparams(("arbitrary",)),
        name="norm_router",
    )(x, g.reshape(1, D_MODEL), mod, wr_hi, wr_lo, rbias.reshape(N_EXPERTS, 1), tri)


def _row_copy(src, src_tok, dst, dst_tok, sem):
    s = pl.multiple_of(src_tok * ROW_WORDS, ROW_WORDS)
    d = pl.multiple_of(dst_tok * ROW_WORDS, ROW_WORDS)
    return pltpu.make_async_copy(src.at[pl.ds(s, ROW_WORDS)], dst.at[pl.ds(d, ROW_WORDS)], sem)


def _dispatch_kernel(zends_ref, tok_ref, dest_hbm, wg_ref, wu_ref, wd_ref, xs_hbm, sh_ref,
                     dest_smem, zero_ref, sem_idx, sem_z, sem_s):
    i = pl.program_id(0)
    tm = tok_ref.shape[0] // ROW_WORDS
    n = pl.num_programs(0)

    def idx_copy(tile):
        return pltpu.make_async_copy(dest_hbm.at[tile], dest_smem, sem_idx)

    @pl.when(i == 0)
    def _():
        idx_copy(0).start()

    blk = EXPERT_ROWS * ROW_WORDS

    def pad_copy(e):
        row = pl.multiple_of((zends_ref[e] - EXPERT_ROWS) * ROW_WORDS, blk)
        return pltpu.make_async_copy(zero_ref, xs_hbm.at[pl.ds(row, blk)], sem_z)

    @pl.when(i == 0)
    def _():
        zero_ref[...] = jnp.zeros_like(zero_ref)

        def start(e, carry):
            @pl.when(zends_ref[e] > 0)
            def _():
                pad_copy(e).start()
            return carry

        lax.fori_loop(0, N_EXPERTS, start, 0)

        def wait(e, carry):
            @pl.when(zends_ref[e] > 0)
            def _():
                pad_copy(e).wait()
            return carry

        lax.fori_loop(0, N_EXPERTS, wait, 0)

    idx_copy(i).wait()

    x = _load_packed(tok_ref, tm, BF16)
    pieces = SHARED_FF // LANES
    per = tm // pieces
    acc = None
    for p in range(pieces):
        for r in range(p * per, (p + 1) * per):
            for k in range(TOP_K):
                _row_copy(tok_ref, r, xs_hbm, dest_smem[k * tm + r], sem_s).start(priority=k % 2)
        c = slice(p * LANES, (p + 1) * LANES)
        g = jnp.dot(x, wg_ref[:, c], preferred_element_type=F32)
        u = jnp.dot(x, wu_ref[:, c], preferred_element_type=F32)
        part = jnp.dot((_silu(g) * u).astype(BF16), wd_ref[c, :], preferred_element_type=F32)
        acc = part if acc is None else acc + part
    sh_ref[...] = acc

    @pl.when(i + 1 < n)
    def _():
        idx_copy(i + 1).start()

    def drain(r, carry):
        for k in range(TOP_K):
            _row_copy(tok_ref, r, xs_hbm, 0, sem_s).wait()
        return carry

    lax.fori_loop(0, tm, drain, 0)


def dispatch_shared(tokens, dest_tiles, zends, wg, wu, wd, *, ntiles):
    tm = tokens.shape[0] // ROW_WORDS // ntiles
    const = lambda i, zends: (0, 0)
    return pl.pallas_call(
        _dispatch_kernel,
        out_shape=(jax.ShapeDtypeStruct((N_BLOCKS * STEP_ROWS * ROW_WORDS, LANES), I32),
                   jax.ShapeDtypeStruct((ntiles * tm, D_MODEL), F32)),
        grid_spec=pltpu.PrefetchScalarGridSpec(
            num_scalar_prefetch=1,
            grid=(ntiles,),
            in_specs=[
                pl.BlockSpec((tm * ROW_WORDS, LANES), lambda i, zends: (i, 0)),
                pl.BlockSpec(memory_space=pl.ANY),
                pl.BlockSpec((D_MODEL, SHARED_FF), const),
                pl.BlockSpec((D_MODEL, SHARED_FF), const),
                pl.BlockSpec((SHARED_FF, D_MODEL), const),
            ],
            out_specs=(pl.BlockSpec(memory_space=pl.ANY),
                       pl.BlockSpec((tm, D_MODEL), lambda i, zends: (i, 0))),
            scratch_shapes=[
                pltpu.SMEM((TOP_K * tm,), I32),
                pltpu.VMEM((EXPERT_ROWS * ROW_WORDS, LANES), I32),
                pltpu.SemaphoreType.DMA,
                pltpu.SemaphoreType.DMA,
                pltpu.SemaphoreType.DMA,
            ],
        ),
        compiler_params=_cparams(("arbitrary",)),
        name="moe_dispatch",
    )(zends, tokens, dest_tiles, wg, wu, wd)


def _experts_kernel(blk_e_ref, nsub_ref, next_e_ref, nused_ref, xs_ref, wg_hbm, wu_hbm, wd_hbm, y_ref,
                    wgs, wus, wds, wgb, wub, wdb, sem_w, *, layer):
    i = pl.program_id(0)
    e = blk_e_ref[i]
    prev = blk_e_ref[jnp.maximum(i - 1, 0)]

    def weight_copies(expert):
        row = layer * N_EXPERTS + expert
        return (pltpu.make_async_copy(wg_hbm.at[row], wgs, sem_w.at[0]),
                pltpu.make_async_copy(wu_hbm.at[row], wus, sem_w.at[1]),
                pltpu.make_async_copy(wd_hbm.at[row], wds, sem_w.at[2]))

    @pl.when(i < nused_ref[0])
    def _():
        @pl.when(i == 0)
        def _():
            for cp in weight_copies(e):
                cp.start()

        @pl.when((i == 0) | (e != prev))
        def _():
            for cp in weight_copies(e):
                cp.wait()
            wgb[...] = wgs[...].astype(BF16)
            wub[...] = wus[...].astype(BF16)
            wdb[...] = wds[...].astype(BF16)

            @pl.when(next_e_ref[i] >= 0)
            def _():
                for cp in weight_copies(next_e_ref[i]):
                    cp.start()

        def mlp(sb):
            tok0 = sb * EXPERT_ROWS
            x = _load_packed(xs_ref, EXPERT_ROWS, BF16, tok0)
            g = jnp.dot(x, wgb[...], preferred_element_type=F32)
            u = jnp.dot(x, wub[...], preferred_element_type=F32)
            hb = (_silu(g) * u).astype(BF16)
            _store_packed(y_ref, jnp.dot(hb, wdb[...], preferred_element_type=F32), tok0)

        def clear(sb):
            rows = EXPERT_ROWS * ROW_WORDS
            y_ref[sb * rows:(sb + 1) * rows, :] = jnp.zeros((rows, LANES), I32)

        nsub = nsub_ref[i]
        for a in range(0, EXPERT_SUBS, 2):
            @pl.when(nsub >= a + 2)
            def _():
                mlp(a)
                mlp(a + 1)

            @pl.when(nsub == a + 1)
            def _():
                mlp(a)
                clear(a + 1)

            @pl.when(nsub <= a)
            def _():
                clear(a)
                clear(a + 1)


def experts(xs, blk_e, nsub, next_e, nused, w_gate, w_up, w_down, *, layer):
    w_gate = w_gate.reshape(DEPTH * N_EXPERTS, D_MODEL, EXPERT_FF)
    w_up = w_up.reshape(DEPTH * N_EXPERTS, D_MODEL, EXPERT_FF)
    w_down = w_down.reshape(DEPTH * N_EXPERTS, EXPERT_FF, D_MODEL)

    def row_map(i, blk_e, nsub, next_e, nused):
        return (jnp.minimum(i, nused[0] - 1), 0)

    return pl.pallas_call(
        functools.partial(_experts_kernel, layer=layer),
        out_shape=jax.ShapeDtypeStruct((N_BLOCKS * STEP_ROWS * ROW_WORDS, LANES), I32),
        grid_spec=pltpu.PrefetchScalarGridSpec(
            num_scalar_prefetch=4,
            grid=(N_BLOCKS,),
            in_specs=[
                pl.BlockSpec((STEP_ROWS * ROW_WORDS, LANES), row_map),
                pl.BlockSpec(memory_space=pl.ANY),
                pl.BlockSpec(memory_space=pl.ANY),
                pl.BlockSpec(memory_space=pl.ANY),
            ],
            out_specs=pl.BlockSpec((STEP_ROWS * ROW_WORDS, LANES), row_map),
            scratch_shapes=[
                pltpu.VMEM((D_MODEL, EXPERT_FF), F32),
                pltpu.VMEM((D_MODEL, EXPERT_FF), F32),
                pltpu.VMEM((EXPERT_FF, D_MODEL), F32),
                pltpu.VMEM((D_MODEL, EXPERT_FF), BF16),
                pltpu.VMEM((D_MODEL, EXPERT_FF), BF16),
                pltpu.VMEM((EXPERT_FF, D_MODEL), BF16),
                pltpu.SemaphoreType.DMA((3,)),
            ],
        ),
        compiler_params=_cparams(("arbitrary",)),
        name="moe_experts",
    )(blk_e, nsub, next_e, nused, xs, w_gate, w_up, w_down)


def _combine_kernel(x_ref, sh_ref, mod_ref, wk_ref, gf_ref, dest_hbm, ys_hbm, o_ref,
                    idx0, idx1, buf0, buf1, sem_idx, sem_g, *, final):
    i = pl.program_id(0)
    n = pl.num_programs(0)
    tm = x_ref.shape[0]
    idx = (idx0, idx1)
    buf = (buf0, buf1)

    def idx_copy(tile, slot):
        return pltpu.make_async_copy(dest_hbm.at[tile], idx[slot], sem_idx)

    def start_rows(slot, r):
        for k in range(TOP_K):
            _row_copy(ys_hbm, idx[slot][k * tm + r], buf[slot].at[k], r, sem_g.at[slot]).start(priority=k % 2)

    def drain(slot):
        def wait_rows(r, carry):
            for k in range(TOP_K):
                _row_copy(ys_hbm, 0, buf[slot].at[k], r, sem_g.at[slot]).wait()
            return carry

        lax.fori_loop(0, tm, wait_rows, 0)

    def finish(slot):
        idx_copy(0, 1 - slot).wait()
        drain(slot)
        idx_copy(jnp.minimum(i + 2, n - 1), slot).start()
        per = tm // ROW_WORDS
        los, his = [], []
        for s in range(ROW_WORDS):
            for r in range(s * per, (s + 1) * per):
                start_rows(1 - slot, r)
            acc_lo = sh_ref[:, s * LANES:(s + 1) * LANES]
            acc_hi = sh_ref[:, HALF + s * LANES:HALF + (s + 1) * LANES]
            for k in range(TOP_K):
                lo, hi = _unpack_words(buf[slot][k, pl.ds(s, tm, stride=ROW_WORDS), :])
                w = wk_ref[:, k:k + 1]
                acc_lo = acc_lo + lo * w
                acc_hi = acc_hi + hi * w
            los.append(acc_lo)
            his.append(acc_hi)
        f = jnp.concatenate(los + his, axis=-1)
        out = x_ref[...] + mod_ref[0, 5:6, :] * f
        if final:
            y = out * lax.rsqrt(jnp.mean(out * out, axis=-1, keepdims=True) + EPS)
            out = y * gf_ref[...]
        o_ref[...] = out

        @pl.when(i == n - 1)
        def _():
            drain(1 - slot)
            idx_copy(0, slot).wait()

    @pl.when(i == 0)
    def _():
        idx_copy(0, 0).start()
        idx_copy(0, 0).wait()

        def first(r, carry):
            start_rows(0, r)
            return carry

        lax.fori_loop(0, tm, first, 0)
        idx_copy(jnp.minimum(1, n - 1), 1).start()

    for slot in range(2):
        @pl.when(i % 2 == slot)
        def _(slot=slot):
            finish(slot)


def combine(x, shared, mod, wk_t, g_final, dest_tiles, ys, *, ntiles, final):
    tm = 128
    tps = SEQ // tm
    return pl.pallas_call(
        functools.partial(_combine_kernel, final=final),
        out_shape=jax.ShapeDtypeStruct((ntiles * tm, D_MODEL), F32),
        grid=(ntiles,),
        in_specs=[
            pl.BlockSpec((tm, D_MODEL), lambda i: (i, 0)),
            pl.BlockSpec((tm, D_MODEL), lambda i: (i, 0)),
            pl.BlockSpec((1, 6, D_MODEL), lambda i: (_mod_row(i, tps), 0, 0)),
            pl.BlockSpec((tm, TOP_K), lambda i: (i, 0)),
            pl.BlockSpec((1, D_MODEL), lambda i: (0, 0)),
            pl.BlockSpec(memory_space=pl.ANY),
            pl.BlockSpec(memory_space=pl.ANY),
        ],
        out_specs=pl.BlockSpec((tm, D_MODEL), lambda i: (i, 0)),
        scratch_shapes=[
            pltpu.SMEM((TOP_K * tm,), I32),
            pltpu.SMEM((TOP_K * tm,), I32),
            pltpu.VMEM((TOP_K, tm * ROW_WORDS, LANES), I32),
            pltpu.VMEM((TOP_K, tm * ROW_WORDS, LANES), I32),
            pltpu.SemaphoreType.DMA,
            pltpu.SemaphoreType.DMA((2,)),
        ],
        compiler_params=_cparams(("arbitrary",)),
        name="moe_combine",
    )(x, shared, mod, wk_t, g_final.reshape(1, D_MODEL), dest_tiles, ys)


def _routing_plan(eidx, rank, tile_counts):
    t = eidx.shape[1]
    tiles = tile_counts.shape[1]
    counts = jnp.sum(tile_counts, axis=1)
    pcounts = (counts + STEP_ROWS - 1) // STEP_ROWS * STEP_ROWS
    pends = jnp.cumsum(pcounts)
    pstarts = pends - pcounts
    base = pstarts[:, None] + jnp.cumsum(tile_counts, axis=1) - tile_counts
    onehot = (eidx.reshape(TOP_K, tiles, 1, t // tiles)
              == jnp.arange(N_EXPERTS, dtype=I32)[None, None, :, None])
    dest = rank + jnp.sum(jnp.where(onehot, base.T[None, :, :, None], 0), axis=2).reshape(TOP_K, t)
    nused = (pends[-1] // STEP_ROWS).astype(I32).reshape(1)
    blocks = jnp.arange(N_BLOCKS, dtype=I32)
    blk_e = jnp.minimum(jnp.sum((pends[None, :] <= (blocks * STEP_ROWS)[:, None]).astype(I32), axis=1),
                        N_EXPERTS - 1).astype(I32)
    nsub_e = (counts + EXPERT_ROWS - 1) // EXPERT_ROWS
    zends = jnp.where(counts > 0, pstarts + nsub_e * EXPERT_ROWS, 0)
    done = (blocks - pstarts[blk_e] // STEP_ROWS) * EXPERT_SUBS
    nsub = jnp.clip(nsub_e[blk_e] - done, 0, EXPERT_SUBS)
    ids = jnp.where(counts > 0, jnp.arange(N_EXPERTS, dtype=I32), N_EXPERTS)
    later = jnp.concatenate([lax.cummin(ids[::-1])[::-1][1:], jnp.full((1,), N_EXPERTS, I32)])
    next_e = jnp.where(later < N_EXPERTS, later, -1)[blk_e]
    return dest.astype(I32), zends.astype(I32), blk_e, nsub.astype(I32), next_e.astype(I32), nused


def _tile_major(dest, tm):
    k, t = dest.shape
    return dest.reshape(k, t // tm, tm).transpose(1, 0, 2).reshape(t // tm, k * tm)


def _rope_tables(n_rows):
    row = jnp.repeat(jnp.arange(n_rows, dtype=F32), GRID_W)
    col = jnp.tile(jnp.arange(GRID_W, dtype=F32), n_rows)
    half = QK_ROPE // 2
    inv = ROPE_BASE ** (-jnp.arange(0, half, 2, dtype=F32) / half)
    ang_r = row[:, None] * inv
    ang_c = col[:, None] * inv
    ang = jnp.concatenate([ang_r, ang_r, ang_c, ang_c], axis=-1)
    n = ang.shape[0]
    ones = jnp.ones((n, QK_NOPE), F32)
    zeros = jnp.zeros((n, QK_NOPE), F32)
    tail = jnp.zeros((n, HEAD_PAD - QK_NOPE - QK_ROPE), F32)
    cos = jnp.concatenate([ones, jnp.cos(ang), tail], axis=-1)
    sin = jnp.concatenate([zeros, jnp.sin(ang), tail], axis=-1)
    return cos, sin


def _identity_tables(n):
    cos = jnp.concatenate([jnp.ones((n, QK_NOPE + QK_ROPE), F32),
                           jnp.zeros((n, HEAD_PAD - QK_NOPE - QK_ROPE), F32)], axis=-1)
    return cos, jnp.zeros((n, HEAD_PAD), F32)


def _rotate_cols(w):
    i = np.arange(QK_ROPE)
    first = (i % (QK_ROPE // 2)) < (QK_ROPE // 4)
    perm = np.where(first, i + QK_ROPE // 4, i - QK_ROPE // 4)
    sign = np.where(first, -1.0, 1.0).astype(np.float32)
    return w[:, perm] * sign


def _mla_weights(g_cq, w_uq, g_ckv, w_ukv):
    dk = QK_NOPE + QK_ROPE
    wq = w_uq.reshape(Q_LORA, MLA_HEADS, dk)
    pad = jnp.zeros((Q_LORA, MLA_HEADS, HEAD_PAD - dk), F32)
    zero_nope = jnp.zeros((Q_LORA, MLA_HEADS, QK_NOPE), F32)
    wq_rot = _rotate_cols(wq[..., QK_NOPE:].reshape(Q_LORA * MLA_HEADS, QK_ROPE)).reshape(
        Q_LORA, MLA_HEADS, QK_ROPE)
    wq_a = jnp.concatenate([wq, pad], axis=-1).reshape(Q_LORA, -1)
    wq_b = jnp.concatenate([zero_nope, wq_rot, pad], axis=-1).reshape(Q_LORA, -1)
    wkv = w_ukv.reshape(KV_LORA, MLA_HEADS, QK_NOPE + V_HEAD)
    wk = jnp.concatenate([wkv[..., :QK_NOPE],
                          jnp.zeros((KV_LORA, MLA_HEADS, HEAD_PAD - QK_NOPE), F32)], axis=-1)
    wv = wkv[..., QK_NOPE:]
    eye = jnp.eye(QK_ROPE, dtype=F32)
    place = jnp.zeros((LANES, MLA_HEADS, HEAD_PAD), F32)
    place_a = place.at[:QK_ROPE, :, QK_NOPE:dk].set(jnp.broadcast_to(eye[:, None, :], (QK_ROPE, MLA_HEADS, QK_ROPE)))
    rot = _rotate_cols(eye)
    place_b = place.at[:QK_ROPE, :, QK_NOPE:dk].set(jnp.broadcast_to(rot[:, None, :], (QK_ROPE, MLA_HEADS, QK_ROPE)))
    return {
        "g_cq": g_cq.reshape(1, -1), "g_ckv": g_ckv.reshape(1, -1),
        "wq2t": jnp.concatenate([wq_a, wq_b], axis=-1).T.astype(BF16),
        "wk": wk.reshape(KV_LORA, -1).astype(BF16),
        "wvt": wv.reshape(KV_LORA, -1).T.astype(BF16),
        "pk": jnp.concatenate([place_a.reshape(LANES, -1), place_b.reshape(LANES, -1)], axis=-1).astype(BF16),
    }


def _pack_w_in(w):
    gates0 = MLA_IN + 2 * CONV_CH + FFT_CH + POOL_CH
    pad = jnp.zeros((D_MODEL, P_MLA_W - MLA_IN), BF16)
    rest = jnp.concatenate([w[:, :MLA_IN].astype(BF16), pad, w[:, MLA_IN:gates0].astype(BF16)], axis=-1)
    return w[:, gates0:].astype(BF16), rest


def _dft_tables(n):
    gw = FFT_CH // FFT_GROUPS

    def angles(rows, m):
        k = jnp.arange(m, dtype=I32)
        return (rows[:, None] * k[None, :] % m).astype(F32) * (2.0 * np.pi / m)

    ac = angles(jnp.arange(gw, dtype=I32), gw)
    cs = jnp.concatenate([jnp.cos(ac), jnp.sin(ac)], axis=-1).astype(BF16)
    ns = int(np.sqrt(n))
    assert ns * ns == n
    a_hi = angles(jnp.arange(ns, dtype=I32) * ns, n)
    a_lo = angles(jnp.arange(ns, dtype=I32), n)
    ch, sh, cl, sl = jnp.cos(a_hi)[:, None, :], jnp.sin(a_hi)[:, None, :], jnp.cos(a_lo)[None], jnp.sin(a_lo)[None]
    cos_n = (ch * cl - sh * sl).reshape(n, n).astype(BF16)
    nsin_n = (-(sh * cl + ch * sl)).reshape(n, n).astype(BF16)
    return cs, cos_n, nsin_n


def _mixers(p, nseq, n, wts, tables, conv_args, pool_args, dft, **kv):
    q4, k4, v = mla_prep(p, wts, tables, row0=0, col0=P_MLA, nseq=nseq, n=n, **kv)
    cv = conv_branch(p, *conv_args, row0=0, nseq=nseq, n=n)
    ff = fft_branch(p, *dft, row0=0, nseq=nseq, n=n)
    po = pool_branch(p, *pool_args, row0=0, nseq=nseq, n=n)
    return q4, k4, v, cv, ff, po


def kernel(x, c, ctx, c_ctx, w_mod, b_mod, g_norm1, g_norm2, w_in, g_cq, w_uq, g_ckv, w_ukv, w_o_mla, conv_w, conv_b, conv_ln_g, conv_ln_b, w_o_conv, w_o_fft, pool_w, pool_scale, w_o_pool, w_out, w_router, router_bias, w_exp_gate, w_exp_up, w_exp_down, w_sh_gate, w_sh_up, w_sh_down, g_final):
    x_lat = x.reshape(T_LAT, D_MODEL)
    x_ctx = ctx.reshape(T_CTX, D_MODEL)
    ctx_tile0 = 0
    cvec = jnp.concatenate([c, c_ctx[None, :], jnp.zeros((SUBLANES - BATCH - 1, D_MODEL), F32)], axis=0)
    mod_all = modulation_all(cvec, w_mod, b_mod).reshape(DEPTH, SUBLANES, 6, D_MODEL)
    rope_lat = _rope_tables(SEQ // GRID_W)
    rope_ctx = _identity_tables(CTX_LEN)
    dft_lat = _dft_tables(SEQ)
    dft_ctx = _dft_tables(CTX_LEN)
    lat_tiles = T_LAT // ROW_TILE
    all_tiles = T_ALL // ROW_TILE

    for l in range(DEPTH):
        last = l == DEPTH - 1
        mod = mod_all[l]
        w_gate, w_rest = _pack_w_in(w_in[l])
        wts = _mla_weights(g_cq[l], w_uq[l], g_ckv[l], w_ukv[l])
        conv_args = (conv_w[l], conv_b[l], conv_ln_g[l], conv_ln_b[l])
        pool_args = (pool_w[l].astype(BF16), pool_scale[l])
        wo4 = jnp.stack([w_o_mla[l], w_o_conv[l], w_o_fft[l], w_o_pool[l]], axis=0).astype(BF16)
        w_out_b = w_out[l].astype(BF16)

        ctx_col0, ctx_cols = (P_MLA, P_MLA_W) if last else (0, P_COLS)
        p = normproj(x_lat, g_norm1[l], mod, w_gate, w_rest, tile0=0, ntiles=lat_tiles, col0=0, ncols=P_COLS)
        p_ctx = normproj(x_ctx, g_norm1[l], mod, w_gate, w_rest, tile0=ctx_tile0, ntiles=all_tiles - lat_tiles,
                         mod_tile0=lat_tiles, col0=ctx_col0, ncols=ctx_cols)
        nkeys = SEQ + CTX_LEN
        q4, k_all, v_all, cv, ff, po = _mixers(p, BATCH, SEQ, wts, rope_lat, conv_args, pool_args, dft_lat,
                                               n_keys=nkeys)
        if last:
            qc, k_all, v_all = mla_prep(p_ctx, wts, rope_ctx, row0=0, col0=0, nseq=BATCH, n=CTX_LEN,
                                        n_keys=nkeys, key0=SEQ, into=(k_all, v_all))
        else:
            qc, k_all, v_all, cvc, ffc, poc = _mixers(p_ctx, BATCH, CTX_LEN, wts, rope_ctx, conv_args,
                                                      pool_args, dft_ctx, n_keys=nkeys, key0=SEQ,
                                                      into=(k_all, v_all))
        at = attention(q4, k_all, v_all, nq=SEQ, nk=nkeys)
        lat_t, ctx_t = T_LAT // 256, T_CTX // 256
        if last:
            ntok = T_LAT
            xa = merge(x_lat, mod, at, cv, ff, po, p, wo4, w_out_b, ntiles=lat_t)
        else:
            ntok = T_ALL
            atc = attention(qc, k_all, v_all, nq=CTX_LEN, nk=CTX_LEN, key0=SEQ)
            xa = merge(x_lat, mod, at, cv, ff, po, p, wo4, w_out_b, ntiles=lat_t, out_tiles=lat_t + ctx_t)
            xa = merge(x_ctx, mod, atc, cvc, ffc, poc, p_ctx, wo4, w_out_b, ntiles=ctx_t, out_tile0=lat_t,
                       out_tiles=lat_t + ctx_t, into=xa)

        wr_t = w_router[l].T
        wr_hi = wr_t.astype(BF16)
        wr_lo = (wr_t - wr_hi.astype(F32)).astype(BF16)
        tokens, eidx, wk, rank, cnt = norm_router(xa, g_norm2[l], mod, wr_hi, wr_lo, router_bias[l],
                                                  ntiles=ntok // ROUTER_TILE)
        dest, zends, blk_e, nsub, next_e, nused = _routing_plan(eidx, rank, cnt[:, ::LANES])
        xs, sh = dispatch_shared(tokens, _tile_major(dest, 256), zends, w_sh_gate[l].astype(BF16),
                                 w_sh_up[l].astype(BF16), w_sh_down[l].astype(BF16), ntiles=ntok // 256)
        ys = experts(xs, blk_e, nsub, next_e, nused, w_exp_gate, w_exp_up, w_exp_down, layer=l)
        xa = combine(xa, sh, mod, wk.T, g_final, _tile_major(dest, 128), ys, ntiles=ntok // 128, final=last)
        x_lat, x_ctx, ctx_tile0 = xa, xa, lat_tiles
    return xa.reshape(BATCH, SEQ, D_MODEL)
```
